```python
import math
import jax, jax.numpy as jnp
from jax import lax
import numpy as np

D_MODEL = 1024
BATCH = 8
SEQ = 2048
DEPTH = 1

N_META = 16
C_CONV = D_MODEL
CONV_WIDTH = 31
N_HEADS = 16
HEAD_DIM = 64
ATTN_W = N_HEADS * HEAD_DIM
Q_BLOCK = 128
D_FF = ((8 * D_MODEL // 3 + 255) // 256) * 256
N_BRANCH = 2
IN_W = 2 * C_CONV + 3 * ATTN_W + N_BRANCH * D_MODEL
RMS_EPS = 1e-6
LN_EPS = 1e-5

kernel_name = "hybrid_conformer_stickbreaking_block"


def rms_norm(x, g):
    xf = x.astype(jnp.float32)
    y = xf * lax.rsqrt(jnp.mean(xf * xf, axis=-1, keepdims=True) + RMS_EPS)
    return (y * g.astype(jnp.float32)).astype(x.dtype)


def layer_norm(x, g, b):
    xf = x.astype(jnp.float32)
    mu = jnp.mean(xf, axis=-1, keepdims=True)
    xc = xf - mu
    var = jnp.mean(xc * xc, axis=-1, keepdims=True)
    y = xc * lax.rsqrt(var + LN_EPS) * g.astype(jnp.float32) + b.astype(jnp.float32)
    return y.astype(x.dtype)


def conformer_conv(u_glu, dw_w, dw_b, ln_g, ln_b, w_out):
    a, gate = jnp.split(u_glu, 2, axis=-1)
    u = a * jax.nn.sigmoid(gate)
    y = lax.conv_general_dilated(
        u, dw_w[:, None, :], window_strides=(1,),
        padding=[(CONV_WIDTH - 1, 0)],
        dimension_numbers=("NWC", "WIO", "NWC"),
        feature_group_count=C_CONV) + dw_b
    y = jax.nn.silu(layer_norm(y, ln_g, ln_b))
    return y @ w_out


def stick_breaking_block(q_blk, k_pre, v_pre, q_start):
    nq, nk = q_blk.shape[1], k_pre.shape[1]
    scale = 1.0 / math.sqrt(HEAD_DIM)
    z = jnp.einsum("bqhd,bkhd->bhqk", q_blk.astype(jnp.float32),
                   k_pre.astype(jnp.float32)) * scale
    t = q_start + jnp.arange(nq)
    s = jnp.arange(nk)
    mask = s[None, :] < t[:, None]
    sp = jnp.where(mask, jax.nn.softplus(z), 0.0)
    r = lax.cumsum(sp, axis=3, reverse=True)
    log_a = jnp.where(mask, z - r, -jnp.inf)
    a = jnp.exp(log_a)
    out = jnp.einsum("bhqk,bkhd->bqhd", a, v_pre.astype(jnp.float32))
    return out.astype(q_blk.dtype)


def stick_breaking_attention(q, k, v):
    L = q.shape[1]
    n_real_blocks = (L - N_META) // Q_BLOCK
    bounds = [(0, N_META)] + [(N_META + i * Q_BLOCK, N_META + (i + 1) * Q_BLOCK)
                              for i in range(n_real_blocks)]
    outs = []
    for (st, en) in bounds:
        outs.append(stick_breaking_block(q[:, st:en], k[:, :en], v[:, :en], st))
    return jnp.concatenate(outs, axis=1)


def _fwd_setup_inputs(seed: int = 0) -> dict:
    key = jax.random.key(seed)
    ks = jax.random.split(key, 20)
    f32 = jnp.float32
    nrm = lambda k, shape, s: jax.random.normal(k, shape, f32) * s
    gain = lambda k, shape: 1.0 + 0.02 * jax.random.normal(k, shape, f32)
    return {
        "x": jax.random.normal(ks[0], (BATCH, SEQ, D_MODEL), f32),
        "meta_tokens": nrm(ks[1], (N_META, D_MODEL), 1.0),
        "pre_mix_g": gain(ks[2], (DEPTH, D_MODEL)),
        "w_in": nrm(ks[3], (DEPTH, D_MODEL, IN_W), D_MODEL ** -0.5),
        "gate_b": nrm(ks[4], (DEPTH, N_BRANCH * D_MODEL), 0.02),
        "dw_w": nrm(ks[5], (DEPTH, CONV_WIDTH, C_CONV), CONV_WIDTH ** -0.5),
        "dw_b": nrm(ks[6], (DEPTH, C_CONV), 0.02),
        "conv_ln_g": gain(ks[7], (DEPTH, C_CONV)),
        "conv_ln_b": nrm(ks[8], (DEPTH, C_CONV), 0.02),
        "w_conv_out": nrm(ks[9], (DEPTH, C_CONV, D_MODEL), C_CONV ** -0.5),
        "w_attn_out": nrm(ks[10], (DEPTH, ATTN_W, D_MODEL), ATTN_W ** -0.5),
        "w_o": nrm(ks[11], (DEPTH, D_MODEL, D_MODEL), D_MODEL ** -0.5),
        "post_mix_g": gain(ks[12], (DEPTH, D_MODEL)),
        "pre_ffn_g": gain(ks[13], (DEPTH, D_MODEL)),
        "w_ffn_in": nrm(ks[14], (DEPTH, D_MODEL, 2 * D_FF), D_MODEL ** -0.5),
        "w_ffn_out": nrm(ks[15], (DEPTH, D_FF, D_MODEL), D_FF ** -0.5),
        "post_ffn_g": gain(ks[16], (DEPTH, D_MODEL)),
    }


def _fwd_reference(x, meta_tokens, pre_mix_g, w_in, gate_b, dw_w, dw_b, conv_ln_g, conv_ln_b,
              w_conv_out, w_attn_out, w_o, post_mix_g, pre_ffn_g, w_ffn_in, w_ffn_out,
              post_ffn_g):
    B = x.shape[0]
    meta = jnp.broadcast_to(meta_tokens[None].astype(x.dtype), (B, N_META, D_MODEL))
    h = jnp.concatenate([meta, x], axis=1)
    L = h.shape[1]
    for l in range(DEPTH):
        u = rms_norm(h, pre_mix_g[l])
        p = u @ w_in[l]
        o1 = 2 * C_CONV
        o2 = o1 + ATTN_W
        o3 = o2 + ATTN_W
        o4 = o3 + ATTN_W
        p_glu = p[..., :o1]
        q = p[..., o1:o2].reshape(B, L, N_HEADS, HEAD_DIM)
        k = p[..., o2:o3].reshape(B, L, N_HEADS, HEAD_DIM)
        v = p[..., o3:o4].reshape(B, L, N_HEADS, HEAD_DIM)
        gates = jax.nn.sigmoid(p[..., o4:] + gate_b[l])
        g_conv, g_attn = jnp.split(gates, 2, axis=-1)

        y_conv = conformer_conv(p_glu, dw_w[l], dw_b[l], conv_ln_g[l], conv_ln_b[l],
                                w_conv_out[l])
        y_attn = stick_breaking_attention(q, k, v).reshape(B, L, ATTN_W) @ w_attn_out[l]

        mix = (g_conv * y_conv + g_attn * y_attn) @ w_o[l]
        h = h + rms_norm(mix, post_mix_g[l])

        u = rms_norm(h, pre_ffn_g[l])
        a, b = jnp.split(u @ w_ffn_in[l], 2, axis=-1)
        f = (jax.nn.silu(a) * b) @ w_ffn_out[l]
        h = h + rms_norm(f, post_ffn_g[l])
    return h[:, N_META:]


import jax as _jax
import jax.numpy as _jnp

TWIN_FORMAT = 'train_step'
FWD_PARAMS = ['x', 'meta_tokens', 'pre_mix_g', 'w_in', 'gate_b', 'dw_w', 'dw_b', 'conv_ln_g', 'conv_ln_b', 'w_conv_out', 'w_attn_out', 'w_o', 'post_mix_g', 'pre_ffn_g', 'w_ffn_in', 'w_ffn_out', 'post_ffn_g']
TWIN_WEIGHTS = ['meta_tokens', 'pre_mix_g', 'w_in', 'gate_b', 'dw_w', 'dw_b', 'conv_ln_g', 'conv_ln_b', 'w_conv_out', 'w_attn_out', 'w_o', 'post_mix_g', 'pre_ffn_g', 'w_ffn_in', 'w_ffn_out', 'post_ffn_g']
TWIN_DIFF_INPUT = 'x'
TWIN_INPUTS = ['x', 'meta_tokens', 'pre_mix_g', 'w_in', 'gate_b', 'dw_w', 'dw_b', 'conv_ln_g', 'conv_ln_b', 'w_conv_out', 'w_attn_out', 'w_o', 'post_mix_g', 'pre_ffn_g', 'w_ffn_in', 'w_ffn_out', 'post_ffn_g', 'loss_target', 'm_meta_tokens', 'm_pre_mix_g', 'm_w_in', 'm_gate_b', 'm_dw_w', 'm_dw_b', 'm_conv_ln_g', 'm_conv_ln_b', 'm_w_conv_out', 'm_w_attn_out', 'm_w_o', 'm_post_mix_g', 'm_pre_ffn_g', 'm_w_ffn_in', 'm_w_ffn_out', 'm_post_ffn_g', 'v_meta_tokens', 'v_pre_mix_g', 'v_w_in', 'v_gate_b', 'v_dw_w', 'v_dw_b', 'v_conv_ln_g', 'v_conv_ln_b', 'v_w_conv_out', 'v_w_attn_out', 'v_w_o', 'v_post_mix_g', 'v_pre_ffn_g', 'v_w_ffn_in', 'v_w_ffn_out', 'v_post_ffn_g']
TWIN_OUTPUTS = ['loss', 'grad_x', 'grad_meta_tokens', 'grad_pre_mix_g', 'grad_w_in', 'grad_gate_b', 'grad_dw_w', 'grad_dw_b', 'grad_conv_ln_g', 'grad_conv_ln_b', 'grad_w_conv_out', 'grad_w_attn_out', 'grad_w_o', 'grad_post_mix_g', 'grad_pre_ffn_g', 'grad_w_ffn_in', 'grad_w_ffn_out', 'grad_post_ffn_g', 'delta_meta_tokens', 'delta_pre_mix_g', 'delta_w_in', 'delta_gate_b', 'delta_dw_w', 'delta_dw_b', 'delta_conv_ln_g', 'delta_conv_ln_b', 'delta_w_conv_out', 'delta_w_attn_out', 'delta_w_o', 'delta_post_mix_g', 'delta_pre_ffn_g', 'delta_w_ffn_in', 'delta_w_ffn_out', 'delta_post_ffn_g', 'new_m_meta_tokens', 'new_m_pre_mix_g', 'new_m_w_in', 'new_m_gate_b', 'new_m_dw_w', 'new_m_dw_b', 'new_m_conv_ln_g', 'new_m_conv_ln_b', 'new_m_w_conv_out', 'new_m_w_attn_out', 'new_m_w_o', 'new_m_post_mix_g', 'new_m_pre_ffn_g', 'new_m_w_ffn_in', 'new_m_w_ffn_out', 'new_m_post_ffn_g', 'new_v_meta_tokens', 'new_v_pre_mix_g', 'new_v_w_in', 'new_v_gate_b', 'new_v_dw_w', 'new_v_dw_b', 'new_v_conv_ln_g', 'new_v_conv_ln_b', 'new_v_w_conv_out', 'new_v_w_attn_out', 'new_v_w_o', 'new_v_post_mix_g', 'new_v_pre_ffn_g', 'new_v_w_ffn_in', 'new_v_w_ffn_out', 'new_v_post_ffn_g']
TWIN_LEAF_KINDS = {'loss': 'loss', 'grad_x': 'grad_x', 'grad_meta_tokens': 'grad_w', 'grad_pre_mix_g': 'grad_w', 'grad_w_in': 'grad_w', 'grad_gate_b': 'grad_w', 'grad_dw_w': 'grad_w', 'grad_dw_b': 'grad_w', 'grad_conv_ln_g': 'grad_w', 'grad_conv_ln_b': 'grad_w', 'grad_w_conv_out': 'grad_w', 'grad_w_attn_out': 'grad_w', 'grad_w_o': 'grad_w', 'grad_post_mix_g': 'grad_w', 'grad_pre_ffn_g': 'grad_w', 'grad_w_ffn_in': 'grad_w', 'grad_w_ffn_out': 'grad_w', 'grad_post_ffn_g': 'grad_w', 'delta_meta_tokens': 'delta_w', 'delta_pre_mix_g': 'delta_w', 'delta_w_in': 'delta_w', 'delta_gate_b': 'delta_w', 'delta_dw_w': 'delta_w', 'delta_dw_b': 'delta_w', 'delta_conv_ln_g': 'delta_w', 'delta_conv_ln_b': 'delta_w', 'delta_w_conv_out': 'delta_w', 'delta_w_attn_out': 'delta_w', 'delta_w_o': 'delta_w', 'delta_post_mix_g': 'delta_w', 'delta_pre_ffn_g': 'delta_w', 'delta_w_ffn_in': 'delta_w', 'delta_w_ffn_out': 'delta_w', 'delta_post_ffn_g': 'delta_w', 'new_m_meta_tokens': 'new_m', 'new_m_pre_mix_g': 'new_m', 'new_m_w_in': 'new_m', 'new_m_gate_b': 'new_m', 'new_m_dw_w': 'new_m', 'new_m_dw_b': 'new_m', 'new_m_conv_ln_g': 'new_m', 'new_m_conv_ln_b': 'new_m', 'new_m_w_conv_out': 'new_m', 'new_m_w_attn_out': 'new_m', 'new_m_w_o': 'new_m', 'new_m_post_mix_g': 'new_m', 'new_m_pre_ffn_g': 'new_m', 'new_m_w_ffn_in': 'new_m', 'new_m_w_ffn_out': 'new_m', 'new_m_post_ffn_g': 'new_m', 'new_v_meta_tokens': 'new_v', 'new_v_pre_mix_g': 'new_v', 'new_v_w_in': 'new_v', 'new_v_gate_b': 'new_v', 'new_v_dw_w': 'new_v', 'new_v_dw_b': 'new_v', 'new_v_conv_ln_g': 'new_v', 'new_v_conv_ln_b': 'new_v', 'new_v_w_conv_out': 'new_v', 'new_v_w_attn_out': 'new_v', 'new_v_w_o': 'new_v', 'new_v_post_mix_g': 'new_v', 'new_v_pre_ffn_g': 'new_v', 'new_v_w_ffn_in': 'new_v', 'new_v_w_ffn_out': 'new_v', 'new_v_post_ffn_g': 'new_v'}


def _forward(args):
    return _fwd_reference(*[args[k] for k in FWD_PARAMS])


def _output_shape():
    out = _jax.eval_shape(lambda: _forward(_fwd_setup_inputs(0)))
    return out.shape, out.dtype

N_MICROBATCH = 1
ADAM_LR = 0.001
ADAM_B1 = 0.9
ADAM_B2 = 0.999
ADAM_EPS = 1e-08
ADAM_WD = 0.01
ADAM_STEP = 10
PER_EXAMPLE_BATCH_AXIS = {'x': 0, 'loss_target': 0}
SHARED_INPUTS = []
_WEIGHT_DTYPES = {'meta_tokens': _jnp.float32, 'pre_mix_g': _jnp.float32, 'w_in': _jnp.float32, 'gate_b': _jnp.float32, 'dw_w': _jnp.float32, 'dw_b': _jnp.float32, 'conv_ln_g': _jnp.float32, 'conv_ln_b': _jnp.float32, 'w_conv_out': _jnp.float32, 'w_attn_out': _jnp.float32, 'w_o': _jnp.float32, 'post_mix_g': _jnp.float32, 'pre_ffn_g': _jnp.float32, 'w_ffn_in': _jnp.float32, 'w_ffn_out': _jnp.float32, 'post_ffn_g': _jnp.float32}
MOMENT_SCALE = {'meta_tokens': 1.461601e-02, 'pre_mix_g': 4.007288e-01, 'w_in': 1.457303e-01, 'gate_b': 1.194796e-01, 'dw_w': 2.349994e-01, 'dw_b': 1.412176e+00, 'conv_ln_g': 5.520939e-01, 'conv_ln_b': 7.678628e-01, 'w_conv_out': 3.476594e-01, 'w_attn_out': 2.428618e-01, 'w_o': 4.468833e-01, 'post_mix_g': 1.604133e+01, 'pre_ffn_g': 4.560073e-01, 'w_ffn_in': 1.917714e-01, 'w_ffn_out': 3.659778e-01, 'post_ffn_g': 1.597655e+01}


def _to_microbatches(a, axis):
    t = _jnp.moveaxis(a, axis, 0)
    t = t.reshape((N_MICROBATCH, t.shape[0] // N_MICROBATCH) + t.shape[1:])
    return _jnp.moveaxis(t, 1, axis + 1)


def setup_inputs(seed: int = 0) -> dict:
    inp = _fwd_setup_inputs(seed)
    key = _jax.random.fold_in(_jax.random.key(seed), 7919)
    shape, _ = _output_shape()
    out = dict(inp)
    out["loss_target"] = _jax.random.normal(_jax.random.fold_in(key, 0), shape, _jnp.float32)
    for i, name in enumerate(TWIN_WEIGHTS):
        w = inp[name].astype(_jnp.float32)
        if MOMENT_SCALE is None:
            s = _jnp.sqrt(_jnp.mean(_jnp.square(w)) + 1e-30)
        else:
            s = MOMENT_SCALE[name]
        km, kv = _jax.random.split(_jax.random.fold_in(key, i + 1))
        out[name] = w
        out["m_" + name] = s * _jax.random.normal(km, w.shape, _jnp.float32)
        out["v_" + name] = (s * s) * _jax.random.uniform(kv, w.shape, _jnp.float32, 0.5, 1.5)
    if N_MICROBATCH > 1:
        for name, axis in PER_EXAMPLE_BATCH_AXIS.items():
            out[name] = _to_microbatches(out[name], axis)
    return {'x': out['x'], 'meta_tokens': out['meta_tokens'], 'pre_mix_g': out['pre_mix_g'], 'w_in': out['w_in'], 'gate_b': out['gate_b'], 'dw_w': out['dw_w'], 'dw_b': out['dw_b'], 'conv_ln_g': out['conv_ln_g'], 'conv_ln_b': out['conv_ln_b'], 'w_conv_out': out['w_conv_out'], 'w_attn_out': out['w_attn_out'], 'w_o': out['w_o'], 'post_mix_g': out['post_mix_g'], 'pre_ffn_g': out['pre_ffn_g'], 'w_ffn_in': out['w_ffn_in'], 'w_ffn_out': out['w_ffn_out'], 'post_ffn_g': out['post_ffn_g'], 'loss_target': out['loss_target'], 'm_meta_tokens': out['m_meta_tokens'], 'm_pre_mix_g': out['m_pre_mix_g'], 'm_w_in': out['m_w_in'], 'm_gate_b': out['m_gate_b'], 'm_dw_w': out['m_dw_w'], 'm_dw_b': out['m_dw_b'], 'm_conv_ln_g': out['m_conv_ln_g'], 'm_conv_ln_b': out['m_conv_ln_b'], 'm_w_conv_out': out['m_w_conv_out'], 'm_w_attn_out': out['m_w_attn_out'], 'm_w_o': out['m_w_o'], 'm_post_mix_g': out['m_post_mix_g'], 'm_pre_ffn_g': out['m_pre_ffn_g'], 'm_w_ffn_in': out['m_w_ffn_in'], 'm_w_ffn_out': out['m_w_ffn_out'], 'm_post_ffn_g': out['m_post_ffn_g'], 'v_meta_tokens': out['v_meta_tokens'], 'v_pre_mix_g': out['v_pre_mix_g'], 'v_w_in': out['v_w_in'], 'v_gate_b': out['v_gate_b'], 'v_dw_w': out['v_dw_w'], 'v_dw_b': out['v_dw_b'], 'v_conv_ln_g': out['v_conv_ln_g'], 'v_conv_ln_b': out['v_conv_ln_b'], 'v_w_conv_out': out['v_w_conv_out'], 'v_w_attn_out': out['v_w_attn_out'], 'v_w_o': out['v_w_o'], 'v_post_mix_g': out['v_post_mix_g'], 'v_pre_ffn_g': out['v_pre_ffn_g'], 'v_w_ffn_in': out['v_w_ffn_in'], 'v_w_ffn_out': out['v_w_ffn_out'], 'v_post_ffn_g': out['v_post_ffn_g']}


def _loss(weights, diff, rest, loss_target):
    with _jax.named_scope("forward"):
        args = {**rest, TWIN_DIFF_INPUT: diff, **{k: w.astype(_WEIGHT_DTYPES[k]) for k, w in weights.items()}}
        y = _forward(args)
    with _jax.named_scope("loss_head"):
        err = _jnp.square(y.astype(_jnp.float32) - loss_target)
        return 0.5 * _jnp.sum(_jnp.mean(err, axis=-1)) if err.ndim else 0.5 * err


def _adamw(w, g, m, v):
    m = ADAM_B1 * m + (1.0 - ADAM_B1) * g
    v = ADAM_B2 * v + (1.0 - ADAM_B2) * _jnp.square(g)
    m_hat = m / (1.0 - ADAM_B1 ** ADAM_STEP)
    v_hat = v / (1.0 - ADAM_B2 ** ADAM_STEP)
    delta = -ADAM_LR * (m_hat / (_jnp.sqrt(v_hat) + ADAM_EPS) + ADAM_WD * w)
    return delta, m, v


def reference(x, meta_tokens, pre_mix_g, w_in, gate_b, dw_w, dw_b, conv_ln_g, conv_ln_b, w_conv_out, w_attn_out, w_o, post_mix_g, pre_ffn_g, w_ffn_in, w_ffn_out, post_ffn_g, loss_target, m_meta_tokens, m_pre_mix_g, m_w_in, m_gate_b, m_dw_w, m_dw_b, m_conv_ln_g, m_conv_ln_b, m_w_conv_out, m_w_attn_out, m_w_o, m_post_mix_g, m_pre_ffn_g, m_w_ffn_in, m_w_ffn_out, m_post_ffn_g, v_meta_tokens, v_pre_mix_g, v_w_in, v_gate_b, v_dw_w, v_dw_b, v_conv_ln_g, v_conv_ln_b, v_w_conv_out, v_w_attn_out, v_w_o, v_post_mix_g, v_pre_ffn_g, v_w_ffn_in, v_w_ffn_out, v_post_ffn_g):
    given = dict(x=x, meta_tokens=meta_tokens, pre_mix_g=pre_mix_g, w_in=w_in, gate_b=gate_b, dw_w=dw_w, dw_b=dw_b, conv_ln_g=conv_ln_g, conv_ln_b=conv_ln_b, w_conv_out=w_conv_out, w_attn_out=w_attn_out, w_o=w_o, post_mix_g=post_mix_g, pre_ffn_g=pre_ffn_g, w_ffn_in=w_ffn_in, w_ffn_out=w_ffn_out, post_ffn_g=post_ffn_g, loss_target=loss_target, m_meta_tokens=m_meta_tokens, m_pre_mix_g=m_pre_mix_g, m_w_in=m_w_in, m_gate_b=m_gate_b, m_dw_w=m_dw_w, m_dw_b=m_dw_b, m_conv_ln_g=m_conv_ln_g, m_conv_ln_b=m_conv_ln_b, m_w_conv_out=m_w_conv_out, m_w_attn_out=m_w_attn_out, m_w_o=m_w_o, m_post_mix_g=m_post_mix_g, m_pre_ffn_g=m_pre_ffn_g, m_w_ffn_in=m_w_ffn_in, m_w_ffn_out=m_w_ffn_out, m_post_ffn_g=m_post_ffn_g, v_meta_tokens=v_meta_tokens, v_pre_mix_g=v_pre_mix_g, v_w_in=v_w_in, v_gate_b=v_gate_b, v_dw_w=v_dw_w, v_dw_b=v_dw_b, v_conv_ln_g=v_conv_ln_g, v_conv_ln_b=v_conv_ln_b, v_w_conv_out=v_w_conv_out, v_w_attn_out=v_w_attn_out, v_w_o=v_w_o, v_post_mix_g=v_post_mix_g, v_pre_ffn_g=v_pre_ffn_g, v_w_ffn_in=v_w_ffn_in, v_w_ffn_out=v_w_ffn_out, v_post_ffn_g=v_post_ffn_g)
    weights = {n: given[n] for n in TWIN_WEIGHTS}
    shared = {n: given[n] for n in SHARED_INPUTS}
    per_example = {n: given[n] for n in ['x']}
    grad_fn = _jax.value_and_grad(_loss, argnums=(0, 1))

    def one_microbatch(ex, loss_target):
        ex = dict(ex)
        diff = ex.pop(TWIN_DIFF_INPUT)
        return grad_fn(weights, diff, {**shared, **ex}, loss_target)

    if N_MICROBATCH == 1:
        loss, (grad_w, grad_x) = one_microbatch(per_example, given["loss_target"])
    else:
        def body(carry, xs):
            loss_sum, grad_sum = carry
            l_k, (gw_k, gx_k) = one_microbatch(xs[0], xs[1])
            with _jax.named_scope("update"):
                return (loss_sum + l_k, _jax.tree.map(_jnp.add, grad_sum, gw_k)), gx_k

        init = (_jnp.zeros((), _jnp.float32), _jax.tree.map(_jnp.zeros_like, weights))
        (loss, grad_w), grad_x = _jax.lax.scan(body, init, (per_example, given["loss_target"]))
    with _jax.named_scope("update"):
        delta_w, new_m, new_v = {}, {}, {}
        for n in TWIN_WEIGHTS:
            delta_w[n], new_m[n], new_v[n] = _adamw(weights[n], grad_w[n], given["m_" + n], given["v_" + n])
    return (loss, grad_x, *[grad_w[n] for n in TWIN_WEIGHTS], *[delta_w[n] for n in TWIN_WEIGHTS],
            *[new_m[n] for n in TWIN_WEIGHTS], *[new_v[n] for n in TWIN_WEIGHTS])
```

```python
import math

import jax
import jax.numpy as jnp
from jax import lax
from jax.experimental import pallas as pl
from jax.experimental.pallas import tpu as pltpu

F32 = jnp.float32
BF16 = jnp.bfloat16

N_META = 16
CONV_WIDTH = 31
CONV_PAD = 32
HEAD_DIM = 64
RMS_EPS = 1e-6
LN_EPS = 1e-5
ATT_BLOCK = 128
LANES = 128
N_CHIPS = 4
N_DEV = 8
MM_ROWS = 544
STAGE_ROWS = 272
WIDE_STAGE_ROWS = 128
VMEM_LIMIT = 56 * 1024 * 1024

ADAM_LR = 0.001
ADAM_B1 = 0.9
ADAM_B2 = 0.999
ADAM_EPS = 1e-08
ADAM_WD = 0.01
ADAM_STEP = 10

MESH = pl.DeviceIdType.MESH
ANY = pl.BlockSpec(memory_space=pl.ANY)


def _params(*sem):
    return pltpu.CompilerParams(dimension_semantics=sem if sem else None, vmem_limit_bytes=VMEM_LIMIT)


def _rms(x, g):
    return x * lax.rsqrt(jnp.mean(x * x, axis=-1, keepdims=True) + RMS_EPS) * g


def f_rms(h, g):
    return (_rms(h, g),)


def f_rms_id(h, g):
    return (_rms(h, g), h)


def f_res_rms(h, m, g):
    return (h + _rms(m, g),)


def f_glu(a, gate):
    return (a * lax.logistic(gate),)


def f_convpost(yc, b, ln_g, ln_b):
    y = yc + b
    mu = jnp.mean(y, axis=-1, keepdims=True)
    xc = y - mu
    var = jnp.mean(xc * xc, axis=-1, keepdims=True)
    yl = xc * lax.rsqrt(var + LN_EPS) * ln_g + ln_b
    return (yl * lax.logistic(yl),)


def f_mix(pc, pa, yc, ya, bc, ba):
    return (lax.logistic(pc + bc) * yc + lax.logistic(pa + ba) * ya,)


def f_swiglu(a, b):
    return (a * lax.logistic(a) * b,)


def _tile(T, target):
    return max(t for t in range(16, target + 1, 16) if T % t == 0)


def _row_map(j):
    return lambda i: (i, j)


def _par_map(j):
    return lambda i: (0, j)


def _rowwise_fwd(name, f, rows, pars, outs, T, tm):
    n_in = len(rows) + len(pars)

    def body(*refs):
        vals = [r[...].astype(F32) for r in refs[:n_in]]
        res = f(*vals)
        for o_ref, o in zip(refs[n_in:], res):
            o_ref[...] = o.astype(o_ref.dtype)

    in_specs = [pl.BlockSpec((tm, w), _row_map(j)) for _, w, j in rows]
    in_specs += [pl.BlockSpec((1, w), _par_map(j)) for _, w, j in pars]
    return pl.pallas_call(
        body, name=name, grid=(T // tm,),
        in_specs=in_specs,
        out_specs=[pl.BlockSpec((tm, w), _row_map(0)) for w, _ in outs],
        out_shape=[jax.ShapeDtypeStruct((T, w), dt) for w, dt in outs],
        compiler_params=_params("parallel"),
    )(*[a for a, _, _ in rows], *[a for a, _, _ in pars])


def _rowwise_bwd(name, f, rows, pars, cots, drow_dtypes, T, tm):
    n_r, n_p, n_c = len(rows), len(pars), len(cots)
    n_in = n_r + n_p + n_c
    keep = [k for k, dt in enumerate(drow_dtypes) if dt is not None]

    def body(*refs):
        rv = [r[...].astype(F32) for r in refs[:n_r]]
        pv = [r[...].astype(F32) for r in refs[n_r:n_r + n_p]]
        cv = [r[...].astype(F32) for r in refs[n_r + n_p:n_in]]
        _, vjp = jax.vjp(f, *rv, *pv)
        g = vjp(tuple(cv))
        drow_refs = refs[n_in:n_in + len(keep)]
        dpar_refs = refs[n_in + len(keep):]
        for r, k in zip(drow_refs, keep):
            r[...] = g[k].astype(r.dtype)

        @pl.when(pl.program_id(0) == 0)
        def _():
            for r in dpar_refs:
                r[...] = jnp.zeros_like(r)

        for r, gp in zip(dpar_refs, g[n_r:]):
            r[...] += gp

    in_specs = [pl.BlockSpec((tm, w), _row_map(j)) for _, w, j in rows]
    in_specs += [pl.BlockSpec((1, w), _par_map(j)) for _, w, j in pars]
    in_specs += [pl.BlockSpec((tm, w), _row_map(j)) for _, w, j in cots]
    out_specs = [pl.BlockSpec((tm, rows[k][1]), _row_map(0)) for k in keep]
    out_specs += [pl.BlockSpec((1, w), _par_map(0)) for _, w, _ in pars]
    out_shape = [jax.ShapeDtypeStruct((T, rows[k][1]), drow_dtypes[k]) for k in keep]
    out_shape += [jax.ShapeDtypeStruct((1, w), F32) for _, w, _ in pars]
    res = pl.pallas_call(
        body, name=name, grid=(T // tm,),
        in_specs=in_specs, out_specs=out_specs, out_shape=out_shape,
        compiler_params=_params("arbitrary"),
    )(*[a for a, _, _ in rows], *[a for a, _, _ in pars], *[a for a, _, _ in cots])
    return res[:len(keep)], res[len(keep):]


NN = (((1,), (0,)), ((), ()))
NT = (((1,), (1,)), ((), ()))
TN = (((0,), (0,)), ((), ()))


def _mm(name, a, b, dims, out_shape, grid, a_spec, b_spec, o_spec, red_axis=None):
    n_red = None if red_axis is None else grid[red_axis]

    def body(a_ref, b_ref, o_ref):
        prod = lax.dot_general(a_ref[...], b_ref[...], dims, preferred_element_type=F32)
        if n_red is None:
            o_ref[...] = prod.astype(o_ref.dtype)
        else:
            @pl.when(pl.program_id(red_axis) == 0)
            def _():
                o_ref[...] = prod

            @pl.when(pl.program_id(red_axis) > 0)
            def _():
                o_ref[...] += prod

    sem = ["parallel"] * len(grid)
    if red_axis is not None:
        sem[red_axis] = "arbitrary"
    return pl.pallas_call(
        body, name=name, grid=grid, in_specs=[a_spec, b_spec], out_specs=o_spec, out_shape=out_shape,
        compiler_params=_params(*sem),
    )(a, b)


def _mm_nn(name, a, w, tm, out_dtype=F32):
    T, K = a.shape
    N = w.shape[1]
    return _mm(name, a, w, NN, jax.ShapeDtypeStruct((T, N), out_dtype), (T // tm,),
               pl.BlockSpec((tm, K), lambda i: (i, 0)), pl.BlockSpec((K, N), lambda i: (0, 0)),
               pl.BlockSpec((tm, N), lambda i: (i, 0)))


def _mm_nt(name, a, w, tm, out_dtype=F32):
    T, N = a.shape
    K = w.shape[0]
    return _mm(name, a, w, NT, jax.ShapeDtypeStruct((T, K), out_dtype), (T // tm,),
               pl.BlockSpec((tm, N), lambda i: (i, 0)), pl.BlockSpec((K, N), lambda i: (0, 0)),
               pl.BlockSpec((tm, K), lambda i: (i, 0)))


def _mm_tn(name, a, b, tm, n_row_blocks):
    T, K = a.shape
    N = b.shape[1]
    kb = K // n_row_blocks
    return _mm(name, a, b, TN, jax.ShapeDtypeStruct((K, N), F32), (n_row_blocks, T // tm),
               pl.BlockSpec((tm, kb), lambda r, t: (t, r)), pl.BlockSpec((tm, N), lambda r, t: (t, 0)),
               pl.BlockSpec((kb, N), lambda r, t: (r, 0)), red_axis=1)


def _mm_nn_cols(name, a, w3, tm):
    T, K = a.shape
    P, _, Ns = w3.shape
    return _mm(name, a, w3, NN, jax.ShapeDtypeStruct((T, P * Ns), F32), (P, T // tm),
               pl.BlockSpec((tm, K), lambda p, i: (i, 0)), pl.BlockSpec((None, K, Ns), lambda p, i: (p, 0, 0)),
               pl.BlockSpec((tm, Ns), lambda p, i: (i, p)))


def _mm_nt_cols(name, a, w3, tm):
    T = a.shape[0]
    P, K, Ns = w3.shape
    return _mm(name, a, w3, NT, jax.ShapeDtypeStruct((T, K), F32), (T // tm, P),
               pl.BlockSpec((tm, Ns), lambda i, p: (i, p)), pl.BlockSpec((None, K, Ns), lambda i, p: (p, 0, 0)),
               pl.BlockSpec((tm, K), lambda i, p: (i, 0)), red_axis=1)


def _mm_tn_cols(name, a, b, tm, P):
    T, K = a.shape
    Ns = b.shape[1] // P
    return _mm(name, a, b, TN, jax.ShapeDtypeStruct((P, K, Ns), F32), (P, T // tm),
               pl.BlockSpec((tm, K), lambda p, t: (t, 0)), pl.BlockSpec((tm, Ns), lambda p, t: (t, p)),
               pl.BlockSpec((None, K, Ns), lambda p, t: (p, 0, 0)), red_axis=1)


def _shift_conv(name, x, w, place, off, T):
    C = x.shape[1]
    tb = ATT_BLOCK
    zero_at = 0 if place else T

    def body(x_ref, w_ref, o_ref, xp_ref):
        xp_ref[pl.ds(zero_at, CONV_PAD), :] = jnp.zeros((CONV_PAD, LANES), F32)
        xp_ref[pl.ds(place, T), :] = x_ref[...]

        def step(t, carry):
            base = pl.multiple_of(t * tb, tb)
            win = xp_ref[pl.ds(base, tb + CONV_PAD), :]
            acc = jnp.zeros((tb, LANES), F32)
            for j in range(CONV_WIDTH):
                acc = acc + win[off + j:off + j + tb, :] * w_ref[pl.ds(j, 1), :]
            o_ref[pl.ds(base, tb), :] = acc
            return carry

        lax.fori_loop(0, T // tb, step, 0)

    return pl.pallas_call(
        body, name=name, grid=(C // LANES,),
        in_specs=[pl.BlockSpec((T, LANES), lambda c: (0, c)), pl.BlockSpec((CONV_WIDTH, LANES), lambda c: (0, c))],
        out_specs=pl.BlockSpec((T, LANES), lambda c: (0, c)),
        out_shape=jax.ShapeDtypeStruct((T, C), F32),
        scratch_shapes=[pltpu.VMEM((T + CONV_PAD, LANES), F32)],
        compiler_params=_params("parallel"),
    )(x, w)


def _conv_dw(name, x, dy, T):
    C = x.shape[1]
    tb = ATT_BLOCK
    off = CONV_PAD - (CONV_WIDTH - 1)

    def body(x_ref, dy_ref, o_ref, xp_ref, acc_ref):
        xp_ref[pl.ds(0, CONV_PAD), :] = jnp.zeros((CONV_PAD, LANES), F32)
        xp_ref[pl.ds(CONV_PAD, T), :] = x_ref[...]
        acc_ref[...] = jnp.zeros_like(acc_ref)

        def step(t, carry):
            base = pl.multiple_of(t * tb, tb)
            win = xp_ref[pl.ds(base, tb + CONV_PAD), :]
            d = dy_ref[pl.ds(base, tb), :]
            for j in range(CONV_WIDTH):
                prod = win[off + j:off + j + tb, :] * d
                acc_ref[j] += jnp.sum(prod.reshape(tb // 8, 8, LANES), axis=0)
            return carry

        lax.fori_loop(0, T // tb, step, 0)
        for j in range(CONV_WIDTH):
            o_ref[pl.ds(j, 1), :] = jnp.sum(acc_ref[j], axis=0, keepdims=True)

    return pl.pallas_call(
        body, name=name, grid=(C // LANES,),
        in_specs=[pl.BlockSpec((T, LANES), lambda c: (0, c)), pl.BlockSpec((T, LANES), lambda c: (0, c))],
        out_specs=pl.BlockSpec((CONV_WIDTH, LANES), lambda c: (0, c)),
        out_shape=jax.ShapeDtypeStruct((CONV_WIDTH, C), F32),
        scratch_shapes=[pltpu.VMEM((T + CONV_PAD, LANES), F32), pltpu.VMEM((CONV_WIDTH, 8, LANES), F32)],
        compiler_params=_params("parallel"),
    )(x, dy)


def _dot(a, b, dims=NN):
    return lax.dot_general(a, b, dims, preferred_element_type=F32)


def _tri_cumsum(x, tri):
    hi = x.astype(BF16)
    lo = (x - hi.astype(F32)).astype(BF16)
    return _dot(hi, tri) + _dot(lo, tri)


def _attn_fwd(q, k, v):
    H, T, dh = q.shape
    B = ATT_BLOCK
    scale = 1.0 / math.sqrt(dh)

    def body(q_ref, k_ref, v_ref, o_ref, rt_ref):
        i = pl.program_id(1)
        qi = q_ref[...]
        row = lax.broadcasted_iota(jnp.int32, (B, B), 0)
        col = lax.broadcasted_iota(jnp.int32, (B, B), 1)
        below = col < row
        tri = (row >= col).astype(BF16)

        def tile(j, c, acc, diagonal):
            sl = pl.ds(pl.multiple_of(j * B, B), B)
            kj = k_ref[sl, :]
            vj = v_ref[sl, :]
            z = _dot(qi, kj, NT) * scale
            sp = jnp.maximum(z, 0.0) + jnp.log(1.0 + jnp.exp(-jnp.abs(z)))
            if diagonal:
                sp = jnp.where(below, sp, 0.0)
            rw = _tri_cumsum(sp, tri)
            a = jnp.exp(z - (rw + c))
            if diagonal:
                a = jnp.where(below, a, 0.0)
            acc = acc + _dot(a.astype(BF16), vj)
            return c + rw[:, 0:1], acc

        c, acc = tile(i, jnp.zeros((B, 1), F32), jnp.zeros((B, dh), F32), True)
        c, acc = lax.fori_loop(0, i, lambda jj, cr: tile(i - 1 - jj, cr[0], cr[1], False), (c, acc))
        o_ref[...] = acc.astype(o_ref.dtype)
        rt_ref[...] = c

    return pl.pallas_call(
        body, name="attn_fwd", grid=(H, T // B),
        in_specs=[pl.BlockSpec((None, B, dh), lambda h, i: (h, i, 0)),
                  pl.BlockSpec((None, T, dh), lambda h, i: (h, 0, 0)),
                  pl.BlockSpec((None, T, dh), lambda h, i: (h, 0, 0))],
        out_specs=[pl.BlockSpec((None, B, dh), lambda h, i: (h, i, 0)),
                   pl.BlockSpec((None, B, 1), lambda h, i: (h, i, 0))],
        out_shape=[jax.ShapeDtypeStruct((H, T, dh), BF16), jax.ShapeDtypeStruct((H, T, 1), F32)],
        compiler_params=_params("parallel", "arbitrary"),
    )(q, k, v)


def _attn_bwd(q, k, v, do, rt):
    H, T, dh = q.shape
    B = ATT_BLOCK
    scale = 1.0 / math.sqrt(dh)

    def body(q_ref, k_ref, v_ref, do_ref, rt_ref, dq_ref, dk_ref, dv_ref):
        i = pl.program_id(1)

        @pl.when(i == 0)
        def _():
            dk_ref[...] = jnp.zeros_like(dk_ref)
            dv_ref[...] = jnp.zeros_like(dv_ref)

        qi = q_ref[...]
        doi = do_ref[...]
        rtot = rt_ref[...]
        row = lax.broadcasted_iota(jnp.int32, (B, B), 0)
        col = lax.broadcasted_iota(jnp.int32, (B, B), 1)
        below = col < row
        tri = (row <= col).astype(BF16)

        def tile(j, pc, gc, dq, diagonal):
            sl = pl.ds(pl.multiple_of(j * B, B), B)
            kj = k_ref[sl, :]
            vj = v_ref[sl, :]
            z = _dot(qi, kj, NT) * scale
            e = jnp.exp(-jnp.abs(z))
            inv = 1.0 / (1.0 + e)
            sp = jnp.maximum(z, 0.0) - jnp.log(inv)
            sg = jnp.where(z >= 0.0, inv, e * inv)
            if diagonal:
                sp = jnp.where(below, sp, 0.0)
            pw = _tri_cumsum(sp, tri)
            a = jnp.exp(z - (rtot - pc - pw + sp))
            if diagonal:
                a = jnp.where(below, a, 0.0)
            g = a * _dot(doi, vj, NT)
            gw = _tri_cumsum(g, tri)
            dz = (g - sg * (gc + gw)) * scale
            if diagonal:
                dz = jnp.where(below, dz, 0.0)
            dzb = dz.astype(BF16)
            dq = dq + _dot(dzb, kj)
            dk_ref[sl, :] += _dot(dzb, qi, TN)
            dv_ref[sl, :] += _dot(a.astype(BF16), doi, TN)
            return pc + pw[:, B - 1:B], gc + gw[:, B - 1:B], dq

        zero = jnp.zeros((B, 1), F32)
        carry = lax.fori_loop(0, i, lambda j, cr: tile(j, cr[0], cr[1], cr[2], False),
                              (zero, zero, jnp.zeros((B, dh), F32)))
        _, _, dq = tile(i, carry[0], carry[1], carry[2], True)
        dq_ref[...] = dq

    blk = pl.BlockSpec((None, B, dh), lambda h, i: (h, i, 0))
    full = pl.BlockSpec((None, T, dh), lambda h, i: (h, 0, 0))
    return pl.pallas_call(
        body, name="attn_bwd", grid=(H, T // B),
        in_specs=[blk, full, full, blk, pl.BlockSpec((None, B, 1), lambda h, i: (h, i, 0))],
        out_specs=[blk, full, full],
        out_shape=[jax.ShapeDtypeStruct((H, T, dh), F32)] * 3,
        compiler_params=_params("parallel", "arbitrary"),
    )(q, k, v, do, rt)


def _loss_head(y, target, tm):
    S, D = y.shape

    def body(y_ref, t_ref, dy_ref, part_ref):
        err = y_ref[...] - t_ref[...]
        dy_ref[...] = err * (1.0 / D)

        @pl.when(pl.program_id(0) == 0)
        def _():
            part_ref[...] = jnp.zeros_like(part_ref)

        part_ref[...] += jnp.sum(err * err, axis=0, keepdims=True)

    spec = pl.BlockSpec((tm, D), lambda i: (i, 0))
    return pl.pallas_call(
        body, name="loss_head", grid=(S // tm,), in_specs=[spec, spec],
        out_specs=[spec, pl.BlockSpec((1, D), lambda i: (0, 0))],
        out_shape=[jax.ShapeDtypeStruct((S, D), F32), jax.ShapeDtypeStruct((1, D), F32)],
        compiler_params=_params("arbitrary"),
    )(y, target)


def _row_tile(R):
    for t in (256, 128, 64, 32, 16, 8):
        if R % t == 0:
            return t
    return R


def _pair_add_bf16(name, g, b1, kind, c_arr):
    P, Rh, C = b1.shape
    tr = _row_tile(Rh)
    nb = Rh // tr

    def body(c_ref, g_ref, b_ref, o_ref):
        o_ref[...] = (g_ref[...] + b_ref[...]).astype(o_ref.dtype)

    if kind == "cols":
        g_spec = pl.BlockSpec((None, tr, C), lambda p, i, c_ref: (p, c_ref[0] * nb + i, 0))
    else:
        g_spec = pl.BlockSpec((tr, C), lambda p, i, c_ref: ((2 * p + c_ref[0]) * nb + i, 0))
    blk = pl.BlockSpec((None, tr, C), lambda p, i, c_ref: (p, i, 0))
    return pl.pallas_call(
        body, name=name,
        grid_spec=pltpu.PrefetchScalarGridSpec(num_scalar_prefetch=1, grid=(P, nb), in_specs=[g_spec, blk], out_specs=blk),
        out_shape=jax.ShapeDtypeStruct((P, Rh, C), BF16),
        compiler_params=_params("parallel", "parallel"),
    )(c_arr, g, b1)


def _sum_slots(name, b):
    P, R, C = b.shape
    tr = _row_tile(R)

    def body(b_ref, o_ref):
        acc = b_ref[0].astype(F32)
        for s in range(1, P):
            acc = acc + b_ref[s].astype(F32)
        o_ref[...] = acc

    return pl.pallas_call(
        body, name=name, grid=(R // tr,),
        in_specs=[pl.BlockSpec((P, tr, C), lambda i: (0, i, 0))],
        out_specs=pl.BlockSpec((tr, C), lambda i: (i, 0)),
        out_shape=jax.ShapeDtypeStruct((R, C), F32),
        compiler_params=_params("parallel"),
    )(b)


def _adamw(name, w, g, m, v):
    R, C = w.shape
    tr = _row_tile(R)
    c1 = 1.0 - ADAM_B1 ** ADAM_STEP
    c2 = 1.0 - ADAM_B2 ** ADAM_STEP

    def body(w_ref, g_ref, m_ref, v_ref, d_ref, nm_ref, nv_ref):
        gg = g_ref[...]
        nm = ADAM_B1 * m_ref[...] + (1.0 - ADAM_B1) * gg
        nv = ADAM_B2 * v_ref[...] + (1.0 - ADAM_B2) * (gg * gg)
        m_hat = nm / c1
        v_hat = nv / c2
        d_ref[...] = -ADAM_LR * (m_hat / (jnp.sqrt(v_hat) + ADAM_EPS) + ADAM_WD * w_ref[...])
        nm_ref[...] = nm
        nv_ref[...] = nv

    spec = pl.BlockSpec((tr, C), lambda i: (i, 0))
    return pl.pallas_call(
        body, name=name, grid=(R // tr,), in_specs=[spec] * 4, out_specs=[spec] * 3,
        out_shape=[jax.ShapeDtypeStruct((R, C), F32)] * 3,
        compiler_params=_params("parallel"),
    )(w, g, m, v)


def _place():
    x, y, c = lax.axis_index("x"), lax.axis_index("y"), lax.axis_index("c")
    other_chips = [(1 - x, y), (x, 1 - y), (1 - x, 1 - y)]
    return x, y, c, other_chips


def _gather_chips(shards):
    n = len(shards)

    def body(*refs):
        ins, outs = refs[:n], refs[n:2 * n]
        ici_send, ici_recv, d2d_send, d2d_recv, local_sem = refs[2 * n:]
        x, y, c, chips = _place()
        me = 2 * x + y
        started = []
        local = []
        for k in range(n):
            rh = ins[k].shape[0] // 2
            mine = pl.ds(c * rh, rh)
            cp = pltpu.make_async_copy(ins[k], outs[k].at[me], local_sem.at[k])
            cp.start()
            local.append(cp)
            for j, (px, py) in enumerate(chips):
                cp = pltpu.make_async_remote_copy(
                    src_ref=ins[k].at[mine], dst_ref=outs[k].at[me, mine],
                    send_sem=ici_send.at[3 * k + j], recv_sem=ici_recv.at[3 * k + j],
                    device_id=(px, py, c), device_id_type=MESH)
                cp.start()
                started.append(cp)
        for k in range(n):
            rh = ins[k].shape[0] // 2
            mine = pl.ds(c * rh, rh)
            for j, (px, py) in enumerate(chips):
                landed = outs[k].at[2 * px + py, mine]
                pltpu.make_async_remote_copy(
                    src_ref=ins[k].at[mine], dst_ref=landed,
                    send_sem=ici_send.at[3 * k + j], recv_sem=ici_recv.at[3 * k + j],
                    device_id=(px, py, c), device_id_type=MESH).wait_recv()
                cp = pltpu.make_async_remote_copy(
                    src_ref=landed, dst_ref=landed,
                    send_sem=d2d_send.at[3 * k + j], recv_sem=d2d_recv.at[3 * k + j],
                    device_id=(x, y, 1 - c), device_id_type=MESH)
                cp.start()
                started.append(cp)
        for k in range(n):
            rh = ins[k].shape[0] // 2
            theirs = pl.ds((1 - c) * rh, rh)
            for j, (px, py) in enumerate(chips):
                landed = outs[k].at[2 * px + py, theirs]
                pltpu.make_async_remote_copy(
                    src_ref=landed, dst_ref=landed,
                    send_sem=d2d_send.at[3 * k + j], recv_sem=d2d_recv.at[3 * k + j],
                    device_id=(x, y, 1 - c), device_id_type=MESH).wait_recv()
        for cp in started:
            cp.wait_send()
        for cp in local:
            cp.wait()

    return pl.pallas_call(
        body, name="gather_weights",
        in_specs=[ANY] * n, out_specs=[ANY] * n,
        out_shape=[jax.ShapeDtypeStruct((N_CHIPS,) + s.shape, s.dtype) for s in shards],
        scratch_shapes=[pltpu.SemaphoreType.DMA((3 * n,))] * 4 + [pltpu.SemaphoreType.DMA((n,))],
        compiler_params=pltpu.CompilerParams(has_side_effects=True),
    )(*shards)


def _half(ref, kind, p, c, rh):
    if kind == "cols":
        return ref.at[p, pl.ds(c * rh, rh)]
    return ref.at[pl.ds((2 * p + c) * rh, rh)]


def _swap_halves(grads, kinds, rhs):
    n = len(grads)

    def body(*refs):
        ins, outs = refs[:n], refs[n:2 * n]
        send_sem, recv_sem = refs[2 * n:]
        x, y, c, _ = _place()
        started = []
        for k in range(n):
            for p in range(N_CHIPS):
                cp = pltpu.make_async_remote_copy(
                    src_ref=_half(ins[k], kinds[k], p, 1 - c, rhs[k]), dst_ref=outs[k].at[p],
                    send_sem=send_sem.at[N_CHIPS * k + p], recv_sem=recv_sem.at[N_CHIPS * k + p],
                    device_id=(x, y, 1 - c), device_id_type=MESH)
                cp.start()
                started.append(cp)
        for cp in started:
            cp.wait()

    out_shape = []
    for g, kind, rh in zip(grads, kinds, rhs):
        out_shape.append(jax.ShapeDtypeStruct((N_CHIPS, rh, g.shape[-1]), g.dtype))
    return pl.pallas_call(
        body, name="grad_swap_halves",
        in_specs=[ANY] * n, out_specs=[ANY] * n, out_shape=out_shape,
        scratch_shapes=[pltpu.SemaphoreType.DMA((N_CHIPS * n,))] * 2,
        compiler_params=pltpu.CompilerParams(has_side_effects=True),
    )(*grads)


def _scatter_partials(parts):
    n = len(parts)

    def body(*refs):
        ins, outs = refs[:n], refs[n:2 * n]
        send_sem, recv_sem, local_sem = refs[2 * n:]
        x, y, c, chips = _place()
        me = 2 * x + y
        started = []
        for k in range(n):
            cp = pltpu.make_async_copy(ins[k].at[me], outs[k].at[me], local_sem.at[k])
            cp.start()
            started.append(cp)
            for j, (px, py) in enumerate(chips):
                cp = pltpu.make_async_remote_copy(
                    src_ref=ins[k].at[2 * px + py], dst_ref=outs[k].at[me],
                    send_sem=send_sem.at[3 * k + j], recv_sem=recv_sem.at[3 * k + j],
                    device_id=(px, py, c), device_id_type=MESH)
                cp.start()
                started.append(cp)
        for k in range(n):
            for j, (px, py) in enumerate(chips):
                landed = outs[k].at[2 * px + py]
                pltpu.make_async_remote_copy(
                    src_ref=landed, dst_ref=landed,
                    send_sem=send_sem.at[3 * k + j], recv_sem=recv_sem.at[3 * k + j],
                    device_id=(px, py, c), device_id_type=MESH).wait_recv()
        for k in range(n):
            started[4 * k].wait()
            for j in range(3):
                started[4 * k + 1 + j].wait_send()

    return pl.pallas_call(
        body, name="grad_scatter_partials",
        in_specs=[ANY] * n, out_specs=[ANY] * n,
        out_shape=[jax.ShapeDtypeStruct(s.shape, s.dtype) for s in parts],
        scratch_shapes=[pltpu.SemaphoreType.DMA((3 * n,))] * 2 + [pltpu.SemaphoreType.DMA((n,))],
        compiler_params=pltpu.CompilerParams(has_side_effects=True),
    )(*parts)


def _join_halves(halves):
    n = len(halves)

    def body(*refs):
        ins, outs = refs[:n], refs[n:2 * n]
        send_sem, recv_sem, local_sem = refs[2 * n:]
        x, y, c, _ = _place()
        started = []
        for k in range(n):
            rh = ins[k].shape[0]
            mine = outs[k].at[pl.ds(c * rh, rh)]
            loc = pltpu.make_async_copy(ins[k], mine, local_sem.at[k])
            loc.start()
            cp = pltpu.make_async_remote_copy(
                src_ref=ins[k], dst_ref=mine, send_sem=send_sem.at[k], recv_sem=recv_sem.at[k],
                device_id=(x, y, 1 - c), device_id_type=MESH)
            cp.start()
            started.append((loc, cp))
        for k in range(n):
            rh = ins[k].shape[0]
            theirs = outs[k].at[pl.ds((1 - c) * rh, rh)]
            pltpu.make_async_remote_copy(
                src_ref=ins[k], dst_ref=theirs, send_sem=send_sem.at[k], recv_sem=recv_sem.at[k],
                device_id=(x, y, 1 - c), device_id_type=MESH).wait_recv()
        for loc, cp in started:
            loc.wait()
            cp.wait_send()

    return pl.pallas_call(
        body, name="grad_join_halves",
        in_specs=[ANY] * n, out_specs=[ANY] * n,
        out_shape=[jax.ShapeDtypeStruct((2 * h.shape[0], h.shape[1]), h.dtype) for h in halves],
        scratch_shapes=[pltpu.SemaphoreType.DMA((n,))] * 3,
        compiler_params=pltpu.CompilerParams(has_side_effects=True),
    )(*halves)


def _gather_all(block):
    def body(in_ref, out_ref, send_sem, recv_sem, local_sem):
        x, y, c, _ = _place()

        def slot(px, py, pc):
            return out_ref.at[4 * px + 2 * py + pc]

        loc = pltpu.make_async_copy(in_ref, slot(x, y, c), local_sem)
        loc.start()
        started = []
        for d in range(1, N_DEV):
            fx, fy, fc = d >> 2, (d >> 1) & 1, d & 1
            cp = pltpu.make_async_remote_copy(
                src_ref=in_ref, dst_ref=slot(x, y, c), send_sem=send_sem.at[d - 1], recv_sem=recv_sem.at[d - 1],
                device_id=(x ^ fx, y ^ fy, c ^ fc), device_id_type=MESH)
            cp.start()
            started.append(cp)
        for d in range(1, N_DEV):
            fx, fy, fc = d >> 2, (d >> 1) & 1, d & 1
            landed = slot(x ^ fx, y ^ fy, c ^ fc)
            pltpu.make_async_remote_copy(
                src_ref=in_ref, dst_ref=landed, send_sem=send_sem.at[d - 1], recv_sem=recv_sem.at[d - 1],
                device_id=(x ^ fx, y ^ fy, c ^ fc), device_id_type=MESH).wait_recv()
        for cp in started:
            cp.wait_send()
        loc.wait()

    return pl.pallas_call(
        body, name="gather_small_grads",
        in_specs=[ANY], out_specs=ANY,
        out_shape=jax.ShapeDtypeStruct((N_DEV,) + block.shape, block.dtype),
        scratch_shapes=[pltpu.SemaphoreType.DMA((N_DEV - 1,))] * 2 + [pltpu.SemaphoreType.DMA(())],
        compiler_params=pltpu.CompilerParams(has_side_effects=True),
    )(block)


def _pack(pieces):
    flat = jnp.concatenate([p.reshape(-1) for p in pieces])
    n = flat.shape[0]
    padded = -(-n // (8 * LANES)) * (8 * LANES)
    return jnp.pad(flat, (0, padded - n)).reshape(-1, LANES)


def _unpack(packed, shapes):
    flat = packed.reshape(-1)
    out, at = [], 0
    for s in shapes:
        n = math.prod(s)
        out.append(flat[at:at + n].reshape(s))
        at += n
    return out


def _heads(a, H):
    T = a.shape[0]
    return a.reshape(T, H, HEAD_DIM).transpose(1, 0, 2)


def _unheads(a):
    H, T, dh = a.shape
    return a.transpose(1, 0, 2).reshape(T, H * dh)


def kernel(x, meta_tokens, pre_mix_g, w_in, gate_b, dw_w, dw_b, conv_ln_g, conv_ln_b, w_conv_out, w_attn_out, w_o, post_mix_g, pre_ffn_g, w_ffn_in, w_ffn_out, post_ffn_g, loss_target, m_meta_tokens, m_pre_mix_g, m_w_in, m_gate_b, m_dw_w, m_dw_b, m_conv_ln_g, m_conv_ln_b, m_w_conv_out, m_w_attn_out, m_w_o, m_post_mix_g, m_pre_ffn_g, m_w_ffn_in, m_w_ffn_out, m_post_ffn_g, v_meta_tokens, v_pre_mix_g, v_w_in, v_gate_b, v_dw_w, v_dw_b, v_conv_ln_g, v_conv_ln_b, v_w_conv_out, v_w_attn_out, v_w_o, v_post_mix_g, v_pre_ffn_g, v_w_ffn_in, v_w_ffn_out, v_post_ffn_g):
    S, D = x.shape[1], x.shape[2]
    L = S + N_META
    T = -(-L // ATT_BLOCK) * ATT_BLOCK
    tm = _tile(T, MM_ROWS)
    ts = _tile(T, STAGE_ROWS)
    tw = _tile(T, WIDE_STAGE_ROWS)
    H = D // HEAD_DIM
    F = w_ffn_out.shape[1] * N_CHIPS
    Dc = D // N_CHIPS
    P = N_CHIPS
    me = 2 * lax.axis_index("x") + lax.axis_index("y")
    c_arr = lax.axis_index("c").astype(jnp.int32).reshape(1)

    dw_w_pad = jnp.pad(dw_w[0], ((0, CONV_PAD - CONV_WIDTH), (0, 0)))
    gathered = _gather_chips([w_in[0].astype(BF16), w_conv_out[0].astype(BF16), w_attn_out[0].astype(BF16),
                              w_o[0].astype(BF16), w_ffn_in[0].astype(BF16), w_ffn_out[0].astype(BF16),
                              meta_tokens, dw_w_pad])
    win3, wfi3 = gathered[0], gathered[4]
    wco, wao, wo = (gathered[k].reshape(D, D) for k in (1, 2, 3))
    wfo = gathered[5].reshape(F, D)
    meta_full = gathered[6].transpose(1, 0, 2).reshape(N_META, D)
    taps = gathered[7].transpose(1, 0, 2).reshape(CONV_PAD, D)[:CONV_WIDTH]

    h0 = jnp.concatenate([meta_full, x[0], jnp.zeros((T - L, D), F32)], axis=0)
    (u1,) = _rowwise_fwd("rms_pre_mix", f_rms, [(h0, D, 0)], [(pre_mix_g, D, 0)], [(D, BF16)], T, ts)
    p = _mm_nn_cols("mm_in", u1, win3, tm)
    (uglu,) = _rowwise_fwd("glu", f_glu, [(p, D, 0), (p, D, 1)], [], [(D, F32)], T, ts)
    yc = _shift_conv("dwconv", uglu, taps, CONV_PAD, CONV_PAD - (CONV_WIDTH - 1), T)
    conv_pars = [(dw_b, D, 0), (conv_ln_g, D, 0), (conv_ln_b, D, 0)]
    (ys,) = _rowwise_fwd("conv_post", f_convpost, [(yc, D, 0)], conv_pars, [(D, BF16)], T, ts)
    y_conv = _mm_nn("mm_conv_out", ys, wco, tm)
    q, k, v = (_heads(p[:, (2 + n) * D:(3 + n) * D].astype(BF16), H) for n in range(3))
    o, rtot = _attn_fwd(q, k, v)
    o2 = _unheads(o)
    y_attn = _mm_nn("mm_attn_out", o2, wao, tm)
    mix_rows = [(p, D, 5), (p, D, 6), (y_conv, D, 0), (y_attn, D, 0)]
    mix_pars = [(gate_b, D, 0), (gate_b, D, 1)]
    (mixin,) = _rowwise_fwd("gate_mix", f_mix, mix_rows, mix_pars, [(D, BF16)], T, ts)
    mix = _mm_nn("mm_o", mixin, wo, tm)
    (h1,) = _rowwise_fwd("res_post_mix", f_res_rms, [(h0, D, 0), (mix, D, 0)], [(post_mix_g, D, 0)], [(D, F32)], T, ts)
    (u2,) = _rowwise_fwd("rms_pre_ffn", f_rms, [(h1, D, 0)], [(pre_ffn_g, D, 0)], [(D, BF16)], T, ts)
    ab = _mm_nn_cols("mm_ffn_in", u2, wfi3, tm)
    (fin,) = _rowwise_fwd("swiglu", f_swiglu, [(ab, F, 0), (ab, F, 1)], [], [(F, BF16)], T, tw)
    f = _mm_nn("mm_ffn_out", fin, wfo, tm)
    (h2,) = _rowwise_fwd("res_post_ffn", f_res_rms, [(h1, D, 0), (f, D, 0)], [(post_ffn_g, D, 0)], [(D, F32)], T, ts)

    dy, part = _loss_head(h2[N_META:L], loss_target[0], _row_tile(S))
    loss = lax.psum(0.5 * jnp.sum(part) / D, ("x", "y", "c"))
    dh2 = jnp.pad(dy, ((N_META, T - L), (0, 0)))

    (df,), (g_post_ffn,) = _rowwise_bwd("res_post_ffn_bwd", f_res_rms, [(h1, D, 0), (f, D, 0)], [(post_ffn_g, D, 0)],
                                        [(dh2, D, 0)], [None, BF16], T, ts)
    dfin = _mm_nt("mm_ffn_out_dx", df, wfo, tm)
    g_wfo = _mm_tn("mm_ffn_out_dw", fin, df, tm, 2)
    (da, db), _ = _rowwise_bwd("swiglu_bwd", f_swiglu, [(ab, F, 0), (ab, F, 1)], [], [(dfin, F, 0)], [BF16, BF16], T, tw)
    dab = jnp.concatenate([da, db], axis=1)
    du2 = _mm_nt_cols("mm_ffn_in_dx", dab, wfi3, tm)
    g_wfi = _mm_tn_cols("mm_ffn_in_dw", u2, dab, tm, P)
    (dh1,), (g_pre_ffn,) = _rowwise_bwd("rms_pre_ffn_bwd", f_rms_id, [(h1, D, 0)], [(pre_ffn_g, D, 0)],
                                        [(du2, D, 0), (dh2, D, 0)], [F32], T, ts)
    (dmix,), (g_post_mix,) = _rowwise_bwd("res_post_mix_bwd", f_res_rms, [(h0, D, 0), (mix, D, 0)], [(post_mix_g, D, 0)],
                                          [(dh1, D, 0)], [None, BF16], T, ts)
    dmixin = _mm_nt("mm_o_dx", dmix, wo, tm)
    g_wo = _mm_tn("mm_o_dw", mixin, dmix, tm, 1)
    (dpc, dpa, dyconv, dyattn), (g_gate_c, g_gate_a) = _rowwise_bwd(
        "gate_mix_bwd", f_mix, mix_rows, mix_pars, [(dmixin, D, 0)], [BF16, BF16, BF16, BF16], T, ts)
    g_wco = _mm_tn("mm_conv_out_dw", ys, dyconv, tm, 1)
    dys = _mm_nt("mm_conv_out_dx", dyconv, wco, tm)
    g_wao = _mm_tn("mm_attn_out_dw", o2, dyattn, tm, 1)
    do2 = _mm_nt("mm_attn_out_dx", dyattn, wao, tm, BF16)
    (dyc,), (g_dw_b, g_ln_g, g_ln_b) = _rowwise_bwd("conv_post_bwd", f_convpost, [(yc, D, 0)], conv_pars,
                                                    [(dys, D, 0)], [F32], T, ts)
    duglu = _shift_conv("dwconv_dx", dyc, taps[::-1], 0, 0, T)
    g_taps = _conv_dw("dwconv_dw", uglu, dyc, T)
    (dp0, dp1), _ = _rowwise_bwd("glu_bwd", f_glu, [(p, D, 0), (p, D, 1)], [], [(duglu, D, 0)], [BF16, BF16], T, ts)
    dq, dk, dv = _attn_bwd(q, k, v, _heads(do2, H), rtot)
    dp = jnp.concatenate([dp0, dp1] + [_unheads(t).astype(BF16) for t in (dq, dk, dv)] + [dpc, dpa], axis=1)
    du1 = _mm_nt_cols("mm_in_dx", dp, win3, tm)
    g_win = _mm_tn_cols("mm_in_dw", u1, dp, tm, P)
    (dh0,), (g_pre_mix,) = _rowwise_bwd("rms_pre_mix_bwd", f_rms_id, [(h0, D, 0)], [(pre_mix_g, D, 0)],
                                        [(du1, D, 0), (dh1, D, 0)], [F32], T, ts)
    grad_x = dh0[N_META:L][None]

    big = [g_win, g_wco, g_wao, g_wo, g_wfi, g_wfo]
    kinds = ["cols", "rows", "rows", "rows", "cols", "rows"]
    rhs = [(g.shape[1] if kind == "cols" else g.shape[0] // P) // 2 for g, kind in zip(big, kinds)]
    from_sibling = _swap_halves(big, kinds, rhs)
    parts = [_pair_add_bf16("grad_pair_add_%d" % n, g, b1, kind, c_arr)
             for n, (g, b1, kind) in enumerate(zip(big, from_sibling, kinds))]
    slots = _scatter_partials(parts)
    halves = [_sum_slots("grad_chip_sum_%d" % n, s) for n, s in enumerate(slots)]
    g_big = _join_halves(halves)

    small_shapes = [(1, D), (1, D), (1, D), (CONV_WIDTH, D), (1, D), (1, D), (1, D), (1, D), (1, D), (1, D), (N_META, D)]
    small = _pack([g_pre_mix, g_gate_c, g_gate_a, g_taps, g_dw_b, g_ln_g, g_ln_b, g_post_mix, g_pre_ffn, g_post_ffn,
                   dh0[:N_META]])
    summed = _sum_slots("small_grad_sum", _gather_all(small))
    (s_pre_mix, s_gate_c, s_gate_a, s_taps, s_dw_b, s_ln_g, s_ln_b, s_post_mix, s_pre_ffn, s_post_ffn,
     s_meta) = _unpack(summed, small_shapes)
    s_gate_b = jnp.concatenate([s_gate_c, s_gate_a], axis=1)
    s_taps = lax.dynamic_slice_in_dim(s_taps, me * Dc, Dc, axis=1)[None]
    s_meta = lax.dynamic_slice_in_dim(s_meta, me * Dc, Dc, axis=1)

    grads = {
        "meta_tokens": s_meta, "pre_mix_g": s_pre_mix, "w_in": g_big[0][None], "gate_b": s_gate_b, "dw_w": s_taps,
        "dw_b": s_dw_b, "conv_ln_g": s_ln_g, "conv_ln_b": s_ln_b, "w_conv_out": g_big[1][None],
        "w_attn_out": g_big[2][None], "w_o": g_big[3][None], "post_mix_g": s_post_mix, "pre_ffn_g": s_pre_ffn,
        "w_ffn_in": g_big[4][None], "w_ffn_out": g_big[5][None], "post_ffn_g": s_post_ffn,
    }
    weights = {
        "meta_tokens": (meta_tokens, m_meta_tokens, v_meta_tokens), "pre_mix_g": (pre_mix_g, m_pre_mix_g, v_pre_mix_g),
        "w_in": (w_in, m_w_in, v_w_in), "gate_b": (gate_b, m_gate_b, v_gate_b), "dw_w": (dw_w, m_dw_w, v_dw_w),
        "dw_b": (dw_b, m_dw_b, v_dw_b), "conv_ln_g": (conv_ln_g, m_conv_ln_g, v_conv_ln_g),
        "conv_ln_b": (conv_ln_b, m_conv_ln_b, v_conv_ln_b), "w_conv_out": (w_conv_out, m_w_conv_out, v_w_conv_out),
        "w_attn_out": (w_attn_out, m_w_attn_out, v_w_attn_out), "w_o": (w_o, m_w_o, v_w_o),
        "post_mix_g": (post_mix_g, m_post_mix_g, v_post_mix_g), "pre_ffn_g": (pre_ffn_g, m_pre_ffn_g, v_pre_ffn_g),
        "w_ffn_in": (w_ffn_in, m_w_ffn_in, v_w_ffn_in), "w_ffn_out": (w_ffn_out, m_w_ffn_out, v_w_ffn_out),
        "post_ffn_g": (post_ffn_g, m_post_ffn_g, v_post_ffn_g),
    }
    names = list(weights)
    big_names = ["w_in", "w_conv_out", "w_attn_out", "w_o", "w_ffn_in", "w_ffn_out"]
    small_names = [n for n in names if n not in big_names]

    delta, new_m, new_v = {}, {}, {}
    for n in big_names:
        w, m, v2 = weights[n]
        d, nm, nv = _adamw("adamw_" + n, w[0], grads[n][0], m[0], v2[0])
        delta[n], new_m[n], new_v[n] = d[None], nm[None], nv[None]
    shapes = [weights[n][0].shape for n in small_names]
    packed = [_pack([weights[n][k] for n in small_names]) for k in range(3)]
    d, nm, nv = _adamw("adamw_small", packed[0], _pack([grads[n] for n in small_names]), packed[1], packed[2])
    for n, dd, mm, vv in zip(small_names, _unpack(d, shapes), _unpack(nm, shapes), _unpack(nv, shapes)):
        delta[n], new_m[n], new_v[n] = dd, mm, vv

    return (loss, grad_x, *[grads[n].reshape(weights[n][0].shape) for n in names], *[delta[n] for n in names],
            *[new_m[n] for n in names], *[new_v[n] for n in names])
```

```python
import math

import jax
import jax.numpy as jnp
from jax import lax
from jax.experimental import pallas as pl
from jax.experimental.pallas import tpu as pltpu

F32 = jnp.float32
BF16 = jnp.bfloat16

N_META = 16
CONV_WIDTH = 31
CONV_PAD = 32
HEAD_DIM = 64
RMS_EPS = 1e-6
LN_EPS = 1e-5
ROW_BLOCK = 128
ATT_BLOCK = 256
ATT_HEADS = 4
LANES = 128
N_CHIPS = 4
N_DEV = 8
MM_ROWS = 544
STAGE_ROWS = 272
WIDE_STAGE_ROWS = 128
VMEM_LIMIT = 56 * 1024 * 1024

ADAM_LR = 0.001
ADAM_B1 = 0.9
ADAM_B2 = 0.999
ADAM_EPS = 1e-08
ADAM_WD = 0.01
ADAM_STEP = 10

MESH = pl.DeviceIdType.MESH
ANY = pl.BlockSpec(memory_space=pl.ANY)


def _params(*sem):
    return pltpu.CompilerParams(dimension_semantics=sem if sem else None, vmem_limit_bytes=VMEM_LIMIT)


def _rms(x, g):
    return x * lax.rsqrt(jnp.mean(x * x, axis=-1, keepdims=True) + RMS_EPS) * g


def f_rms(h, g):
    return (_rms(h, g),)


def f_rms_id(h, g):
    return (_rms(h, g), h)


def f_res_rms(h, m, g):
    return (h + _rms(m, g),)


def f_glu(a, gate):
    return (a * lax.logistic(gate),)


def f_convpost(yc, b, ln_g, ln_b):
    y = yc + b
    mu = jnp.mean(y, axis=-1, keepdims=True)
    xc = y - mu
    var = jnp.mean(xc * xc, axis=-1, keepdims=True)
    yl = xc * lax.rsqrt(var + LN_EPS) * ln_g + ln_b
    return (yl * lax.logistic(yl),)


def f_mix(pc, pa, yc, ya, bc, ba):
    return (lax.logistic(pc + bc) * yc + lax.logistic(pa + ba) * ya,)


def f_swiglu(a, b):
    return (a * lax.logistic(a) * b,)


def _tile(T, target):
    return max(t for t in range(16, target + 1, 16) if T % t == 0)


def _row_map(j):
    return lambda i: (i, j)


def _par_map(j):
    return lambda i: (0, j)


def _rowwise_fwd(name, f, rows, pars, outs, T, tm):
    n_in = len(rows) + len(pars)

    def body(*refs):
        vals = [r[...].astype(F32) for r in refs[:n_in]]
        res = f(*vals)
        for o_ref, o in zip(refs[n_in:], res):
            o_ref[...] = o.astype(o_ref.dtype)

    in_specs = [pl.BlockSpec((tm, w), _row_map(j)) for _, w, j in rows]
    in_specs += [pl.BlockSpec((1, w), _par_map(j)) for _, w, j in pars]
    return pl.pallas_call(
        body, name=name, grid=(T // tm,),
        in_specs=in_specs,
        out_specs=[pl.BlockSpec((tm, w), _row_map(0)) for w, _ in outs],
        out_shape=[jax.ShapeDtypeStruct((T, w), dt) for w, dt in outs],
        compiler_params=_params("parallel"),
    )(*[a for a, _, _ in rows], *[a for a, _, _ in pars])


def _rowwise_bwd(name, f, rows, pars, cots, drow_dtypes, T, tm):
    n_r, n_p, n_c = len(rows), len(pars), len(cots)
    n_in = n_r + n_p + n_c
    keep = [k for k, dt in enumerate(drow_dtypes) if dt is not None]

    def body(*refs):
        rv = [r[...].astype(F32) for r in refs[:n_r]]
        pv = [r[...].astype(F32) for r in refs[n_r:n_r + n_p]]
        cv = [r[...].astype(F32) for r in refs[n_r + n_p:n_in]]
        _, vjp = jax.vjp(f, *rv, *pv)
        g = vjp(tuple(cv))
        drow_refs = refs[n_in:n_in + len(keep)]
        dpar_refs = refs[n_in + len(keep):]
        for r, k in zip(drow_refs, keep):
            r[...] = g[k].astype(r.dtype)

        @pl.when(pl.program_id(0) == 0)
        def _():
            for r in dpar_refs:
                r[...] = jnp.zeros_like(r)

        for r, gp in zip(dpar_refs, g[n_r:]):
            r[...] += gp

    in_specs = [pl.BlockSpec((tm, w), _row_map(j)) for _, w, j in rows]
    in_specs += [pl.BlockSpec((1, w), _par_map(j)) for _, w, j in pars]
    in_specs += [pl.BlockSpec((tm, w), _row_map(j)) for _, w, j in cots]
    out_specs = [pl.BlockSpec((tm, rows[k][1]), _row_map(0)) for k in keep]
    out_specs += [pl.BlockSpec((1, w), _par_map(0)) for _, w, _ in pars]
    out_shape = [jax.ShapeDtypeStruct((T, rows[k][1]), drow_dtypes[k]) for k in keep]
    out_shape += [jax.ShapeDtypeStruct((1, w), F32) for _, w, _ in pars]
    res = pl.pallas_call(
        body, name=name, grid=(T // tm,),
        in_specs=in_specs, out_specs=out_specs, out_shape=out_shape,
        compiler_params=_params("arbitrary"),
    )(*[a for a, _, _ in rows], *[a for a, _, _ in pars], *[a for a, _, _ in cots])
    return res[:len(keep)], res[len(keep):]


NN = (((1,), (0,)), ((), ()))
NT = (((1,), (1,)), ((), ()))
TN = (((0,), (0,)), ((), ()))


def _mm(name, a, b, dims, out_shape, grid, a_spec, b_spec, o_spec, red_axis=None):
    n_red = None if red_axis is None else grid[red_axis]

    def body(a_ref, b_ref, o_ref):
        prod = lax.dot_general(a_ref[...], b_ref[...], dims, preferred_element_type=F32)
        if n_red is None:
            o_ref[...] = prod.astype(o_ref.dtype)
        else:
            @pl.when(pl.program_id(red_axis) == 0)
            def _():
                o_ref[...] = prod

            @pl.when(pl.program_id(red_axis) > 0)
            def _():
                o_ref[...] += prod

    sem = ["parallel"] * len(grid)
    if red_axis is not None:
        sem[red_axis] = "arbitrary"
    return pl.pallas_call(
        body, name=name, grid=grid, in_specs=[a_spec, b_spec], out_specs=o_spec, out_shape=out_shape,
        compiler_params=_params(*sem),
    )(a, b)


def _mm_nn(name, a, w, tm, out_dtype=F32):
    T, K = a.shape
    N = w.shape[1]
    return _mm(name, a, w, NN, jax.ShapeDtypeStruct((T, N), out_dtype), (T // tm,),
               pl.BlockSpec((tm, K), lambda i: (i, 0)), pl.BlockSpec((K, N), lambda i: (0, 0)),
               pl.BlockSpec((tm, N), lambda i: (i, 0)))


def _mm_nt(name, a, w, tm, out_dtype=F32):
    T, N = a.shape
    K = w.shape[0]
    return _mm(name, a, w, NT, jax.ShapeDtypeStruct((T, K), out_dtype), (T // tm,),
               pl.BlockSpec((tm, N), lambda i: (i, 0)), pl.BlockSpec((K, N), lambda i: (0, 0)),
               pl.BlockSpec((tm, K), lambda i: (i, 0)))


def _mm_tn(name, a, b, tm, n_row_blocks):
    T, K = a.shape
    N = b.shape[1]
    kb = K // n_row_blocks
    return _mm(name, a, b, TN, jax.ShapeDtypeStruct((K, N), F32), (n_row_blocks, T // tm),
               pl.BlockSpec((tm, kb), lambda r, t: (t, r)), pl.BlockSpec((tm, N), lambda r, t: (t, 0)),
               pl.BlockSpec((kb, N), lambda r, t: (r, 0)), red_axis=1)


def _mm_nn_cols(name, a, w3, tm):
    T, K = a.shape
    P, _, Ns = w3.shape
    return _mm(name, a, w3, NN, jax.ShapeDtypeStruct((T, P * Ns), F32), (P, T // tm),
               pl.BlockSpec((tm, K), lambda p, i: (i, 0)), pl.BlockSpec((None, K, Ns), lambda p, i: (p, 0, 0)),
               pl.BlockSpec((tm, Ns), lambda p, i: (i, p)))


def _mm_nt_cols(name, a, w3, tm):
    T = a.shape[0]
    P, K, Ns = w3.shape
    return _mm(name, a, w3, NT, jax.ShapeDtypeStruct((T, K), F32), (T // tm, P),
               pl.BlockSpec((tm, Ns), lambda i, p: (i, p)), pl.BlockSpec((None, K, Ns), lambda i, p: (p, 0, 0)),
               pl.BlockSpec((tm, K), lambda i, p: (i, 0)), red_axis=1)


def _mm_tn_cols(name, a, b, tm, P):
    T, K = a.shape
    Ns = b.shape[1] // P
    return _mm(name, a, b, TN, jax.ShapeDtypeStruct((P, K, Ns), F32), (P, T // tm),
               pl.BlockSpec((tm, K), lambda p, t: (t, 0)), pl.BlockSpec((tm, Ns), lambda p, t: (t, p)),
               pl.BlockSpec((None, K, Ns), lambda p, t: (p, 0, 0)), red_axis=1)


def _shift_conv(name, x, w, place, off, T):
    C = x.shape[1]
    tb = ROW_BLOCK
    zero_at = 0 if place else T

    def body(x_ref, w_ref, o_ref, xp_ref):
        xp_ref[pl.ds(zero_at, CONV_PAD), :] = jnp.zeros((CONV_PAD, LANES), F32)
        xp_ref[pl.ds(place, T), :] = x_ref[...]

        def step(t, carry):
            base = pl.multiple_of(t * tb, tb)
            win = xp_ref[pl.ds(base, tb + CONV_PAD), :]
            acc = jnp.zeros((tb, LANES), F32)
            for j in range(CONV_WIDTH):
                acc = acc + win[off + j:off + j + tb, :] * w_ref[pl.ds(j, 1), :]
            o_ref[pl.ds(base, tb), :] = acc
            return carry

        lax.fori_loop(0, T // tb, step, 0)

    return pl.pallas_call(
        body, name=name, grid=(C // LANES,),
        in_specs=[pl.BlockSpec((T, LANES), lambda c: (0, c)), pl.BlockSpec((CONV_WIDTH, LANES), lambda c: (0, c))],
        out_specs=pl.BlockSpec((T, LANES), lambda c: (0, c)),
        out_shape=jax.ShapeDtypeStruct((T, C), F32),
        scratch_shapes=[pltpu.VMEM((T + CONV_PAD, LANES), F32)],
        compiler_params=_params("parallel"),
    )(x, w)


def _conv_dw(name, x, dy, T):
    C = x.shape[1]
    tb = ROW_BLOCK
    off = CONV_PAD - (CONV_WIDTH - 1)

    def body(x_ref, dy_ref, o_ref, xp_ref, acc_ref):
        xp_ref[pl.ds(0, CONV_PAD), :] = jnp.zeros((CONV_PAD, LANES), F32)
        xp_ref[pl.ds(CONV_PAD, T), :] = x_ref[...]
        acc_ref[...] = jnp.zeros_like(acc_ref)

        def step(t, carry):
            base = pl.multiple_of(t * tb, tb)
            win = xp_ref[pl.ds(base, tb + CONV_PAD), :]
            d = dy_ref[pl.ds(base, tb), :]
            for j in range(CONV_WIDTH):
                prod = win[off + j:off + j + tb, :] * d
                acc_ref[j] += jnp.sum(prod.reshape(tb // 8, 8, LANES), axis=0)
            return carry

        lax.fori_loop(0, T // tb, step, 0)
        for j in range(CONV_WIDTH):
            o_ref[pl.ds(j, 1), :] = jnp.sum(acc_ref[j], axis=0, keepdims=True)

    return pl.pallas_call(
        body, name=name, grid=(C // LANES,),
        in_specs=[pl.BlockSpec((T, LANES), lambda c: (0, c)), pl.BlockSpec((T, LANES), lambda c: (0, c))],
        out_specs=pl.BlockSpec((CONV_WIDTH, LANES), lambda c: (0, c)),
        out_shape=jax.ShapeDtypeStruct((CONV_WIDTH, C), F32),
        scratch_shapes=[pltpu.VMEM((T + CONV_PAD, LANES), F32), pltpu.VMEM((CONV_WIDTH, 8, LANES), F32)],
        compiler_params=_params("parallel"),
    )(x, dy)


def _dot(a, b, dims=NN):
    return lax.dot_general(a, b, dims, preferred_element_type=F32)


def _tri_cumsum(x, tri):
    hi = x.astype(BF16)
    lo = (x - hi.astype(F32)).astype(BF16)
    return _dot(hi, tri) + _dot(lo, tri)


def _attn_fwd(q, k, v):
    H, T, dh = q.shape
    B = ATT_BLOCK
    scale = 1.0 / math.sqrt(dh)

    G = ATT_HEADS

    def body(q_ref, k_ref, v_ref, o_ref, rt_ref):
        i = pl.program_id(1)
        row = lax.broadcasted_iota(jnp.int32, (B, B), 0)
        col = lax.broadcasted_iota(jnp.int32, (B, B), 1)
        below = col < row
        tri = (row >= col).astype(BF16)

        def tile(j, carry, diagonal):
            sl = pl.ds(pl.multiple_of(j * B, B), B)
            out = []
            for g in range(G):
                c, acc = carry[g]
                z = _dot(q_ref[g], k_ref[g, sl, :], NT) * scale
                sp = jnp.maximum(z, 0.0) + jnp.log(1.0 + jnp.exp(-jnp.abs(z)))
                if diagonal:
                    sp = jnp.where(below, sp, 0.0)
                rw = _tri_cumsum(sp, tri)
                a = jnp.exp(z - (rw + c))
                if diagonal:
                    a = jnp.where(below, a, 0.0)
                acc = acc + _dot(a.astype(BF16), v_ref[g, sl, :])
                out.append((c + rw[:, 0:1], acc))
            return tuple(out)

        carry = tile(i, tuple((jnp.zeros((B, 1), F32), jnp.zeros((B, dh), F32)) for _ in range(G)), True)
        carry = lax.fori_loop(0, i, lambda jj, cr: tile(i - 1 - jj, cr, False), carry)
        for g in range(G):
            o_ref[g] = carry[g][1].astype(o_ref.dtype)
            rt_ref[g] = carry[g][0]

    return pl.pallas_call(
        body, name="attn_fwd", grid=(H // G, T // B),
        in_specs=[pl.BlockSpec((G, B, dh), lambda h, i: (h, i, 0)),
                  pl.BlockSpec((G, T, dh), lambda h, i: (h, 0, 0)),
                  pl.BlockSpec((G, T, dh), lambda h, i: (h, 0, 0))],
        out_specs=[pl.BlockSpec((G, B, dh), lambda h, i: (h, i, 0)),
                   pl.BlockSpec((G, B, 1), lambda h, i: (h, i, 0))],
        out_shape=[jax.ShapeDtypeStruct((H, T, dh), BF16), jax.ShapeDtypeStruct((H, T, 1), F32)],
        compiler_params=_params("parallel", "arbitrary"),
    )(q, k, v)


def _attn_bwd(q, k, v, do, rt):
    H, T, dh = q.shape
    B = ATT_BLOCK
    scale = 1.0 / math.sqrt(dh)
    G = ATT_HEADS

    def body(q_ref, k_ref, v_ref, do_ref, rt_ref, dq_ref, dk_ref, dv_ref):
        i = pl.program_id(1)

        @pl.when(i == 0)
        def _():
            dk_ref[...] = jnp.zeros_like(dk_ref)
            dv_ref[...] = jnp.zeros_like(dv_ref)

        row = lax.broadcasted_iota(jnp.int32, (B, B), 0)
        col = lax.broadcasted_iota(jnp.int32, (B, B), 1)
        below = col < row
        tri = (row <= col).astype(BF16)

        def tile(j, carry, diagonal):
            sl = pl.ds(pl.multiple_of(j * B, B), B)
            out = []
            for h in range(G):
                pc, gc, dq = carry[h]
                qi, kj, vj, doi = q_ref[h], k_ref[h, sl, :], v_ref[h, sl, :], do_ref[h]
                z = _dot(qi, kj, NT) * scale
                e = jnp.exp(-jnp.abs(z))
                inv = 1.0 / (1.0 + e)
                sp = jnp.maximum(z, 0.0) - jnp.log(inv)
                sg = jnp.where(z >= 0.0, inv, e * inv)
                if diagonal:
                    sp = jnp.where(below, sp, 0.0)
                pw = _tri_cumsum(sp, tri)
                a = jnp.exp(z - (rt_ref[h] - pc - pw + sp))
                if diagonal:
                    a = jnp.where(below, a, 0.0)
                g = a * _dot(doi, vj, NT)
                gw = _tri_cumsum(g, tri)
                dz = (g - sg * (gc + gw)) * scale
                if diagonal:
                    dz = jnp.where(below, dz, 0.0)
                dzb = dz.astype(BF16)
                dq = dq + _dot(dzb, kj)
                dk_ref[h, sl, :] += _dot(dzb, qi, TN)
                dv_ref[h, sl, :] += _dot(a.astype(BF16), doi, TN)
                out.append((pc + pw[:, B - 1:B], gc + gw[:, B - 1:B], dq))
            return tuple(out)

        zero = jnp.zeros((B, 1), F32)
        carry = lax.fori_loop(0, i, lambda j, cr: tile(j, cr, False),
                              tuple((zero, zero, jnp.zeros((B, dh), F32)) for _ in range(G)))
        carry = tile(i, carry, True)
        for h in range(G):
            dq_ref[h] = carry[h][2]

    blk = pl.BlockSpec((G, B, dh), lambda h, i: (h, i, 0))
    full = pl.BlockSpec((G, T, dh), lambda h, i: (h, 0, 0))
    return pl.pallas_call(
        body, name="attn_bwd", grid=(H // G, T // B),
        in_specs=[blk, full, full, blk, pl.BlockSpec((G, B, 1), lambda h, i: (h, i, 0))],
        out_specs=[blk, full, full],
        out_shape=[jax.ShapeDtypeStruct((H, T, dh), F32)] * 3,
        compiler_params=_params("parallel", "arbitrary"),
    )(q, k, v, do, rt)


def _loss_head(y, target, tm):
    S, D = y.shape

    def body(y_ref, t_ref, dy_ref, part_ref):
        err = y_ref[...] - t_ref[...]
        dy_ref[...] = err * (1.0 / D)

        @pl.when(pl.program_id(0) == 0)
        def _():
            part_ref[...] = jnp.zeros_like(part_ref)

        part_ref[...] += jnp.sum(err * err, axis=0, keepdims=True)

    spec = pl.BlockSpec((tm, D), lambda i: (i, 0))
    return pl.pallas_call(
        body, name="loss_head", grid=(S // tm,), in_specs=[spec, spec],
        out_specs=[spec, pl.BlockSpec((1, D), lambda i: (0, 0))],
        out_shape=[jax.ShapeDtypeStruct((S, D), F32), jax.ShapeDtypeStruct((1, D), F32)],
        compiler_params=_params("arbitrary"),
    )(y, target)


def _row_tile(R):
    for t in (256, 128, 64, 32, 16, 8):
        if R % t == 0:
            return t
    return R


def _pair_add_bf16(name, g, b1, kind, c_arr):
    P, Rh, C = b1.shape
    tr = _row_tile(Rh)
    nb = Rh // tr

    def body(c_ref, g_ref, b_ref, o_ref):
        o_ref[...] = (g_ref[...] + b_ref[...]).astype(o_ref.dtype)

    if kind == "cols":
        g_spec = pl.BlockSpec((None, tr, C), lambda p, i, c_ref: (p, c_ref[0] * nb + i, 0))
    else:
        g_spec = pl.BlockSpec((tr, C), lambda p, i, c_ref: ((2 * p + c_ref[0]) * nb + i, 0))
    blk = pl.BlockSpec((None, tr, C), lambda p, i, c_ref: (p, i, 0))
    return pl.pallas_call(
        body, name=name,
        grid_spec=pltpu.PrefetchScalarGridSpec(num_scalar_prefetch=1, grid=(P, nb), in_specs=[g_spec, blk], out_specs=blk),
        out_shape=jax.ShapeDtypeStruct((P, Rh, C), BF16),
        compiler_params=_params("parallel", "parallel"),
    )(c_arr, g, b1)


def _sum_slots(name, b):
    P, R, C = b.shape
    tr = _row_tile(R)

    def body(b_ref, o_ref):
        acc = b_ref[0].astype(F32)
        for s in range(1, P):
            acc = acc + b_ref[s].astype(F32)
        o_ref[...] = acc

    return pl.pallas_call(
        body, name=name, grid=(R // tr,),
        in_specs=[pl.BlockSpec((P, tr, C), lambda i: (0, i, 0))],
        out_specs=pl.BlockSpec((tr, C), lambda i: (i, 0)),
        out_shape=jax.ShapeDtypeStruct((R, C), F32),
        compiler_params=_params("parallel"),
    )(b)


def _adamw(name, w, g, m, v):
    R, C = w.shape
    tr = _row_tile(R)
    c1 = 1.0 - ADAM_B1 ** ADAM_STEP
    c2 = 1.0 - ADAM_B2 ** ADAM_STEP

    def body(w_ref, g_ref, m_ref, v_ref, d_ref, nm_ref, nv_ref):
        gg = g_ref[...]
        nm = ADAM_B1 * m_ref[...] + (1.0 - ADAM_B1) * gg
        nv = ADAM_B2 * v_ref[...] + (1.0 - ADAM_B2) * (gg * gg)
        m_hat = nm / c1
        v_hat = nv / c2
        d_ref[...] = -ADAM_LR * (m_hat / (jnp.sqrt(v_hat) + ADAM_EPS) + ADAM_WD * w_ref[...])
        nm_ref[...] = nm
        nv_ref[...] = nv

    spec = pl.BlockSpec((tr, C), lambda i: (i, 0))
    return pl.pallas_call(
        body, name=name, grid=(R // tr,), in_specs=[spec] * 4, out_specs=[spec] * 3,
        out_shape=[jax.ShapeDtypeStruct((R, C), F32)] * 3,
        compiler_params=_params("parallel"),
    )(w, g, m, v)


def _place():
    x, y, c = lax.axis_index("x"), lax.axis_index("y"), lax.axis_index("c")
    other_chips = [(1 - x, y), (x, 1 - y), (1 - x, 1 - y)]
    return x, y, c, other_chips


def _gather_chips(shards):
    n = len(shards)

    def body(*refs):
        ins, outs = refs[:n], refs[n:2 * n]
        ici_send, ici_recv, d2d_send, d2d_recv, local_sem = refs[2 * n:]
        x, y, c, chips = _place()
        me = 2 * x + y
        started = []
        local = []
        for k in range(n):
            rh = ins[k].shape[0] // 2
            mine = pl.ds(c * rh, rh)
            cp = pltpu.make_async_copy(ins[k], outs[k].at[me], local_sem.at[k])
            cp.start()
            local.append(cp)
            for j, (px, py) in enumerate(chips):
                cp = pltpu.make_async_remote_copy(
                    src_ref=ins[k].at[mine], dst_ref=outs[k].at[me, mine],
                    send_sem=ici_send.at[3 * k + j], recv_sem=ici_recv.at[3 * k + j],
                    device_id=(px, py, c), device_id_type=MESH)
                cp.start()
                started.append(cp)
        for k in range(n):
            rh = ins[k].shape[0] // 2
            mine = pl.ds(c * rh, rh)
            for j, (px, py) in enumerate(chips):
                landed = outs[k].at[2 * px + py, mine]
                pltpu.make_async_remote_copy(
                    src_ref=ins[k].at[mine], dst_ref=landed,
                    send_sem=ici_send.at[3 * k + j], recv_sem=ici_recv.at[3 * k + j],
                    device_id=(px, py, c), device_id_type=MESH).wait_recv()
                cp = pltpu.make_async_remote_copy(
                    src_ref=landed, dst_ref=landed,
                    send_sem=d2d_send.at[3 * k + j], recv_sem=d2d_recv.at[3 * k + j],
                    device_id=(x, y, 1 - c), device_id_type=MESH)
                cp.start()
                started.append(cp)
        for k in range(n):
            rh = ins[k].shape[0] // 2
            theirs = pl.ds((1 - c) * rh, rh)
            for j, (px, py) in enumerate(chips):
                landed = outs[k].at[2 * px + py, theirs]
                pltpu.make_async_remote_copy(
                    src_ref=landed, dst_ref=landed,
                    send_sem=d2d_send.at[3 * k + j], recv_sem=d2d_recv.at[3 * k + j],
                    device_id=(x, y, 1 - c), device_id_type=MESH).wait_recv()
        for cp in started:
            cp.wait_send()
        for cp in local:
            cp.wait()

    return pl.pallas_call(
        body, name="gather_weights",
        in_specs=[ANY] * n, out_specs=[ANY] * n,
        out_shape=[jax.ShapeDtypeStruct((N_CHIPS,) + s.shape, s.dtype) for s in shards],
        scratch_shapes=[pltpu.SemaphoreType.DMA((3 * n,))] * 4 + [pltpu.SemaphoreType.DMA((n,))],
        compiler_params=pltpu.CompilerParams(has_side_effects=True),
    )(*shards)


def _half(ref, kind, p, c, rh):
    if kind == "cols":
        return ref.at[p, pl.ds(c * rh, rh)]
    return ref.at[pl.ds((2 * p + c) * rh, rh)]


def _swap_halves(grads, kinds, rhs):
    n = len(grads)

    def body(*refs):
        ins, outs = refs[:n], refs[n:2 * n]
        send_sem, recv_sem = refs[2 * n:]
        x, y, c, _ = _place()
        started = []
        for k in range(n):
            for p in range(N_CHIPS):
                cp = pltpu.make_async_remote_copy(
                    src_ref=_half(ins[k], kinds[k], p, 1 - c, rhs[k]), dst_ref=outs[k].at[p],
                    send_sem=send_sem.at[N_CHIPS * k + p], recv_sem=recv_sem.at[N_CHIPS * k + p],
                    device_id=(x, y, 1 - c), device_id_type=MESH)
                cp.start()
                started.append(cp)
        for cp in started:
            cp.wait()

    out_shape = []
    for g, kind, rh in zip(grads, kinds, rhs):
        out_shape.append(jax.ShapeDtypeStruct((N_CHIPS, rh, g.shape[-1]), g.dtype))
    return pl.pallas_call(
        body, name="grad_swap_halves",
        in_specs=[ANY] * n, out_specs=[ANY] * n, out_shape=out_shape,
        scratch_shapes=[pltpu.SemaphoreType.DMA((N_CHIPS * n,))] * 2,
        compiler_params=pltpu.CompilerParams(has_side_effects=True),
    )(*grads)


def _scatter_partials(parts):
    n = len(parts)

    def body(*refs):
        ins, outs = refs[:n], refs[n:2 * n]
        send_sem, recv_sem, local_sem = refs[2 * n:]
        x, y, c, chips = _place()
        me = 2 * x + y
        started = []
        for k in range(n):
            cp = pltpu.make_async_copy(ins[k].at[me], outs[k].at[me], local_sem.at[k])
            cp.start()
            started.append(cp)
            for j, (px, py) in enumerate(chips):
                cp = pltpu.make_async_remote_copy(
                    src_ref=ins[k].at[2 * px + py], dst_ref=outs[k].at[me],
                    send_sem=send_sem.at[3 * k + j], recv_sem=recv_sem.at[3 * k + j],
                    device_id=(px, py, c), device_id_type=MESH)
                cp.start()
                started.append(cp)
        for k in range(n):
            for j, (px, py) in enumerate(chips):
                landed = outs[k].at[2 * px + py]
                pltpu.make_async_remote_copy(
                    src_ref=landed, dst_ref=landed,
                    send_sem=send_sem.at[3 * k + j], recv_sem=recv_sem.at[3 * k + j],
                    device_id=(px, py, c), device_id_type=MESH).wait_recv()
        for k in range(n):
            started[4 * k].wait()
            for j in range(3):
                started[4 * k + 1 + j].wait_send()

    return pl.pallas_call(
        body, name="grad_scatter_partials",
        in_specs=[ANY] * n, out_specs=[ANY] * n,
        out_shape=[jax.ShapeDtypeStruct(s.shape, s.dtype) for s in parts],
        scratch_shapes=[pltpu.SemaphoreType.DMA((3 * n,))] * 2 + [pltpu.SemaphoreType.DMA((n,))],
        compiler_params=pltpu.CompilerParams(has_side_effects=True),
    )(*parts)


def _join_halves(halves):
    n = len(halves)

    def body(*refs):
        ins, outs = refs[:n], refs[n:2 * n]
        send_sem, recv_sem, local_sem = refs[2 * n:]
        x, y, c, _ = _place()
        started = []
        for k in range(n):
            rh = ins[k].shape[0]
            mine = outs[k].at[pl.ds(c * rh, rh)]
            loc = pltpu.make_async_copy(ins[k], mine, local_sem.at[k])
            loc.start()
            cp = pltpu.make_async_remote_copy(
                src_ref=ins[k], dst_ref=mine, send_sem=send_sem.at[k], recv_sem=recv_sem.at[k],
                device_id=(x, y, 1 - c), device_id_type=MESH)
            cp.start()
            started.append((loc, cp))
        for k in range(n):
            rh = ins[k].shape[0]
            theirs = outs[k].at[pl.ds((1 - c) * rh, rh)]
            pltpu.make_async_remote_copy(
                src_ref=ins[k], dst_ref=theirs, send_sem=send_sem.at[k], recv_sem=recv_sem.at[k],
                device_id=(x, y, 1 - c), device_id_type=MESH).wait_recv()
        for loc, cp in started:
            loc.wait()
            cp.wait_send()

    return pl.pallas_call(
        body, name="grad_join_halves",
        in_specs=[ANY] * n, out_specs=[ANY] * n,
        out_shape=[jax.ShapeDtypeStruct((2 * h.shape[0], h.shape[1]), h.dtype) for h in halves],
        scratch_shapes=[pltpu.SemaphoreType.DMA((n,))] * 3,
        compiler_params=pltpu.CompilerParams(has_side_effects=True),
    )(*halves)


def _gather_all(block):
    def body(in_ref, out_ref, send_sem, recv_sem, local_sem):
        x, y, c, _ = _place()

        def slot(px, py, pc):
            return out_ref.at[4 * px + 2 * py + pc]

        loc = pltpu.make_async_copy(in_ref, slot(x, y, c), local_sem)
        loc.start()
        started = []
        for d in range(1, N_DEV):
            fx, fy, fc = d >> 2, (d >> 1) & 1, d & 1
            cp = pltpu.make_async_remote_copy(
                src_ref=in_ref, dst_ref=slot(x, y, c), send_sem=send_sem.at[d - 1], recv_sem=recv_sem.at[d - 1],
                device_id=(x ^ fx, y ^ fy, c ^ fc), device_id_type=MESH)
            cp.start()
            started.append(cp)
        for d in range(1, N_DEV):
            fx, fy, fc = d >> 2, (d >> 1) & 1, d & 1
            landed = slot(x ^ fx, y ^ fy, c ^ fc)
            pltpu.make_async_remote_copy(
                src_ref=in_ref, dst_ref=landed, send_sem=send_sem.at[d - 1], recv_sem=recv_sem.at[d - 1],
                device_id=(x ^ fx, y ^ fy, c ^ fc), device_id_type=MESH).wait_recv()
        for cp in started:
            cp.wait_send()
        loc.wait()

    return pl.pallas_call(
        body, name="gather_small_grads",
        in_specs=[ANY], out_specs=ANY,
        out_shape=jax.ShapeDtypeStruct((N_DEV,) + block.shape, block.dtype),
        scratch_shapes=[pltpu.SemaphoreType.DMA((N_DEV - 1,))] * 2 + [pltpu.SemaphoreType.DMA(())],
        compiler_params=pltpu.CompilerParams(has_side_effects=True),
    )(block)


def _pack(pieces):
    flat = jnp.concatenate([p.reshape(-1) for p in pieces])
    n = flat.shape[0]
    padded = -(-n // (8 * LANES)) * (8 * LANES)
    return jnp.pad(flat, (0, padded - n)).reshape(-1, LANES)


def _unpack(packed, shapes):
    flat = packed.reshape(-1)
    out, at = [], 0
    for s in shapes:
        n = math.prod(s)
        out.append(flat[at:at + n].reshape(s))
        at += n
    return out


def _heads(a, H, Ta):
    T = a.shape[0]
    return jnp.pad(a.reshape(T, H, HEAD_DIM).transpose(1, 0, 2), ((0, 0), (0, Ta - T), (0, 0)))


def _unheads(a, T):
    H, _, dh = a.shape
    return a[:, :T].transpose(1, 0, 2).reshape(T, H * dh)


def kernel(x, meta_tokens, pre_mix_g, w_in, gate_b, dw_w, dw_b, conv_ln_g, conv_ln_b, w_conv_out, w_attn_out, w_o, post_mix_g, pre_ffn_g, w_ffn_in, w_ffn_out, post_ffn_g, loss_target, m_meta_tokens, m_pre_mix_g, m_w_in, m_gate_b, m_dw_w, m_dw_b, m_conv_ln_g, m_conv_ln_b, m_w_conv_out, m_w_attn_out, m_w_o, m_post_mix_g, m_pre_ffn_g, m_w_ffn_in, m_w_ffn_out, m_post_ffn_g, v_meta_tokens, v_pre_mix_g, v_w_in, v_gate_b, v_dw_w, v_dw_b, v_conv_ln_g, v_conv_ln_b, v_w_conv_out, v_w_attn_out, v_w_o, v_post_mix_g, v_pre_ffn_g, v_w_ffn_in, v_w_ffn_out, v_post_ffn_g):
    S, D = x.shape[1], x.shape[2]
    L = S + N_META
    T = -(-L // ROW_BLOCK) * ROW_BLOCK
    Ta = -(-L // ATT_BLOCK) * ATT_BLOCK
    tm = _tile(T, MM_ROWS)
    ts = _tile(T, STAGE_ROWS)
    tw = _tile(T, WIDE_STAGE_ROWS)
    H = D // HEAD_DIM
    F = w_ffn_out.shape[1] * N_CHIPS
    Dc = D // N_CHIPS
    P = N_CHIPS
    me = 2 * lax.axis_index("x") + lax.axis_index("y")
    c_arr = lax.axis_index("c").astype(jnp.int32).reshape(1)

    dw_w_pad = jnp.pad(dw_w[0], ((0, CONV_PAD - CONV_WIDTH), (0, 0)))
    gathered = _gather_chips([w_in[0].astype(BF16), w_conv_out[0].astype(BF16), w_attn_out[0].astype(BF16),
                              w_o[0].astype(BF16), w_ffn_in[0].astype(BF16), w_ffn_out[0].astype(BF16),
                              meta_tokens, dw_w_pad])
    win3, wfi3 = gathered[0], gathered[4]
    wco, wao, wo = (gathered[k].reshape(D, D) for k in (1, 2, 3))
    wfo = gathered[5].reshape(F, D)
    meta_full = gathered[6].transpose(1, 0, 2).reshape(N_META, D)
    taps = gathered[7].transpose(1, 0, 2).reshape(CONV_PAD, D)[:CONV_WIDTH]

    h0 = jnp.concatenate([meta_full, x[0], jnp.zeros((T - L, D), F32)], axis=0)
    (u1,) = _rowwise_fwd("rms_pre_mix", f_rms, [(h0, D, 0)], [(pre_mix_g, D, 0)], [(D, BF16)], T, ts)
    p = _mm_nn_cols("mm_in", u1, win3, tm)
    (uglu,) = _rowwise_fwd("glu", f_glu, [(p, D, 0), (p, D, 1)], [], [(D, F32)], T, ts)
    yc = _shift_conv("dwconv", uglu, taps, CONV_PAD, CONV_PAD - (CONV_WIDTH - 1), T)
    conv_pars = [(dw_b, D, 0), (conv_ln_g, D, 0), (conv_ln_b, D, 0)]
    (ys,) = _rowwise_fwd("conv_post", f_convpost, [(yc, D, 0)], conv_pars, [(D, BF16)], T, ts)
    y_conv = _mm_nn("mm_conv_out", ys, wco, tm)
    q, k, v = (_heads(p[:, (2 + n) * D:(3 + n) * D].astype(BF16), H, Ta) for n in range(3))
    o, rtot = _attn_fwd(q, k, v)
    o2 = _unheads(o, T)
    y_attn = _mm_nn("mm_attn_out", o2, wao, tm)
    mix_rows = [(p, D, 5), (p, D, 6), (y_conv, D, 0), (y_attn, D, 0)]
    mix_pars = [(gate_b, D, 0), (gate_b, D, 1)]
    (mixin,) = _rowwise_fwd("gate_mix", f_mix, mix_rows, mix_pars, [(D, BF16)], T, ts)
    mix = _mm_nn("mm_o", mixin, wo, tm)
    (h1,) = _rowwise_fwd("res_post_mix", f_res_rms, [(h0, D, 0), (mix, D, 0)], [(post_mix_g, D, 0)], [(D, F32)], T, ts)
    (u2,) = _rowwise_fwd("rms_pre_ffn", f_rms, [(h1, D, 0)], [(pre_ffn_g, D, 0)], [(D, BF16)], T, ts)
    ab = _mm_nn_cols("mm_ffn_in", u2, wfi3, tm)
    (fin,) = _rowwise_fwd("swiglu", f_swiglu, [(ab, F, 0), (ab, F, 1)], [], [(F, BF16)], T, tw)
    f = _mm_nn("mm_ffn_out", fin, wfo, tm)
    (h2,) = _rowwise_fwd("res_post_ffn", f_res_rms, [(h1, D, 0), (f, D, 0)], [(post_ffn_g, D, 0)], [(D, F32)], T, ts)

    dy, part = _loss_head(h2[N_META:L], loss_target[0], _row_tile(S))
    loss = lax.psum(0.5 * jnp.sum(part) / D, ("x", "y", "c"))
    dh2 = jnp.pad(dy, ((N_META, T - L), (0, 0)))

    (df,), (g_post_ffn,) = _rowwise_bwd("res_post_ffn_bwd", f_res_rms, [(h1, D, 0), (f, D, 0)], [(post_ffn_g, D, 0)],
                                        [(dh2, D, 0)], [None, BF16], T, ts)
    dfin = _mm_nt("mm_ffn_out_dx", df, wfo, tm)
    g_wfo = _mm_tn("mm_ffn_out_dw", fin, df, tm, 2)
    (da, db), _ = _rowwise_bwd("swiglu_bwd", f_swiglu, [(ab, F, 0), (ab, F, 1)], [], [(dfin, F, 0)], [BF16, BF16], T, tw)
    dab = jnp.concatenate([da, db], axis=1)
    du2 = _mm_nt_cols("mm_ffn_in_dx", dab, wfi3, tm)
    g_wfi = _mm_tn_cols("mm_ffn_in_dw", u2, dab, tm, P)
    (dh1,), (g_pre_ffn,) = _rowwise_bwd("rms_pre_ffn_bwd", f_rms_id, [(h1, D, 0)], [(pre_ffn_g, D, 0)],
                                        [(du2, D, 0), (dh2, D, 0)], [F32], T, ts)
    (dmix,), (g_post_mix,) = _rowwise_bwd("res_post_mix_bwd", f_res_rms, [(h0, D, 0), (mix, D, 0)], [(post_mix_g, D, 0)],
                                          [(dh1, D, 0)], [None, BF16], T, ts)
    dmixin = _mm_nt("mm_o_dx", dmix, wo, tm)
    g_wo = _mm_tn("mm_o_dw", mixin, dmix, tm, 1)
    (dpc, dpa, dyconv, dyattn), (g_gate_c, g_gate_a) = _rowwise_bwd(
        "gate_mix_bwd", f_mix, mix_rows, mix_pars, [(dmixin, D, 0)], [BF16, BF16, BF16, BF16], T, ts)
    g_wco = _mm_tn("mm_conv_out_dw", ys, dyconv, tm, 1)
    dys = _mm_nt("mm_conv_out_dx", dyconv, wco, tm)
    g_wao = _mm_tn("mm_attn_out_dw", o2, dyattn, tm, 1)
    do2 = _mm_nt("mm_attn_out_dx", dyattn, wao, tm, BF16)
    (dyc,), (g_dw_b, g_ln_g, g_ln_b) = _rowwise_bwd("conv_post_bwd", f_convpost, [(yc, D, 0)], conv_pars,
                                                    [(dys, D, 0)], [F32], T, ts)
    duglu = _shift_conv("dwconv_dx", dyc, taps[::-1], 0, 0, T)
    g_taps = _conv_dw("dwconv_dw", uglu, dyc, T)
    (dp0, dp1), _ = _rowwise_bwd("glu_bwd", f_glu, [(p, D, 0), (p, D, 1)], [], [(duglu, D, 0)], [BF16, BF16], T, ts)
    dq, dk, dv = _attn_bwd(q, k, v, _heads(do2, H, Ta), rtot)
    dp = jnp.concatenate([dp0, dp1] + [_unheads(t, T).astype(BF16) for t in (dq, dk, dv)] + [dpc, dpa], axis=1)
    du1 = _mm_nt_cols("mm_in_dx", dp, win3, tm)
    g_win = _mm_tn_cols("mm_in_dw", u1, dp, tm, P)
    (dh0,), (g_pre_mix,) = _rowwise_bwd("rms_pre_mix_bwd", f_rms_id, [(h0, D, 0)], [(pre_mix_g, D, 0)],
                                        [(du1, D, 0), (dh1, D, 0)], [F32], T, ts)
    grad_x = dh0[N_META:L][None]

    big = [g_win, g_wco, g_wao, g_wo, g_wfi, g_wfo]
    kinds = ["cols", "rows", "rows", "rows", "cols", "rows"]
    rhs = [(g.shape[1] if kind == "cols" else g.shape[0] // P) // 2 for g, kind in zip(big, kinds)]
    from_sibling = _swap_halves(big, kinds, rhs)
    parts = [_pair_add_bf16("grad_pair_add_%d" % n, g, b1, kind, c_arr)
             for n, (g, b1, kind) in enumerate(zip(big, from_sibling, kinds))]
    slots = _scatter_partials(parts)
    halves = [_sum_slots("grad_chip_sum_%d" % n, s) for n, s in enumerate(slots)]
    g_big = _join_halves(halves)

    small_shapes = [(1, D), (1, D), (1, D), (CONV_WIDTH, D), (1, D), (1, D), (1, D), (1, D), (1, D), (1, D), (N_META, D)]
    small = _pack([g_pre_mix, g_gate_c, g_gate_a, g_taps, g_dw_b, g_ln_g, g_ln_b, g_post_mix, g_pre_ffn, g_post_ffn,
                   dh0[:N_META]])
    summed = _sum_slots("small_grad_sum", _gather_all(small))
    (s_pre_mix, s_gate_c, s_gate_a, s_taps, s_dw_b, s_ln_g, s_ln_b, s_post_mix, s_pre_ffn, s_post_ffn,
     s_meta) = _unpack(summed, small_shapes)
    s_gate_b = jnp.concatenate([s_gate_c, s_gate_a], axis=1)
    s_taps = lax.dynamic_slice_in_dim(s_taps, me * Dc, Dc, axis=1)[None]
    s_meta = lax.dynamic_slice_in_dim(s_meta, me * Dc, Dc, axis=1)

    grads = {
        "meta_tokens": s_meta, "pre_mix_g": s_pre_mix, "w_in": g_big[0][None], "gate_b": s_gate_b, "dw_w": s_taps,
        "dw_b": s_dw_b, "conv_ln_g": s_ln_g, "conv_ln_b": s_ln_b, "w_conv_out": g_big[1][None],
        "w_attn_out": g_big[2][None], "w_o": g_big[3][None], "post_mix_g": s_post_mix, "pre_ffn_g": s_pre_ffn,
        "w_ffn_in": g_big[4][None], "w_ffn_out": g_big[5][None], "post_ffn_g": s_post_ffn,
    }
    weights = {
        "meta_tokens": (meta_tokens, m_meta_tokens, v_meta_tokens), "pre_mix_g": (pre_mix_g, m_pre_mix_g, v_pre_mix_g),
        "w_in": (w_in, m_w_in, v_w_in), "gate_b": (gate_b, m_gate_b, v_gate_b), "dw_w": (dw_w, m_dw_w, v_dw_w),
        "dw_b": (dw_b, m_dw_b, v_dw_b), "conv_ln_g": (conv_ln_g, m_conv_ln_g, v_conv_ln_g),
        "conv_ln_b": (conv_ln_b, m_conv_ln_b, v_conv_ln_b), "w_conv_out": (w_conv_out, m_w_conv_out, v_w_conv_out),
        "w_attn_out": (w_attn_out, m_w_attn_out, v_w_attn_out), "w_o": (w_o, m_w_o, v_w_o),
        "post_mix_g": (post_mix_g, m_post_mix_g, v_post_mix_g), "pre_ffn_g": (pre_ffn_g, m_pre_ffn_g, v_pre_ffn_g),
        "w_ffn_in": (w_ffn_in, m_w_ffn_in, v_w_ffn_in), "w_ffn_out": (w_ffn_out, m_w_ffn_out, v_w_ffn_out),
        "post_ffn_g": (post_ffn_g, m_post_ffn_g, v_post_ffn_g),
    }
    names = list(weights)
    big_names = ["w_in", "w_conv_out", "w_attn_out", "w_o", "w_ffn_in", "w_ffn_out"]
    small_names = [n for n in names if n not in big_names]

    delta, new_m, new_v = {}, {}, {}
    for n in big_names:
        w, m, v2 = weights[n]
        d, nm, nv = _adamw("adamw_" + n, w[0], grads[n][0], m[0], v2[0])
        delta[n], new_m[n], new_v[n] = d[None], nm[None], nv[None]
    shapes = [weights[n][0].shape for n in small_names]
    packed = [_pack([weights[n][k] for n in small_names]) for k in range(3)]
    d, nm, nv = _adamw("adamw_small", packed[0], _pack([grads[n] for n in small_names]), packed[1], packed[2])
    for n, dd, mm, vv in zip(small_names, _unpack(d, shapes), _unpack(nm, shapes), _unpack(nv, shapes)):
        delta[n], new_m[n], new_v[n] = dd, mm, vv

    return (loss, grad_x, *[grads[n].reshape(weights[n][0].shape) for n in names], *[delta[n] for n in names],
            *[new_m[n] for n in names], *[new_v[n] for n in names])
```

```python
import math

import jax
import jax.numpy as jnp
from jax import lax
from jax.experimental import pallas as pl
from jax.experimental.pallas import tpu as pltpu

F32 = jnp.float32
BF16 = jnp.bfloat16

N_META = 16
CONV_WIDTH = 31
CONV_PAD = 32
HEAD_DIM = 64
RMS_EPS = 1e-6
LN_EPS = 1e-5
ROW_BLOCK = 128
ATT_BLOCK = 256
ATT_HEADS = 4
LANES = 128
N_CHIPS = 4
N_DEV = 8
MM_ROWS = 544
STAGE_ROWS = 272
WIDE_STAGE_ROWS = 128
VMEM_LIMIT = 56 * 1024 * 1024

ADAM_LR = 0.001
ADAM_B1 = 0.9
ADAM_B2 = 0.999
ADAM_EPS = 1e-08
ADAM_WD = 0.01
ADAM_STEP = 10

MESH = pl.DeviceIdType.MESH
ANY = pl.BlockSpec(memory_space=pl.ANY)


def _params(*sem):
    return pltpu.CompilerParams(dimension_semantics=sem if sem else None, vmem_limit_bytes=VMEM_LIMIT)


def _rms(x, g):
    return x * lax.rsqrt(jnp.mean(x * x, axis=-1, keepdims=True) + RMS_EPS) * g


def f_rms(h, g):
    return (_rms(h, g),)


def f_rms_id(h, g):
    return (_rms(h, g), h)


def f_res_rms(h, m, g):
    return (h + _rms(m, g),)


def f_glu(a, gate):
    return (a * lax.logistic(gate),)


def f_convpost(yc, b, ln_g, ln_b):
    y = yc + b
    mu = jnp.mean(y, axis=-1, keepdims=True)
    xc = y - mu
    var = jnp.mean(xc * xc, axis=-1, keepdims=True)
    yl = xc * lax.rsqrt(var + LN_EPS) * ln_g + ln_b
    return (yl * lax.logistic(yl),)


def f_mix(pc, pa, yc, ya, bc, ba):
    return (lax.logistic(pc + bc) * yc + lax.logistic(pa + ba) * ya,)


def f_swiglu(a, b):
    return (a * lax.logistic(a) * b,)


def _tile(T, target):
    return max(t for t in range(16, target + 1, 16) if T % t == 0)


def _row_map(j):
    return lambda i: (i, j)


def _par_map(j):
    return lambda i: (0, j)


def _rowwise_fwd(name, f, rows, pars, outs, T, tm):
    n_in = len(rows) + len(pars)

    def body(*refs):
        vals = [r[...].astype(F32) for r in refs[:n_in]]
        res = f(*vals)
        for o_ref, o in zip(refs[n_in:], res):
            o_ref[...] = o.astype(o_ref.dtype)

    in_specs = [pl.BlockSpec((tm, w), _row_map(j)) for _, w, j in rows]
    in_specs += [pl.BlockSpec((1, w), _par_map(j)) for _, w, j in pars]
    return pl.pallas_call(
        body, name=name, grid=(T // tm,),
        in_specs=in_specs,
        out_specs=[pl.BlockSpec((tm, w), _row_map(0)) for w, _ in outs],
        out_shape=[jax.ShapeDtypeStruct((T, w), dt) for w, dt in outs],
        compiler_params=_params("parallel"),
    )(*[a for a, _, _ in rows], *[a for a, _, _ in pars])


def _rowwise_bwd(name, f, rows, pars, cots, drow_dtypes, T, tm):
    n_r, n_p, n_c = len(rows), len(pars), len(cots)
    n_in = n_r + n_p + n_c
    keep = [k for k, dt in enumerate(drow_dtypes) if dt is not None]

    def body(*refs):
        rv = [r[...].astype(F32) for r in refs[:n_r]]
        pv = [r[...].astype(F32) for r in refs[n_r:n_r + n_p]]
        cv = [r[...].astype(F32) for r in refs[n_r + n_p:n_in]]
        _, vjp = jax.vjp(f, *rv, *pv)
        g = vjp(tuple(cv))
        drow_refs = refs[n_in:n_in + len(keep)]
        dpar_refs = refs[n_in + len(keep):]
        for r, k in zip(drow_refs, keep):
            r[...] = g[k].astype(r.dtype)

        @pl.when(pl.program_id(0) == 0)
        def _():
            for r in dpar_refs:
                r[...] = jnp.zeros_like(r)

        for r, gp in zip(dpar_refs, g[n_r:]):
            r[...] += gp

    in_specs = [pl.BlockSpec((tm, w), _row_map(j)) for _, w, j in rows]
    in_specs += [pl.BlockSpec((1, w), _par_map(j)) for _, w, j in pars]
    in_specs += [pl.BlockSpec((tm, w), _row_map(j)) for _, w, j in cots]
    out_specs = [pl.BlockSpec((tm, rows[k][1]), _row_map(0)) for k in keep]
    out_specs += [pl.BlockSpec((1, w), _par_map(0)) for _, w, _ in pars]
    out_shape = [jax.ShapeDtypeStruct((T, rows[k][1]), drow_dtypes[k]) for k in keep]
    out_shape += [jax.ShapeDtypeStruct((1, w), F32) for _, w, _ in pars]
    res = pl.pallas_call(
        body, name=name, grid=(T // tm,),
        in_specs=in_specs, out_specs=out_specs, out_shape=out_shape,
        compiler_params=_params("arbitrary"),
    )(*[a for a, _, _ in rows], *[a for a, _, _ in pars], *[a for a, _, _ in cots])
    return res[:len(keep)], res[len(keep):]


NN = (((1,), (0,)), ((), ()))
NT = (((1,), (1,)), ((), ()))
TN = (((0,), (0,)), ((), ()))


def _mm(name, a, b, dims, out_shape, grid, a_spec, b_spec, o_spec, red_axis=None):
    n_red = None if red_axis is None else grid[red_axis]

    def body(a_ref, b_ref, o_ref):
        prod = lax.dot_general(a_ref[...], b_ref[...], dims, preferred_element_type=F32)
        if n_red is None:
            o_ref[...] = prod.astype(o_ref.dtype)
        else:
            @pl.when(pl.program_id(red_axis) == 0)
            def _():
                o_ref[...] = prod

            @pl.when(pl.program_id(red_axis) > 0)
            def _():
                o_ref[...] += prod

    sem = ["parallel"] * len(grid)
    if red_axis is not None:
        sem[red_axis] = "arbitrary"
    return pl.pallas_call(
        body, name=name, grid=grid, in_specs=[a_spec, b_spec], out_specs=o_spec, out_shape=out_shape,
        compiler_params=_params(*sem),
    )(a, b)


def _mm_nn(name, a, w, tm, out_dtype=F32):
    T, K = a.shape
    N = w.shape[1]
    return _mm(name, a, w, NN, jax.ShapeDtypeStruct((T, N), out_dtype), (T // tm,),
               pl.BlockSpec((tm, K), lambda i: (i, 0)), pl.BlockSpec((K, N), lambda i: (0, 0)),
               pl.BlockSpec((tm, N), lambda i: (i, 0)))


def _mm_nt(name, a, w, tm, out_dtype=F32):
    T, N = a.shape
    K = w.shape[0]
    return _mm(name, a, w, NT, jax.ShapeDtypeStruct((T, K), out_dtype), (T // tm,),
               pl.BlockSpec((tm, N), lambda i: (i, 0)), pl.BlockSpec((K, N), lambda i: (0, 0)),
               pl.BlockSpec((tm, K), lambda i: (i, 0)))


def _mm_tn(name, a, b, tm, n_row_blocks):
    T, K = a.shape
    N = b.shape[1]
    kb = K // n_row_blocks
    return _mm(name, a, b, TN, jax.ShapeDtypeStruct((K, N), F32), (n_row_blocks, T // tm),
               pl.BlockSpec((tm, kb), lambda r, t: (t, r)), pl.BlockSpec((tm, N), lambda r, t: (t, 0)),
               pl.BlockSpec((kb, N), lambda r, t: (r, 0)), red_axis=1)


def _mm_nn_cols(name, a, w3, tm):
    T, K = a.shape
    P, _, Ns = w3.shape
    return _mm(name, a, w3, NN, jax.ShapeDtypeStruct((T, P * Ns), F32), (P, T // tm),
               pl.BlockSpec((tm, K), lambda p, i: (i, 0)), pl.BlockSpec((None, K, Ns), lambda p, i: (p, 0, 0)),
               pl.BlockSpec((tm, Ns), lambda p, i: (i, p)))


def _mm_nt_cols(name, a, w3, tm):
    T = a.shape[0]
    P, K, Ns = w3.shape
    return _mm(name, a, w3, NT, jax.ShapeDtypeStruct((T, K), F32), (T // tm, P),
               pl.BlockSpec((tm, Ns), lambda i, p: (i, p)), pl.BlockSpec((None, K, Ns), lambda i, p: (p, 0, 0)),
               pl.BlockSpec((tm, K), lambda i, p: (i, 0)), red_axis=1)


def _mm_tn_cols(name, a, b, tm, P):
    T, K = a.shape
    Ns = b.shape[1] // P
    return _mm(name, a, b, TN, jax.ShapeDtypeStruct((P, K, Ns), F32), (P, T // tm),
               pl.BlockSpec((tm, K), lambda p, t: (t, 0)), pl.BlockSpec((tm, Ns), lambda p, t: (t, p)),
               pl.BlockSpec((None, K, Ns), lambda p, t: (p, 0, 0)), red_axis=1)


def _shift_conv(name, x, w, place, off, T):
    C = x.shape[1]
    tb = ROW_BLOCK
    zero_at = 0 if place else T

    def body(x_ref, w_ref, o_ref, xp_ref):
        xp_ref[pl.ds(zero_at, CONV_PAD), :] = jnp.zeros((CONV_PAD, LANES), F32)
        xp_ref[pl.ds(place, T), :] = x_ref[...]

        def step(t, carry):
            base = pl.multiple_of(t * tb, tb)
            win = xp_ref[pl.ds(base, tb + CONV_PAD), :]
            acc = jnp.zeros((tb, LANES), F32)
            for j in range(CONV_WIDTH):
                acc = acc + win[off + j:off + j + tb, :] * w_ref[pl.ds(j, 1), :]
            o_ref[pl.ds(base, tb), :] = acc
            return carry

        lax.fori_loop(0, T // tb, step, 0)

    return pl.pallas_call(
        body, name=name, grid=(C // LANES,),
        in_specs=[pl.BlockSpec((T, LANES), lambda c: (0, c)), pl.BlockSpec((CONV_WIDTH, LANES), lambda c: (0, c))],
        out_specs=pl.BlockSpec((T, LANES), lambda c: (0, c)),
        out_shape=jax.ShapeDtypeStruct((T, C), F32),
        scratch_shapes=[pltpu.VMEM((T + CONV_PAD, LANES), F32)],
        compiler_params=_params("parallel"),
    )(x, w)


def _conv_dw(name, x, dy, T):
    C = x.shape[1]
    tb = ROW_BLOCK
    off = CONV_PAD - (CONV_WIDTH - 1)

    def body(x_ref, dy_ref, o_ref, xp_ref, acc_ref):
        xp_ref[pl.ds(0, CONV_PAD), :] = jnp.zeros((CONV_PAD, LANES), F32)
        xp_ref[pl.ds(CONV_PAD, T), :] = x_ref[...]
        acc_ref[...] = jnp.zeros_like(acc_ref)

        def step(t, carry):
            base = pl.multiple_of(t * tb, tb)
            win = xp_ref[pl.ds(base, tb + CONV_PAD), :]
            d = dy_ref[pl.ds(base, tb), :]
            for j in range(CONV_WIDTH):
                prod = win[off + j:off + j + tb, :] * d
                acc_ref[j] += jnp.sum(prod.reshape(tb // 8, 8, LANES), axis=0)
            return carry

        lax.fori_loop(0, T // tb, step, 0)
        for j in range(CONV_WIDTH):
            o_ref[pl.ds(j, 1), :] = jnp.sum(acc_ref[j], axis=0, keepdims=True)

    return pl.pallas_call(
        body, name=name, grid=(C // LANES,),
        in_specs=[pl.BlockSpec((T, LANES), lambda c: (0, c)), pl.BlockSpec((T, LANES), lambda c: (0, c))],
        out_specs=pl.BlockSpec((CONV_WIDTH, LANES), lambda c: (0, c)),
        out_shape=jax.ShapeDtypeStruct((CONV_WIDTH, C), F32),
        scratch_shapes=[pltpu.VMEM((T + CONV_PAD, LANES), F32), pltpu.VMEM((CONV_WIDTH, 8, LANES), F32)],
        compiler_params=_params("parallel"),
    )(x, dy)


def _dot(a, b, dims=NN):
    return lax.dot_general(a, b, dims, preferred_element_type=F32)


def _tri_cumsum(x, tri):
    hi = x.astype(BF16)
    lo = (x - hi.astype(F32)).astype(BF16)
    return _dot(hi, tri) + _dot(lo, tri)


def _attn_fwd(q, k, v):
    H, T, dh = q.shape
    B = ATT_BLOCK
    scale = 1.0 / math.sqrt(dh)

    G = ATT_HEADS

    def body(q_ref, k_ref, v_ref, o_ref, rt_ref):
        i = pl.program_id(1)
        row = lax.broadcasted_iota(jnp.int32, (B, B), 0)
        col = lax.broadcasted_iota(jnp.int32, (B, B), 1)
        below = col < row
        tri = (row >= col).astype(BF16)

        def tile(j, carry, diagonal):
            sl = pl.ds(pl.multiple_of(j * B, B), B)
            out = []
            for g in range(G):
                c, acc = carry[g]
                z = _dot(q_ref[g], k_ref[g, sl, :], NT) * scale
                sp = jnp.maximum(z, 0.0) + jnp.log(1.0 + jnp.exp(-jnp.abs(z)))
                if diagonal:
                    sp = jnp.where(below, sp, 0.0)
                rw = _tri_cumsum(sp, tri)
                a = jnp.exp(z - (rw + c))
                if diagonal:
                    a = jnp.where(below, a, 0.0)
                acc = acc + _dot(a.astype(BF16), v_ref[g, sl, :])
                out.append((c + rw[:, 0:1], acc))
            return tuple(out)

        carry = tile(i, tuple((jnp.zeros((B, 1), F32), jnp.zeros((B, dh), F32)) for _ in range(G)), True)
        carry = lax.fori_loop(0, i, lambda jj, cr: tile(i - 1 - jj, cr, False), carry)
        for g in range(G):
            o_ref[g] = carry[g][1].astype(o_ref.dtype)
            rt_ref[g] = carry[g][0]

    return pl.pallas_call(
        body, name="attn_fwd", grid=(H // G, T // B),
        in_specs=[pl.BlockSpec((G, B, dh), lambda h, i: (h, i, 0)),
                  pl.BlockSpec((G, T, dh), lambda h, i: (h, 0, 0)),
                  pl.BlockSpec((G, T, dh), lambda h, i: (h, 0, 0))],
        out_specs=[pl.BlockSpec((G, B, dh), lambda h, i: (h, i, 0)),
                   pl.BlockSpec((G, B, 1), lambda h, i: (h, i, 0))],
        out_shape=[jax.ShapeDtypeStruct((H, T, dh), BF16), jax.ShapeDtypeStruct((H, T, 1), F32)],
        compiler_params=_params("parallel", "arbitrary"),
    )(q, k, v)


def _attn_bwd(q, k, v, do, rt):
    H, T, dh = q.shape
    B = ATT_BLOCK
    scale = 1.0 / math.sqrt(dh)
    G = ATT_HEADS

    def body(q_ref, k_ref, v_ref, do_ref, rt_ref, dq_ref, dk_ref, dv_ref):
        i = pl.program_id(1)

        @pl.when(i == 0)
        def _():
            dk_ref[...] = jnp.zeros_like(dk_ref)
            dv_ref[...] = jnp.zeros_like(dv_ref)

        row = lax.broadcasted_iota(jnp.int32, (B, B), 0)
        col = lax.broadcasted_iota(jnp.int32, (B, B), 1)
        below = col < row
        tri = (row <= col).astype(BF16)

        def tile(j, carry, diagonal):
            sl = pl.ds(pl.multiple_of(j * B, B), B)
            out = []
            for h in range(G):
                pc, gc, dq = carry[h]
                qi, kj, vj, doi = q_ref[h], k_ref[h, sl, :], v_ref[h, sl, :], do_ref[h]
                z = _dot(qi, kj, NT) * scale
                e = jnp.exp(-jnp.abs(z))
                inv = 1.0 / (1.0 + e)
                sp = jnp.maximum(z, 0.0) - jnp.log(inv)
                sg = jnp.where(z >= 0.0, inv, e * inv)
                if diagonal:
                    sp = jnp.where(below, sp, 0.0)
                pw = _tri_cumsum(sp, tri)
                a = jnp.exp(z - (rt_ref[h] - pc - pw + sp))
                if diagonal:
                    a = jnp.where(below, a, 0.0)
                g = a * _dot(doi, vj, NT)
                gw = _tri_cumsum(g, tri)
                dz = (g - sg * (gc + gw)) * scale
                if diagonal:
                    dz = jnp.where(below, dz, 0.0)
                dzb = dz.astype(BF16)
                dq = dq + _dot(dzb, kj)
                dk_ref[h, sl, :] += _dot(dzb, qi, TN)
                dv_ref[h, sl, :] += _dot(a.astype(BF16), doi, TN)
                out.append((pc + pw[:, B - 1:B], gc + gw[:, B - 1:B], dq))
            return tuple(out)

        zero = jnp.zeros((B, 1), F32)
        carry = lax.fori_loop(0, i, lambda j, cr: tile(j, cr, False),
                              tuple((zero, zero, jnp.zeros((B, dh), F32)) for _ in range(G)))
        carry = tile(i, carry, True)
        for h in range(G):
            dq_ref[h] = carry[h][2]

    blk = pl.BlockSpec((G, B, dh), lambda h, i: (h, i, 0))
    full = pl.BlockSpec((G, T, dh), lambda h, i: (h, 0, 0))
    return pl.pallas_call(
        body, name="attn_bwd", grid=(H // G, T // B),
        in_specs=[blk, full, full, blk, pl.BlockSpec((G, B, 1), lambda h, i: (h, i, 0))],
        out_specs=[blk, full, full],
        out_shape=[jax.ShapeDtypeStruct((H, T, dh), F32)] * 3,
        compiler_params=_params("parallel", "arbitrary"),
    )(q, k, v, do, rt)


def _loss_head(y, target, tm):
    S, D = y.shape

    def body(y_ref, t_ref, dy_ref, part_ref):
        err = y_ref[...] - t_ref[...]
        dy_ref[...] = err * (1.0 / D)

        @pl.when(pl.program_id(0) == 0)
        def _():
            part_ref[...] = jnp.zeros_like(part_ref)

        part_ref[...] += jnp.sum(err * err, axis=0, keepdims=True)

    spec = pl.BlockSpec((tm, D), lambda i: (i, 0))
    return pl.pallas_call(
        body, name="loss_head", grid=(S // tm,), in_specs=[spec, spec],
        out_specs=[spec, pl.BlockSpec((1, D), lambda i: (0, 0))],
        out_shape=[jax.ShapeDtypeStruct((S, D), F32), jax.ShapeDtypeStruct((1, D), F32)],
        compiler_params=_params("arbitrary"),
    )(y, target)


def _row_tile(R):
    for t in (256, 128, 64, 32, 16, 8):
        if R % t == 0:
            return t
    return R


def _pair_add_bf16(name, g, b1, kind, c_arr):
    P, Rh, C = b1.shape
    tr = _row_tile(Rh)
    nb = Rh // tr

    def body(c_ref, g_ref, b_ref, o_ref):
        o_ref[...] = (g_ref[...] + b_ref[...]).astype(o_ref.dtype)

    if kind == "cols":
        g_spec = pl.BlockSpec((None, tr, C), lambda p, i, c_ref: (p, c_ref[0] * nb + i, 0))
    else:
        g_spec = pl.BlockSpec((tr, C), lambda p, i, c_ref: ((2 * p + c_ref[0]) * nb + i, 0))
    blk = pl.BlockSpec((None, tr, C), lambda p, i, c_ref: (p, i, 0))
    return pl.pallas_call(
        body, name=name,
        grid_spec=pltpu.PrefetchScalarGridSpec(num_scalar_prefetch=1, grid=(P, nb), in_specs=[g_spec, blk], out_specs=blk),
        out_shape=jax.ShapeDtypeStruct((P, Rh, C), BF16),
        compiler_params=_params("parallel", "parallel"),
    )(c_arr, g, b1)


def _sum_slots(name, b, half_arr=None):
    P, R, C = b.shape
    tr = _row_tile(R)
    nb = R // tr

    def body(*refs):
        b_ref, o_ref = refs[-2:]
        acc = b_ref[0].astype(F32)
        for s in range(1, P):
            acc = acc + b_ref[s].astype(F32)
        o_ref[...] = acc

    if half_arr is None:
        return pl.pallas_call(
            body, name=name, grid=(nb,),
            in_specs=[pl.BlockSpec((P, tr, C), lambda i: (0, i, 0))],
            out_specs=pl.BlockSpec((tr, C), lambda i: (i, 0)),
            out_shape=jax.ShapeDtypeStruct((R, C), F32),
            compiler_params=_params("parallel"),
        )(b)
    return pl.pallas_call(
        body, name=name,
        grid_spec=pltpu.PrefetchScalarGridSpec(
            num_scalar_prefetch=1, grid=(nb,),
            in_specs=[pl.BlockSpec((P, tr, C), lambda i, half: (0, i, 0))],
            out_specs=pl.BlockSpec((tr, C), lambda i, half: (half[0] * nb + i, 0))),
        out_shape=jax.ShapeDtypeStruct((2 * R, C), F32),
        compiler_params=_params("parallel"),
    )(half_arr, b)


def _adamw(name, w, g, m, v):
    R, C = w.shape
    tr = _row_tile(R)
    c1 = 1.0 - ADAM_B1 ** ADAM_STEP
    c2 = 1.0 - ADAM_B2 ** ADAM_STEP

    def body(w_ref, g_ref, m_ref, v_ref, d_ref, nm_ref, nv_ref):
        gg = g_ref[...]
        nm = ADAM_B1 * m_ref[...] + (1.0 - ADAM_B1) * gg
        nv = ADAM_B2 * v_ref[...] + (1.0 - ADAM_B2) * (gg * gg)
        m_hat = nm / c1
        v_hat = nv / c2
        d_ref[...] = -ADAM_LR * (m_hat / (jnp.sqrt(v_hat) + ADAM_EPS) + ADAM_WD * w_ref[...])
        nm_ref[...] = nm
        nv_ref[...] = nv

    spec = pl.BlockSpec((tr, C), lambda i: (i, 0))
    return pl.pallas_call(
        body, name=name, grid=(R // tr,), in_specs=[spec] * 4, out_specs=[spec] * 3,
        out_shape=[jax.ShapeDtypeStruct((R, C), F32)] * 3,
        compiler_params=_params("parallel"),
    )(w, g, m, v)


def _place():
    x, y, c = lax.axis_index("x"), lax.axis_index("y"), lax.axis_index("c")
    other_chips = [(1 - x, y), (x, 1 - y), (1 - x, 1 - y)]
    return x, y, c, other_chips


def _into_slot(name, w, dtype, slot_arr, n_slots):
    R, C = w.shape
    tr = _row_tile(R)

    def body(slot_ref, w_ref, o_ref):
        o_ref[...] = w_ref[...].astype(o_ref.dtype)

    return pl.pallas_call(
        body, name=name,
        grid_spec=pltpu.PrefetchScalarGridSpec(
            num_scalar_prefetch=1, grid=(R // tr,),
            in_specs=[pl.BlockSpec((tr, C), lambda i, slot: (i, 0))],
            out_specs=pl.BlockSpec((None, tr, C), lambda i, slot: (slot[0], i, 0))),
        out_shape=jax.ShapeDtypeStruct((n_slots, R, C), dtype),
        compiler_params=_params("parallel"),
    )(slot_arr, w)


def _gather_chips(bufs):
    n = len(bufs)

    def body(*refs):
        outs = refs[n:2 * n]
        ici_send, ici_recv, d2d_send, d2d_recv = refs[2 * n:]
        x, y, c, chips = _place()
        me = 2 * x + y
        started = []
        for k in range(n):
            rh = outs[k].shape[1] // 2
            mine = outs[k].at[me, pl.ds(c * rh, rh)]
            for j, (px, py) in enumerate(chips):
                cp = pltpu.make_async_remote_copy(
                    src_ref=mine, dst_ref=mine,
                    send_sem=ici_send.at[3 * k + j], recv_sem=ici_recv.at[3 * k + j],
                    device_id=(px, py, c), device_id_type=MESH)
                cp.start()
                started.append(cp)
        for k in range(n):
            rh = outs[k].shape[1] // 2
            for j, (px, py) in enumerate(chips):
                landed = outs[k].at[2 * px + py, pl.ds(c * rh, rh)]
                pltpu.make_async_remote_copy(
                    src_ref=landed, dst_ref=landed,
                    send_sem=ici_send.at[3 * k + j], recv_sem=ici_recv.at[3 * k + j],
                    device_id=(px, py, c), device_id_type=MESH).wait_recv()
                cp = pltpu.make_async_remote_copy(
                    src_ref=landed, dst_ref=landed,
                    send_sem=d2d_send.at[3 * k + j], recv_sem=d2d_recv.at[3 * k + j],
                    device_id=(x, y, 1 - c), device_id_type=MESH)
                cp.start()
                started.append(cp)
        for k in range(n):
            rh = outs[k].shape[1] // 2
            for j, (px, py) in enumerate(chips):
                landed = outs[k].at[2 * px + py, pl.ds((1 - c) * rh, rh)]
                pltpu.make_async_remote_copy(
                    src_ref=landed, dst_ref=landed,
                    send_sem=d2d_send.at[3 * k + j], recv_sem=d2d_recv.at[3 * k + j],
                    device_id=(x, y, 1 - c), device_id_type=MESH).wait_recv()
        for cp in started:
            cp.wait_send()

    return pl.pallas_call(
        body, name="gather_weights",
        in_specs=[ANY] * n, out_specs=[ANY] * n,
        out_shape=[jax.ShapeDtypeStruct(b.shape, b.dtype) for b in bufs],
        input_output_aliases={k: k for k in range(n)},
        scratch_shapes=[pltpu.SemaphoreType.DMA((3 * n,))] * 4,
        compiler_params=pltpu.CompilerParams(has_side_effects=True),
    )(*bufs)


def _half(ref, kind, p, c, rh):
    if kind == "cols":
        return ref.at[p, pl.ds(c * rh, rh)]
    return ref.at[pl.ds((2 * p + c) * rh, rh)]


def _swap_halves(grads, kinds, rhs):
    n = len(grads)

    def body(*refs):
        ins, outs = refs[:n], refs[n:2 * n]
        send_sem, recv_sem = refs[2 * n:]
        x, y, c, _ = _place()
        started = []
        for k in range(n):
            for p in range(N_CHIPS):
                cp = pltpu.make_async_remote_copy(
                    src_ref=_half(ins[k], kinds[k], p, 1 - c, rhs[k]), dst_ref=outs[k].at[p],
                    send_sem=send_sem.at[N_CHIPS * k + p], recv_sem=recv_sem.at[N_CHIPS * k + p],
                    device_id=(x, y, 1 - c), device_id_type=MESH)
                cp.start()
                started.append(cp)
        for cp in started:
            cp.wait()

    out_shape = []
    for g, kind, rh in zip(grads, kinds, rhs):
        out_shape.append(jax.ShapeDtypeStruct((N_CHIPS, rh, g.shape[-1]), g.dtype))
    return pl.pallas_call(
        body, name="grad_swap_halves",
        in_specs=[ANY] * n, out_specs=[ANY] * n, out_shape=out_shape,
        scratch_shapes=[pltpu.SemaphoreType.DMA((N_CHIPS * n,))] * 2,
        compiler_params=pltpu.CompilerParams(has_side_effects=True),
    )(*grads)


def _scatter_partials(parts):
    n = len(parts)

    def body(*refs):
        ins, outs = refs[:n], refs[n:2 * n]
        send_sem, recv_sem, local_sem = refs[2 * n:]
        x, y, c, chips = _place()
        me = 2 * x + y
        started = []
        for k in range(n):
            cp = pltpu.make_async_copy(ins[k].at[me], outs[k].at[me], local_sem.at[k])
            cp.start()
            started.append(cp)
            for j, (px, py) in enumerate(chips):
                cp = pltpu.make_async_remote_copy(
                    src_ref=ins[k].at[2 * px + py], dst_ref=outs[k].at[me],
                    send_sem=send_sem.at[3 * k + j], recv_sem=recv_sem.at[3 * k + j],
                    device_id=(px, py, c), device_id_type=MESH)
                cp.start()
                started.append(cp)
        for k in range(n):
            for j, (px, py) in enumerate(chips):
                landed = outs[k].at[2 * px + py]
                pltpu.make_async_remote_copy(
                    src_ref=landed, dst_ref=landed,
                    send_sem=send_sem.at[3 * k + j], recv_sem=recv_sem.at[3 * k + j],
                    device_id=(px, py, c), device_id_type=MESH).wait_recv()
        for k in range(n):
            started[4 * k].wait()
            for j in range(3):
                started[4 * k + 1 + j].wait_send()

    return pl.pallas_call(
        body, name="grad_scatter_partials",
        in_specs=[ANY] * n, out_specs=[ANY] * n,
        out_shape=[jax.ShapeDtypeStruct(s.shape, s.dtype) for s in parts],
        scratch_shapes=[pltpu.SemaphoreType.DMA((3 * n,))] * 2 + [pltpu.SemaphoreType.DMA((n,))],
        compiler_params=pltpu.CompilerParams(has_side_effects=True),
    )(*parts)


def _join_halves(fulls):
    n = len(fulls)

    def body(*refs):
        outs = refs[n:2 * n]
        send_sem, recv_sem = refs[2 * n:]
        x, y, c, _ = _place()
        started = []
        for k in range(n):
            rh = outs[k].shape[0] // 2
            mine = outs[k].at[pl.ds(c * rh, rh)]
            cp = pltpu.make_async_remote_copy(
                src_ref=mine, dst_ref=mine, send_sem=send_sem.at[k], recv_sem=recv_sem.at[k],
                device_id=(x, y, 1 - c), device_id_type=MESH)
            cp.start()
            started.append(cp)
        for k in range(n):
            rh = outs[k].shape[0] // 2
            theirs = outs[k].at[pl.ds((1 - c) * rh, rh)]
            pltpu.make_async_remote_copy(
                src_ref=theirs, dst_ref=theirs, send_sem=send_sem.at[k], recv_sem=recv_sem.at[k],
                device_id=(x, y, 1 - c), device_id_type=MESH).wait_recv()
        for cp in started:
            cp.wait_send()

    return pl.pallas_call(
        body, name="grad_join_halves",
        in_specs=[ANY] * n, out_specs=[ANY] * n,
        out_shape=[jax.ShapeDtypeStruct(f.shape, f.dtype) for f in fulls],
        input_output_aliases={k: k for k in range(n)},
        scratch_shapes=[pltpu.SemaphoreType.DMA((n,))] * 2,
        compiler_params=pltpu.CompilerParams(has_side_effects=True),
    )(*fulls)


def _gather_all(block):
    def body(in_ref, out_ref, send_sem, recv_sem, local_sem):
        x, y, c, _ = _place()

        def slot(px, py, pc):
            return out_ref.at[4 * px + 2 * py + pc]

        loc = pltpu.make_async_copy(in_ref, slot(x, y, c), local_sem)
        loc.start()
        started = []
        for d in range(1, N_DEV):
            fx, fy, fc = d >> 2, (d >> 1) & 1, d & 1
            cp = pltpu.make_async_remote_copy(
                src_ref=in_ref, dst_ref=slot(x, y, c), send_sem=send_sem.at[d - 1], recv_sem=recv_sem.at[d - 1],
                device_id=(x ^ fx, y ^ fy, c ^ fc), device_id_type=MESH)
            cp.start()
            started.append(cp)
        for d in range(1, N_DEV):
            fx, fy, fc = d >> 2, (d >> 1) & 1, d & 1
            landed = slot(x ^ fx, y ^ fy, c ^ fc)
            pltpu.make_async_remote_copy(
                src_ref=in_ref, dst_ref=landed, send_sem=send_sem.at[d - 1], recv_sem=recv_sem.at[d - 1],
                device_id=(x ^ fx, y ^ fy, c ^ fc), device_id_type=MESH).wait_recv()
        for cp in started:
            cp.wait_send()
        loc.wait()

    return pl.pallas_call(
        body, name="gather_small_grads",
        in_specs=[ANY], out_specs=ANY,
        out_shape=jax.ShapeDtypeStruct((N_DEV,) + block.shape, block.dtype),
        scratch_shapes=[pltpu.SemaphoreType.DMA((N_DEV - 1,))] * 2 + [pltpu.SemaphoreType.DMA(())],
        compiler_params=pltpu.CompilerParams(has_side_effects=True),
    )(block)


def _pack(pieces):
    flat = jnp.concatenate([p.reshape(-1) for p in pieces])
    n = flat.shape[0]
    padded = -(-n // (8 * LANES)) * (8 * LANES)
    return jnp.pad(flat, (0, padded - n)).reshape(-1, LANES)


def _unpack(packed, shapes):
    flat = packed.reshape(-1)
    out, at = [], 0
    for s in shapes:
        n = math.prod(s)
        out.append(flat[at:at + n].reshape(s))
        at += n
    return out


def _heads(a, H, Ta):
    T = a.shape[0]
    return jnp.pad(a.reshape(T, H, HEAD_DIM).transpose(1, 0, 2), ((0, 0), (0, Ta - T), (0, 0)))


def _unheads(a, T):
    H, _, dh = a.shape
    return a[:, :T].transpose(1, 0, 2).reshape(T, H * dh)


def kernel(x, meta_tokens, pre_mix_g, w_in, gate_b, dw_w, dw_b, conv_ln_g, conv_ln_b, w_conv_out, w_attn_out, w_o, post_mix_g, pre_ffn_g, w_ffn_in, w_ffn_out, post_ffn_g, loss_target, m_meta_tokens, m_pre_mix_g, m_w_in, m_gate_b, m_dw_w, m_dw_b, m_conv_ln_g, m_conv_ln_b, m_w_conv_out, m_w_attn_out, m_w_o, m_post_mix_g, m_pre_ffn_g, m_w_ffn_in, m_w_ffn_out, m_post_ffn_g, v_meta_tokens, v_pre_mix_g, v_w_in, v_gate_b, v_dw_w, v_dw_b, v_conv_ln_g, v_conv_ln_b, v_w_conv_out, v_w_attn_out, v_w_o, v_post_mix_g, v_pre_ffn_g, v_w_ffn_in, v_w_ffn_out, v_post_ffn_g):
    S, D = x.shape[1], x.shape[2]
    L = S + N_META
    T = -(-L // ROW_BLOCK) * ROW_BLOCK
    Ta = -(-L // ATT_BLOCK) * ATT_BLOCK
    tm = _tile(T, MM_ROWS)
    ts = _tile(T, STAGE_ROWS)
    tw = _tile(T, WIDE_STAGE_ROWS)
    H = D // HEAD_DIM
    F = w_ffn_out.shape[1] * N_CHIPS
    Dc = D // N_CHIPS
    P = N_CHIPS
    me = 2 * lax.axis_index("x") + lax.axis_index("y")
    c_arr = lax.axis_index("c").astype(jnp.int32).reshape(1)

    dw_w_pad = jnp.pad(dw_w[0], ((0, CONV_PAD - CONV_WIDTH), (0, 0)))
    me_arr = me.astype(jnp.int32).reshape(1)
    to_gather = [("w_in", w_in[0], BF16), ("w_conv_out", w_conv_out[0], BF16), ("w_attn_out", w_attn_out[0], BF16),
                 ("w_o", w_o[0], BF16), ("w_ffn_in", w_ffn_in[0], BF16), ("w_ffn_out", w_ffn_out[0], BF16),
                 ("meta", meta_tokens, F32), ("taps", dw_w_pad, F32)]
    gathered = _gather_chips([_into_slot("slot_" + n, w, dt, me_arr, P) for n, w, dt in to_gather])
    win3, wfi3 = gathered[0], gathered[4]
    wco, wao, wo = (gathered[k].reshape(D, D) for k in (1, 2, 3))
    wfo = gathered[5].reshape(F, D)
    meta_full = gathered[6].transpose(1, 0, 2).reshape(N_META, D)
    taps = gathered[7].transpose(1, 0, 2).reshape(CONV_PAD, D)[:CONV_WIDTH]

    h0 = jnp.concatenate([meta_full, x[0], jnp.zeros((T - L, D), F32)], axis=0)
    (u1,) = _rowwise_fwd("rms_pre_mix", f_rms, [(h0, D, 0)], [(pre_mix_g, D, 0)], [(D, BF16)], T, ts)
    p = _mm_nn_cols("mm_in", u1, win3, tm)
    (uglu,) = _rowwise_fwd("glu", f_glu, [(p, D, 0), (p, D, 1)], [], [(D, F32)], T, ts)
    yc = _shift_conv("dwconv", uglu, taps, CONV_PAD, CONV_PAD - (CONV_WIDTH - 1), T)
    conv_pars = [(dw_b, D, 0), (conv_ln_g, D, 0), (conv_ln_b, D, 0)]
    (ys,) = _rowwise_fwd("conv_post", f_convpost, [(yc, D, 0)], conv_pars, [(D, BF16)], T, ts)
    y_conv = _mm_nn("mm_conv_out", ys, wco, tm)
    q, k, v = (_heads(p[:, (2 + n) * D:(3 + n) * D].astype(BF16), H, Ta) for n in range(3))
    o, rtot = _attn_fwd(q, k, v)
    o2 = _unheads(o, T)
    y_attn = _mm_nn("mm_attn_out", o2, wao, tm)
    mix_rows = [(p, D, 5), (p, D, 6), (y_conv, D, 0), (y_attn, D, 0)]
    mix_pars = [(gate_b, D, 0), (gate_b, D, 1)]
    (mixin,) = _rowwise_fwd("gate_mix", f_mix, mix_rows, mix_pars, [(D, BF16)], T, ts)
    mix = _mm_nn("mm_o", mixin, wo, tm)
    (h1,) = _rowwise_fwd("res_post_mix", f_res_rms, [(h0, D, 0), (mix, D, 0)], [(post_mix_g, D, 0)], [(D, F32)], T, ts)
    (u2,) = _rowwise_fwd("rms_pre_ffn", f_rms, [(h1, D, 0)], [(pre_ffn_g, D, 0)], [(D, BF16)], T, ts)
    ab = _mm_nn_cols("mm_ffn_in", u2, wfi3, tm)
    (fin,) = _rowwise_fwd("swiglu", f_swiglu, [(ab, F, 0), (ab, F, 1)], [], [(F, BF16)], T, tw)
    f = _mm_nn("mm_ffn_out", fin, wfo, tm)
    (h2,) = _rowwise_fwd("res_post_ffn", f_res_rms, [(h1, D, 0), (f, D, 0)], [(post_ffn_g, D, 0)], [(D, F32)], T, ts)

    dy, part = _loss_head(h2[N_META:L], loss_target[0], _row_tile(S))
    loss = lax.psum(0.5 * jnp.sum(part) / D, ("x", "y", "c"))
    dh2 = jnp.pad(dy, ((N_META, T - L), (0, 0)))

    (df,), (g_post_ffn,) = _rowwise_bwd("res_post_ffn_bwd", f_res_rms, [(h1, D, 0), (f, D, 0)], [(post_ffn_g, D, 0)],
                                        [(dh2, D, 0)], [None, BF16], T, ts)
    dfin = _mm_nt("mm_ffn_out_dx", df, wfo, tm)
    g_wfo = _mm_tn("mm_ffn_out_dw", fin, df, tm, 2)
    (da, db), _ = _rowwise_bwd("swiglu_bwd", f_swiglu, [(ab, F, 0), (ab, F, 1)], [], [(dfin, F, 0)], [BF16, BF16], T, tw)
    dab = jnp.concatenate([da, db], axis=1)
    du2 = _mm_nt_cols("mm_ffn_in_dx", dab, wfi3, tm)
    g_wfi = _mm_tn_cols("mm_ffn_in_dw", u2, dab, tm, P)
    (dh1,), (g_pre_ffn,) = _rowwise_bwd("rms_pre_ffn_bwd", f_rms_id, [(h1, D, 0)], [(pre_ffn_g, D, 0)],
                                        [(du2, D, 0), (dh2, D, 0)], [F32], T, ts)
    (dmix,), (g_post_mix,) = _rowwise_bwd("res_post_mix_bwd", f_res_rms, [(h0, D, 0), (mix, D, 0)], [(post_mix_g, D, 0)],
                                          [(dh1, D, 0)], [None, BF16], T, ts)
    dmixin = _mm_nt("mm_o_dx", dmix, wo, tm)
    g_wo = _mm_tn("mm_o_dw", mixin, dmix, tm, 1)
    (dpc, dpa, dyconv, dyattn), (g_gate_c, g_gate_a) = _rowwise_bwd(
        "gate_mix_bwd", f_mix, mix_rows, mix_pars, [(dmixin, D, 0)], [BF16, BF16, BF16, BF16], T, ts)
    g_wco = _mm_tn("mm_conv_out_dw", ys, dyconv, tm, 1)
    dys = _mm_nt("mm_conv_out_dx", dyconv, wco, tm)
    g_wao = _mm_tn("mm_attn_out_dw", o2, dyattn, tm, 1)
    do2 = _mm_nt("mm_attn_out_dx", dyattn, wao, tm, BF16)
    (dyc,), (g_dw_b, g_ln_g, g_ln_b) = _rowwise_bwd("conv_post_bwd", f_convpost, [(yc, D, 0)], conv_pars,
                                                    [(dys, D, 0)], [F32], T, ts)
    duglu = _shift_conv("dwconv_dx", dyc, taps[::-1], 0, 0, T)
    g_taps = _conv_dw("dwconv_dw", uglu, dyc, T)
    (dp0, dp1), _ = _rowwise_bwd("glu_bwd", f_glu, [(p, D, 0), (p, D, 1)], [], [(duglu, D, 0)], [BF16, BF16], T, ts)
    dq, dk, dv = _attn_bwd(q, k, v, _heads(do2, H, Ta), rtot)
    dp = jnp.concatenate([dp0, dp1] + [_unheads(t, T).astype(BF16) for t in (dq, dk, dv)] + [dpc, dpa], axis=1)
    du1 = _mm_nt_cols("mm_in_dx", dp, win3, tm)
    g_win = _mm_tn_cols("mm_in_dw", u1, dp, tm, P)
    (dh0,), (g_pre_mix,) = _rowwise_bwd("rms_pre_mix_bwd", f_rms_id, [(h0, D, 0)], [(pre_mix_g, D, 0)],
                                        [(du1, D, 0), (dh1, D, 0)], [F32], T, ts)
    grad_x = dh0[N_META:L][None]

    big = [g_win, g_wco, g_wao, g_wo, g_wfi, g_wfo]
    kinds = ["cols", "rows", "rows", "rows", "cols", "rows"]
    rhs = [(g.shape[1] if kind == "cols" else g.shape[0] // P) // 2 for g, kind in zip(big, kinds)]
    from_sibling = _swap_halves(big, kinds, rhs)
    parts = [_pair_add_bf16("grad_pair_add_%d" % n, g, b1, kind, c_arr)
             for n, (g, b1, kind) in enumerate(zip(big, from_sibling, kinds))]
    slots = _scatter_partials(parts)
    g_big = _join_halves([_sum_slots("grad_chip_sum_%d" % n, s, c_arr) for n, s in enumerate(slots)])

    small_shapes = [(1, D), (1, D), (1, D), (CONV_WIDTH, D), (1, D), (1, D), (1, D), (1, D), (1, D), (1, D), (N_META, D)]
    small = _pack([g_pre_mix, g_gate_c, g_gate_a, g_taps, g_dw_b, g_ln_g, g_ln_b, g_post_mix, g_pre_ffn, g_post_ffn,
                   dh0[:N_META]])
    summed = _sum_slots("small_grad_sum", _gather_all(small))
    (s_pre_mix, s_gate_c, s_gate_a, s_taps, s_dw_b, s_ln_g, s_ln_b, s_post_mix, s_pre_ffn, s_post_ffn,
     s_meta) = _unpack(summed, small_shapes)
    s_gate_b = jnp.concatenate([s_gate_c, s_gate_a], axis=1)
    s_taps = lax.dynamic_slice_in_dim(s_taps, me * Dc, Dc, axis=1)[None]
    s_meta = lax.dynamic_slice_in_dim(s_meta, me * Dc, Dc, axis=1)

    grads = {
        "meta_tokens": s_meta, "pre_mix_g": s_pre_mix, "w_in": g_big[0][None], "gate_b": s_gate_b, "dw_w": s_taps,
        "dw_b": s_dw_b, "conv_ln_g": s_ln_g, "conv_ln_b": s_ln_b, "w_conv_out": g_big[1][None],
        "w_attn_out": g_big[2][None], "w_o": g_big[3][None], "post_mix_g": s_post_mix, "pre_ffn_g": s_pre_ffn,
        "w_ffn_in": g_big[4][None], "w_ffn_out": g_big[5][None], "post_ffn_g": s_post_ffn,
    }
    weights = {
        "meta_tokens": (meta_tokens, m_meta_tokens, v_meta_tokens), "pre_mix_g": (pre_mix_g, m_pre_mix_g, v_pre_mix_g),
        "w_in": (w_in, m_w_in, v_w_in), "gate_b": (gate_b, m_gate_b, v_gate_b), "dw_w": (dw_w, m_dw_w, v_dw_w),
        "dw_b": (dw_b, m_dw_b, v_dw_b), "conv_ln_g": (conv_ln_g, m_conv_ln_g, v_conv_ln_g),
        "conv_ln_b": (conv_ln_b, m_conv_ln_b, v_conv_ln_b), "w_conv_out": (w_conv_out, m_w_conv_out, v_w_conv_out),
        "w_attn_out": (w_attn_out, m_w_attn_out, v_w_attn_out), "w_o": (w_o, m_w_o, v_w_o),
        "post_mix_g": (post_mix_g, m_post_mix_g, v_post_mix_g), "pre_ffn_g": (pre_ffn_g, m_pre_ffn_g, v_pre_ffn_g),
        "w_ffn_in": (w_ffn_in, m_w_ffn_in, v_w_ffn_in), "w_ffn_out": (w_ffn_out, m_w_ffn_out, v_w_ffn_out),
        "post_ffn_g": (post_ffn_g, m_post_ffn_g, v_post_ffn_g),
    }
    names = list(weights)
    big_names = ["w_in", "w_conv_out", "w_attn_out", "w_o", "w_ffn_in", "w_ffn_out"]
    small_names = [n for n in names if n not in big_names]

    delta, new_m, new_v = {}, {}, {}
    for n in big_names:
        w, m, v2 = weights[n]
        d, nm, nv = _adamw("adamw_" + n, w[0], grads[n][0], m[0], v2[0])
        delta[n], new_m[n], new_v[n] = d[None], nm[None], nv[None]
    shapes = [weights[n][0].shape for n in small_names]
    packed = [_pack([weights[n][k] for n in small_names]) for k in range(3)]
    d, nm, nv = _adamw("adamw_small", packed[0], _pack([grads[n] for n in small_names]), packed[1], packed[2])
    for n, dd, mm, vv in zip(small_names, _unpack(d, shapes), _unpack(nm, shapes), _unpack(nv, shapes)):
        delta[n], new_m[n], new_v[n] = dd, mm, vv

    return (loss, grad_x, *[grads[n].reshape(weights[n][0].shape) for n in names], *[delta[n] for n in names],
            *[new_m[n] for n in names], *[new_v[n] for n in names])
```

```python
import math

import jax
import jax.numpy as jnp
from jax import lax
from jax.experimental import pallas as pl
from jax.experimental.pallas import tpu as pltpu

F32 = jnp.float32
BF16 = jnp.bfloat16

N_META = 16
CONV_WIDTH = 31
CONV_PAD = 32
HEAD_DIM = 64
RMS_EPS = 1e-6
LN_EPS = 1e-5
ROW_BLOCK = 128
ATT_BLOCK = 256
ATT_HEADS = 4
LANES = 128
N_CHIPS = 4
N_DEV = 8
MM_ROWS = 544
STAGE_ROWS = 272
WIDE_STAGE_ROWS = 128
VMEM_LIMIT = 56 * 1024 * 1024

ADAM_LR = 0.001
ADAM_B1 = 0.9
ADAM_B2 = 0.999
ADAM_EPS = 1e-08
ADAM_WD = 0.01
ADAM_STEP = 10

MESH = pl.DeviceIdType.MESH
ANY = pl.BlockSpec(memory_space=pl.ANY)


def _params(*sem):
    return pltpu.CompilerParams(dimension_semantics=sem if sem else None, vmem_limit_bytes=VMEM_LIMIT)


def _rms(x, g):
    return x * lax.rsqrt(jnp.mean(x * x, axis=-1, keepdims=True) + RMS_EPS) * g


def f_rms(h, g):
    return (_rms(h, g),)


def f_rms_id(h, g):
    return (_rms(h, g), h)


def f_res_rms(h, m, g):
    return (h + _rms(m, g),)


def f_glu(a, gate):
    return (a * lax.logistic(gate),)


def f_convpost(yc, b, ln_g, ln_b):
    y = yc + b
    mu = jnp.mean(y, axis=-1, keepdims=True)
    xc = y - mu
    var = jnp.mean(xc * xc, axis=-1, keepdims=True)
    yl = xc * lax.rsqrt(var + LN_EPS) * ln_g + ln_b
    return (yl * lax.logistic(yl),)


def f_mix(pc, pa, yc, ya, bc, ba):
    return (lax.logistic(pc + bc) * yc + lax.logistic(pa + ba) * ya,)


def f_swiglu(a, b):
    return (a * lax.logistic(a) * b,)


def _tile(T, target):
    return max(t for t in range(16, target + 1, 16) if T % t == 0)


def _row_map(j):
    return lambda i: (i, j)


def _par_map(j):
    return lambda i: (0, j)


def _rowwise_fwd(name, f, rows, pars, outs, T, tm):
    n_in = len(rows) + len(pars)

    def body(*refs):
        vals = [r[...].astype(F32) for r in refs[:n_in]]
        res = f(*vals)
        for o_ref, o in zip(refs[n_in:], res):
            o_ref[...] = o.astype(o_ref.dtype)

    in_specs = [pl.BlockSpec((tm, w), _row_map(j)) for _, w, j in rows]
    in_specs += [pl.BlockSpec((1, w), _par_map(j)) for _, w, j in pars]
    return pl.pallas_call(
        body, name=name, grid=(T // tm,),
        in_specs=in_specs,
        out_specs=[pl.BlockSpec((tm, w), _row_map(0)) for w, _ in outs],
        out_shape=[jax.ShapeDtypeStruct((T, w), dt) for w, dt in outs],
        compiler_params=_params("parallel"),
    )(*[a for a, _, _ in rows], *[a for a, _, _ in pars])


def _rowwise_bwd(name, f, rows, pars, cots, drow_dtypes, T, tm):
    n_r, n_p, n_c = len(rows), len(pars), len(cots)
    n_in = n_r + n_p + n_c
    keep = [k for k, dt in enumerate(drow_dtypes) if dt is not None]

    def body(*refs):
        rv = [r[...].astype(F32) for r in refs[:n_r]]
        pv = [r[...].astype(F32) for r in refs[n_r:n_r + n_p]]
        cv = [r[...].astype(F32) for r in refs[n_r + n_p:n_in]]
        _, vjp = jax.vjp(f, *rv, *pv)
        g = vjp(tuple(cv))
        drow_refs = refs[n_in:n_in + len(keep)]
        dpar_refs = refs[n_in + len(keep):]
        for r, k in zip(drow_refs, keep):
            r[...] = g[k].astype(r.dtype)

        @pl.when(pl.program_id(0) == 0)
        def _():
            for r in dpar_refs:
                r[...] = jnp.zeros_like(r)

        for r, gp in zip(dpar_refs, g[n_r:]):
            r[...] += gp

    in_specs = [pl.BlockSpec((tm, w), _row_map(j)) for _, w, j in rows]
    in_specs += [pl.BlockSpec((1, w), _par_map(j)) for _, w, j in pars]
    in_specs += [pl.BlockSpec((tm, w), _row_map(j)) for _, w, j in cots]
    out_specs = [pl.BlockSpec((tm, rows[k][1]), _row_map(0)) for k in keep]
    out_specs += [pl.BlockSpec((1, w), _par_map(0)) for _, w, _ in pars]
    out_shape = [jax.ShapeDtypeStruct((T, rows[k][1]), drow_dtypes[k]) for k in keep]
    out_shape += [jax.ShapeDtypeStruct((1, w), F32) for _, w, _ in pars]
    res = pl.pallas_call(
        body, name=name, grid=(T // tm,),
        in_specs=in_specs, out_specs=out_specs, out_shape=out_shape,
        compiler_params=_params("arbitrary"),
    )(*[a for a, _, _ in rows], *[a for a, _, _ in pars], *[a for a, _, _ in cots])
    return res[:len(keep)], res[len(keep):]


NN = (((1,), (0,)), ((), ()))
NT = (((1,), (1,)), ((), ()))
TN = (((0,), (0,)), ((), ()))


def _mm(name, a, b, dims, out_shape, grid, a_spec, b_spec, o_spec, red_axis=None):
    n_red = None if red_axis is None else grid[red_axis]

    def body(a_ref, b_ref, o_ref):
        prod = lax.dot_general(a_ref[...], b_ref[...], dims, preferred_element_type=F32)
        if n_red is None:
            o_ref[...] = prod.astype(o_ref.dtype)
        else:
            @pl.when(pl.program_id(red_axis) == 0)
            def _():
                o_ref[...] = prod

            @pl.when(pl.program_id(red_axis) > 0)
            def _():
                o_ref[...] += prod

    sem = ["parallel"] * len(grid)
    if red_axis is not None:
        sem[red_axis] = "arbitrary"
    return pl.pallas_call(
        body, name=name, grid=grid, in_specs=[a_spec, b_spec], out_specs=o_spec, out_shape=out_shape,
        compiler_params=_params(*sem),
    )(a, b)


def _mm_nn(name, a, w, tm, out_dtype=F32):
    T, K = a.shape
    N = w.shape[1]
    return _mm(name, a, w, NN, jax.ShapeDtypeStruct((T, N), out_dtype), (T // tm,),
               pl.BlockSpec((tm, K), lambda i: (i, 0)), pl.BlockSpec((K, N), lambda i: (0, 0)),
               pl.BlockSpec((tm, N), lambda i: (i, 0)))


def _mm_nt(name, a, w, tm, out_dtype=F32):
    T, N = a.shape
    K = w.shape[0]
    return _mm(name, a, w, NT, jax.ShapeDtypeStruct((T, K), out_dtype), (T // tm,),
               pl.BlockSpec((tm, N), lambda i: (i, 0)), pl.BlockSpec((K, N), lambda i: (0, 0)),
               pl.BlockSpec((tm, K), lambda i: (i, 0)))


def _mm_tn(name, a, b, tm, n_row_blocks):
    T, K = a.shape
    N = b.shape[1]
    kb = K // n_row_blocks
    return _mm(name, a, b, TN, jax.ShapeDtypeStruct((K, N), F32), (n_row_blocks, T // tm),
               pl.BlockSpec((tm, kb), lambda r, t: (t, r)), pl.BlockSpec((tm, N), lambda r, t: (t, 0)),
               pl.BlockSpec((kb, N), lambda r, t: (r, 0)), red_axis=1)


def _mm_nn_cols(name, a, w3, tm):
    T, K = a.shape
    P, _, Ns = w3.shape
    return _mm(name, a, w3, NN, jax.ShapeDtypeStruct((T, P * Ns), F32), (P, T // tm),
               pl.BlockSpec((tm, K), lambda p, i: (i, 0)), pl.BlockSpec((None, K, Ns), lambda p, i: (p, 0, 0)),
               pl.BlockSpec((tm, Ns), lambda p, i: (i, p)))


def _mm_nt_cols(name, a, w3, tm):
    T = a.shape[0]
    P, K, Ns = w3.shape
    return _mm(name, a, w3, NT, jax.ShapeDtypeStruct((T, K), F32), (T // tm, P),
               pl.BlockSpec((tm, Ns), lambda i, p: (i, p)), pl.BlockSpec((None, K, Ns), lambda i, p: (p, 0, 0)),
               pl.BlockSpec((tm, K), lambda i, p: (i, 0)), red_axis=1)


def _mm_tn_cols(name, a, b, tm, P):
    T, K = a.shape
    Ns = b.shape[1] // P
    return _mm(name, a, b, TN, jax.ShapeDtypeStruct((P, K, Ns), F32), (P, T // tm),
               pl.BlockSpec((tm, K), lambda p, t: (t, 0)), pl.BlockSpec((tm, Ns), lambda p, t: (t, p)),
               pl.BlockSpec((None, K, Ns), lambda p, t: (p, 0, 0)), red_axis=1)


def _shift_conv(name, x, w, place, off, T):
    C = x.shape[1]
    tb = ROW_BLOCK
    zero_at = 0 if place else T

    def body(x_ref, w_ref, o_ref, xp_ref):
        xp_ref[pl.ds(zero_at, CONV_PAD), :] = jnp.zeros((CONV_PAD, LANES), F32)
        xp_ref[pl.ds(place, T), :] = x_ref[...]

        def step(t, carry):
            base = pl.multiple_of(t * tb, tb)
            win = xp_ref[pl.ds(base, tb + CONV_PAD), :]
            acc = jnp.zeros((tb, LANES), F32)
            for j in range(CONV_WIDTH):
                acc = acc + win[off + j:off + j + tb, :] * w_ref[pl.ds(j, 1), :]
            o_ref[pl.ds(base, tb), :] = acc
            return carry

        lax.fori_loop(0, T // tb, step, 0)

    return pl.pallas_call(
        body, name=name, grid=(C // LANES,),
        in_specs=[pl.BlockSpec((T, LANES), lambda c: (0, c)), pl.BlockSpec((CONV_WIDTH, LANES), lambda c: (0, c))],
        out_specs=pl.BlockSpec((T, LANES), lambda c: (0, c)),
        out_shape=jax.ShapeDtypeStruct((T, C), F32),
        scratch_shapes=[pltpu.VMEM((T + CONV_PAD, LANES), F32)],
        compiler_params=_params("parallel"),
    )(x, w)


def _conv_dw(name, x, dy, T):
    C = x.shape[1]
    tb = ROW_BLOCK
    off = CONV_PAD - (CONV_WIDTH - 1)

    def body(x_ref, dy_ref, o_ref, xp_ref, acc_ref):
        xp_ref[pl.ds(0, CONV_PAD), :] = jnp.zeros((CONV_PAD, LANES), F32)
        xp_ref[pl.ds(CONV_PAD, T), :] = x_ref[...]
        acc_ref[...] = jnp.zeros_like(acc_ref)

        def step(t, carry):
            base = pl.multiple_of(t * tb, tb)
            win = xp_ref[pl.ds(base, tb + CONV_PAD), :]
            d = dy_ref[pl.ds(base, tb), :]
            for j in range(CONV_WIDTH):
                prod = win[off + j:off + j + tb, :] * d
                acc_ref[j] += jnp.sum(prod.reshape(tb // 8, 8, LANES), axis=0)
            return carry

        lax.fori_loop(0, T // tb, step, 0)
        for j in range(CONV_WIDTH):
            o_ref[pl.ds(j, 1), :] = jnp.sum(acc_ref[j], axis=0, keepdims=True)

    return pl.pallas_call(
        body, name=name, grid=(C // LANES,),
        in_specs=[pl.BlockSpec((T, LANES), lambda c: (0, c)), pl.BlockSpec((T, LANES), lambda c: (0, c))],
        out_specs=pl.BlockSpec((CONV_WIDTH, LANES), lambda c: (0, c)),
        out_shape=jax.ShapeDtypeStruct((CONV_WIDTH, C), F32),
        scratch_shapes=[pltpu.VMEM((T + CONV_PAD, LANES), F32), pltpu.VMEM((CONV_WIDTH, 8, LANES), F32)],
        compiler_params=_params("parallel"),
    )(x, dy)


def _dot(a, b, dims=NN):
    return lax.dot_general(a, b, dims, preferred_element_type=F32)


def _tri_cumsum(x, tri):
    return _dot(x.astype(BF16), tri)


def _attn_fwd(q, k, v):
    H, T, dh = q.shape
    B = ATT_BLOCK
    G = ATT_HEADS

    def body(q_ref, k_ref, v_ref, o_ref, rt_ref):
        i = pl.program_id(1)
        row = lax.broadcasted_iota(jnp.int32, (B, B), 0)
        col = lax.broadcasted_iota(jnp.int32, (B, B), 1)
        below = col < row
        tri = (row >= col).astype(BF16)

        def tile(j, carry, diagonal):
            sl = pl.ds(pl.multiple_of(j * B, B), B)
            out = []
            for g in range(G):
                c, acc = carry[g]
                z = _dot(q_ref[g], k_ref[g, sl, :], NT)
                sp = jnp.maximum(z, 0.0) + jnp.log(1.0 + jnp.exp(-jnp.abs(z)))
                if diagonal:
                    sp = jnp.where(below, sp, 0.0)
                rw = _tri_cumsum(sp, tri)
                a = jnp.exp(z - (rw + c))
                if diagonal:
                    a = jnp.where(below, a, 0.0)
                acc = acc + _dot(a.astype(BF16), v_ref[g, sl, :])
                out.append((c + rw[:, 0:1], acc))
            return tuple(out)

        carry = tile(i, tuple((jnp.zeros((B, 1), F32), jnp.zeros((B, dh), F32)) for _ in range(G)), True)
        carry = lax.fori_loop(0, i, lambda jj, cr: tile(i - 1 - jj, cr, False), carry)
        for g in range(G):
            o_ref[g] = carry[g][1].astype(o_ref.dtype)
            rt_ref[g] = carry[g][0]

    return pl.pallas_call(
        body, name="attn_fwd", grid=(H // G, T // B),
        in_specs=[pl.BlockSpec((G, B, dh), lambda h, i: (h, i, 0)),
                  pl.BlockSpec((G, T, dh), lambda h, i: (h, 0, 0)),
                  pl.BlockSpec((G, T, dh), lambda h, i: (h, 0, 0))],
        out_specs=[pl.BlockSpec((G, B, dh), lambda h, i: (h, i, 0)),
                   pl.BlockSpec((G, B, 1), lambda h, i: (h, i, 0))],
        out_shape=[jax.ShapeDtypeStruct((H, T, dh), BF16), jax.ShapeDtypeStruct((H, T, 1), F32)],
        compiler_params=_params("parallel", "arbitrary"),
    )(q, k, v)


def _attn_bwd(q, k, v, do, rt):
    H, T, dh = q.shape
    B = ATT_BLOCK
    scale = 1.0 / math.sqrt(dh)
    G = ATT_HEADS

    def body(q_ref, k_ref, v_ref, do_ref, rt_ref, dq_ref, dk_ref, dv_ref):
        i = pl.program_id(1)

        @pl.when(i == 0)
        def _():
            dk_ref[...] = jnp.zeros_like(dk_ref)
            dv_ref[...] = jnp.zeros_like(dv_ref)

        row = lax.broadcasted_iota(jnp.int32, (B, B), 0)
        col = lax.broadcasted_iota(jnp.int32, (B, B), 1)
        below = col < row
        tri = (row <= col).astype(BF16)

        def tile(j, carry, diagonal):
            sl = pl.ds(pl.multiple_of(j * B, B), B)
            out = []
            for h in range(G):
                pc, gc, dq = carry[h]
                qi, kj, vj, doi = q_ref[h], k_ref[h, sl, :], v_ref[h, sl, :], do_ref[h]
                z = _dot(qi, kj, NT)
                e = jnp.exp(-jnp.abs(z))
                inv = 1.0 / (1.0 + e)
                sp = jnp.maximum(z, 0.0) - jnp.log(inv)
                sg = jnp.where(z >= 0.0, inv, e * inv)
                if diagonal:
                    sp = jnp.where(below, sp, 0.0)
                pw = _tri_cumsum(sp, tri)
                a = jnp.exp(z - (rt_ref[h] - pc - pw + sp))
                if diagonal:
                    a = jnp.where(below, a, 0.0)
                g = a * _dot(doi, vj, NT)
                gw = _tri_cumsum(g, tri)
                dz = g - sg * (gc + gw)
                if diagonal:
                    dz = jnp.where(below, dz, 0.0)
                dzb = dz.astype(BF16)
                dq = dq + _dot(dzb, kj)
                dk_ref[h, sl, :] += _dot(dzb, qi, TN)
                dv_ref[h, sl, :] += _dot(a.astype(BF16), doi, TN)
                out.append((pc + pw[:, B - 1:B], gc + gw[:, B - 1:B], dq))
            return tuple(out)

        zero = jnp.zeros((B, 1), F32)
        carry = lax.fori_loop(0, i, lambda j, cr: tile(j, cr, False),
                              tuple((zero, zero, jnp.zeros((B, dh), F32)) for _ in range(G)))
        carry = tile(i, carry, True)
        for h in range(G):
            dq_ref[h] = carry[h][2] * scale

    blk = pl.BlockSpec((G, B, dh), lambda h, i: (h, i, 0))
    full = pl.BlockSpec((G, T, dh), lambda h, i: (h, 0, 0))
    return pl.pallas_call(
        body, name="attn_bwd", grid=(H // G, T // B),
        in_specs=[blk, full, full, blk, pl.BlockSpec((G, B, 1), lambda h, i: (h, i, 0))],
        out_specs=[blk, full, full],
        out_shape=[jax.ShapeDtypeStruct((H, T, dh), F32)] * 3,
        compiler_params=_params("parallel", "arbitrary"),
    )(q, k, v, do, rt)


def _loss_head(y, target, tm):
    S, D = y.shape

    def body(y_ref, t_ref, dy_ref, part_ref):
        err = y_ref[...] - t_ref[...]
        dy_ref[...] = err * (1.0 / D)

        @pl.when(pl.program_id(0) == 0)
        def _():
            part_ref[...] = jnp.zeros_like(part_ref)

        part_ref[...] += jnp.sum(err * err, axis=0, keepdims=True)

    spec = pl.BlockSpec((tm, D), lambda i: (i, 0))
    return pl.pallas_call(
        body, name="loss_head", grid=(S // tm,), in_specs=[spec, spec],
        out_specs=[spec, pl.BlockSpec((1, D), lambda i: (0, 0))],
        out_shape=[jax.ShapeDtypeStruct((S, D), F32), jax.ShapeDtypeStruct((1, D), F32)],
        compiler_params=_params("arbitrary"),
    )(y, target)


def _row_tile(R):
    for t in (256, 128, 64, 32, 16, 8):
        if R % t == 0:
            return t
    return R


def _pair_add_bf16(name, g, b1, kind, c_arr):
    P, Rh, C = b1.shape
    tr = _row_tile(Rh)
    nb = Rh // tr

    def body(c_ref, g_ref, b_ref, o_ref):
        o_ref[...] = (g_ref[...] + b_ref[...]).astype(o_ref.dtype)

    if kind == "cols":
        g_spec = pl.BlockSpec((None, tr, C), lambda p, i, c_ref: (p, c_ref[0] * nb + i, 0))
    else:
        g_spec = pl.BlockSpec((tr, C), lambda p, i, c_ref: ((2 * p + c_ref[0]) * nb + i, 0))
    blk = pl.BlockSpec((None, tr, C), lambda p, i, c_ref: (p, i, 0))
    return pl.pallas_call(
        body, name=name,
        grid_spec=pltpu.PrefetchScalarGridSpec(num_scalar_prefetch=1, grid=(P, nb), in_specs=[g_spec, blk], out_specs=blk),
        out_shape=jax.ShapeDtypeStruct((P, Rh, C), BF16),
        compiler_params=_params("parallel", "parallel"),
    )(c_arr, g, b1)


def _sum_slots(name, b, half_arr=None):
    P, R, C = b.shape
    tr = _row_tile(R)
    nb = R // tr

    def body(*refs):
        b_ref, o_ref = refs[-2:]
        acc = b_ref[0].astype(F32)
        for s in range(1, P):
            acc = acc + b_ref[s].astype(F32)
        o_ref[...] = acc

    if half_arr is None:
        return pl.pallas_call(
            body, name=name, grid=(nb,),
            in_specs=[pl.BlockSpec((P, tr, C), lambda i: (0, i, 0))],
            out_specs=pl.BlockSpec((tr, C), lambda i: (i, 0)),
            out_shape=jax.ShapeDtypeStruct((R, C), F32),
            compiler_params=_params("parallel"),
        )(b)
    return pl.pallas_call(
        body, name=name,
        grid_spec=pltpu.PrefetchScalarGridSpec(
            num_scalar_prefetch=1, grid=(nb,),
            in_specs=[pl.BlockSpec((P, tr, C), lambda i, half: (0, i, 0))],
            out_specs=pl.BlockSpec((tr, C), lambda i, half: (half[0] * nb + i, 0))),
        out_shape=jax.ShapeDtypeStruct((2 * R, C), F32),
        compiler_params=_params("parallel"),
    )(half_arr, b)


def _adamw(name, w, g, m, v):
    R, C = w.shape
    tr = _row_tile(R)
    c1 = 1.0 - ADAM_B1 ** ADAM_STEP
    c2 = 1.0 - ADAM_B2 ** ADAM_STEP

    def body(w_ref, g_ref, m_ref, v_ref, d_ref, nm_ref, nv_ref):
        gg = g_ref[...]
        nm = ADAM_B1 * m_ref[...] + (1.0 - ADAM_B1) * gg
        nv = ADAM_B2 * v_ref[...] + (1.0 - ADAM_B2) * (gg * gg)
        m_hat = nm / c1
        v_hat = nv / c2
        d_ref[...] = -ADAM_LR * (m_hat / (jnp.sqrt(v_hat) + ADAM_EPS) + ADAM_WD * w_ref[...])
        nm_ref[...] = nm
        nv_ref[...] = nv

    spec = pl.BlockSpec((tr, C), lambda i: (i, 0))
    return pl.pallas_call(
        body, name=name, grid=(R // tr,), in_specs=[spec] * 4, out_specs=[spec] * 3,
        out_shape=[jax.ShapeDtypeStruct((R, C), F32)] * 3,
        compiler_params=_params("parallel"),
    )(w, g, m, v)


def _place():
    x, y, c = lax.axis_index("x"), lax.axis_index("y"), lax.axis_index("c")
    other_chips = [(1 - x, y), (x, 1 - y), (1 - x, 1 - y)]
    return x, y, c, other_chips


def _into_slot(name, w, dtype, slot_arr, n_slots):
    R, C = w.shape
    tr = _row_tile(R)

    def body(slot_ref, w_ref, o_ref):
        o_ref[...] = w_ref[...].astype(o_ref.dtype)

    return pl.pallas_call(
        body, name=name,
        grid_spec=pltpu.PrefetchScalarGridSpec(
            num_scalar_prefetch=1, grid=(R // tr,),
            in_specs=[pl.BlockSpec((tr, C), lambda i, slot: (i, 0))],
            out_specs=pl.BlockSpec((None, tr, C), lambda i, slot: (slot[0], i, 0))),
        out_shape=jax.ShapeDtypeStruct((n_slots, R, C), dtype),
        compiler_params=_params("parallel"),
    )(slot_arr, w)


def _gather_chips(bufs):
    n = len(bufs)

    def body(*refs):
        outs = refs[n:2 * n]
        ici_send, ici_recv, d2d_send, d2d_recv = refs[2 * n:]
        x, y, c, chips = _place()
        me = 2 * x + y
        started = []
        for k in range(n):
            rh = outs[k].shape[1] // 2
            mine = outs[k].at[me, pl.ds(c * rh, rh)]
            for j, (px, py) in enumerate(chips):
                cp = pltpu.make_async_remote_copy(
                    src_ref=mine, dst_ref=mine,
                    send_sem=ici_send.at[3 * k + j], recv_sem=ici_recv.at[3 * k + j],
                    device_id=(px, py, c), device_id_type=MESH)
                cp.start()
                started.append(cp)
        for k in range(n):
            rh = outs[k].shape[1] // 2
            for j, (px, py) in enumerate(chips):
                landed = outs[k].at[2 * px + py, pl.ds(c * rh, rh)]
                pltpu.make_async_remote_copy(
                    src_ref=landed, dst_ref=landed,
                    send_sem=ici_send.at[3 * k + j], recv_sem=ici_recv.at[3 * k + j],
                    device_id=(px, py, c), device_id_type=MESH).wait_recv()
                cp = pltpu.make_async_remote_copy(
                    src_ref=landed, dst_ref=landed,
                    send_sem=d2d_send.at[3 * k + j], recv_sem=d2d_recv.at[3 * k + j],
                    device_id=(x, y, 1 - c), device_id_type=MESH)
                cp.start()
                started.append(cp)
        for k in range(n):
            rh = outs[k].shape[1] // 2
            for j, (px, py) in enumerate(chips):
                landed = outs[k].at[2 * px + py, pl.ds((1 - c) * rh, rh)]
                pltpu.make_async_remote_copy(
                    src_ref=landed, dst_ref=landed,
                    send_sem=d2d_send.at[3 * k + j], recv_sem=d2d_recv.at[3 * k + j],
                    device_id=(x, y, 1 - c), device_id_type=MESH).wait_recv()
        for cp in started:
            cp.wait_send()

    return pl.pallas_call(
        body, name="gather_weights",
        in_specs=[ANY] * n, out_specs=[ANY] * n,
        out_shape=[jax.ShapeDtypeStruct(b.shape, b.dtype) for b in bufs],
        input_output_aliases={k: k for k in range(n)},
        scratch_shapes=[pltpu.SemaphoreType.DMA((3 * n,))] * 4,
        compiler_params=pltpu.CompilerParams(has_side_effects=True),
    )(*bufs)


def _half(ref, kind, p, c, rh):
    if kind == "cols":
        return ref.at[p, pl.ds(c * rh, rh)]
    return ref.at[pl.ds((2 * p + c) * rh, rh)]


def _swap_halves(grads, kinds, rhs):
    n = len(grads)

    def body(*refs):
        ins, outs = refs[:n], refs[n:2 * n]
        send_sem, recv_sem = refs[2 * n:]
        x, y, c, _ = _place()
        started = []
        for k in range(n):
            for p in range(N_CHIPS):
                cp = pltpu.make_async_remote_copy(
                    src_ref=_half(ins[k], kinds[k], p, 1 - c, rhs[k]), dst_ref=outs[k].at[p],
                    send_sem=send_sem.at[N_CHIPS * k + p], recv_sem=recv_sem.at[N_CHIPS * k + p],
                    device_id=(x, y, 1 - c), device_id_type=MESH)
                cp.start()
                started.append(cp)
        for cp in started:
            cp.wait()

    out_shape = []
    for g, kind, rh in zip(grads, kinds, rhs):
        out_shape.append(jax.ShapeDtypeStruct((N_CHIPS, rh, g.shape[-1]), g.dtype))
    return pl.pallas_call(
        body, name="grad_swap_halves",
        in_specs=[ANY] * n, out_specs=[ANY] * n, out_shape=out_shape,
        scratch_shapes=[pltpu.SemaphoreType.DMA((N_CHIPS * n,))] * 2,
        compiler_params=pltpu.CompilerParams(has_side_effects=True),
    )(*grads)


def _scatter_partials(parts):
    n = len(parts)

    def body(*refs):
        ins, outs = refs[:n], refs[n:2 * n]
        send_sem, recv_sem, local_sem = refs[2 * n:]
        x, y, c, chips = _place()
        me = 2 * x + y
        started = []
        for k in range(n):
            cp = pltpu.make_async_copy(ins[k].at[me], outs[k].at[me], local_sem.at[k])
            cp.start()
            started.append(cp)
            for j, (px, py) in enumerate(chips):
                cp = pltpu.make_async_remote_copy(
                    src_ref=ins[k].at[2 * px + py], dst_ref=outs[k].at[me],
                    send_sem=send_sem.at[3 * k + j], recv_sem=recv_sem.at[3 * k + j],
                    device_id=(px, py, c), device_id_type=MESH)
                cp.start()
                started.append(cp)
        for k in range(n):
            for j, (px, py) in enumerate(chips):
                landed = outs[k].at[2 * px + py]
                pltpu.make_async_remote_copy(
                    src_ref=landed, dst_ref=landed,
                    send_sem=send_sem.at[3 * k + j], recv_sem=recv_sem.at[3 * k + j],
                    device_id=(px, py, c), device_id_type=MESH).wait_recv()
        for k in range(n):
            started[4 * k].wait()
            for j in range(3):
                started[4 * k + 1 + j].wait_send()

    return pl.pallas_call(
        body, name="grad_scatter_partials",
        in_specs=[ANY] * n, out_specs=[ANY] * n,
        out_shape=[jax.ShapeDtypeStruct(s.shape, s.dtype) for s in parts],
        scratch_shapes=[pltpu.SemaphoreType.DMA((3 * n,))] * 2 + [pltpu.SemaphoreType.DMA((n,))],
        compiler_params=pltpu.CompilerParams(has_side_effects=True),
    )(*parts)


def _join_halves(fulls):
    n = len(fulls)

    def body(*refs):
        outs = refs[n:2 * n]
        send_sem, recv_sem = refs[2 * n:]
        x, y, c, _ = _place()
        started = []
        for k in range(n):
            rh = outs[k].shape[0] // 2
            mine = outs[k].at[pl.ds(c * rh, rh)]
            cp = pltpu.make_async_remote_copy(
                src_ref=mine, dst_ref=mine, send_sem=send_sem.at[k], recv_sem=recv_sem.at[k],
                device_id=(x, y, 1 - c), device_id_type=MESH)
            cp.start()
            started.append(cp)
        for k in range(n):
            rh = outs[k].shape[0] // 2
            theirs = outs[k].at[pl.ds((1 - c) * rh, rh)]
            pltpu.make_async_remote_copy(
                src_ref=theirs, dst_ref=theirs, send_sem=send_sem.at[k], recv_sem=recv_sem.at[k],
                device_id=(x, y, 1 - c), device_id_type=MESH).wait_recv()
        for cp in started:
            cp.wait_send()

    return pl.pallas_call(
        body, name="grad_join_halves",
        in_specs=[ANY] * n, out_specs=[ANY] * n,
        out_shape=[jax.ShapeDtypeStruct(f.shape, f.dtype) for f in fulls],
        input_output_aliases={k: k for k in range(n)},
        scratch_shapes=[pltpu.SemaphoreType.DMA((n,))] * 2,
        compiler_params=pltpu.CompilerParams(has_side_effects=True),
    )(*fulls)


def _gather_all(block):
    def body(in_ref, out_ref, send_sem, recv_sem, local_sem):
        x, y, c, _ = _place()

        def slot(px, py, pc):
            return out_ref.at[4 * px + 2 * py + pc]

        loc = pltpu.make_async_copy(in_ref, slot(x, y, c), local_sem)
        loc.start()
        started = []
        for d in range(1, N_DEV):
            fx, fy, fc = d >> 2, (d >> 1) & 1, d & 1
            cp = pltpu.make_async_remote_copy(
                src_ref=in_ref, dst_ref=slot(x, y, c), send_sem=send_sem.at[d - 1], recv_sem=recv_sem.at[d - 1],
                device_id=(x ^ fx, y ^ fy, c ^ fc), device_id_type=MESH)
            cp.start()
            started.append(cp)
        for d in range(1, N_DEV):
            fx, fy, fc = d >> 2, (d >> 1) & 1, d & 1
            landed = slot(x ^ fx, y ^ fy, c ^ fc)
            pltpu.make_async_remote_copy(
                src_ref=in_ref, dst_ref=landed, send_sem=send_sem.at[d - 1], recv_sem=recv_sem.at[d - 1],
                device_id=(x ^ fx, y ^ fy, c ^ fc), device_id_type=MESH).wait_recv()
        for cp in started:
            cp.wait_send()
        loc.wait()

    return pl.pallas_call(
        body, name="gather_small_grads",
        in_specs=[ANY], out_specs=ANY,
        out_shape=jax.ShapeDtypeStruct((N_DEV,) + block.shape, block.dtype),
        scratch_shapes=[pltpu.SemaphoreType.DMA((N_DEV - 1,))] * 2 + [pltpu.SemaphoreType.DMA(())],
        compiler_params=pltpu.CompilerParams(has_side_effects=True),
    )(block)


def _pack(pieces):
    flat = jnp.concatenate([p.reshape(-1) for p in pieces])
    n = flat.shape[0]
    padded = -(-n // (8 * LANES)) * (8 * LANES)
    return jnp.pad(flat, (0, padded - n)).reshape(-1, LANES)


def _unpack(packed, shapes):
    flat = packed.reshape(-1)
    out, at = [], 0
    for s in shapes:
        n = math.prod(s)
        out.append(flat[at:at + n].reshape(s))
        at += n
    return out


def _heads(a, H, Ta):
    T = a.shape[0]
    return jnp.pad(a.reshape(T, H, HEAD_DIM).transpose(1, 0, 2), ((0, 0), (0, Ta - T), (0, 0)))


def _unheads(a, T):
    H, _, dh = a.shape
    return a[:, :T].transpose(1, 0, 2).reshape(T, H * dh)


def kernel(x, meta_tokens, pre_mix_g, w_in, gate_b, dw_w, dw_b, conv_ln_g, conv_ln_b, w_conv_out, w_attn_out, w_o, post_mix_g, pre_ffn_g, w_ffn_in, w_ffn_out, post_ffn_g, loss_target, m_meta_tokens, m_pre_mix_g, m_w_in, m_gate_b, m_dw_w, m_dw_b, m_conv_ln_g, m_conv_ln_b, m_w_conv_out, m_w_attn_out, m_w_o, m_post_mix_g, m_pre_ffn_g, m_w_ffn_in, m_w_ffn_out, m_post_ffn_g, v_meta_tokens, v_pre_mix_g, v_w_in, v_gate_b, v_dw_w, v_dw_b, v_conv_ln_g, v_conv_ln_b, v_w_conv_out, v_w_attn_out, v_w_o, v_post_mix_g, v_pre_ffn_g, v_w_ffn_in, v_w_ffn_out, v_post_ffn_g):
    S, D = x.shape[1], x.shape[2]
    L = S + N_META
    T = -(-L // ROW_BLOCK) * ROW_BLOCK
    Ta = -(-L // ATT_BLOCK) * ATT_BLOCK
    tm = _tile(T, MM_ROWS)
    ts = _tile(T, STAGE_ROWS)
    tw = _tile(T, WIDE_STAGE_ROWS)
    H = D // HEAD_DIM
    F = w_ffn_out.shape[1] * N_CHIPS
    Dc = D // N_CHIPS
    P = N_CHIPS
    me = 2 * lax.axis_index("x") + lax.axis_index("y")
    c_arr = lax.axis_index("c").astype(jnp.int32).reshape(1)

    dw_w_pad = jnp.pad(dw_w[0], ((0, CONV_PAD - CONV_WIDTH), (0, 0)))
    me_arr = me.astype(jnp.int32).reshape(1)
    to_gather = [("w_in", w_in[0], BF16), ("w_conv_out", w_conv_out[0], BF16), ("w_attn_out", w_attn_out[0], BF16),
                 ("w_o", w_o[0], BF16), ("w_ffn_in", w_ffn_in[0], BF16), ("w_ffn_out", w_ffn_out[0], BF16),
                 ("meta", meta_tokens, F32), ("taps", dw_w_pad, F32)]
    gathered = _gather_chips([_into_slot("slot_" + n, w, dt, me_arr, P) for n, w, dt in to_gather])
    win3, wfi3 = gathered[0], gathered[4]
    wco, wao, wo = (gathered[k].reshape(D, D) for k in (1, 2, 3))
    wfo = gathered[5].reshape(F, D)
    meta_full = gathered[6].transpose(1, 0, 2).reshape(N_META, D)
    taps = gathered[7].transpose(1, 0, 2).reshape(CONV_PAD, D)[:CONV_WIDTH]

    h0 = jnp.concatenate([meta_full, x[0], jnp.zeros((T - L, D), F32)], axis=0)
    (u1,) = _rowwise_fwd("rms_pre_mix", f_rms, [(h0, D, 0)], [(pre_mix_g, D, 0)], [(D, BF16)], T, ts)
    p = _mm_nn_cols("mm_in", u1, win3, tm)
    (uglu,) = _rowwise_fwd("glu", f_glu, [(p, D, 0), (p, D, 1)], [], [(D, F32)], T, ts)
    yc = _shift_conv("dwconv", uglu, taps, CONV_PAD, CONV_PAD - (CONV_WIDTH - 1), T)
    conv_pars = [(dw_b, D, 0), (conv_ln_g, D, 0), (conv_ln_b, D, 0)]
    (ys,) = _rowwise_fwd("conv_post", f_convpost, [(yc, D, 0)], conv_pars, [(D, BF16)], T, ts)
    y_conv = _mm_nn("mm_conv_out", ys, wco, tm)
    qkv_scale = (1.0 / math.sqrt(HEAD_DIM), 1.0, 1.0)
    q, k, v = (_heads((p[:, (2 + n) * D:(3 + n) * D] * qkv_scale[n]).astype(BF16), H, Ta) for n in range(3))
    o, rtot = _attn_fwd(q, k, v)
    o2 = _unheads(o, T)
    y_attn = _mm_nn("mm_attn_out", o2, wao, tm)
    mix_rows = [(p, D, 5), (p, D, 6), (y_conv, D, 0), (y_attn, D, 0)]
    mix_pars = [(gate_b, D, 0), (gate_b, D, 1)]
    (mixin,) = _rowwise_fwd("gate_mix", f_mix, mix_rows, mix_pars, [(D, BF16)], T, ts)
    mix = _mm_nn("mm_o", mixin, wo, tm)
    (h1,) = _rowwise_fwd("res_post_mix", f_res_rms, [(h0, D, 0), (mix, D, 0)], [(post_mix_g, D, 0)], [(D, F32)], T, ts)
    (u2,) = _rowwise_fwd("rms_pre_ffn", f_rms, [(h1, D, 0)], [(pre_ffn_g, D, 0)], [(D, BF16)], T, ts)
    ab = _mm_nn_cols("mm_ffn_in", u2, wfi3, tm)
    (fin,) = _rowwise_fwd("swiglu", f_swiglu, [(ab, F, 0), (ab, F, 1)], [], [(F, BF16)], T, tw)
    f = _mm_nn("mm_ffn_out", fin, wfo, tm)
    (h2,) = _rowwise_fwd("res_post_ffn", f_res_rms, [(h1, D, 0), (f, D, 0)], [(post_ffn_g, D, 0)], [(D, F32)], T, ts)

    dy, part = _loss_head(h2[N_META:L], loss_target[0], _row_tile(S))
    loss = lax.psum(0.5 * jnp.sum(part) / D, ("x", "y", "c"))
    dh2 = jnp.pad(dy, ((N_META, T - L), (0, 0)))

    (df,), (g_post_ffn,) = _rowwise_bwd("res_post_ffn_bwd", f_res_rms, [(h1, D, 0), (f, D, 0)], [(post_ffn_g, D, 0)],
                                        [(dh2, D, 0)], [None, BF16], T, ts)
    dfin = _mm_nt("mm_ffn_out_dx", df, wfo, tm)
    g_wfo = _mm_tn("mm_ffn_out_dw", fin, df, tm, 2)
    (da, db), _ = _rowwise_bwd("swiglu_bwd", f_swiglu, [(ab, F, 0), (ab, F, 1)], [], [(dfin, F, 0)], [BF16, BF16], T, tw)
    dab = jnp.concatenate([da, db], axis=1)
    du2 = _mm_nt_cols("mm_ffn_in_dx", dab, wfi3, tm)
    g_wfi = _mm_tn_cols("mm_ffn_in_dw", u2, dab, tm, P)
    (dh1,), (g_pre_ffn,) = _rowwise_bwd("rms_pre_ffn_bwd", f_rms_id, [(h1, D, 0)], [(pre_ffn_g, D, 0)],
                                        [(du2, D, 0), (dh2, D, 0)], [F32], T, ts)
    (dmix,), (g_post_mix,) = _rowwise_bwd("res_post_mix_bwd", f_res_rms, [(h0, D, 0), (mix, D, 0)], [(post_mix_g, D, 0)],
                                          [(dh1, D, 0)], [None, BF16], T, ts)
    dmixin = _mm_nt("mm_o_dx", dmix, wo, tm)
    g_wo = _mm_tn("mm_o_dw", mixin, dmix, tm, 1)
    (dpc, dpa, dyconv, dyattn), (g_gate_c, g_gate_a) = _rowwise_bwd(
        "gate_mix_bwd", f_mix, mix_rows, mix_pars, [(dmixin, D, 0)], [BF16, BF16, BF16, BF16], T, ts)
    g_wco = _mm_tn("mm_conv_out_dw", ys, dyconv, tm, 1)
    dys = _mm_nt("mm_conv_out_dx", dyconv, wco, tm)
    g_wao = _mm_tn("mm_attn_out_dw", o2, dyattn, tm, 1)
    do2 = _mm_nt("mm_attn_out_dx", dyattn, wao, tm, BF16)
    (dyc,), (g_dw_b, g_ln_g, g_ln_b) = _rowwise_bwd("conv_post_bwd", f_convpost, [(yc, D, 0)], conv_pars,
                                                    [(dys, D, 0)], [F32], T, ts)
    duglu = _shift_conv("dwconv_dx", dyc, taps[::-1], 0, 0, T)
    g_taps = _conv_dw("dwconv_dw", uglu, dyc, T)
    (dp0, dp1), _ = _rowwise_bwd("glu_bwd", f_glu, [(p, D, 0), (p, D, 1)], [], [(duglu, D, 0)], [BF16, BF16], T, ts)
    dq, dk, dv = _attn_bwd(q, k, v, _heads(do2, H, Ta), rtot)
    dp = jnp.concatenate([dp0, dp1] + [_unheads(t, T).astype(BF16) for t in (dq, dk, dv)] + [dpc, dpa], axis=1)
    du1 = _mm_nt_cols("mm_in_dx", dp, win3, tm)
    g_win = _mm_tn_cols("mm_in_dw", u1, dp, tm, P)
    (dh0,), (g_pre_mix,) = _rowwise_bwd("rms_pre_mix_bwd", f_rms_id, [(h0, D, 0)], [(pre_mix_g, D, 0)],
                                        [(du1, D, 0), (dh1, D, 0)], [F32], T, ts)
    grad_x = dh0[N_META:L][None]

    big = [g_win, g_wco, g_wao, g_wo, g_wfi, g_wfo]
    kinds = ["cols", "rows", "rows", "rows", "cols", "rows"]
    rhs = [(g.shape[1] if kind == "cols" else g.shape[0] // P) // 2 for g, kind in zip(big, kinds)]
    from_sibling = _swap_halves(big, kinds, rhs)
    parts = [_pair_add_bf16("grad_pair_add_%d" % n, g, b1, kind, c_arr)
             for n, (g, b1, kind) in enumerate(zip(big, from_sibling, kinds))]
    slots = _scatter_partials(parts)
    g_big = _join_halves([_sum_slots("grad_chip_sum_%d" % n, s, c_arr) for n, s in enumerate(slots)])

    small_shapes = [(1, D), (1, D), (1, D), (CONV_WIDTH, D), (1, D), (1, D), (1, D), (1, D), (1, D), (1, D), (N_META, D)]
    small = _pack([g_pre_mix, g_gate_c, g_gate_a, g_taps, g_dw_b, g_ln_g, g_ln_b, g_post_mix, g_pre_ffn, g_post_ffn,
                   dh0[:N_META]])
    summed = _sum_slots("small_grad_sum", _gather_all(small))
    (s_pre_mix, s_gate_c, s_gate_a, s_taps, s_dw_b, s_ln_g, s_ln_b, s_post_mix, s_pre_ffn, s_post_ffn,
     s_meta) = _unpack(summed, small_shapes)
    s_gate_b = jnp.concatenate([s_gate_c, s_gate_a], axis=1)
    s_taps = lax.dynamic_slice_in_dim(s_taps, me * Dc, Dc, axis=1)[None]
    s_meta = lax.dynamic_slice_in_dim(s_meta, me * Dc, Dc, axis=1)

    grads = {
        "meta_tokens": s_meta, "pre_mix_g": s_pre_mix, "w_in": g_big[0][None], "gate_b": s_gate_b, "dw_w": s_taps,
        "dw_b": s_dw_b, "conv_ln_g": s_ln_g, "conv_ln_b": s_ln_b, "w_conv_out": g_big[1][None],
        "w_attn_out": g_big[2][None], "w_o": g_big[3][None], "post_mix_g": s_post_mix, "pre_ffn_g": s_pre_ffn,
        "w_ffn_in": g_big[4][None], "w_ffn_out": g_big[5][None], "post_ffn_g": s_post_ffn,
    }
    weights = {
        "meta_tokens": (meta_tokens, m_meta_tokens, v_meta_tokens), "pre_mix_g": (pre_mix_g, m_pre_mix_g, v_pre_mix_g),
        "w_in": (w_in, m_w_in, v_w_in), "gate_b": (gate_b, m_gate_b, v_gate_b), "dw_w": (dw_w, m_dw_w, v_dw_w),
        "dw_b": (dw_b, m_dw_b, v_dw_b), "conv_ln_g": (conv_ln_g, m_conv_ln_g, v_conv_ln_g),
        "conv_ln_b": (conv_ln_b, m_conv_ln_b, v_conv_ln_b), "w_conv_out": (w_conv_out, m_w_conv_out, v_w_conv_out),
        "w_attn_out": (w_attn_out, m_w_attn_out, v_w_attn_out), "w_o": (w_o, m_w_o, v_w_o),
        "post_mix_g": (post_mix_g, m_post_mix_g, v_post_mix_g), "pre_ffn_g": (pre_ffn_g, m_pre_ffn_g, v_pre_ffn_g),
        "w_ffn_in": (w_ffn_in, m_w_ffn_in, v_w_ffn_in), "w_ffn_out": (w_ffn_out, m_w_ffn_out, v_w_ffn_out),
        "post_ffn_g": (post_ffn_g, m_post_ffn_g, v_post_ffn_g),
    }
    names = list(weights)
    big_names = ["w_in", "w_conv_out", "w_attn_out", "w_o", "w_ffn_in", "w_ffn_out"]
    small_names = [n for n in names if n not in big_names]

    delta, new_m, new_v = {}, {}, {}
    for n in big_names:
        w, m, v2 = weights[n]
        d, nm, nv = _adamw("adamw_" + n, w[0], grads[n][0], m[0], v2[0])
        delta[n], new_m[n], new_v[n] = d[None], nm[None], nv[None]
    shapes = [weights[n][0].shape for n in small_names]
    packed = [_pack([weights[n][k] for n in small_names]) for k in range(3)]
    d, nm, nv = _adamw("adamw_small", packed[0], _pack([grads[n] for n in small_names]), packed[1], packed[2])
    for n, dd, mm, vv in zip(small_names, _unpack(d, shapes), _unpack(nm, shapes), _unpack(nv, shapes)):
        delta[n], new_m[n], new_v[n] = dd, mm, vv

    return (loss, grad_x, *[grads[n].reshape(weights[n][0].shape) for n in names], *[delta[n] for n in names],
            *[new_m[n] for n in names], *[new_v[n] for n in names])
```

```python
import math

import jax
import jax.numpy as jnp
from jax import lax
from jax.experimental import pallas as pl
from jax.experimental.pallas import tpu as pltpu

F32 = jnp.float32
BF16 = jnp.bfloat16

N_META = 16
CONV_WIDTH = 31
CONV_PAD = 32
HEAD_DIM = 64
RMS_EPS = 1e-6
LN_EPS = 1e-5
ROW_BLOCK = 128
ATT_BLOCK = 256
ATT_HEADS = 4
LANES = 128
N_CHIPS = 4
N_DEV = 8
MM_ROWS = 544
STAGE_ROWS = 272
WIDE_STAGE_ROWS = 128
VMEM_LIMIT = 56 * 1024 * 1024

ADAM_LR = 0.001
ADAM_B1 = 0.9
ADAM_B2 = 0.999
ADAM_EPS = 1e-08
ADAM_WD = 0.01
ADAM_STEP = 10

MESH = pl.DeviceIdType.MESH
ANY = pl.BlockSpec(memory_space=pl.ANY)


def _params(*sem):
    return pltpu.CompilerParams(dimension_semantics=sem if sem else None, vmem_limit_bytes=VMEM_LIMIT)


def _rms(x, g):
    return x * lax.rsqrt(jnp.mean(x * x, axis=-1, keepdims=True) + RMS_EPS) * g


def f_rms(h, g):
    return (_rms(h, g),)


def f_rms_id(h, g):
    return (_rms(h, g), h)


def f_res_rms(h, m, g):
    return (h + _rms(m, g),)


def f_glu(a, gate):
    return (a * lax.logistic(gate),)


def f_convpost(yc, b, ln_g, ln_b):
    y = yc + b
    mu = jnp.mean(y, axis=-1, keepdims=True)
    xc = y - mu
    var = jnp.mean(xc * xc, axis=-1, keepdims=True)
    yl = xc * lax.rsqrt(var + LN_EPS) * ln_g + ln_b
    return (yl * lax.logistic(yl),)


def f_mix(pc, pa, yc, ya, bc, ba):
    return (lax.logistic(pc + bc) * yc + lax.logistic(pa + ba) * ya,)


def f_swiglu(a, b):
    return (a * lax.logistic(a) * b,)


def _tile(T, target):
    return max(t for t in range(16, target + 1, 16) if T % t == 0)


def _row_map(j):
    return lambda i: (i, j)


def _par_map(j):
    return lambda i: (0, j)


def _rowwise_fwd(name, f, rows, pars, outs, T, tm):
    n_in = len(rows) + len(pars)

    def body(*refs):
        vals = [r[...].astype(F32) for r in refs[:n_in]]
        res = f(*vals)
        for o_ref, o in zip(refs[n_in:], res):
            o_ref[...] = o.astype(o_ref.dtype)

    in_specs = [pl.BlockSpec((tm, w), _row_map(j)) for _, w, j in rows]
    in_specs += [pl.BlockSpec((1, w), _par_map(j)) for _, w, j in pars]
    return pl.pallas_call(
        body, name=name, grid=(T // tm,),
        in_specs=in_specs,
        out_specs=[pl.BlockSpec((tm, w), _row_map(0)) for w, _ in outs],
        out_shape=[jax.ShapeDtypeStruct((T, w), dt) for w, dt in outs],
        compiler_params=_params("parallel"),
    )(*[a for a, _, _ in rows], *[a for a, _, _ in pars])


def _rowwise_bwd(name, f, rows, pars, cots, drow_dtypes, T, tm):
    n_r, n_p, n_c = len(rows), len(pars), len(cots)
    n_in = n_r + n_p + n_c
    keep = [k for k, dt in enumerate(drow_dtypes) if dt is not None]

    def body(*refs):
        rv = [r[...].astype(F32) for r in refs[:n_r]]
        pv = [r[...].astype(F32) for r in refs[n_r:n_r + n_p]]
        cv = [r[...].astype(F32) for r in refs[n_r + n_p:n_in]]
        _, vjp = jax.vjp(f, *rv, *pv)
        g = vjp(tuple(cv))
        drow_refs = refs[n_in:n_in + len(keep)]
        dpar_refs = refs[n_in + len(keep):]
        for r, k in zip(drow_refs, keep):
            r[...] = g[k].astype(r.dtype)

        @pl.when(pl.program_id(0) == 0)
        def _():
            for r in dpar_refs:
                r[...] = jnp.zeros_like(r)

        for r, gp in zip(dpar_refs, g[n_r:]):
            r[...] += gp

    in_specs = [pl.BlockSpec((tm, w), _row_map(j)) for _, w, j in rows]
    in_specs += [pl.BlockSpec((1, w), _par_map(j)) for _, w, j in pars]
    in_specs += [pl.BlockSpec((tm, w), _row_map(j)) for _, w, j in cots]
    out_specs = [pl.BlockSpec((tm, rows[k][1]), _row_map(0)) for k in keep]
    out_specs += [pl.BlockSpec((1, w), _par_map(0)) for _, w, _ in pars]
    out_shape = [jax.ShapeDtypeStruct((T, rows[k][1]), drow_dtypes[k]) for k in keep]
    out_shape += [jax.ShapeDtypeStruct((1, w), F32) for _, w, _ in pars]
    res = pl.pallas_call(
        body, name=name, grid=(T // tm,),
        in_specs=in_specs, out_specs=out_specs, out_shape=out_shape,
        compiler_params=_params("arbitrary"),
    )(*[a for a, _, _ in rows], *[a for a, _, _ in pars], *[a for a, _, _ in cots])
    return res[:len(keep)], res[len(keep):]


NN = (((1,), (0,)), ((), ()))
NT = (((1,), (1,)), ((), ()))
TN = (((0,), (0,)), ((), ()))


def _mm(name, a, b, dims, out_shape, grid, a_spec, b_spec, o_spec, red_axis=None, init=None):
    n_red = None if red_axis is None else grid[red_axis]

    def body(a_ref, b_ref, *rest):
        o_ref = rest[-1]
        prod = lax.dot_general(a_ref[...], b_ref[...], dims, preferred_element_type=F32)
        if n_red is None:
            o_ref[...] = prod.astype(o_ref.dtype)
        else:
            @pl.when(pl.program_id(red_axis) == 0)
            def _():
                o_ref[...] = prod

            @pl.when(pl.program_id(red_axis) > 0)
            def _():
                o_ref[...] += prod

    sem = ["parallel"] * len(grid)
    if red_axis is not None:
        sem[red_axis] = "arbitrary"
    if init is None:
        return pl.pallas_call(
            body, name=name, grid=grid, in_specs=[a_spec, b_spec], out_specs=o_spec, out_shape=out_shape,
            compiler_params=_params(*sem),
        )(a, b)
    return pl.pallas_call(
        body, name=name, grid=grid, in_specs=[a_spec, b_spec, ANY], out_specs=o_spec, out_shape=out_shape,
        input_output_aliases={2: 0}, compiler_params=_params(*sem),
    )(a, b, init)


def _mm_nn(name, a, w, tm, out_dtype=F32, rows=None):
    T, K = a.shape
    T = rows or T
    N = w.shape[1]
    return _mm(name, a, w, NN, jax.ShapeDtypeStruct((T, N), out_dtype), (T // tm,),
               pl.BlockSpec((tm, K), lambda i: (i, 0)), pl.BlockSpec((K, N), lambda i: (0, 0)),
               pl.BlockSpec((tm, N), lambda i: (i, 0)))


def _mm_nt(name, a, w, tm, out_dtype=F32, out_rows=None):
    T, N = a.shape
    K = w.shape[0]
    init = None if out_rows is None else jnp.zeros((out_rows, K), out_dtype)
    return _mm(name, a, w, NT, jax.ShapeDtypeStruct((out_rows or T, K), out_dtype), (T // tm,),
               pl.BlockSpec((tm, N), lambda i: (i, 0)), pl.BlockSpec((K, N), lambda i: (0, 0)),
               pl.BlockSpec((tm, K), lambda i: (i, 0)), init=init)


def _mm_tn(name, a, b, tm, n_row_blocks):
    K = a.shape[1]
    T, N = b.shape
    kb = K // n_row_blocks
    return _mm(name, a, b, TN, jax.ShapeDtypeStruct((K, N), F32), (n_row_blocks, T // tm),
               pl.BlockSpec((tm, kb), lambda r, t: (t, r)), pl.BlockSpec((tm, N), lambda r, t: (t, 0)),
               pl.BlockSpec((kb, N), lambda r, t: (r, 0)), red_axis=1)


def _mm_nn_cols(name, a, w3, tm):
    T, K = a.shape
    P, _, Ns = w3.shape
    return _mm(name, a, w3, NN, jax.ShapeDtypeStruct((T, P * Ns), F32), (P, T // tm),
               pl.BlockSpec((tm, K), lambda p, i: (i, 0)), pl.BlockSpec((None, K, Ns), lambda p, i: (p, 0, 0)),
               pl.BlockSpec((tm, Ns), lambda p, i: (i, p)))


def _mm_nt_cols(name, a, w3, tm):
    T = a.shape[0]
    P, K, Ns = w3.shape
    return _mm(name, a, w3, NT, jax.ShapeDtypeStruct((T, K), F32), (T // tm, P),
               pl.BlockSpec((tm, Ns), lambda i, p: (i, p)), pl.BlockSpec((None, K, Ns), lambda i, p: (p, 0, 0)),
               pl.BlockSpec((tm, K), lambda i, p: (i, 0)), red_axis=1)


def _mm_tn_cols(name, a, b, tm, P):
    T, K = a.shape
    Ns = b.shape[1] // P
    return _mm(name, a, b, TN, jax.ShapeDtypeStruct((P, K, Ns), F32), (P, T // tm),
               pl.BlockSpec((tm, K), lambda p, t: (t, 0)), pl.BlockSpec((tm, Ns), lambda p, t: (t, p)),
               pl.BlockSpec((None, K, Ns), lambda p, t: (p, 0, 0)), red_axis=1)


def _shift_conv(name, x, w, place, off, T):
    C = x.shape[1]
    tb = ROW_BLOCK
    zero_at = 0 if place else T

    def body(x_ref, w_ref, o_ref, xp_ref):
        xp_ref[pl.ds(zero_at, CONV_PAD), :] = jnp.zeros((CONV_PAD, LANES), F32)
        xp_ref[pl.ds(place, T), :] = x_ref[...]

        def step(t, carry):
            base = pl.multiple_of(t * tb, tb)
            win = xp_ref[pl.ds(base, tb + CONV_PAD), :]
            acc = jnp.zeros((tb, LANES), F32)
            for j in range(CONV_WIDTH):
                acc = acc + win[off + j:off + j + tb, :] * w_ref[pl.ds(j, 1), :]
            o_ref[pl.ds(base, tb), :] = acc
            return carry

        lax.fori_loop(0, T // tb, step, 0)

    return pl.pallas_call(
        body, name=name, grid=(C // LANES,),
        in_specs=[pl.BlockSpec((T, LANES), lambda c: (0, c)), pl.BlockSpec((CONV_WIDTH, LANES), lambda c: (0, c))],
        out_specs=pl.BlockSpec((T, LANES), lambda c: (0, c)),
        out_shape=jax.ShapeDtypeStruct((T, C), F32),
        scratch_shapes=[pltpu.VMEM((T + CONV_PAD, LANES), F32)],
        compiler_params=_params("parallel"),
    )(x, w)


def _conv_dw(name, x, dy, T):
    C = x.shape[1]
    tb = ROW_BLOCK
    off = CONV_PAD - (CONV_WIDTH - 1)

    def body(x_ref, dy_ref, o_ref, xp_ref, acc_ref):
        xp_ref[pl.ds(0, CONV_PAD), :] = jnp.zeros((CONV_PAD, LANES), F32)
        xp_ref[pl.ds(CONV_PAD, T), :] = x_ref[...]
        acc_ref[...] = jnp.zeros_like(acc_ref)

        def step(t, carry):
            base = pl.multiple_of(t * tb, tb)
            win = xp_ref[pl.ds(base, tb + CONV_PAD), :]
            d = dy_ref[pl.ds(base, tb), :]
            for j in range(CONV_WIDTH):
                prod = win[off + j:off + j + tb, :] * d
                acc_ref[j] += jnp.sum(prod.reshape(tb // 8, 8, LANES), axis=0)
            return carry

        lax.fori_loop(0, T // tb, step, 0)
        for j in range(CONV_WIDTH):
            o_ref[pl.ds(j, 1), :] = jnp.sum(acc_ref[j], axis=0, keepdims=True)

    return pl.pallas_call(
        body, name=name, grid=(C // LANES,),
        in_specs=[pl.BlockSpec((T, LANES), lambda c: (0, c)), pl.BlockSpec((T, LANES), lambda c: (0, c))],
        out_specs=pl.BlockSpec((CONV_WIDTH, LANES), lambda c: (0, c)),
        out_shape=jax.ShapeDtypeStruct((CONV_WIDTH, C), F32),
        scratch_shapes=[pltpu.VMEM((T + CONV_PAD, LANES), F32), pltpu.VMEM((CONV_WIDTH, 8, LANES), F32)],
        compiler_params=_params("parallel"),
    )(x, dy)


def _dot(a, b, dims=NN):
    return lax.dot_general(a, b, dims, preferred_element_type=F32)


def _tri_cumsum(x, tri):
    return _dot(x.astype(BF16), tri)


def _qkv_split(p, D, T, Ta):
    tb = ROW_BLOCK
    nt = T // tb
    scale = 1.0 / math.sqrt(HEAD_DIM)

    def body(q_ref, k_ref, v_ref, qo_ref, ko_ref, vo_ref):
        live = pl.program_id(0) < nt
        qo_ref[...] = jnp.where(live, q_ref[...] * scale, 0.0).astype(BF16)
        ko_ref[...] = jnp.where(live, k_ref[...], 0.0).astype(BF16)
        vo_ref[...] = jnp.where(live, v_ref[...], 0.0).astype(BF16)

    def col(n):
        return lambda i: (jnp.minimum(i, nt - 1), n)

    return pl.pallas_call(
        body, name="qkv_split", grid=(Ta // tb,),
        in_specs=[pl.BlockSpec((tb, D), col(2 + n)) for n in range(3)],
        out_specs=[pl.BlockSpec((tb, D), lambda i: (i, 0))] * 3,
        out_shape=[jax.ShapeDtypeStruct((Ta, D), BF16)] * 3,
        compiler_params=_params("parallel"),
    )(p, p, p)


def _pair_lanes(g):
    return slice((g // 2) * LANES, (g // 2 + 1) * LANES)


def _own_lanes(x_ref, B, G):
    first = lax.broadcasted_iota(jnp.int32, (B, LANES), 1) < HEAD_DIM
    out = []
    for g in range(G):
        x2 = x_ref[:, _pair_lanes(g)]
        out.append(jnp.where(first if g % 2 == 0 else jnp.logical_not(first), x2, jnp.zeros_like(x2)))
    return first, out


def _attn_fwd(q, k, v):
    T, D = q.shape
    H = D // HEAD_DIM
    B = ATT_BLOCK
    G = ATT_HEADS
    W = G // 2 * LANES

    def body(q_ref, k_ref, v_ref, o_ref, rt_ref):
        i = pl.program_id(1)
        row = lax.broadcasted_iota(jnp.int32, (B, B), 0)
        col = lax.broadcasted_iota(jnp.int32, (B, B), 1)
        below = col < row
        tri = (row >= col).astype(BF16)
        first, qs = _own_lanes(q_ref, B, G)

        def tile(j, carry, diagonal):
            sl = pl.ds(pl.multiple_of(j * B, B), B)
            out = []
            for g in range(G):
                c, acc = carry[g]
                z = _dot(qs[g], k_ref[sl, _pair_lanes(g)], NT)
                sp = jnp.maximum(z, 0.0) + jnp.log(1.0 + jnp.exp(-jnp.abs(z)))
                if diagonal:
                    sp = jnp.where(below, sp, 0.0)
                rw = _tri_cumsum(sp, tri)
                a = jnp.exp(z - (rw + c))
                if diagonal:
                    a = jnp.where(below, a, 0.0)
                acc = acc + _dot(a.astype(BF16), v_ref[sl, _pair_lanes(g)])
                out.append((c + rw[:, 0:1], acc))
            return tuple(out)

        carry = tile(i, tuple((jnp.zeros((B, 1), F32), jnp.zeros((B, LANES), F32)) for _ in range(G)), True)
        carry = lax.fori_loop(0, i, lambda jj, cr: tile(i - 1 - jj, cr, False), carry)
        for g in range(0, G, 2):
            o_ref[:, _pair_lanes(g)] = jnp.where(first, carry[g][1], carry[g + 1][1]).astype(o_ref.dtype)
        for g in range(G):
            rt_ref[g] = carry[g][0]

    return pl.pallas_call(
        body, name="attn_fwd", grid=(H // G, T // B),
        in_specs=[pl.BlockSpec((B, W), lambda h, i: (i, h)),
                  pl.BlockSpec((T, W), lambda h, i: (0, h)),
                  pl.BlockSpec((T, W), lambda h, i: (0, h))],
        out_specs=[pl.BlockSpec((B, W), lambda h, i: (i, h)),
                   pl.BlockSpec((G, B, 1), lambda h, i: (h, i, 0))],
        out_shape=[jax.ShapeDtypeStruct((T, D), BF16), jax.ShapeDtypeStruct((H, T, 1), F32)],
        compiler_params=_params("parallel", "arbitrary"),
    )(q, k, v)


def _attn_bwd(q, k, v, do, rt):
    T, D = q.shape
    H = D // HEAD_DIM
    B = ATT_BLOCK
    nq = T // B
    scale = 1.0 / math.sqrt(HEAD_DIM)
    G = ATT_HEADS
    W = G // 2 * LANES

    def body(q_ref, k_ref, v_ref, do_ref, rt_ref, dq_ref, dk_ref, dv_ref, dk_acc, dv_acc):
        i = pl.program_id(1)

        @pl.when(i == 0)
        def _():
            dk_acc[...] = jnp.zeros_like(dk_acc)
            dv_acc[...] = jnp.zeros_like(dv_acc)

        row = lax.broadcasted_iota(jnp.int32, (B, B), 0)
        col = lax.broadcasted_iota(jnp.int32, (B, B), 1)
        below = col < row
        tri = (row <= col).astype(BF16)
        first, qs = _own_lanes(q_ref, B, G)
        _, dos = _own_lanes(do_ref, B, G)

        def tile(j, carry, diagonal):
            sl = pl.ds(pl.multiple_of(j * B, B), B)
            out = []
            for h in range(G):
                pc, gc, dq = carry[h]
                qi, doi = qs[h], dos[h]
                kj, vj = k_ref[sl, _pair_lanes(h)], v_ref[sl, _pair_lanes(h)]
                z = _dot(qi, kj, NT)
                e = jnp.exp(-jnp.abs(z))
                inv = 1.0 / (1.0 + e)
                sp = jnp.maximum(z, 0.0) - jnp.log(inv)
                sg = jnp.where(z >= 0.0, inv, e * inv)
                if diagonal:
                    sp = jnp.where(below, sp, 0.0)
                pw = _tri_cumsum(sp, tri)
                a = jnp.exp(z - (rt_ref[h] - pc - pw + sp))
                if diagonal:
                    a = jnp.where(below, a, 0.0)
                g = a * _dot(doi, vj, NT)
                gw = _tri_cumsum(g, tri)
                dz = g - sg * (gc + gw)
                if diagonal:
                    dz = jnp.where(below, dz, 0.0)
                dzb = dz.astype(BF16)
                dq = dq + _dot(dzb, kj)
                dk_acc[sl, _pair_lanes(h)] += _dot(dzb, qi, TN)
                dv_acc[sl, _pair_lanes(h)] += _dot(a.astype(BF16), doi, TN)
                out.append((pc + pw[:, B - 1:B], gc + gw[:, B - 1:B], dq))
            return tuple(out)

        zero = jnp.zeros((B, 1), F32)
        carry = lax.fori_loop(0, i, lambda j, cr: tile(j, cr, False),
                              tuple((zero, zero, jnp.zeros((B, LANES), F32)) for _ in range(G)))
        carry = tile(i, carry, True)
        for h in range(0, G, 2):
            dq_ref[:, _pair_lanes(h)] = (jnp.where(first, carry[h][2], carry[h + 1][2]) * scale).astype(dq_ref.dtype)

        @pl.when(i == nq - 1)
        def _():
            dk_ref[...] = dk_acc[...].astype(dk_ref.dtype)
            dv_ref[...] = dv_acc[...].astype(dv_ref.dtype)

    blk = pl.BlockSpec((B, W), lambda h, i: (i, h))
    full = pl.BlockSpec((T, W), lambda h, i: (0, h))
    return pl.pallas_call(
        body, name="attn_bwd", grid=(H // G, nq),
        in_specs=[blk, full, full, blk, pl.BlockSpec((G, B, 1), lambda h, i: (h, i, 0))],
        out_specs=[blk, full, full],
        out_shape=[jax.ShapeDtypeStruct((T, D), BF16)] * 3,
        scratch_shapes=[pltpu.VMEM((T, W), F32)] * 2,
        compiler_params=_params("parallel", "arbitrary"),
    )(q, k, v, do, rt)


def _loss_head(y, target, tm):
    S, D = y.shape

    def body(y_ref, t_ref, dy_ref, part_ref):
        err = y_ref[...] - t_ref[...]
        dy_ref[...] = err * (1.0 / D)

        @pl.when(pl.program_id(0) == 0)
        def _():
            part_ref[...] = jnp.zeros_like(part_ref)

        part_ref[...] += jnp.sum(err * err, axis=0, keepdims=True)

    spec = pl.BlockSpec((tm, D), lambda i: (i, 0))
    return pl.pallas_call(
        body, name="loss_head", grid=(S // tm,), in_specs=[spec, spec],
        out_specs=[spec, pl.BlockSpec((1, D), lambda i: (0, 0))],
        out_shape=[jax.ShapeDtypeStruct((S, D), F32), jax.ShapeDtypeStruct((1, D), F32)],
        compiler_params=_params("arbitrary"),
    )(y, target)


def _row_tile(R):
    for t in (256, 128, 64, 32, 16, 8):
        if R % t == 0:
            return t
    return R


def _pair_add_bf16(name, g, b1, kind, c_arr):
    P, Rh, C = b1.shape
    tr = _row_tile(Rh)
    nb = Rh // tr

    def body(c_ref, g_ref, b_ref, o_ref):
        o_ref[...] = (g_ref[...] + b_ref[...]).astype(o_ref.dtype)

    if kind == "cols":
        g_spec = pl.BlockSpec((None, tr, C), lambda p, i, c_ref: (p, c_ref[0] * nb + i, 0))
    else:
        g_spec = pl.BlockSpec((tr, C), lambda p, i, c_ref: ((2 * p + c_ref[0]) * nb + i, 0))
    blk = pl.BlockSpec((None, tr, C), lambda p, i, c_ref: (p, i, 0))
    return pl.pallas_call(
        body, name=name,
        grid_spec=pltpu.PrefetchScalarGridSpec(num_scalar_prefetch=1, grid=(P, nb), in_specs=[g_spec, blk], out_specs=blk),
        out_shape=jax.ShapeDtypeStruct((P, Rh, C), BF16),
        compiler_params=_params("parallel", "parallel"),
    )(c_arr, g, b1)


def _sum_slots(name, b, half_arr=None):
    P, R, C = b.shape
    tr = _row_tile(R)
    nb = R // tr

    def body(*refs):
        b_ref, o_ref = refs[-2:]
        acc = b_ref[0].astype(F32)
        for s in range(1, P):
            acc = acc + b_ref[s].astype(F32)
        o_ref[...] = acc

    if half_arr is None:
        return pl.pallas_call(
            body, name=name, grid=(nb,),
            in_specs=[pl.BlockSpec((P, tr, C), lambda i: (0, i, 0))],
            out_specs=pl.BlockSpec((tr, C), lambda i: (i, 0)),
            out_shape=jax.ShapeDtypeStruct((R, C), F32),
            compiler_params=_params("parallel"),
        )(b)
    return pl.pallas_call(
        body, name=name,
        grid_spec=pltpu.PrefetchScalarGridSpec(
            num_scalar_prefetch=1, grid=(nb,),
            in_specs=[pl.BlockSpec((P, tr, C), lambda i, half: (0, i, 0))],
            out_specs=pl.BlockSpec((tr, C), lambda i, half: (half[0] * nb + i, 0))),
        out_shape=jax.ShapeDtypeStruct((2 * R, C), F32),
        compiler_params=_params("parallel"),
    )(half_arr, b)


def _adamw(name, w, g, m, v):
    R, C = w.shape
    tr = _row_tile(R)
    c1 = 1.0 - ADAM_B1 ** ADAM_STEP
    c2 = 1.0 - ADAM_B2 ** ADAM_STEP

    def body(w_ref, g_ref, m_ref, v_ref, d_ref, nm_ref, nv_ref):
        gg = g_ref[...]
        nm = ADAM_B1 * m_ref[...] + (1.0 - ADAM_B1) * gg
        nv = ADAM_B2 * v_ref[...] + (1.0 - ADAM_B2) * (gg * gg)
        m_hat = nm / c1
        v_hat = nv / c2
        d_ref[...] = -ADAM_LR * (m_hat / (jnp.sqrt(v_hat) + ADAM_EPS) + ADAM_WD * w_ref[...])
        nm_ref[...] = nm
        nv_ref[...] = nv

    spec = pl.BlockSpec((tr, C), lambda i: (i, 0))
    return pl.pallas_call(
        body, name=name, grid=(R // tr,), in_specs=[spec] * 4, out_specs=[spec] * 3,
        out_shape=[jax.ShapeDtypeStruct((R, C), F32)] * 3,
        compiler_params=_params("parallel"),
    )(w, g, m, v)


def _place():
    x, y, c = lax.axis_index("x"), lax.axis_index("y"), lax.axis_index("c")
    other_chips = [(1 - x, y), (x, 1 - y), (1 - x, 1 - y)]
    return x, y, c, other_chips


def _into_slot(name, w, dtype, slot_arr, n_slots):
    R, C = w.shape
    tr = _row_tile(R)

    def body(slot_ref, w_ref, o_ref):
        o_ref[...] = w_ref[...].astype(o_ref.dtype)

    return pl.pallas_call(
        body, name=name,
        grid_spec=pltpu.PrefetchScalarGridSpec(
            num_scalar_prefetch=1, grid=(R // tr,),
            in_specs=[pl.BlockSpec((tr, C), lambda i, slot: (i, 0))],
            out_specs=pl.BlockSpec((None, tr, C), lambda i, slot: (slot[0], i, 0))),
        out_shape=jax.ShapeDtypeStruct((n_slots, R, C), dtype),
        compiler_params=_params("parallel"),
    )(slot_arr, w)


def _gather_chips(bufs):
    n = len(bufs)

    def body(*refs):
        outs = refs[n:2 * n]
        ici_send, ici_recv, d2d_send, d2d_recv = refs[2 * n:]
        x, y, c, chips = _place()
        me = 2 * x + y
        started = []
        for k in range(n):
            rh = outs[k].shape[1] // 2
            mine = outs[k].at[me, pl.ds(c * rh, rh)]
            for j, (px, py) in enumerate(chips):
                cp = pltpu.make_async_remote_copy(
                    src_ref=mine, dst_ref=mine,
                    send_sem=ici_send.at[3 * k + j], recv_sem=ici_recv.at[3 * k + j],
                    device_id=(px, py, c), device_id_type=MESH)
                cp.start()
                started.append(cp)
        for k in range(n):
            rh = outs[k].shape[1] // 2
            for j, (px, py) in enumerate(chips):
                landed = outs[k].at[2 * px + py, pl.ds(c * rh, rh)]
                pltpu.make_async_remote_copy(
                    src_ref=landed, dst_ref=landed,
                    send_sem=ici_send.at[3 * k + j], recv_sem=ici_recv.at[3 * k + j],
                    device_id=(px, py, c), device_id_type=MESH).wait_recv()
                cp = pltpu.make_async_remote_copy(
                    src_ref=landed, dst_ref=landed,
                    send_sem=d2d_send.at[3 * k + j], recv_sem=d2d_recv.at[3 * k + j],
                    device_id=(x, y, 1 - c), device_id_type=MESH)
                cp.start()
                started.append(cp)
        for k in range(n):
            rh = outs[k].shape[1] // 2
            for j, (px, py) in enumerate(chips):
                landed = outs[k].at[2 * px + py, pl.ds((1 - c) * rh, rh)]
                pltpu.make_async_remote_copy(
                    src_ref=landed, dst_ref=landed,
                    send_sem=d2d_send.at[3 * k + j], recv_sem=d2d_recv.at[3 * k + j],
                    device_id=(x, y, 1 - c), device_id_type=MESH).wait_recv()
        for cp in started:
            cp.wait_send()

    return pl.pallas_call(
        body, name="gather_weights",
        in_specs=[ANY] * n, out_specs=[ANY] * n,
        out_shape=[jax.ShapeDtypeStruct(b.shape, b.dtype) for b in bufs],
        input_output_aliases={k: k for k in range(n)},
        scratch_shapes=[pltpu.SemaphoreType.DMA((3 * n,))] * 4,
        compiler_params=pltpu.CompilerParams(has_side_effects=True),
    )(*bufs)


def _half(ref, kind, p, c, rh):
    if kind == "cols":
        return ref.at[p, pl.ds(c * rh, rh)]
    return ref.at[pl.ds((2 * p + c) * rh, rh)]


def _swap_halves(grads, kinds, rhs):
    n = len(grads)

    def body(*refs):
        ins, outs = refs[:n], refs[n:2 * n]
        send_sem, recv_sem = refs[2 * n:]
        x, y, c, _ = _place()
        started = []
        for k in range(n):
            for p in range(N_CHIPS):
                cp = pltpu.make_async_remote_copy(
                    src_ref=_half(ins[k], kinds[k], p, 1 - c, rhs[k]), dst_ref=outs[k].at[p],
                    send_sem=send_sem.at[N_CHIPS * k + p], recv_sem=recv_sem.at[N_CHIPS * k + p],
                    device_id=(x, y, 1 - c), device_id_type=MESH)
                cp.start()
                started.append(cp)
        for cp in started:
            cp.wait()

    out_shape = []
    for g, kind, rh in zip(grads, kinds, rhs):
        out_shape.append(jax.ShapeDtypeStruct((N_CHIPS, rh, g.shape[-1]), g.dtype))
    return pl.pallas_call(
        body, name="grad_swap_halves",
        in_specs=[ANY] * n, out_specs=[ANY] * n, out_shape=out_shape,
        scratch_shapes=[pltpu.SemaphoreType.DMA((N_CHIPS * n,))] * 2,
        compiler_params=pltpu.CompilerParams(has_side_effects=True),
    )(*grads)


def _scatter_partials(parts):
    n = len(parts)

    def body(*refs):
        ins, outs = refs[:n], refs[n:2 * n]
        send_sem, recv_sem, local_sem = refs[2 * n:]
        x, y, c, chips = _place()
        me = 2 * x + y
        started = []
        for k in range(n):
            cp = pltpu.make_async_copy(ins[k].at[me], outs[k].at[me], local_sem.at[k])
            cp.start()
            started.append(cp)
            for j, (px, py) in enumerate(chips):
                cp = pltpu.make_async_remote_copy(
                    src_ref=ins[k].at[2 * px + py], dst_ref=outs[k].at[me],
                    send_sem=send_sem.at[3 * k + j], recv_sem=recv_sem.at[3 * k + j],
                    device_id=(px, py, c), device_id_type=MESH)
                cp.start()
                started.append(cp)
        for k in range(n):
            for j, (px, py) in enumerate(chips):
                landed = outs[k].at[2 * px + py]
                pltpu.make_async_remote_copy(
                    src_ref=landed, dst_ref=landed,
                    send_sem=send_sem.at[3 * k + j], recv_sem=recv_sem.at[3 * k + j],
                    device_id=(px, py, c), device_id_type=MESH).wait_recv()
        for k in range(n):
            started[4 * k].wait()
            for j in range(3):
                started[4 * k + 1 + j].wait_send()

    return pl.pallas_call(
        body, name="grad_scatter_partials",
        in_specs=[ANY] * n, out_specs=[ANY] * n,
        out_shape=[jax.ShapeDtypeStruct(s.shape, s.dtype) for s in parts],
        scratch_shapes=[pltpu.SemaphoreType.DMA((3 * n,))] * 2 + [pltpu.SemaphoreType.DMA((n,))],
        compiler_params=pltpu.CompilerParams(has_side_effects=True),
    )(*parts)


def _join_halves(fulls):
    n = len(fulls)

    def body(*refs):
        outs = refs[n:2 * n]
        send_sem, recv_sem = refs[2 * n:]
        x, y, c, _ = _place()
        started = []
        for k in range(n):
            rh = outs[k].shape[0] // 2
            mine = outs[k].at[pl.ds(c * rh, rh)]
            cp = pltpu.make_async_remote_copy(
                src_ref=mine, dst_ref=mine, send_sem=send_sem.at[k], recv_sem=recv_sem.at[k],
                device_id=(x, y, 1 - c), device_id_type=MESH)
            cp.start()
            started.append(cp)
        for k in range(n):
            rh = outs[k].shape[0] // 2
            theirs = outs[k].at[pl.ds((1 - c) * rh, rh)]
            pltpu.make_async_remote_copy(
                src_ref=theirs, dst_ref=theirs, send_sem=send_sem.at[k], recv_sem=recv_sem.at[k],
                device_id=(x, y, 1 - c), device_id_type=MESH).wait_recv()
        for cp in started:
            cp.wait_send()

    return pl.pallas_call(
        body, name="grad_join_halves",
        in_specs=[ANY] * n, out_specs=[ANY] * n,
        out_shape=[jax.ShapeDtypeStruct(f.shape, f.dtype) for f in fulls],
        input_output_aliases={k: k for k in range(n)},
        scratch_shapes=[pltpu.SemaphoreType.DMA((n,))] * 2,
        compiler_params=pltpu.CompilerParams(has_side_effects=True),
    )(*fulls)


def _gather_all(block):
    def body(in_ref, out_ref, send_sem, recv_sem, local_sem):
        x, y, c, _ = _place()

        def slot(px, py, pc):
            return out_ref.at[4 * px + 2 * py + pc]

        loc = pltpu.make_async_copy(in_ref, slot(x, y, c), local_sem)
        loc.start()
        started = []
        for d in range(1, N_DEV):
            fx, fy, fc = d >> 2, (d >> 1) & 1, d & 1
            cp = pltpu.make_async_remote_copy(
                src_ref=in_ref, dst_ref=slot(x, y, c), send_sem=send_sem.at[d - 1], recv_sem=recv_sem.at[d - 1],
                device_id=(x ^ fx, y ^ fy, c ^ fc), device_id_type=MESH)
            cp.start()
            started.append(cp)
        for d in range(1, N_DEV):
            fx, fy, fc = d >> 2, (d >> 1) & 1, d & 1
            landed = slot(x ^ fx, y ^ fy, c ^ fc)
            pltpu.make_async_remote_copy(
                src_ref=in_ref, dst_ref=landed, send_sem=send_sem.at[d - 1], recv_sem=recv_sem.at[d - 1],
                device_id=(x ^ fx, y ^ fy, c ^ fc), device_id_type=MESH).wait_recv()
        for cp in started:
            cp.wait_send()
        loc.wait()

    return pl.pallas_call(
        body, name="gather_small_grads",
        in_specs=[ANY], out_specs=ANY,
        out_shape=jax.ShapeDtypeStruct((N_DEV,) + block.shape, block.dtype),
        scratch_shapes=[pltpu.SemaphoreType.DMA((N_DEV - 1,))] * 2 + [pltpu.SemaphoreType.DMA(())],
        compiler_params=pltpu.CompilerParams(has_side_effects=True),
    )(block)


def _pack(pieces):
    flat = jnp.concatenate([p.reshape(-1) for p in pieces])
    n = flat.shape[0]
    padded = -(-n // (8 * LANES)) * (8 * LANES)
    return jnp.pad(flat, (0, padded - n)).reshape(-1, LANES)


def _unpack(packed, shapes):
    flat = packed.reshape(-1)
    out, at = [], 0
    for s in shapes:
        n = math.prod(s)
        out.append(flat[at:at + n].reshape(s))
        at += n
    return out


def kernel(x, meta_tokens, pre_mix_g, w_in, gate_b, dw_w, dw_b, conv_ln_g, conv_ln_b, w_conv_out, w_attn_out, w_o, post_mix_g, pre_ffn_g, w_ffn_in, w_ffn_out, post_ffn_g, loss_target, m_meta_tokens, m_pre_mix_g, m_w_in, m_gate_b, m_dw_w, m_dw_b, m_conv_ln_g, m_conv_ln_b, m_w_conv_out, m_w_attn_out, m_w_o, m_post_mix_g, m_pre_ffn_g, m_w_ffn_in, m_w_ffn_out, m_post_ffn_g, v_meta_tokens, v_pre_mix_g, v_w_in, v_gate_b, v_dw_w, v_dw_b, v_conv_ln_g, v_conv_ln_b, v_w_conv_out, v_w_attn_out, v_w_o, v_post_mix_g, v_pre_ffn_g, v_w_ffn_in, v_w_ffn_out, v_post_ffn_g):
    S, D = x.shape[1], x.shape[2]
    L = S + N_META
    T = -(-L // ROW_BLOCK) * ROW_BLOCK
    Ta = -(-L // ATT_BLOCK) * ATT_BLOCK
    tm = _tile(T, MM_ROWS)
    ts = _tile(T, STAGE_ROWS)
    tw = _tile(T, WIDE_STAGE_ROWS)
    H = D // HEAD_DIM
    F = w_ffn_out.shape[1] * N_CHIPS
    Dc = D // N_CHIPS
    P = N_CHIPS
    me = 2 * lax.axis_index("x") + lax.axis_index("y")
    c_arr = lax.axis_index("c").astype(jnp.int32).reshape(1)

    dw_w_pad = jnp.pad(dw_w[0], ((0, CONV_PAD - CONV_WIDTH), (0, 0)))
    me_arr = me.astype(jnp.int32).reshape(1)
    to_gather = [("w_in", w_in[0], BF16), ("w_conv_out", w_conv_out[0], BF16), ("w_attn_out", w_attn_out[0], BF16),
                 ("w_o", w_o[0], BF16), ("w_ffn_in", w_ffn_in[0], BF16), ("w_ffn_out", w_ffn_out[0], BF16),
                 ("meta", meta_tokens, F32), ("taps", dw_w_pad, F32)]
    gathered = _gather_chips([_into_slot("slot_" + n, w, dt, me_arr, P) for n, w, dt in to_gather])
    win3, wfi3 = gathered[0], gathered[4]
    wco, wao, wo = (gathered[k].reshape(D, D) for k in (1, 2, 3))
    wfo = gathered[5].reshape(F, D)
    meta_full = gathered[6].transpose(1, 0, 2).reshape(N_META, D)
    taps = gathered[7].transpose(1, 0, 2).reshape(CONV_PAD, D)[:CONV_WIDTH]

    h0 = jnp.concatenate([meta_full, x[0], jnp.zeros((T - L, D), F32)], axis=0)
    (u1,) = _rowwise_fwd("rms_pre_mix", f_rms, [(h0, D, 0)], [(pre_mix_g, D, 0)], [(D, BF16)], T, ts)
    p = _mm_nn_cols("mm_in", u1, win3, tm)
    (uglu,) = _rowwise_fwd("glu", f_glu, [(p, D, 0), (p, D, 1)], [], [(D, F32)], T, ts)
    yc = _shift_conv("dwconv", uglu, taps, CONV_PAD, CONV_PAD - (CONV_WIDTH - 1), T)
    conv_pars = [(dw_b, D, 0), (conv_ln_g, D, 0), (conv_ln_b, D, 0)]
    (ys,) = _rowwise_fwd("conv_post", f_convpost, [(yc, D, 0)], conv_pars, [(D, BF16)], T, ts)
    y_conv = _mm_nn("mm_conv_out", ys, wco, tm)
    q, k, v = _qkv_split(p, D, T, Ta)
    o2, rtot = _attn_fwd(q, k, v)
    y_attn = _mm_nn("mm_attn_out", o2, wao, tm, rows=T)
    mix_rows = [(p, D, 5), (p, D, 6), (y_conv, D, 0), (y_attn, D, 0)]
    mix_pars = [(gate_b, D, 0), (gate_b, D, 1)]
    (mixin,) = _rowwise_fwd("gate_mix", f_mix, mix_rows, mix_pars, [(D, BF16)], T, ts)
    mix = _mm_nn("mm_o", mixin, wo, tm)
    (h1,) = _rowwise_fwd("res_post_mix", f_res_rms, [(h0, D, 0), (mix, D, 0)], [(post_mix_g, D, 0)], [(D, F32)], T, ts)
    (u2,) = _rowwise_fwd("rms_pre_ffn", f_rms, [(h1, D, 0)], [(pre_ffn_g, D, 0)], [(D, BF16)], T, ts)
    ab = _mm_nn_cols("mm_ffn_in", u2, wfi3, tm)
    (fin,) = _rowwise_fwd("swiglu", f_swiglu, [(ab, F, 0), (ab, F, 1)], [], [(F, BF16)], T, tw)
    f = _mm_nn("mm_ffn_out", fin, wfo, tm)
    (h2,) = _rowwise_fwd("res_post_ffn", f_res_rms, [(h1, D, 0), (f, D, 0)], [(post_ffn_g, D, 0)], [(D, F32)], T, ts)

    dy, part = _loss_head(h2[N_META:L], loss_target[0], _row_tile(S))
    loss = lax.psum(0.5 * jnp.sum(part) / D, ("x", "y", "c"))
    dh2 = jnp.pad(dy, ((N_META, T - L), (0, 0)))

    (df,), (g_post_ffn,) = _rowwise_bwd("res_post_ffn_bwd", f_res_rms, [(h1, D, 0), (f, D, 0)], [(post_ffn_g, D, 0)],
                                        [(dh2, D, 0)], [None, BF16], T, ts)
    dfin = _mm_nt("mm_ffn_out_dx", df, wfo, tm)
    g_wfo = _mm_tn("mm_ffn_out_dw", fin, df, tm, 2)
    (da, db), _ = _rowwise_bwd("swiglu_bwd", f_swiglu, [(ab, F, 0), (ab, F, 1)], [], [(dfin, F, 0)], [BF16, BF16], T, tw)
    dab = jnp.concatenate([da, db], axis=1)
    du2 = _mm_nt_cols("mm_ffn_in_dx", dab, wfi3, tm)
    g_wfi = _mm_tn_cols("mm_ffn_in_dw", u2, dab, tm, P)
    (dh1,), (g_pre_ffn,) = _rowwise_bwd("rms_pre_ffn_bwd", f_rms_id, [(h1, D, 0)], [(pre_ffn_g, D, 0)],
                                        [(du2, D, 0), (dh2, D, 0)], [F32], T, ts)
    (dmix,), (g_post_mix,) = _rowwise_bwd("res_post_mix_bwd", f_res_rms, [(h0, D, 0), (mix, D, 0)], [(post_mix_g, D, 0)],
                                          [(dh1, D, 0)], [None, BF16], T, ts)
    dmixin = _mm_nt("mm_o_dx", dmix, wo, tm)
    g_wo = _mm_tn("mm_o_dw", mixin, dmix, tm, 1)
    (dpc, dpa, dyconv, dyattn), (g_gate_c, g_gate_a) = _rowwise_bwd(
        "gate_mix_bwd", f_mix, mix_rows, mix_pars, [(dmixin, D, 0)], [BF16, BF16, BF16, BF16], T, ts)
    g_wco = _mm_tn("mm_conv_out_dw", ys, dyconv, tm, 1)
    dys = _mm_nt("mm_conv_out_dx", dyconv, wco, tm)
    g_wao = _mm_tn("mm_attn_out_dw", o2, dyattn, tm, 1)
    do2 = _mm_nt("mm_attn_out_dx", dyattn, wao, tm, BF16, out_rows=Ta)
    (dyc,), (g_dw_b, g_ln_g, g_ln_b) = _rowwise_bwd("conv_post_bwd", f_convpost, [(yc, D, 0)], conv_pars,
                                                    [(dys, D, 0)], [F32], T, ts)
    duglu = _shift_conv("dwconv_dx", dyc, taps[::-1], 0, 0, T)
    g_taps = _conv_dw("dwconv_dw", uglu, dyc, T)
    (dp0, dp1), _ = _rowwise_bwd("glu_bwd", f_glu, [(p, D, 0), (p, D, 1)], [], [(duglu, D, 0)], [BF16, BF16], T, ts)
    dq, dk, dv = _attn_bwd(q, k, v, do2, rtot)
    dp = jnp.concatenate([dp0, dp1, dq[:T], dk[:T], dv[:T], dpc, dpa], axis=1)
    du1 = _mm_nt_cols("mm_in_dx", dp, win3, tm)
    g_win = _mm_tn_cols("mm_in_dw", u1, dp, tm, P)
    (dh0,), (g_pre_mix,) = _rowwise_bwd("rms_pre_mix_bwd", f_rms_id, [(h0, D, 0)], [(pre_mix_g, D, 0)],
                                        [(du1, D, 0), (dh1, D, 0)], [F32], T, ts)
    grad_x = dh0[N_META:L][None]

    big = [g_win, g_wco, g_wao, g_wo, g_wfi, g_wfo]
    kinds = ["cols", "rows", "rows", "rows", "cols", "rows"]
    rhs = [(g.shape[1] if kind == "cols" else g.shape[0] // P) // 2 for g, kind in zip(big, kinds)]
    from_sibling = _swap_halves(big, kinds, rhs)
    parts = [_pair_add_bf16("grad_pair_add_%d" % n, g, b1, kind, c_arr)
             for n, (g, b1, kind) in enumerate(zip(big, from_sibling, kinds))]
    slots = _scatter_partials(parts)
    g_big = _join_halves([_sum_slots("grad_chip_sum_%d" % n, s, c_arr) for n, s in enumerate(slots)])

    small_shapes = [(1, D), (1, D), (1, D), (CONV_WIDTH, D), (1, D), (1, D), (1, D), (1, D), (1, D), (1, D), (N_META, D)]
    small = _pack([g_pre_mix, g_gate_c, g_gate_a, g_taps, g_dw_b, g_ln_g, g_ln_b, g_post_mix, g_pre_ffn, g_post_ffn,
                   dh0[:N_META]])
    summed = _sum_slots("small_grad_sum", _gather_all(small))
    (s_pre_mix, s_gate_c, s_gate_a, s_taps, s_dw_b, s_ln_g, s_ln_b, s_post_mix, s_pre_ffn, s_post_ffn,
     s_meta) = _unpack(summed, small_shapes)
    s_gate_b = jnp.concatenate([s_gate_c, s_gate_a], axis=1)
    s_taps = lax.dynamic_slice_in_dim(s_taps, me * Dc, Dc, axis=1)[None]
    s_meta = lax.dynamic_slice_in_dim(s_meta, me * Dc, Dc, axis=1)

    grads = {
        "meta_tokens": s_meta, "pre_mix_g": s_pre_mix, "w_in": g_big[0][None], "gate_b": s_gate_b, "dw_w": s_taps,
        "dw_b": s_dw_b, "conv_ln_g": s_ln_g, "conv_ln_b": s_ln_b, "w_conv_out": g_big[1][None],
        "w_attn_out": g_big[2][None], "w_o": g_big[3][None], "post_mix_g": s_post_mix, "pre_ffn_g": s_pre_ffn,
        "w_ffn_in": g_big[4][None], "w_ffn_out": g_big[5][None], "post_ffn_g": s_post_ffn,
    }
    weights = {
        "meta_tokens": (meta_tokens, m_meta_tokens, v_meta_tokens), "pre_mix_g": (pre_mix_g, m_pre_mix_g, v_pre_mix_g),
        "w_in": (w_in, m_w_in, v_w_in), "gate_b": (gate_b, m_gate_b, v_gate_b), "dw_w": (dw_w, m_dw_w, v_dw_w),
        "dw_b": (dw_b, m_dw_b, v_dw_b), "conv_ln_g": (conv_ln_g, m_conv_ln_g, v_conv_ln_g),
        "conv_ln_b": (conv_ln_b, m_conv_ln_b, v_conv_ln_b), "w_conv_out": (w_conv_out, m_w_conv_out, v_w_conv_out),
        "w_attn_out": (w_attn_out, m_w_attn_out, v_w_attn_out), "w_o": (w_o, m_w_o, v_w_o),
        "post_mix_g": (post_mix_g, m_post_mix_g, v_post_mix_g), "pre_ffn_g": (pre_ffn_g, m_pre_ffn_g, v_pre_ffn_g),
        "w_ffn_in": (w_ffn_in, m_w_ffn_in, v_w_ffn_in), "w_ffn_out": (w_ffn_out, m_w_ffn_out, v_w_ffn_out),
        "post_ffn_g": (post_ffn_g, m_post_ffn_g, v_post_ffn_g),
    }
    names = list(weights)
    big_names = ["w_in", "w_conv_out", "w_attn_out", "w_o", "w_ffn_in", "w_ffn_out"]
    small_names = [n for n in names if n not in big_names]

    delta, new_m, new_v = {}, {}, {}
    for n in big_names:
        w, m, v2 = weights[n]
        d, nm, nv = _adamw("adamw_" + n, w[0], grads[n][0], m[0], v2[0])
        delta[n], new_m[n], new_v[n] = d[None], nm[None], nv[None]
    shapes = [weights[n][0].shape for n in small_names]
    packed = [_pack([weights[n][k] for n in small_names]) for k in range(3)]
    d, nm, nv = _adamw("adamw_small", packed[0], _pack([grads[n] for n in small_names]), packed[1], packed[2])
    for n, dd, mm, vv in zip(small_names, _unpack(d, shapes), _unpack(nm, shapes), _unpack(nv, shapes)):
        delta[n], new_m[n], new_v[n] = dd, mm, vv

    return (loss, grad_x, *[grads[n].reshape(weights[n][0].shape) for n in names], *[delta[n] for n in names],
            *[new_m[n] for n in names], *[new_v[n] for n in names])
```

```python
import math

import jax
import jax.numpy as jnp
from jax import lax
from jax.experimental import pallas as pl
from jax.experimental.pallas import tpu as pltpu

F32 = jnp.float32
BF16 = jnp.bfloat16

N_META = 16
CONV_WIDTH = 31
CONV_PAD = 32
HEAD_DIM = 64
RMS_EPS = 1e-6
LN_EPS = 1e-5
ROW_BLOCK = 128
ATT_BLOCK = 256
ATT_HEADS = 4
LANES = 128
N_CHIPS = 4
N_DEV = 8
MM_ROWS = 544
STAGE_ROWS = 272
WIDE_STAGE_ROWS = 128
VMEM_LIMIT = 56 * 1024 * 1024

ADAM_LR = 0.001
ADAM_B1 = 0.9
ADAM_B2 = 0.999
ADAM_EPS = 1e-08
ADAM_WD = 0.01
ADAM_STEP = 10

MESH = pl.DeviceIdType.MESH
ANY = pl.BlockSpec(memory_space=pl.ANY)
HBM = pl.BlockSpec(memory_space=pltpu.HBM)
SEM = pl.BlockSpec(memory_space=pltpu.SEMAPHORE)
EFFECT = pltpu.SideEffectType.DATAFLOW_SIDE_EFFECTING


def _params(*sem):
    return pltpu.CompilerParams(dimension_semantics=sem if sem else None, vmem_limit_bytes=VMEM_LIMIT)


def _rms(x, g):
    return x * lax.rsqrt(jnp.mean(x * x, axis=-1, keepdims=True) + RMS_EPS) * g


def f_rms(h, g):
    return (_rms(h, g),)


def f_rms_id(h, g):
    return (_rms(h, g), h)


def f_res_rms(h, m, g):
    return (h + _rms(m, g),)


def f_glu(a, gate):
    return (a * lax.logistic(gate),)


def f_convpost(yc, b, ln_g, ln_b):
    y = yc + b
    mu = jnp.mean(y, axis=-1, keepdims=True)
    xc = y - mu
    var = jnp.mean(xc * xc, axis=-1, keepdims=True)
    yl = xc * lax.rsqrt(var + LN_EPS) * ln_g + ln_b
    return (yl * lax.logistic(yl),)


def f_mix(pc, pa, yc, ya, bc, ba):
    return (lax.logistic(pc + bc) * yc + lax.logistic(pa + ba) * ya,)


def f_swiglu(a, b):
    return (a * lax.logistic(a) * b,)


def _tile(T, target):
    return max(t for t in range(16, target + 1, 16) if T % t == 0)


def _row_map(j):
    return lambda i: (i, j)


def _par_map(j):
    return lambda i: (0, j)


def _rowwise_fwd(name, f, rows, pars, outs, T, tm):
    n_in = len(rows) + len(pars)

    def body(*refs):
        vals = [r[...].astype(F32) for r in refs[:n_in]]
        res = f(*vals)
        for o_ref, o in zip(refs[n_in:], res):
            o_ref[...] = o.astype(o_ref.dtype)

    in_specs = [pl.BlockSpec((tm, w), _row_map(j)) for _, w, j in rows]
    in_specs += [pl.BlockSpec((1, w), _par_map(j)) for _, w, j in pars]
    return pl.pallas_call(
        body, name=name, grid=(T // tm,),
        in_specs=in_specs,
        out_specs=[pl.BlockSpec((tm, w), _row_map(0)) for w, _ in outs],
        out_shape=[jax.ShapeDtypeStruct((T, w), dt) for w, dt in outs],
        compiler_params=_params("parallel"),
    )(*[a for a, _, _ in rows], *[a for a, _, _ in pars])


def _rowwise_bwd(name, f, rows, pars, cots, drow_dtypes, T, tm):
    n_r, n_p, n_c = len(rows), len(pars), len(cots)
    n_in = n_r + n_p + n_c
    keep = [k for k, dt in enumerate(drow_dtypes) if dt is not None]

    def body(*refs):
        rv = [r[...].astype(F32) for r in refs[:n_r]]
        pv = [r[...].astype(F32) for r in refs[n_r:n_r + n_p]]
        cv = [r[...].astype(F32) for r in refs[n_r + n_p:n_in]]
        _, vjp = jax.vjp(f, *rv, *pv)
        g = vjp(tuple(cv))
        drow_refs = refs[n_in:n_in + len(keep)]
        dpar_refs = refs[n_in + len(keep):]
        for r, k in zip(drow_refs, keep):
            r[...] = g[k].astype(r.dtype)

        @pl.when(pl.program_id(0) == 0)
        def _():
            for r in dpar_refs:
                r[...] = jnp.zeros_like(r)

        for r, gp in zip(dpar_refs, g[n_r:]):
            r[...] += gp

    in_specs = [pl.BlockSpec((tm, w), _row_map(j)) for _, w, j in rows]
    in_specs += [pl.BlockSpec((1, w), _par_map(j)) for _, w, j in pars]
    in_specs += [pl.BlockSpec((tm, w), _row_map(j)) for _, w, j in cots]
    out_specs = [pl.BlockSpec((tm, rows[k][1]), _row_map(0)) for k in keep]
    out_specs += [pl.BlockSpec((1, w), _par_map(0)) for _, w, _ in pars]
    out_shape = [jax.ShapeDtypeStruct((T, rows[k][1]), drow_dtypes[k]) for k in keep]
    out_shape += [jax.ShapeDtypeStruct((1, w), F32) for _, w, _ in pars]
    res = pl.pallas_call(
        body, name=name, grid=(T // tm,),
        in_specs=in_specs, out_specs=out_specs, out_shape=out_shape,
        compiler_params=_params("arbitrary"),
    )(*[a for a, _, _ in rows], *[a for a, _, _ in pars], *[a for a, _, _ in cots])
    return res[:len(keep)], res[len(keep):]


NN = (((1,), (0,)), ((), ()))
NT = (((1,), (1,)), ((), ()))
TN = (((0,), (0,)), ((), ()))


def _mm(name, a, b, dims, out_shape, grid, a_spec, b_spec, o_spec, red_axis=None, init=None):
    n_red = None if red_axis is None else grid[red_axis]

    def body(a_ref, b_ref, *rest):
        o_ref = rest[-1]
        prod = lax.dot_general(a_ref[...], b_ref[...], dims, preferred_element_type=F32)
        if n_red is None:
            o_ref[...] = prod.astype(o_ref.dtype)
        else:
            @pl.when(pl.program_id(red_axis) == 0)
            def _():
                o_ref[...] = prod

            @pl.when(pl.program_id(red_axis) > 0)
            def _():
                o_ref[...] += prod

    sem = ["parallel"] * len(grid)
    if red_axis is not None:
        sem[red_axis] = "arbitrary"
    if init is None:
        return pl.pallas_call(
            body, name=name, grid=grid, in_specs=[a_spec, b_spec], out_specs=o_spec, out_shape=out_shape,
            compiler_params=_params(*sem),
        )(a, b)
    return pl.pallas_call(
        body, name=name, grid=grid, in_specs=[a_spec, b_spec, ANY], out_specs=o_spec, out_shape=out_shape,
        input_output_aliases={2: 0}, compiler_params=_params(*sem),
    )(a, b, init)


def _mm_nn(name, a, w, tm, out_dtype=F32, rows=None):
    T, K = a.shape
    T = rows or T
    N = w.shape[1]
    return _mm(name, a, w, NN, jax.ShapeDtypeStruct((T, N), out_dtype), (T // tm,),
               pl.BlockSpec((tm, K), lambda i: (i, 0)), pl.BlockSpec((K, N), lambda i: (0, 0)),
               pl.BlockSpec((tm, N), lambda i: (i, 0)))


def _mm_nt(name, a, w, tm, out_dtype=F32, out_rows=None):
    T, N = a.shape
    K = w.shape[0]
    init = None if out_rows is None else jnp.zeros((out_rows, K), out_dtype)
    return _mm(name, a, w, NT, jax.ShapeDtypeStruct((out_rows or T, K), out_dtype), (T // tm,),
               pl.BlockSpec((tm, N), lambda i: (i, 0)), pl.BlockSpec((K, N), lambda i: (0, 0)),
               pl.BlockSpec((tm, K), lambda i: (i, 0)), init=init)


def _mm_tn(name, a, b, tm, n_row_blocks):
    K = a.shape[1]
    T, N = b.shape
    kb = K // n_row_blocks
    return _mm(name, a, b, TN, jax.ShapeDtypeStruct((K, N), F32), (n_row_blocks, T // tm),
               pl.BlockSpec((tm, kb), lambda r, t: (t, r)), pl.BlockSpec((tm, N), lambda r, t: (t, 0)),
               pl.BlockSpec((kb, N), lambda r, t: (r, 0)), red_axis=1)


def _mm_nn_cols(name, a, w3, tm):
    T, K = a.shape
    P, _, Ns = w3.shape
    return _mm(name, a, w3, NN, jax.ShapeDtypeStruct((T, P * Ns), F32), (P, T // tm),
               pl.BlockSpec((tm, K), lambda p, i: (i, 0)), pl.BlockSpec((None, K, Ns), lambda p, i: (p, 0, 0)),
               pl.BlockSpec((tm, Ns), lambda p, i: (i, p)))


def _mm_nt_cols(name, a, w3, tm):
    T = a.shape[0]
    P, K, Ns = w3.shape
    return _mm(name, a, w3, NT, jax.ShapeDtypeStruct((T, K), F32), (T // tm, P),
               pl.BlockSpec((tm, Ns), lambda i, p: (i, p)), pl.BlockSpec((None, K, Ns), lambda i, p: (p, 0, 0)),
               pl.BlockSpec((tm, K), lambda i, p: (i, 0)), red_axis=1)


def _mm_tn_cols(name, a, b, tm, P):
    T, K = a.shape
    Ns = b.shape[1] // P
    return _mm(name, a, b, TN, jax.ShapeDtypeStruct((P, K, Ns), F32), (P, T // tm),
               pl.BlockSpec((tm, K), lambda p, t: (t, 0)), pl.BlockSpec((tm, Ns), lambda p, t: (t, p)),
               pl.BlockSpec((None, K, Ns), lambda p, t: (p, 0, 0)), red_axis=1)


def _tap_windows(win, off, tb):
    out = []
    for b in range(8):
        taps = [j for j in range(CONV_WIDTH) if (off + j) % 8 == b]
        if taps:
            shifted = win[b:b + tb + CONV_PAD, :]
            out += [(shifted, off + j - b, j) for j in taps]
    return out


def _shift_conv(name, x, w, place, off, T):
    C = x.shape[1]
    tb = ROW_BLOCK
    zero_at = 0 if place else T

    def body(x_ref, w_ref, o_ref, xp_ref):
        xp_ref[pl.ds(zero_at, CONV_PAD), :] = jnp.zeros((CONV_PAD, LANES), F32)
        xp_ref[pl.ds(T + CONV_PAD, 8), :] = jnp.zeros((8, LANES), F32)
        xp_ref[pl.ds(place, T), :] = x_ref[...]

        def step(t, carry):
            base = pl.multiple_of(t * tb, tb)
            win = xp_ref[pl.ds(base, tb + CONV_PAD + 8), :]
            acc = jnp.zeros((tb, LANES), F32)
            for shifted, at, j in _tap_windows(win, off, tb):
                acc = acc + shifted[at:at + tb, :] * w_ref[pl.ds(j, 1), :]
            o_ref[pl.ds(base, tb), :] = acc
            return carry

        lax.fori_loop(0, T // tb, step, 0)

    return pl.pallas_call(
        body, name=name, grid=(C // LANES,),
        in_specs=[pl.BlockSpec((T, LANES), lambda c: (0, c)), pl.BlockSpec((CONV_WIDTH, LANES), lambda c: (0, c))],
        out_specs=pl.BlockSpec((T, LANES), lambda c: (0, c)),
        out_shape=jax.ShapeDtypeStruct((T, C), F32),
        scratch_shapes=[pltpu.VMEM((T + CONV_PAD + 8, LANES), F32)],
        compiler_params=_params("parallel"),
    )(x, w)


def _conv_dw(name, x, dy, T):
    C = x.shape[1]
    tb = ROW_BLOCK
    off = CONV_PAD - (CONV_WIDTH - 1)

    def body(x_ref, dy_ref, o_ref, xp_ref, acc_ref):
        xp_ref[pl.ds(0, CONV_PAD), :] = jnp.zeros((CONV_PAD, LANES), F32)
        xp_ref[pl.ds(T + CONV_PAD, 8), :] = jnp.zeros((8, LANES), F32)
        xp_ref[pl.ds(CONV_PAD, T), :] = x_ref[...]
        acc_ref[...] = jnp.zeros_like(acc_ref)

        def step(t, carry):
            base = pl.multiple_of(t * tb, tb)
            win = xp_ref[pl.ds(base, tb + CONV_PAD + 8), :]
            d = dy_ref[pl.ds(base, tb), :]
            for shifted, at, j in _tap_windows(win, off, tb):
                prod = shifted[at:at + tb, :] * d
                acc_ref[j] += jnp.sum(prod.reshape(tb // 8, 8, LANES), axis=0)
            return carry

        lax.fori_loop(0, T // tb, step, 0)
        for j in range(CONV_WIDTH):
            o_ref[pl.ds(j, 1), :] = jnp.sum(acc_ref[j], axis=0, keepdims=True)

    return pl.pallas_call(
        body, name=name, grid=(C // LANES,),
        in_specs=[pl.BlockSpec((T, LANES), lambda c: (0, c)), pl.BlockSpec((T, LANES), lambda c: (0, c))],
        out_specs=pl.BlockSpec((CONV_WIDTH, LANES), lambda c: (0, c)),
        out_shape=jax.ShapeDtypeStruct((CONV_WIDTH, C), F32),
        scratch_shapes=[pltpu.VMEM((T + CONV_PAD + 8, LANES), F32), pltpu.VMEM((CONV_WIDTH, 8, LANES), F32)],
        compiler_params=_params("parallel"),
    )(x, dy)


def _dot(a, b, dims=NN):
    return lax.dot_general(a, b, dims, preferred_element_type=F32)


def _tri_cumsum(x, tri):
    return _dot(x.astype(BF16), tri)


def _qkv_split(p, D, T, Ta):
    tb = ROW_BLOCK
    nt = T // tb
    scale = 1.0 / math.sqrt(HEAD_DIM)

    def body(q_ref, k_ref, v_ref, qo_ref, ko_ref, vo_ref):
        live = pl.program_id(0) < nt
        qo_ref[...] = jnp.where(live, q_ref[...] * scale, 0.0).astype(BF16)
        ko_ref[...] = jnp.where(live, k_ref[...], 0.0).astype(BF16)
        vo_ref[...] = jnp.where(live, v_ref[...], 0.0).astype(BF16)

    def col(n):
        return lambda i: (jnp.minimum(i, nt - 1), n)

    return pl.pallas_call(
        body, name="qkv_split", grid=(Ta // tb,),
        in_specs=[pl.BlockSpec((tb, D), col(2 + n)) for n in range(3)],
        out_specs=[pl.BlockSpec((tb, D), lambda i: (i, 0))] * 3,
        out_shape=[jax.ShapeDtypeStruct((Ta, D), BF16)] * 3,
        compiler_params=_params("parallel"),
    )(p, p, p)


def _pair_lanes(g):
    return slice((g // 2) * LANES, (g // 2 + 1) * LANES)


def _own_lanes(x_ref, B, G):
    first = lax.broadcasted_iota(jnp.int32, (B, LANES), 1) < HEAD_DIM
    out = []
    for g in range(G):
        x2 = x_ref[:, _pair_lanes(g)]
        out.append(jnp.where(first if g % 2 == 0 else jnp.logical_not(first), x2, jnp.zeros_like(x2)))
    return first, out


def _attn_fwd(q, k, v):
    T, D = q.shape
    H = D // HEAD_DIM
    B = ATT_BLOCK
    G = ATT_HEADS
    W = G // 2 * LANES

    def body(q_ref, k_ref, v_ref, o_ref, rt_ref):
        i = pl.program_id(1)
        row = lax.broadcasted_iota(jnp.int32, (B, B), 0)
        col = lax.broadcasted_iota(jnp.int32, (B, B), 1)
        below = col < row
        tri = (row >= col).astype(BF16)
        first, qs = _own_lanes(q_ref, B, G)

        def tile(j, carry, diagonal):
            sl = pl.ds(pl.multiple_of(j * B, B), B)
            out = []
            for g in range(G):
                c, acc = carry[g]
                z = _dot(qs[g], k_ref[sl, _pair_lanes(g)], NT)
                sp = jnp.maximum(z, 0.0) + jnp.log(1.0 + jnp.exp(-jnp.abs(z)))
                if diagonal:
                    sp = jnp.where(below, sp, 0.0)
                rw = _tri_cumsum(sp, tri)
                a = jnp.exp(z - (rw + c))
                if diagonal:
                    a = jnp.where(below, a, 0.0)
                acc = acc + _dot(a.astype(BF16), v_ref[sl, _pair_lanes(g)])
                out.append((c + rw[:, 0:1], acc))
            return tuple(out)

        carry = tile(i, tuple((jnp.zeros((B, 1), F32), jnp.zeros((B, LANES), F32)) for _ in range(G)), True)
        carry = lax.fori_loop(0, i, lambda jj, cr: tile(i - 1 - jj, cr, False), carry)
        for g in range(0, G, 2):
            o_ref[:, _pair_lanes(g)] = jnp.where(first, carry[g][1], carry[g + 1][1]).astype(o_ref.dtype)
        for g in range(G):
            rt_ref[g] = carry[g][0]

    return pl.pallas_call(
        body, name="attn_fwd", grid=(H // G, T // B),
        in_specs=[pl.BlockSpec((B, W), lambda h, i: (i, h)),
                  pl.BlockSpec((T, W), lambda h, i: (0, h)),
                  pl.BlockSpec((T, W), lambda h, i: (0, h))],
        out_specs=[pl.BlockSpec((B, W), lambda h, i: (i, h)),
                   pl.BlockSpec((G, B, 1), lambda h, i: (h, i, 0))],
        out_shape=[jax.ShapeDtypeStruct((T, D), BF16), jax.ShapeDtypeStruct((H, T, 1), F32)],
        compiler_params=_params("parallel", "arbitrary"),
    )(q, k, v)


def _attn_bwd(q, k, v, do, rt):
    T, D = q.shape
    H = D // HEAD_DIM
    B = ATT_BLOCK
    nq = T // B
    scale = 1.0 / math.sqrt(HEAD_DIM)
    G = ATT_HEADS
    W = G // 2 * LANES

    def body(q_ref, k_ref, v_ref, do_ref, rt_ref, dq_ref, dk_ref, dv_ref, dk_acc, dv_acc):
        i = pl.program_id(1)

        @pl.when(i == 0)
        def _():
            dk_acc[...] = jnp.zeros_like(dk_acc)
            dv_acc[...] = jnp.zeros_like(dv_acc)

        row = lax.broadcasted_iota(jnp.int32, (B, B), 0)
        col = lax.broadcasted_iota(jnp.int32, (B, B), 1)
        below = col < row
        tri = (row <= col).astype(BF16)
        first, qs = _own_lanes(q_ref, B, G)
        _, dos = _own_lanes(do_ref, B, G)

        def tile(j, carry, diagonal):
            sl = pl.ds(pl.multiple_of(j * B, B), B)
            out = []
            for h in range(G):
                pc, gc, dq = carry[h]
                qi, doi = qs[h], dos[h]
                kj, vj = k_ref[sl, _pair_lanes(h)], v_ref[sl, _pair_lanes(h)]
                z = _dot(qi, kj, NT)
                e = jnp.exp(-jnp.abs(z))
                inv = 1.0 / (1.0 + e)
                sp = jnp.maximum(z, 0.0) - jnp.log(inv)
                sg = jnp.where(z >= 0.0, inv, e * inv)
                if diagonal:
                    sp = jnp.where(below, sp, 0.0)
                pw = _tri_cumsum(sp, tri)
                a = jnp.exp(z - (rt_ref[h] - pc - pw + sp))
                if diagonal:
                    a = jnp.where(below, a, 0.0)
                g = a * _dot(doi, vj, NT)
                gw = _tri_cumsum(g, tri)
                dz = g - sg * (gc + gw)
                if diagonal:
                    dz = jnp.where(below, dz, 0.0)
                dzb = dz.astype(BF16)
                dq = dq + _dot(dzb, kj)
                dk_acc[sl, _pair_lanes(h)] += _dot(dzb, qi, TN)
                dv_acc[sl, _pair_lanes(h)] += _dot(a.astype(BF16), doi, TN)
                out.append((pc + pw[:, B - 1:B], gc + gw[:, B - 1:B], dq))
            return tuple(out)

        zero = jnp.zeros((B, 1), F32)
        carry = lax.fori_loop(0, i, lambda j, cr: tile(j, cr, False),
                              tuple((zero, zero, jnp.zeros((B, LANES), F32)) for _ in range(G)))
        carry = tile(i, carry, True)
        for h in range(0, G, 2):
            dq_ref[:, _pair_lanes(h)] = (jnp.where(first, carry[h][2], carry[h + 1][2]) * scale).astype(dq_ref.dtype)

        @pl.when(i == nq - 1)
        def _():
            dk_ref[...] = dk_acc[...].astype(dk_ref.dtype)
            dv_ref[...] = dv_acc[...].astype(dv_ref.dtype)

    blk = pl.BlockSpec((B, W), lambda h, i: (i, h))
    full = pl.BlockSpec((T, W), lambda h, i: (0, h))
    return pl.pallas_call(
        body, name="attn_bwd", grid=(H // G, nq),
        in_specs=[blk, full, full, blk, pl.BlockSpec((G, B, 1), lambda h, i: (h, i, 0))],
        out_specs=[blk, full, full],
        out_shape=[jax.ShapeDtypeStruct((T, D), BF16)] * 3,
        scratch_shapes=[pltpu.VMEM((T, W), F32)] * 2,
        compiler_params=_params("parallel", "arbitrary"),
    )(q, k, v, do, rt)


def _loss_head(y, target, tm):
    S, D = y.shape

    def body(y_ref, t_ref, dy_ref, part_ref):
        err = y_ref[...] - t_ref[...]
        dy_ref[...] = err * (1.0 / D)

        @pl.when(pl.program_id(0) == 0)
        def _():
            part_ref[...] = jnp.zeros_like(part_ref)

        part_ref[...] += jnp.sum(err * err, axis=0, keepdims=True)

    spec = pl.BlockSpec((tm, D), lambda i: (i, 0))
    return pl.pallas_call(
        body, name="loss_head", grid=(S // tm,), in_specs=[spec, spec],
        out_specs=[spec, pl.BlockSpec((1, D), lambda i: (0, 0))],
        out_shape=[jax.ShapeDtypeStruct((S, D), F32), jax.ShapeDtypeStruct((1, D), F32)],
        compiler_params=_params("arbitrary"),
    )(y, target)


def _row_tile(R):
    for t in (256, 128, 64, 32, 16, 8):
        if R % t == 0:
            return t
    return R


def _pair_add_bf16(name, g, b1, kind, c_arr, me_arr=None):
    P, Rh, C = b1.shape
    tr = _row_tile(Rh)
    nb = Rh // tr
    own = me_arr is not None

    def body(*refs):
        g_ref, b_ref, o_ref = refs[1 + own:4 + own]
        val = (g_ref[...] + b_ref[...]).astype(o_ref.dtype)
        o_ref[...] = val
        if own:
            @pl.when(pl.program_id(1) == refs[1][0])
            def _():
                refs[-1][...] = val

    if kind == "cols":
        g_spec = pl.BlockSpec((None, tr, C), lambda i, p, c_ref, *_: (p, c_ref[0] * nb + i, 0))
    else:
        g_spec = pl.BlockSpec((tr, C), lambda i, p, c_ref, *_: ((2 * p + c_ref[0]) * nb + i, 0))
    blk = pl.BlockSpec((None, tr, C), lambda i, p, *_: (p, i, 0))
    shape = jax.ShapeDtypeStruct((P, Rh, C), BF16)
    if not own:
        return pl.pallas_call(
            body, name=name,
            grid_spec=pltpu.PrefetchScalarGridSpec(num_scalar_prefetch=1, grid=(nb, P), in_specs=[g_spec, blk], out_specs=blk),
            out_shape=shape, compiler_params=_params("parallel", "parallel"),
        )(c_arr, g, b1)
    mine = pl.BlockSpec((None, tr, C), lambda i, p, c_ref, me_ref: (me_ref[0], i, 0))
    return pl.pallas_call(
        body, name=name,
        grid_spec=pltpu.PrefetchScalarGridSpec(num_scalar_prefetch=2, grid=(nb, P), in_specs=[g_spec, blk], out_specs=[blk, mine]),
        out_shape=[shape, shape], compiler_params=_params("parallel", "arbitrary"),
    )(c_arr, me_arr, g, b1)


def _sum_slots(name, b, half_arr=None):
    P, R, C = b.shape
    tr = _row_tile(R)
    nb = R // tr

    def body(*refs):
        b_ref, o_ref = refs[-2:]
        acc = b_ref[0].astype(F32)
        for s in range(1, P):
            acc = acc + b_ref[s].astype(F32)
        o_ref[...] = acc

    if half_arr is None:
        return pl.pallas_call(
            body, name=name, grid=(nb,),
            in_specs=[pl.BlockSpec((P, tr, C), lambda i: (0, i, 0))],
            out_specs=pl.BlockSpec((tr, C), lambda i: (i, 0)),
            out_shape=jax.ShapeDtypeStruct((R, C), F32),
            compiler_params=_params("parallel"),
        )(b)
    return pl.pallas_call(
        body, name=name,
        grid_spec=pltpu.PrefetchScalarGridSpec(
            num_scalar_prefetch=1, grid=(nb,),
            in_specs=[pl.BlockSpec((P, tr, C), lambda i, half: (0, i, 0))],
            out_specs=pl.BlockSpec((tr, C), lambda i, half: (half[0] * nb + i, 0))),
        out_shape=jax.ShapeDtypeStruct((2 * R, C), F32),
        compiler_params=_params("parallel"),
    )(half_arr, b)


def _adamw(name, w, g, m, v):
    R, C = w.shape
    tr = _row_tile(R)
    c1 = 1.0 - ADAM_B1 ** ADAM_STEP
    c2 = 1.0 - ADAM_B2 ** ADAM_STEP

    def body(w_ref, g_ref, m_ref, v_ref, d_ref, nm_ref, nv_ref):
        gg = g_ref[...]
        nm = ADAM_B1 * m_ref[...] + (1.0 - ADAM_B1) * gg
        nv = ADAM_B2 * v_ref[...] + (1.0 - ADAM_B2) * (gg * gg)
        m_hat = nm / c1
        v_hat = nv / c2
        d_ref[...] = -ADAM_LR * (m_hat / (jnp.sqrt(v_hat) + ADAM_EPS) + ADAM_WD * w_ref[...])
        nm_ref[...] = nm
        nv_ref[...] = nv

    spec = pl.BlockSpec((tr, C), lambda i: (i, 0))
    return pl.pallas_call(
        body, name=name, grid=(R // tr,), in_specs=[spec] * 4, out_specs=[spec] * 3,
        out_shape=[jax.ShapeDtypeStruct((R, C), F32)] * 3,
        compiler_params=_params("parallel"),
    )(w, g, m, v)


def _place():
    x, y, c = lax.axis_index("x"), lax.axis_index("y"), lax.axis_index("c")
    other_chips = [(1 - x, y), (x, 1 - y), (1 - x, 1 - y)]
    return x, y, c, other_chips


def _into_slot(name, w, dtype, slot_arr, n_slots):
    R, C = w.shape
    tr = _row_tile(R)

    def body(slot_ref, w_ref, o_ref):
        o_ref[...] = w_ref[...].astype(o_ref.dtype)

    return pl.pallas_call(
        body, name=name,
        grid_spec=pltpu.PrefetchScalarGridSpec(
            num_scalar_prefetch=1, grid=(R // tr,),
            in_specs=[pl.BlockSpec((tr, C), lambda i, slot: (i, 0))],
            out_specs=pl.BlockSpec((None, tr, C), lambda i, slot: (slot[0], i, 0))),
        out_shape=jax.ShapeDtypeStruct((n_slots, R, C), dtype),
        compiler_params=_params("parallel"),
    )(slot_arr, w)


def _gather_chips(bufs):
    n = len(bufs)

    def body(*refs):
        outs = refs[n:2 * n]
        ici_send, ici_recv, d2d_send, d2d_recv = refs[2 * n:]
        x, y, c, chips = _place()
        me = 2 * x + y
        started = []
        for k in range(n):
            rh = outs[k].shape[1] // 2
            mine = outs[k].at[me, pl.ds(c * rh, rh)]
            for j, (px, py) in enumerate(chips):
                cp = pltpu.make_async_remote_copy(
                    src_ref=mine, dst_ref=mine,
                    send_sem=ici_send.at[3 * k + j], recv_sem=ici_recv.at[3 * k + j],
                    device_id=(px, py, c), device_id_type=MESH)
                cp.start()
                started.append(cp)
        for k in range(n):
            rh = outs[k].shape[1] // 2
            for j, (px, py) in enumerate(chips):
                landed = outs[k].at[2 * px + py, pl.ds(c * rh, rh)]
                pltpu.make_async_remote_copy(
                    src_ref=landed, dst_ref=landed,
                    send_sem=ici_send.at[3 * k + j], recv_sem=ici_recv.at[3 * k + j],
                    device_id=(px, py, c), device_id_type=MESH).wait_recv()
                cp = pltpu.make_async_remote_copy(
                    src_ref=landed, dst_ref=landed,
                    send_sem=d2d_send.at[3 * k + j], recv_sem=d2d_recv.at[3 * k + j],
                    device_id=(x, y, 1 - c), device_id_type=MESH)
                cp.start()
                started.append(cp)
        for k in range(n):
            rh = outs[k].shape[1] // 2
            for j, (px, py) in enumerate(chips):
                landed = outs[k].at[2 * px + py, pl.ds((1 - c) * rh, rh)]
                pltpu.make_async_remote_copy(
                    src_ref=landed, dst_ref=landed,
                    send_sem=d2d_send.at[3 * k + j], recv_sem=d2d_recv.at[3 * k + j],
                    device_id=(x, y, 1 - c), device_id_type=MESH).wait_recv()
        for cp in started:
            cp.wait_send()

    return pl.pallas_call(
        body, name="gather_weights",
        in_specs=[ANY] * n, out_specs=[ANY] * n,
        out_shape=[jax.ShapeDtypeStruct(b.shape, b.dtype) for b in bufs],
        input_output_aliases={k: k for k in range(n)},
        scratch_shapes=[pltpu.SemaphoreType.DMA((3 * n,))] * 4,
        compiler_params=pltpu.CompilerParams(has_side_effects=True),
    )(*bufs)


def _gather_forward(bufs):
    n = len(bufs)

    def body(*refs):
        outs = refs[n:2 * n]
        d2d_send, d2d_recv = refs[2 * n:]
        x, y, c, chips = _place()
        started = []
        for k in range(n):
            rh = outs[k].shape[1] // 2
            for j, (px, py) in enumerate(chips):
                landed = outs[k].at[2 * px + py, pl.ds(c * rh, rh)]
                cp = pltpu.make_async_remote_copy(
                    src_ref=landed, dst_ref=landed,
                    send_sem=d2d_send.at[3 * k + j], recv_sem=d2d_recv.at[3 * k + j],
                    device_id=(x, y, 1 - c), device_id_type=MESH)
                cp.start()
                started.append(cp)
        for k in range(n):
            rh = outs[k].shape[1] // 2
            for j, (px, py) in enumerate(chips):
                landed = outs[k].at[2 * px + py, pl.ds((1 - c) * rh, rh)]
                pltpu.make_async_remote_copy(
                    src_ref=landed, dst_ref=landed,
                    send_sem=d2d_send.at[3 * k + j], recv_sem=d2d_recv.at[3 * k + j],
                    device_id=(x, y, 1 - c), device_id_type=MESH).wait_recv()
        for cp in started:
            cp.wait_send()

    return pl.pallas_call(
        body, name="gather_rest_forward",
        in_specs=[ANY] * n, out_specs=[ANY] * n,
        out_shape=[jax.ShapeDtypeStruct(b.shape, b.dtype) for b in bufs],
        input_output_aliases={k: k for k in range(n)},
        scratch_shapes=[pltpu.SemaphoreType.DMA((3 * n,))] * 2,
        compiler_params=pltpu.CompilerParams(has_side_effects=True),
    )(*bufs)


def _to_chips_start(name, arrays, n, src_fn, dst_fn, after):
    m = len(arrays)

    def body(*refs):
        send_sem, recv_sem = refs[m + 1], refs[m + 2]
        thru = refs[m + 3:2 * m + 3]
        token = refs[2 * m + 3]
        x, y, c, chips = _place()
        for k in range(n):
            for j, (px, py) in enumerate(chips):
                pltpu.make_async_remote_copy(
                    src_ref=src_fn(thru, k, px, py, x, y, c), dst_ref=dst_fn(thru, k, px, py, x, y, c),
                    send_sem=send_sem.at[3 * k + j], recv_sem=recv_sem.at[3 * k + j],
                    device_id=(px, py, c), device_id_type=MESH).start()
        token[...] = jnp.zeros_like(token)

    res = pl.pallas_call(
        body, name=name,
        out_shape=(pltpu.SemaphoreType.DMA((3 * n,)), pltpu.SemaphoreType.DMA((3 * n,)),
                   *[pltpu.HBM(a.shape, a.dtype) for a in arrays], jax.ShapeDtypeStruct((8, LANES), F32)),
        in_specs=[HBM] * m + [ANY],
        out_specs=(SEM, SEM, *[HBM] * m, pl.BlockSpec(memory_space=pltpu.VMEM)),
        input_output_aliases={i: i + 2 for i in range(m)},
        compiler_params=pltpu.CompilerParams(has_side_effects=EFFECT),
    )(*[pltpu.with_memory_space_constraint(a, pltpu.HBM) for a in arrays], after)
    return res[0], res[1], list(res[2:2 + m]), res[2 + m]


def _to_chips_wait(name, send_sem, recv_sem, arrays, n, src_fn, land_fn, after):
    m = len(arrays)

    def body(*refs):
        send, recv = refs[m], refs[m + 1]
        outs = refs[m + 3:]
        x, y, c, chips = _place()
        for k in range(n):
            for j, (px, py) in enumerate(chips):
                cp = pltpu.make_async_remote_copy(
                    src_ref=src_fn(outs, k, px, py, x, y, c), dst_ref=land_fn(outs, k, px, py, x, y, c),
                    send_sem=send.at[3 * k + j], recv_sem=recv.at[3 * k + j],
                    device_id=(px, py, c), device_id_type=MESH)
                cp.wait_send()
                cp.wait_recv()

    res = pl.pallas_call(
        body, name=name,
        out_shape=[pltpu.HBM(a.shape, a.dtype) for a in arrays],
        in_specs=[HBM] * m + [SEM, SEM, ANY], out_specs=[HBM] * m,
        input_output_aliases={i: i for i in range(m)},
        compiler_params=pltpu.CompilerParams(has_side_effects=EFFECT),
    )(*arrays, send_sem, recv_sem, after)
    return list(res)


def _slot_half(refs, k, chip, c):
    rh = refs[k].shape[1] // 2
    return refs[k].at[chip, pl.ds(c * rh, rh)]


def _ag_mine(refs, k, px, py, x, y, c):
    return _slot_half(refs, k, 2 * x + y, c)


def _ag_theirs(refs, k, px, py, x, y, c):
    return _slot_half(refs, k, 2 * px + py, c)


def _rs_ends(n):
    def src(refs, k, px, py, x, y, c):
        return refs[k].at[2 * px + py]

    def dst(refs, k, px, py, x, y, c):
        return refs[n + k].at[2 * x + y]

    def land(refs, k, px, py, x, y, c):
        return refs[n + k].at[2 * px + py]

    return src, dst, land


def _half(ref, kind, p, c, rh):
    if kind == "cols":
        return ref.at[p, pl.ds(c * rh, rh)]
    return ref.at[pl.ds((2 * p + c) * rh, rh)]


def _swap_halves(name, grads, kinds, rhs):
    n = len(grads)

    def body(*refs):
        ins, outs = refs[:n], refs[n:2 * n]
        send_sem, recv_sem = refs[2 * n:]
        x, y, c, _ = _place()
        started = []
        for k in range(n):
            for p in range(N_CHIPS):
                cp = pltpu.make_async_remote_copy(
                    src_ref=_half(ins[k], kinds[k], p, 1 - c, rhs[k]), dst_ref=outs[k].at[p],
                    send_sem=send_sem.at[N_CHIPS * k + p], recv_sem=recv_sem.at[N_CHIPS * k + p],
                    device_id=(x, y, 1 - c), device_id_type=MESH)
                cp.start()
                started.append(cp)
        for cp in started:
            cp.wait()

    out_shape = []
    for g, kind, rh in zip(grads, kinds, rhs):
        out_shape.append(jax.ShapeDtypeStruct((N_CHIPS, rh, g.shape[-1]), g.dtype))
    return pl.pallas_call(
        body, name=name,
        in_specs=[ANY] * n, out_specs=[ANY] * n, out_shape=out_shape,
        scratch_shapes=[pltpu.SemaphoreType.DMA((N_CHIPS * n,))] * 2,
        compiler_params=pltpu.CompilerParams(has_side_effects=True),
    )(*grads)


def _scatter_partials(name, parts):
    n = len(parts)

    def body(*refs):
        ins, outs = refs[:n], refs[n:2 * n]
        send_sem, recv_sem, local_sem = refs[2 * n:]
        x, y, c, chips = _place()
        me = 2 * x + y
        started = []
        for k in range(n):
            cp = pltpu.make_async_copy(ins[k].at[me], outs[k].at[me], local_sem.at[k])
            cp.start()
            started.append(cp)
            for j, (px, py) in enumerate(chips):
                cp = pltpu.make_async_remote_copy(
                    src_ref=ins[k].at[2 * px + py], dst_ref=outs[k].at[me],
                    send_sem=send_sem.at[3 * k + j], recv_sem=recv_sem.at[3 * k + j],
                    device_id=(px, py, c), device_id_type=MESH)
                cp.start()
                started.append(cp)
        for k in range(n):
            for j, (px, py) in enumerate(chips):
                landed = outs[k].at[2 * px + py]
                pltpu.make_async_remote_copy(
                    src_ref=landed, dst_ref=landed,
                    send_sem=send_sem.at[3 * k + j], recv_sem=recv_sem.at[3 * k + j],
                    device_id=(px, py, c), device_id_type=MESH).wait_recv()
        for k in range(n):
            started[4 * k].wait()
            for j in range(3):
                started[4 * k + 1 + j].wait_send()

    return pl.pallas_call(
        body, name=name,
        in_specs=[ANY] * n, out_specs=[ANY] * n,
        out_shape=[jax.ShapeDtypeStruct(s.shape, s.dtype) for s in parts],
        scratch_shapes=[pltpu.SemaphoreType.DMA((3 * n,))] * 2 + [pltpu.SemaphoreType.DMA((n,))],
        compiler_params=pltpu.CompilerParams(has_side_effects=True),
    )(*parts)


def _join_halves(name, fulls):
    n = len(fulls)

    def body(*refs):
        outs = refs[n:2 * n]
        send_sem, recv_sem = refs[2 * n:]
        x, y, c, _ = _place()
        started = []
        for k in range(n):
            rh = outs[k].shape[0] // 2
            mine = outs[k].at[pl.ds(c * rh, rh)]
            cp = pltpu.make_async_remote_copy(
                src_ref=mine, dst_ref=mine, send_sem=send_sem.at[k], recv_sem=recv_sem.at[k],
                device_id=(x, y, 1 - c), device_id_type=MESH)
            cp.start()
            started.append(cp)
        for k in range(n):
            rh = outs[k].shape[0] // 2
            theirs = outs[k].at[pl.ds((1 - c) * rh, rh)]
            pltpu.make_async_remote_copy(
                src_ref=theirs, dst_ref=theirs, send_sem=send_sem.at[k], recv_sem=recv_sem.at[k],
                device_id=(x, y, 1 - c), device_id_type=MESH).wait_recv()
        for cp in started:
            cp.wait_send()

    return pl.pallas_call(
        body, name=name,
        in_specs=[ANY] * n, out_specs=[ANY] * n,
        out_shape=[jax.ShapeDtypeStruct(f.shape, f.dtype) for f in fulls],
        input_output_aliases={k: k for k in range(n)},
        scratch_shapes=[pltpu.SemaphoreType.DMA((n,))] * 2,
        compiler_params=pltpu.CompilerParams(has_side_effects=True),
    )(*fulls)


def _gather_all(block):
    def body(in_ref, out_ref, send_sem, recv_sem, local_sem):
        x, y, c, _ = _place()

        def slot(px, py, pc):
            return out_ref.at[4 * px + 2 * py + pc]

        loc = pltpu.make_async_copy(in_ref, slot(x, y, c), local_sem)
        loc.start()
        started = []
        for d in range(1, N_DEV):
            fx, fy, fc = d >> 2, (d >> 1) & 1, d & 1
            cp = pltpu.make_async_remote_copy(
                src_ref=in_ref, dst_ref=slot(x, y, c), send_sem=send_sem.at[d - 1], recv_sem=recv_sem.at[d - 1],
                device_id=(x ^ fx, y ^ fy, c ^ fc), device_id_type=MESH)
            cp.start()
            started.append(cp)
        for d in range(1, N_DEV):
            fx, fy, fc = d >> 2, (d >> 1) & 1, d & 1
            landed = slot(x ^ fx, y ^ fy, c ^ fc)
            pltpu.make_async_remote_copy(
                src_ref=in_ref, dst_ref=landed, send_sem=send_sem.at[d - 1], recv_sem=recv_sem.at[d - 1],
                device_id=(x ^ fx, y ^ fy, c ^ fc), device_id_type=MESH).wait_recv()
        for cp in started:
            cp.wait_send()
        loc.wait()

    return pl.pallas_call(
        body, name="gather_small_grads",
        in_specs=[ANY], out_specs=ANY,
        out_shape=jax.ShapeDtypeStruct((N_DEV,) + block.shape, block.dtype),
        scratch_shapes=[pltpu.SemaphoreType.DMA((N_DEV - 1,))] * 2 + [pltpu.SemaphoreType.DMA(())],
        compiler_params=pltpu.CompilerParams(has_side_effects=True),
    )(block)


def _pack(pieces):
    flat = jnp.concatenate([p.reshape(-1) for p in pieces])
    n = flat.shape[0]
    padded = -(-n // (8 * LANES)) * (8 * LANES)
    return jnp.pad(flat, (0, padded - n)).reshape(-1, LANES)


def _unpack(packed, shapes):
    flat = packed.reshape(-1)
    out, at = [], 0
    for s in shapes:
        n = math.prod(s)
        out.append(flat[at:at + n].reshape(s))
        at += n
    return out


def kernel(x, meta_tokens, pre_mix_g, w_in, gate_b, dw_w, dw_b, conv_ln_g, conv_ln_b, w_conv_out, w_attn_out, w_o, post_mix_g, pre_ffn_g, w_ffn_in, w_ffn_out, post_ffn_g, loss_target, m_meta_tokens, m_pre_mix_g, m_w_in, m_gate_b, m_dw_w, m_dw_b, m_conv_ln_g, m_conv_ln_b, m_w_conv_out, m_w_attn_out, m_w_o, m_post_mix_g, m_pre_ffn_g, m_w_ffn_in, m_w_ffn_out, m_post_ffn_g, v_meta_tokens, v_pre_mix_g, v_w_in, v_gate_b, v_dw_w, v_dw_b, v_conv_ln_g, v_conv_ln_b, v_w_conv_out, v_w_attn_out, v_w_o, v_post_mix_g, v_pre_ffn_g, v_w_ffn_in, v_w_ffn_out, v_post_ffn_g):
    S, D = x.shape[1], x.shape[2]
    L = S + N_META
    T = -(-L // ROW_BLOCK) * ROW_BLOCK
    Ta = -(-L // ATT_BLOCK) * ATT_BLOCK
    tm = _tile(T, MM_ROWS)
    ts = _tile(T, STAGE_ROWS)
    tw = _tile(T, WIDE_STAGE_ROWS)
    H = D // HEAD_DIM
    F = w_ffn_out.shape[1] * N_CHIPS
    Dc = D // N_CHIPS
    P = N_CHIPS
    me = 2 * lax.axis_index("x") + lax.axis_index("y")
    c_arr = lax.axis_index("c").astype(jnp.int32).reshape(1)

    dw_w_pad = jnp.pad(dw_w[0], ((0, CONV_PAD - CONV_WIDTH), (0, 0)))
    me_arr = me.astype(jnp.int32).reshape(1)
    to_gather = [("w_in", w_in[0], BF16), ("w_conv_out", w_conv_out[0], BF16), ("w_attn_out", w_attn_out[0], BF16),
                 ("w_o", w_o[0], BF16), ("w_ffn_in", w_ffn_in[0], BF16), ("w_ffn_out", w_ffn_out[0], BF16),
                 ("meta", meta_tokens, F32), ("taps", dw_w_pad, F32)]
    slot = {n: _into_slot("slot_" + n, w, dt, me_arr, P) for n, w, dt in to_gather}
    win3, meta4, taps4 = _gather_chips([slot["w_in"], slot["meta"], slot["taps"]])
    meta_full = meta4.transpose(1, 0, 2).reshape(N_META, D)
    taps = taps4.transpose(1, 0, 2).reshape(CONV_PAD, D)[:CONV_WIDTH]
    later = ["w_conv_out", "w_attn_out", "w_o", "w_ffn_in", "w_ffn_out"]
    ag_send, ag_recv, in_flight, ag_token = _to_chips_start(
        "gather_rest_start", [slot[n] for n in later], len(later), _ag_mine, _ag_mine, meta4)

    h0 = jnp.concatenate([meta_full, x[0], jnp.zeros((T - L, D), F32)], axis=0)
    (u1,) = _rowwise_fwd("rms_pre_mix", f_rms, [(h0, D, 0)], [(pre_mix_g + ag_token[0:1, 0:1], D, 0)], [(D, BF16)], T, ts)
    p = _mm_nn_cols("mm_in", u1, win3, tm)
    q, k, v = _qkv_split(p, D, T, Ta)
    o2, rtot = _attn_fwd(q, k, v)
    landed = _to_chips_wait("gather_rest_wait", ag_send, ag_recv, in_flight, len(later), _ag_mine, _ag_theirs, o2)
    wco4, wao4, wo4, wfi3, wfo4 = _gather_forward(landed)
    wco, wao, wo = (w.reshape(D, D) for w in (wco4, wao4, wo4))
    wfo = wfo4.reshape(F, D)
    (uglu,) = _rowwise_fwd("glu", f_glu, [(p, D, 0), (p, D, 1)], [], [(D, F32)], T, ts)
    yc = _shift_conv("dwconv", uglu, taps, CONV_PAD, CONV_PAD - (CONV_WIDTH - 1), T)
    conv_pars = [(dw_b, D, 0), (conv_ln_g, D, 0), (conv_ln_b, D, 0)]
    (ys,) = _rowwise_fwd("conv_post", f_convpost, [(yc, D, 0)], conv_pars, [(D, BF16)], T, ts)
    y_conv = _mm_nn("mm_conv_out", ys, wco, tm)
    y_attn = _mm_nn("mm_attn_out", o2, wao, tm, rows=T)
    mix_rows = [(p, D, 5), (p, D, 6), (y_conv, D, 0), (y_attn, D, 0)]
    mix_pars = [(gate_b, D, 0), (gate_b, D, 1)]
    (mixin,) = _rowwise_fwd("gate_mix", f_mix, mix_rows, mix_pars, [(D, BF16)], T, ts)
    mix = _mm_nn("mm_o", mixin, wo, tm)
    (h1,) = _rowwise_fwd("res_post_mix", f_res_rms, [(h0, D, 0), (mix, D, 0)], [(post_mix_g, D, 0)], [(D, F32)], T, ts)
    (u2,) = _rowwise_fwd("rms_pre_ffn", f_rms, [(h1, D, 0)], [(pre_ffn_g, D, 0)], [(D, BF16)], T, ts)
    ab = _mm_nn_cols("mm_ffn_in", u2, wfi3, tm)
    (fin,) = _rowwise_fwd("swiglu", f_swiglu, [(ab, F, 0), (ab, F, 1)], [], [(F, BF16)], T, tw)
    f = _mm_nn("mm_ffn_out", fin, wfo, tm)
    (h2,) = _rowwise_fwd("res_post_ffn", f_res_rms, [(h1, D, 0), (f, D, 0)], [(post_ffn_g, D, 0)], [(D, F32)], T, ts)

    dy, part = _loss_head(h2[N_META:L], loss_target[0], _row_tile(S))
    loss = lax.psum(0.5 * jnp.sum(part) / D, ("x", "y", "c"))
    dh2 = jnp.pad(dy, ((N_META, T - L), (0, 0)))

    (df,), (g_post_ffn,) = _rowwise_bwd("res_post_ffn_bwd", f_res_rms, [(h1, D, 0), (f, D, 0)], [(post_ffn_g, D, 0)],
                                        [(dh2, D, 0)], [None, BF16], T, ts)
    dfin = _mm_nt("mm_ffn_out_dx", df, wfo, tm)
    g_wfo = _mm_tn("mm_ffn_out_dw", fin, df, tm, 2)
    (da, db), _ = _rowwise_bwd("swiglu_bwd", f_swiglu, [(ab, F, 0), (ab, F, 1)], [], [(dfin, F, 0)], [BF16, BF16], T, tw)
    dab = jnp.concatenate([da, db], axis=1)
    du2 = _mm_nt_cols("mm_ffn_in_dx", dab, wfi3, tm)
    g_wfi = _mm_tn_cols("mm_ffn_in_dw", u2, dab, tm, P)
    (dh1,), (g_pre_ffn,) = _rowwise_bwd("rms_pre_ffn_bwd", f_rms_id, [(h1, D, 0)], [(pre_ffn_g, D, 0)],
                                        [(du2, D, 0), (dh2, D, 0)], [F32], T, ts)
    (dmix,), (g_post_mix,) = _rowwise_bwd("res_post_mix_bwd", f_res_rms, [(h0, D, 0), (mix, D, 0)], [(post_mix_g, D, 0)],
                                          [(dh1, D, 0)], [None, BF16], T, ts)
    dmixin = _mm_nt("mm_o_dx", dmix, wo, tm)
    g_wo = _mm_tn("mm_o_dw", mixin, dmix, tm, 1)
    (dpc, dpa, dyconv, dyattn), (g_gate_c, g_gate_a) = _rowwise_bwd(
        "gate_mix_bwd", f_mix, mix_rows, mix_pars, [(dmixin, D, 0)], [BF16, BF16, BF16, BF16], T, ts)
    g_wco = _mm_tn("mm_conv_out_dw", ys, dyconv, tm, 1)
    dys = _mm_nt("mm_conv_out_dx", dyconv, wco, tm)
    g_wao = _mm_tn("mm_attn_out_dw", o2, dyattn, tm, 1)
    do2 = _mm_nt("mm_attn_out_dx", dyattn, wao, tm, BF16, out_rows=Ta)

    early = [g_wco, g_wao, g_wo, g_wfi, g_wfo]
    early_kinds = ["rows", "rows", "rows", "cols", "rows"]
    early_rhs = [(g.shape[1] if kind == "cols" else g.shape[0] // P) // 2 for g, kind in zip(early, early_kinds)]
    early_sib = _swap_halves("grad_swap_halves_early", early, early_kinds, early_rhs)
    early_pairs = [_pair_add_bf16("grad_pair_add_%d" % (n + 1), g, b1, kind, c_arr, me_arr)
                   for n, (g, b1, kind) in enumerate(zip(early, early_sib, early_kinds))]
    rs_src, rs_dst, rs_land = _rs_ends(len(early))
    rs_send, rs_recv, rs_flight, rs_token = _to_chips_start(
        "grad_scatter_start", [pr[0] for pr in early_pairs] + [pr[1] for pr in early_pairs], len(early),
        rs_src, rs_dst, early_pairs[-1][1])
    rtot_bwd = rtot + rs_token[0, 0]
    (dyc,), (g_dw_b, g_ln_g, g_ln_b) = _rowwise_bwd("conv_post_bwd", f_convpost, [(yc, D, 0)], conv_pars,
                                                    [(dys, D, 0)], [F32], T, ts)
    duglu = _shift_conv("dwconv_dx", dyc, taps[::-1], 0, 0, T)
    g_taps = _conv_dw("dwconv_dw", uglu, dyc, T)
    (dp0, dp1), _ = _rowwise_bwd("glu_bwd", f_glu, [(p, D, 0), (p, D, 1)], [], [(duglu, D, 0)], [BF16, BF16], T, ts)
    dq, dk, dv = _attn_bwd(q, k, v, do2, rtot_bwd)
    rs_done = _to_chips_wait("grad_scatter_wait", rs_send, rs_recv, rs_flight, len(early), rs_src, rs_land, dq)
    early_slots = rs_done[len(early):]
    dp = jnp.concatenate([dp0, dp1, dq[:T], dk[:T], dv[:T], dpc, dpa], axis=1)
    du1 = _mm_nt_cols("mm_in_dx", dp, win3, tm)
    g_win = _mm_tn_cols("mm_in_dw", u1, dp, tm, P)
    (dh0,), (g_pre_mix,) = _rowwise_bwd("rms_pre_mix_bwd", f_rms_id, [(h0, D, 0)], [(pre_mix_g, D, 0)],
                                        [(du1, D, 0), (dh1, D, 0)], [F32], T, ts)
    grad_x = dh0[N_META:L][None]

    (win_sib,) = _swap_halves("grad_swap_halves_in", [g_win], ["cols"], [g_win.shape[1] // 2])
    (win_slots,) = _scatter_partials("grad_scatter_in", [_pair_add_bf16("grad_pair_add_0", g_win, win_sib, "cols", c_arr)])
    g_big = _join_halves("grad_join_halves", [_sum_slots("grad_chip_sum_%d" % n, s, c_arr)
                                              for n, s in enumerate([win_slots] + early_slots)])

    small_shapes = [(1, D), (1, D), (1, D), (CONV_WIDTH, D), (1, D), (1, D), (1, D), (1, D), (1, D), (1, D), (N_META, D)]
    small = _pack([g_pre_mix, g_gate_c, g_gate_a, g_taps, g_dw_b, g_ln_g, g_ln_b, g_post_mix, g_pre_ffn, g_post_ffn,
                   dh0[:N_META]])
    summed = _sum_slots("small_grad_sum", _gather_all(small))
    (s_pre_mix, s_gate_c, s_gate_a, s_taps, s_dw_b, s_ln_g, s_ln_b, s_post_mix, s_pre_ffn, s_post_ffn,
     s_meta) = _unpack(summed, small_shapes)
    s_gate_b = jnp.concatenate([s_gate_c, s_gate_a], axis=1)
    s_taps = lax.dynamic_slice_in_dim(s_taps, me * Dc, Dc, axis=1)[None]
    s_meta = lax.dynamic_slice_in_dim(s_meta, me * Dc, Dc, axis=1)

    grads = {
        "meta_tokens": s_meta, "pre_mix_g": s_pre_mix, "w_in": g_big[0][None], "gate_b": s_gate_b, "dw_w": s_taps,
        "dw_b": s_dw_b, "conv_ln_g": s_ln_g, "conv_ln_b": s_ln_b, "w_conv_out": g_big[1][None],
        "w_attn_out": g_big[2][None], "w_o": g_big[3][None], "post_mix_g": s_post_mix, "pre_ffn_g": s_pre_ffn,
        "w_ffn_in": g_big[4][None], "w_ffn_out": g_big[5][None], "post_ffn_g": s_post_ffn,
    }
    weights = {
        "meta_tokens": (meta_tokens, m_meta_tokens, v_meta_tokens), "pre_mix_g": (pre_mix_g, m_pre_mix_g, v_pre_mix_g),
        "w_in": (w_in, m_w_in, v_w_in), "gate_b": (gate_b, m_gate_b, v_gate_b), "dw_w": (dw_w, m_dw_w, v_dw_w),
        "dw_b": (dw_b, m_dw_b, v_dw_b), "conv_ln_g": (conv_ln_g, m_conv_ln_g, v_conv_ln_g),
        "conv_ln_b": (conv_ln_b, m_conv_ln_b, v_conv_ln_b), "w_conv_out": (w_conv_out, m_w_conv_out, v_w_conv_out),
        "w_attn_out": (w_attn_out, m_w_attn_out, v_w_attn_out), "w_o": (w_o, m_w_o, v_w_o),
        "post_mix_g": (post_mix_g, m_post_mix_g, v_post_mix_g), "pre_ffn_g": (pre_ffn_g, m_pre_ffn_g, v_pre_ffn_g),
        "w_ffn_in": (w_ffn_in, m_w_ffn_in, v_w_ffn_in), "w_ffn_out": (w_ffn_out, m_w_ffn_out, v_w_ffn_out),
        "post_ffn_g": (post_ffn_g, m_post_ffn_g, v_post_ffn_g),
    }
    names = list(weights)
    big_names = ["w_in", "w_conv_out", "w_attn_out", "w_o", "w_ffn_in", "w_ffn_out"]
    small_names = [n for n in names if n not in big_names]

    delta, new_m, new_v = {}, {}, {}
    for n in big_names:
        w, m, v2 = weights[n]
        d, nm, nv = _adamw("adamw_" + n, w[0], grads[n][0], m[0], v2[0])
        delta[n], new_m[n], new_v[n] = d[None], nm[None], nv[None]
    shapes = [weights[n][0].shape for n in small_names]
    packed = [_pack([weights[n][k] for n in small_names]) for k in range(3)]
    d, nm, nv = _adamw("adamw_small", packed[0], _pack([grads[n] for n in small_names]), packed[1], packed[2])
    for n, dd, mm, vv in zip(small_names, _unpack(d, shapes), _unpack(nm, shapes), _unpack(nv, shapes)):
        delta[n], new_m[n], new_v[n] = dd, mm, vv

    return (loss, grad_x, *[grads[n].reshape(weights[n][0].shape) for n in names], *[delta[n] for n in names],
            *[new_m[n] for n in names], *[new_v[n] for n in names])
```

```python
import math

import jax
import jax.numpy as jnp
from jax import lax
from jax.experimental import pallas as pl
from jax.experimental.pallas import tpu as pltpu

F32 = jnp.float32
BF16 = jnp.bfloat16

N_META = 16
CONV_WIDTH = 31
CONV_PAD = 32
HEAD_DIM = 64
RMS_EPS = 1e-6
LN_EPS = 1e-5
ROW_BLOCK = 128
MXU_WIDTH = 256
ATT_BLOCK = MXU_WIDTH
ATT_HEADS = 4
LANES = 128
N_CHIPS = 4
N_DEV = 8
MM_ROWS = 544
CONTRACT_ROWS = 2176
STAGE_ROWS = 272
WIDE_STAGE_ROWS = 128
VMEM_LIMIT = 56 * 1024 * 1024

ADAM_LR = 0.001
ADAM_B1 = 0.9
ADAM_B2 = 0.999
ADAM_EPS = 1e-08
ADAM_WD = 0.01
ADAM_STEP = 10

MESH = pl.DeviceIdType.MESH
ANY = pl.BlockSpec(memory_space=pl.ANY)
HBM = pl.BlockSpec(memory_space=pltpu.HBM)
SEM = pl.BlockSpec(memory_space=pltpu.SEMAPHORE)
EFFECT = pltpu.SideEffectType.DATAFLOW_SIDE_EFFECTING


def _params(*sem):
    return pltpu.CompilerParams(dimension_semantics=sem if sem else None, vmem_limit_bytes=VMEM_LIMIT)


def _rms(x, g):
    return x * lax.rsqrt(jnp.mean(x * x, axis=-1, keepdims=True) + RMS_EPS) * g


def f_rms(h, g):
    return (_rms(h, g),)


def f_rms_id(h, g):
    return (_rms(h, g), h)


def f_res_rms(h, m, g):
    return (h + _rms(m, g),)


def f_glu(a, gate):
    return (a * lax.logistic(gate),)


def f_convpost(yc, b, ln_g, ln_b):
    y = yc + b
    mu = jnp.mean(y, axis=-1, keepdims=True)
    xc = y - mu
    var = jnp.mean(xc * xc, axis=-1, keepdims=True)
    yl = xc * lax.rsqrt(var + LN_EPS) * ln_g + ln_b
    return (yl * lax.logistic(yl),)


def f_mix(pc, pa, yc, ya, bc, ba):
    return (lax.logistic(pc + bc) * yc + lax.logistic(pa + ba) * ya,)


def f_swiglu(a, b):
    return (a * lax.logistic(a) * b,)


def _tile(T, target):
    return max(t for t in range(16, target + 1, 16) if T % t == 0)


def _row_map(j):
    return lambda i: (i, j)


def _par_map(j):
    return lambda i: (0, j)


def _rowwise_fwd(name, f, rows, pars, outs, T, tm):
    n_in = len(rows) + len(pars)

    def body(*refs):
        vals = [r[...].astype(F32) for r in refs[:n_in]]
        res = f(*vals)
        for o_ref, o in zip(refs[n_in:], res):
            o_ref[...] = o.astype(o_ref.dtype)

    in_specs = [pl.BlockSpec((tm, w), _row_map(j)) for _, w, j in rows]
    in_specs += [pl.BlockSpec((1, w), _par_map(j)) for _, w, j in pars]
    return pl.pallas_call(
        body, name=name, grid=(T // tm,),
        in_specs=in_specs,
        out_specs=[pl.BlockSpec((tm, w), _row_map(0)) for w, _ in outs],
        out_shape=[jax.ShapeDtypeStruct((T, w), dt) for w, dt in outs],
        compiler_params=_params("parallel"),
    )(*[a for a, _, _ in rows], *[a for a, _, _ in pars])


def _rowwise_bwd(name, f, rows, pars, cots, drow_dtypes, T, tm):
    n_r, n_p, n_c = len(rows), len(pars), len(cots)
    n_in = n_r + n_p + n_c
    keep = [k for k, dt in enumerate(drow_dtypes) if dt is not None]

    def body(*refs):
        rv = [r[...].astype(F32) for r in refs[:n_r]]
        pv = [r[...].astype(F32) for r in refs[n_r:n_r + n_p]]
        cv = [r[...].astype(F32) for r in refs[n_r + n_p:n_in]]
        _, vjp = jax.vjp(f, *rv, *pv)
        g = vjp(tuple(cv))
        drow_refs = refs[n_in:n_in + len(keep)]
        dpar_refs = refs[n_in + len(keep):]
        for r, k in zip(drow_refs, keep):
            r[...] = g[k].astype(r.dtype)

        @pl.when(pl.program_id(0) == 0)
        def _():
            for r in dpar_refs:
                r[...] = jnp.zeros_like(r)

        for r, gp in zip(dpar_refs, g[n_r:]):
            r[...] += gp

    in_specs = [pl.BlockSpec((tm, w), _row_map(j)) for _, w, j in rows]
    in_specs += [pl.BlockSpec((1, w), _par_map(j)) for _, w, j in pars]
    in_specs += [pl.BlockSpec((tm, w), _row_map(j)) for _, w, j in cots]
    out_specs = [pl.BlockSpec((tm, rows[k][1]), _row_map(0)) for k in keep]
    out_specs += [pl.BlockSpec((1, w), _par_map(0)) for _, w, _ in pars]
    out_shape = [jax.ShapeDtypeStruct((T, rows[k][1]), drow_dtypes[k]) for k in keep]
    out_shape += [jax.ShapeDtypeStruct((1, w), F32) for _, w, _ in pars]
    res = pl.pallas_call(
        body, name=name, grid=(T // tm,),
        in_specs=in_specs, out_specs=out_specs, out_shape=out_shape,
        compiler_params=_params("arbitrary"),
    )(*[a for a, _, _ in rows], *[a for a, _, _ in pars], *[a for a, _, _ in cots])
    return res[:len(keep)], res[len(keep):]


NN = (((1,), (0,)), ((), ()))
NT = (((1,), (1,)), ((), ()))
TN = (((0,), (0,)), ((), ()))


def _mm(name, a, b, dims, out_shape, grid, a_spec, b_spec, o_spec, red_axis=None, init=None):
    n_red = None if red_axis is None else grid[red_axis]

    def body(a_ref, b_ref, *rest):
        o_ref = rest[-1]
        prod = lax.dot_general(a_ref[...], b_ref[...], dims, preferred_element_type=F32)
        if n_red is None:
            o_ref[...] = prod.astype(o_ref.dtype)
        else:
            @pl.when(pl.program_id(red_axis) == 0)
            def _():
                o_ref[...] = prod

            @pl.when(pl.program_id(red_axis) > 0)
            def _():
                o_ref[...] += prod

    sem = ["parallel"] * len(grid)
    if red_axis is not None:
        sem[red_axis] = "arbitrary"
    if init is None:
        return pl.pallas_call(
            body, name=name, grid=grid, in_specs=[a_spec, b_spec], out_specs=o_spec, out_shape=out_shape,
            compiler_params=_params(*sem),
        )(a, b)
    return pl.pallas_call(
        body, name=name, grid=grid, in_specs=[a_spec, b_spec, ANY], out_specs=o_spec, out_shape=out_shape,
        input_output_aliases={2: 0}, compiler_params=_params(*sem),
    )(a, b, init)


def _mm_nn(name, a, w, tm, out_dtype=F32, rows=None):
    T, K = a.shape
    T = rows or T
    N = w.shape[1]
    return _mm(name, a, w, NN, jax.ShapeDtypeStruct((T, N), out_dtype), (T // tm,),
               pl.BlockSpec((tm, K), lambda i: (i, 0)), pl.BlockSpec((K, N), lambda i: (0, 0)),
               pl.BlockSpec((tm, N), lambda i: (i, 0)))


def _mm_nt(name, a, w, tm, out_dtype=F32, out_rows=None):
    T, N = a.shape
    K = w.shape[0]
    init = None if out_rows is None else jnp.zeros((out_rows, K), out_dtype)
    return _mm(name, a, w, NT, jax.ShapeDtypeStruct((out_rows or T, K), out_dtype), (T // tm,),
               pl.BlockSpec((tm, N), lambda i: (i, 0)), pl.BlockSpec((K, N), lambda i: (0, 0)),
               pl.BlockSpec((tm, K), lambda i: (i, 0)), init=init)


def _mm_tn(name, a, b, tm, n_row_blocks):
    K = a.shape[1]
    T, N = b.shape
    kb = K // n_row_blocks
    return _mm(name, a, b, TN, jax.ShapeDtypeStruct((K, N), F32), (n_row_blocks, T // tm),
               pl.BlockSpec((tm, kb), lambda r, t: (t, r)), pl.BlockSpec((tm, N), lambda r, t: (t, 0)),
               pl.BlockSpec((kb, N), lambda r, t: (r, 0)), red_axis=1)


def _mm_nn_cols(name, a, w3, tm):
    T, K = a.shape
    P, _, Ns = w3.shape
    return _mm(name, a, w3, NN, jax.ShapeDtypeStruct((T, P * Ns), F32), (P, T // tm),
               pl.BlockSpec((tm, K), lambda p, i: (i, 0)), pl.BlockSpec((None, K, Ns), lambda p, i: (p, 0, 0)),
               pl.BlockSpec((tm, Ns), lambda p, i: (i, p)))


def _mm_nt_cols(name, a, w3, tm):
    T = a.shape[0]
    P, K, Ns = w3.shape
    return _mm(name, a, w3, NT, jax.ShapeDtypeStruct((T, K), F32), (T // tm, P),
               pl.BlockSpec((tm, Ns), lambda i, p: (i, p)), pl.BlockSpec((None, K, Ns), lambda i, p: (p, 0, 0)),
               pl.BlockSpec((tm, K), lambda i, p: (i, 0)), red_axis=1)


def _mm_tn_cols(name, a, b, tm, P):
    T, K = a.shape
    Ns = b.shape[1] // P
    return _mm(name, a, b, TN, jax.ShapeDtypeStruct((P, K, Ns), F32), (P, T // tm),
               pl.BlockSpec((tm, K), lambda p, t: (t, 0)), pl.BlockSpec((tm, Ns), lambda p, t: (t, p)),
               pl.BlockSpec((None, K, Ns), lambda p, t: (p, 0, 0)), red_axis=1)


def _tap_windows(win, off, tb):
    out = []
    for b in range(8):
        taps = [j for j in range(CONV_WIDTH) if (off + j) % 8 == b]
        if taps:
            shifted = win[b:b + tb + CONV_PAD, :]
            out += [(shifted, off + j - b, j) for j in taps]
    return out


def _shift_conv(name, x, w, place, off, T):
    C = x.shape[1]
    tb = ROW_BLOCK
    zero_at = 0 if place else T

    def body(x_ref, w_ref, o_ref, xp_ref):
        xp_ref[pl.ds(zero_at, CONV_PAD), :] = jnp.zeros((CONV_PAD, LANES), F32)
        xp_ref[pl.ds(T + CONV_PAD, 8), :] = jnp.zeros((8, LANES), F32)
        xp_ref[pl.ds(place, T), :] = x_ref[...]

        def step(t, carry):
            base = pl.multiple_of(t * tb, tb)
            win = xp_ref[pl.ds(base, tb + CONV_PAD + 8), :]
            acc = jnp.zeros((tb, LANES), F32)
            for shifted, at, j in _tap_windows(win, off, tb):
                acc = acc + shifted[at:at + tb, :] * w_ref[pl.ds(j, 1), :]
            o_ref[pl.ds(base, tb), :] = acc
            return carry

        lax.fori_loop(0, T // tb, step, 0)

    return pl.pallas_call(
        body, name=name, grid=(C // LANES,),
        in_specs=[pl.BlockSpec((T, LANES), lambda c: (0, c)), pl.BlockSpec((CONV_WIDTH, LANES), lambda c: (0, c))],
        out_specs=pl.BlockSpec((T, LANES), lambda c: (0, c)),
        out_shape=jax.ShapeDtypeStruct((T, C), F32),
        scratch_shapes=[pltpu.VMEM((T + CONV_PAD + 8, LANES), F32)],
        compiler_params=_params("parallel"),
    )(x, w)


def _conv_dw(name, x, dy, T):
    C = x.shape[1]
    tb = ROW_BLOCK
    off = CONV_PAD - (CONV_WIDTH - 1)

    def body(x_ref, dy_ref, o_ref, xp_ref, acc_ref):
        xp_ref[pl.ds(0, CONV_PAD), :] = jnp.zeros((CONV_PAD, LANES), F32)
        xp_ref[pl.ds(T + CONV_PAD, 8), :] = jnp.zeros((8, LANES), F32)
        xp_ref[pl.ds(CONV_PAD, T), :] = x_ref[...]
        acc_ref[...] = jnp.zeros_like(acc_ref)

        def step(t, carry):
            base = pl.multiple_of(t * tb, tb)
            win = xp_ref[pl.ds(base, tb + CONV_PAD + 8), :]
            d = dy_ref[pl.ds(base, tb), :]
            for shifted, at, j in _tap_windows(win, off, tb):
                prod = shifted[at:at + tb, :] * d
                acc_ref[j] += jnp.sum(prod.reshape(tb // 8, 8, LANES), axis=0)
            return carry

        lax.fori_loop(0, T // tb, step, 0)
        for j in range(CONV_WIDTH):
            o_ref[pl.ds(j, 1), :] = jnp.sum(acc_ref[j], axis=0, keepdims=True)

    return pl.pallas_call(
        body, name=name, grid=(C // LANES,),
        in_specs=[pl.BlockSpec((T, LANES), lambda c: (0, c)), pl.BlockSpec((T, LANES), lambda c: (0, c))],
        out_specs=pl.BlockSpec((CONV_WIDTH, LANES), lambda c: (0, c)),
        out_shape=jax.ShapeDtypeStruct((CONV_WIDTH, C), F32),
        scratch_shapes=[pltpu.VMEM((T + CONV_PAD + 8, LANES), F32), pltpu.VMEM((CONV_WIDTH, 8, LANES), F32)],
        compiler_params=_params("parallel"),
    )(x, dy)


def _dot(a, b, dims=NN):
    return lax.dot_general(a, b, dims, preferred_element_type=F32)


def _tri_cumsum(x, tri):
    return _dot(x.astype(BF16), tri)


def _qkv_split(p, D, T, Ta):
    tb = ROW_BLOCK
    nt = T // tb
    scale = 1.0 / math.sqrt(HEAD_DIM)

    def body(q_ref, k_ref, v_ref, qo_ref, ko_ref, vo_ref):
        live = pl.program_id(0) < nt
        qo_ref[...] = jnp.where(live, q_ref[...] * scale, 0.0).astype(BF16)
        ko_ref[...] = jnp.where(live, k_ref[...], 0.0).astype(BF16)
        vo_ref[...] = jnp.where(live, v_ref[...], 0.0).astype(BF16)

    def col(n):
        return lambda i: (jnp.minimum(i, nt - 1), n)

    return pl.pallas_call(
        body, name="qkv_split", grid=(Ta // tb,),
        in_specs=[pl.BlockSpec((tb, D), col(2 + n)) for n in range(3)],
        out_specs=[pl.BlockSpec((tb, D), lambda i: (i, 0))] * 3,
        out_shape=[jax.ShapeDtypeStruct((Ta, D), BF16)] * 3,
        compiler_params=_params("parallel"),
    )(p, p, p)


def _pair_lanes(g):
    return slice((g // 2) * LANES, (g // 2 + 1) * LANES)


def _own_lanes(x_ref, B, G):
    first = lax.broadcasted_iota(jnp.int32, (B, LANES), 1) < HEAD_DIM
    out = []
    for g in range(G):
        x2 = x_ref[:, _pair_lanes(g)]
        out.append(jnp.where(first if g % 2 == 0 else jnp.logical_not(first), x2, jnp.zeros_like(x2)))
    return first, out


def _attn_fwd(q, k, v):
    T, D = q.shape
    H = D // HEAD_DIM
    B = ATT_BLOCK
    G = ATT_HEADS
    W = G // 2 * LANES

    def body(q_ref, k_ref, v_ref, o_ref, rt_ref):
        i = pl.program_id(1)
        row = lax.broadcasted_iota(jnp.int32, (B, B), 0)
        col = lax.broadcasted_iota(jnp.int32, (B, B), 1)
        below = col < row
        tri = (row >= col).astype(BF16)
        first, qs = _own_lanes(q_ref, B, G)

        def tile(j, carry, diagonal):
            sl = pl.ds(pl.multiple_of(j * B, B), B)
            out = []
            for g in range(G):
                c, acc = carry[g]
                z = _dot(qs[g], k_ref[sl, _pair_lanes(g)], NT)
                sp = jnp.maximum(z, 0.0) + jnp.log(1.0 + jnp.exp(-jnp.abs(z)))
                if diagonal:
                    sp = jnp.where(below, sp, 0.0)
                rw = _tri_cumsum(sp, tri)
                a = jnp.exp(z - (rw + c))
                if diagonal:
                    a = jnp.where(below, a, 0.0)
                acc = acc + _dot(a.astype(BF16), v_ref[sl, _pair_lanes(g)])
                out.append((c + rw[:, 0:1], acc))
            return tuple(out)

        carry = tile(i, tuple((jnp.zeros((B, 1), F32), jnp.zeros((B, LANES), F32)) for _ in range(G)), True)
        carry = lax.fori_loop(0, i, lambda jj, cr: tile(i - 1 - jj, cr, False), carry)
        for g in range(0, G, 2):
            o_ref[:, _pair_lanes(g)] = jnp.where(first, carry[g][1], carry[g + 1][1]).astype(o_ref.dtype)
        for g in range(G):
            rt_ref[g] = carry[g][0]

    return pl.pallas_call(
        body, name="attn_fwd", grid=(H // G, T // B),
        in_specs=[pl.BlockSpec((B, W), lambda h, i: (i, h)),
                  pl.BlockSpec((T, W), lambda h, i: (0, h)),
                  pl.BlockSpec((T, W), lambda h, i: (0, h))],
        out_specs=[pl.BlockSpec((B, W), lambda h, i: (i, h)),
                   pl.BlockSpec((G, B, 1), lambda h, i: (h, i, 0))],
        out_shape=[jax.ShapeDtypeStruct((T, D), BF16), jax.ShapeDtypeStruct((H, T, 1), F32)],
        compiler_params=_params("parallel", "arbitrary"),
    )(q, k, v)


def _attn_bwd(q, k, v, do, rt):
    T, D = q.shape
    H = D // HEAD_DIM
    B = ATT_BLOCK
    nq = T // B
    scale = 1.0 / math.sqrt(HEAD_DIM)
    G = ATT_HEADS
    W = G // 2 * LANES

    def body(q_ref, k_ref, v_ref, do_ref, rt_ref, dq_ref, dk_ref, dv_ref, dk_acc, dv_acc):
        i = pl.program_id(1)

        @pl.when(i == 0)
        def _():
            dk_acc[...] = jnp.zeros_like(dk_acc)
            dv_acc[...] = jnp.zeros_like(dv_acc)

        row = lax.broadcasted_iota(jnp.int32, (B, B), 0)
        col = lax.broadcasted_iota(jnp.int32, (B, B), 1)
        below = col < row
        tri = (row <= col).astype(BF16)
        first, qs = _own_lanes(q_ref, B, G)
        _, dos = _own_lanes(do_ref, B, G)

        def tile(j, carry, diagonal):
            sl = pl.ds(pl.multiple_of(j * B, B), B)
            out = []
            for h in range(G):
                pc, gc, dq = carry[h]
                qi, doi = qs[h], dos[h]
                kj, vj = k_ref[sl, _pair_lanes(h)], v_ref[sl, _pair_lanes(h)]
                z = _dot(qi, kj, NT)
                e = jnp.exp(-jnp.abs(z))
                inv = 1.0 / (1.0 + e)
                sp = jnp.maximum(z, 0.0) - jnp.log(inv)
                sg = jnp.where(z >= 0.0, inv, e * inv)
                if diagonal:
                    sp = jnp.where(below, sp, 0.0)
                pw = _tri_cumsum(sp, tri)
                a = jnp.exp(z - (rt_ref[h] - pc - pw + sp))
                if diagonal:
                    a = jnp.where(below, a, 0.0)
                g = a * _dot(doi, vj, NT)
                gw = _tri_cumsum(g, tri)
                dz = g - sg * (gc + gw)
                if diagonal:
                    dz = jnp.where(below, dz, 0.0)
                dzb = dz.astype(BF16)
                dq = dq + _dot(dzb, kj)
                dk_acc[sl, _pair_lanes(h)] += _dot(dzb, qi, TN)
                dv_acc[sl, _pair_lanes(h)] += _dot(a.astype(BF16), doi, TN)
                out.append((pc + pw[:, B - 1:B], gc + gw[:, B - 1:B], dq))
            return tuple(out)

        zero = jnp.zeros((B, 1), F32)
        carry = lax.fori_loop(0, i, lambda j, cr: tile(j, cr, False),
                              tuple((zero, zero, jnp.zeros((B, LANES), F32)) for _ in range(G)))
        carry = tile(i, carry, True)
        for h in range(0, G, 2):
            dq_ref[:, _pair_lanes(h)] = (jnp.where(first, carry[h][2], carry[h + 1][2]) * scale).astype(dq_ref.dtype)

        @pl.when(i == nq - 1)
        def _():
            dk_ref[...] = dk_acc[...].astype(dk_ref.dtype)
            dv_ref[...] = dv_acc[...].astype(dv_ref.dtype)

    blk = pl.BlockSpec((B, W), lambda h, i: (i, h))
    full = pl.BlockSpec((T, W), lambda h, i: (0, h))
    return pl.pallas_call(
        body, name="attn_bwd", grid=(H // G, nq),
        in_specs=[blk, full, full, blk, pl.BlockSpec((G, B, 1), lambda h, i: (h, i, 0))],
        out_specs=[blk, full, full],
        out_shape=[jax.ShapeDtypeStruct((T, D), BF16)] * 3,
        scratch_shapes=[pltpu.VMEM((T, W), F32)] * 2,
        compiler_params=_params("parallel", "arbitrary"),
    )(q, k, v, do, rt)


def _loss_head(y, target, tm):
    S, D = y.shape

    def body(y_ref, t_ref, dy_ref, part_ref):
        err = y_ref[...] - t_ref[...]
        dy_ref[...] = err * (1.0 / D)

        @pl.when(pl.program_id(0) == 0)
        def _():
            part_ref[...] = jnp.zeros_like(part_ref)

        part_ref[...] += jnp.sum(err * err, axis=0, keepdims=True)

    spec = pl.BlockSpec((tm, D), lambda i: (i, 0))
    return pl.pallas_call(
        body, name="loss_head", grid=(S // tm,), in_specs=[spec, spec],
        out_specs=[spec, pl.BlockSpec((1, D), lambda i: (0, 0))],
        out_shape=[jax.ShapeDtypeStruct((S, D), F32), jax.ShapeDtypeStruct((1, D), F32)],
        compiler_params=_params("arbitrary"),
    )(y, target)


def _row_tile(R):
    for t in (256, 128, 64, 32, 16, 8):
        if R % t == 0:
            return t
    return R


def _pair_add_bf16(name, g, b1, kind, c_arr, me_arr=None):
    P, Rh, C = b1.shape
    tr = _row_tile(Rh)
    nb = Rh // tr
    own = me_arr is not None

    def body(*refs):
        g_ref, b_ref, o_ref = refs[1 + own:4 + own]
        val = (g_ref[...] + b_ref[...]).astype(o_ref.dtype)
        o_ref[...] = val
        if own:
            @pl.when(pl.program_id(1) == refs[1][0])
            def _():
                refs[-1][...] = val

    if kind == "cols":
        g_spec = pl.BlockSpec((None, tr, C), lambda i, p, c_ref, *_: (p, c_ref[0] * nb + i, 0))
    else:
        g_spec = pl.BlockSpec((tr, C), lambda i, p, c_ref, *_: ((2 * p + c_ref[0]) * nb + i, 0))
    blk = pl.BlockSpec((None, tr, C), lambda i, p, *_: (p, i, 0))
    shape = jax.ShapeDtypeStruct((P, Rh, C), BF16)
    if not own:
        return pl.pallas_call(
            body, name=name,
            grid_spec=pltpu.PrefetchScalarGridSpec(num_scalar_prefetch=1, grid=(nb, P), in_specs=[g_spec, blk], out_specs=blk),
            out_shape=shape, compiler_params=_params("parallel", "parallel"),
        )(c_arr, g, b1)
    mine = pl.BlockSpec((None, tr, C), lambda i, p, c_ref, me_ref: (me_ref[0], i, 0))
    return pl.pallas_call(
        body, name=name,
        grid_spec=pltpu.PrefetchScalarGridSpec(num_scalar_prefetch=2, grid=(nb, P), in_specs=[g_spec, blk], out_specs=[blk, mine]),
        out_shape=[shape, shape], compiler_params=_params("parallel", "arbitrary"),
    )(c_arr, me_arr, g, b1)


def _sum_slots(name, b, half_arr=None):
    P, R, C = b.shape
    tr = _row_tile(R)
    nb = R // tr

    def body(*refs):
        b_ref, o_ref = refs[-2:]
        acc = b_ref[0].astype(F32)
        for s in range(1, P):
            acc = acc + b_ref[s].astype(F32)
        o_ref[...] = acc

    if half_arr is None:
        return pl.pallas_call(
            body, name=name, grid=(nb,),
            in_specs=[pl.BlockSpec((P, tr, C), lambda i: (0, i, 0))],
            out_specs=pl.BlockSpec((tr, C), lambda i: (i, 0)),
            out_shape=jax.ShapeDtypeStruct((R, C), F32),
            compiler_params=_params("parallel"),
        )(b)
    return pl.pallas_call(
        body, name=name,
        grid_spec=pltpu.PrefetchScalarGridSpec(
            num_scalar_prefetch=1, grid=(nb,),
            in_specs=[pl.BlockSpec((P, tr, C), lambda i, half: (0, i, 0))],
            out_specs=pl.BlockSpec((tr, C), lambda i, half: (half[0] * nb + i, 0))),
        out_shape=jax.ShapeDtypeStruct((2 * R, C), F32),
        compiler_params=_params("parallel"),
    )(half_arr, b)


def _adamw(name, w, g, m, v):
    R, C = w.shape
    tr = _row_tile(R)
    c1 = 1.0 - ADAM_B1 ** ADAM_STEP
    c2 = 1.0 - ADAM_B2 ** ADAM_STEP

    def body(w_ref, g_ref, m_ref, v_ref, d_ref, nm_ref, nv_ref):
        gg = g_ref[...]
        nm = ADAM_B1 * m_ref[...] + (1.0 - ADAM_B1) * gg
        nv = ADAM_B2 * v_ref[...] + (1.0 - ADAM_B2) * (gg * gg)
        m_hat = nm / c1
        v_hat = nv / c2
        d_ref[...] = -ADAM_LR * (m_hat / (jnp.sqrt(v_hat) + ADAM_EPS) + ADAM_WD * w_ref[...])
        nm_ref[...] = nm
        nv_ref[...] = nv

    spec = pl.BlockSpec((tr, C), lambda i: (i, 0))
    return pl.pallas_call(
        body, name=name, grid=(R // tr,), in_specs=[spec] * 4, out_specs=[spec] * 3,
        out_shape=[jax.ShapeDtypeStruct((R, C), F32)] * 3,
        compiler_params=_params("parallel"),
    )(w, g, m, v)


def _place():
    x, y, c = lax.axis_index("x"), lax.axis_index("y"), lax.axis_index("c")
    other_chips = [(1 - x, y), (x, 1 - y), (1 - x, 1 - y)]
    return x, y, c, other_chips


def _into_slot(name, w, dtype, slot_arr, n_slots):
    R, C = w.shape
    tr = _row_tile(R)

    def body(slot_ref, w_ref, o_ref):
        o_ref[...] = w_ref[...].astype(o_ref.dtype)

    return pl.pallas_call(
        body, name=name,
        grid_spec=pltpu.PrefetchScalarGridSpec(
            num_scalar_prefetch=1, grid=(R // tr,),
            in_specs=[pl.BlockSpec((tr, C), lambda i, slot: (i, 0))],
            out_specs=pl.BlockSpec((None, tr, C), lambda i, slot: (slot[0], i, 0))),
        out_shape=jax.ShapeDtypeStruct((n_slots, R, C), dtype),
        compiler_params=_params("parallel"),
    )(slot_arr, w)


def _gather_chips(bufs):
    n = len(bufs)

    def body(*refs):
        outs = refs[n:2 * n]
        ici_send, ici_recv, d2d_send, d2d_recv = refs[2 * n:]
        x, y, c, chips = _place()
        me = 2 * x + y
        started = []
        for k in range(n):
            rh = outs[k].shape[1] // 2
            mine = outs[k].at[me, pl.ds(c * rh, rh)]
            for j, (px, py) in enumerate(chips):
                cp = pltpu.make_async_remote_copy(
                    src_ref=mine, dst_ref=mine,
                    send_sem=ici_send.at[3 * k + j], recv_sem=ici_recv.at[3 * k + j],
                    device_id=(px, py, c), device_id_type=MESH)
                cp.start()
                started.append(cp)
        for k in range(n):
            rh = outs[k].shape[1] // 2
            for j, (px, py) in enumerate(chips):
                landed = outs[k].at[2 * px + py, pl.ds(c * rh, rh)]
                pltpu.make_async_remote_copy(
                    src_ref=landed, dst_ref=landed,
                    send_sem=ici_send.at[3 * k + j], recv_sem=ici_recv.at[3 * k + j],
                    device_id=(px, py, c), device_id_type=MESH).wait_recv()
                cp = pltpu.make_async_remote_copy(
                    src_ref=landed, dst_ref=landed,
                    send_sem=d2d_send.at[3 * k + j], recv_sem=d2d_recv.at[3 * k + j],
                    device_id=(x, y, 1 - c), device_id_type=MESH)
                cp.start()
                started.append(cp)
        for k in range(n):
            rh = outs[k].shape[1] // 2
            for j, (px, py) in enumerate(chips):
                landed = outs[k].at[2 * px + py, pl.ds((1 - c) * rh, rh)]
                pltpu.make_async_remote_copy(
                    src_ref=landed, dst_ref=landed,
                    send_sem=d2d_send.at[3 * k + j], recv_sem=d2d_recv.at[3 * k + j],
                    device_id=(x, y, 1 - c), device_id_type=MESH).wait_recv()
        for cp in started:
            cp.wait_send()

    return pl.pallas_call(
        body, name="gather_weights",
        in_specs=[ANY] * n, out_specs=[ANY] * n,
        out_shape=[jax.ShapeDtypeStruct(b.shape, b.dtype) for b in bufs],
        input_output_aliases={k: k for k in range(n)},
        scratch_shapes=[pltpu.SemaphoreType.DMA((3 * n,))] * 4,
        compiler_params=pltpu.CompilerParams(has_side_effects=True),
    )(*bufs)


def _gather_forward(bufs):
    n = len(bufs)

    def body(*refs):
        outs = refs[n:2 * n]
        d2d_send, d2d_recv = refs[2 * n:]
        x, y, c, chips = _place()
        started = []
        for k in range(n):
            rh = outs[k].shape[1] // 2
            for j, (px, py) in enumerate(chips):
                landed = outs[k].at[2 * px + py, pl.ds(c * rh, rh)]
                cp = pltpu.make_async_remote_copy(
                    src_ref=landed, dst_ref=landed,
                    send_sem=d2d_send.at[3 * k + j], recv_sem=d2d_recv.at[3 * k + j],
                    device_id=(x, y, 1 - c), device_id_type=MESH)
                cp.start()
                started.append(cp)
        for k in range(n):
            rh = outs[k].shape[1] // 2
            for j, (px, py) in enumerate(chips):
                landed = outs[k].at[2 * px + py, pl.ds((1 - c) * rh, rh)]
                pltpu.make_async_remote_copy(
                    src_ref=landed, dst_ref=landed,
                    send_sem=d2d_send.at[3 * k + j], recv_sem=d2d_recv.at[3 * k + j],
                    device_id=(x, y, 1 - c), device_id_type=MESH).wait_recv()
        for cp in started:
            cp.wait_send()

    return pl.pallas_call(
        body, name="gather_rest_forward",
        in_specs=[ANY] * n, out_specs=[ANY] * n,
        out_shape=[jax.ShapeDtypeStruct(b.shape, b.dtype) for b in bufs],
        input_output_aliases={k: k for k in range(n)},
        scratch_shapes=[pltpu.SemaphoreType.DMA((3 * n,))] * 2,
        compiler_params=pltpu.CompilerParams(has_side_effects=True),
    )(*bufs)


def _to_chips_start(name, arrays, n, src_fn, dst_fn, after):
    m = len(arrays)

    def body(*refs):
        send_sem, recv_sem = refs[m + 1], refs[m + 2]
        thru = refs[m + 3:2 * m + 3]
        token = refs[2 * m + 3]
        x, y, c, chips = _place()
        for k in range(n):
            for j, (px, py) in enumerate(chips):
                pltpu.make_async_remote_copy(
                    src_ref=src_fn(thru, k, px, py, x, y, c), dst_ref=dst_fn(thru, k, px, py, x, y, c),
                    send_sem=send_sem.at[3 * k + j], recv_sem=recv_sem.at[3 * k + j],
                    device_id=(px, py, c), device_id_type=MESH).start()
        token[...] = jnp.zeros_like(token)

    res = pl.pallas_call(
        body, name=name,
        out_shape=(pltpu.SemaphoreType.DMA((3 * n,)), pltpu.SemaphoreType.DMA((3 * n,)),
                   *[pltpu.HBM(a.shape, a.dtype) for a in arrays], jax.ShapeDtypeStruct((8, LANES), F32)),
        in_specs=[HBM] * m + [ANY],
        out_specs=(SEM, SEM, *[HBM] * m, pl.BlockSpec(memory_space=pltpu.VMEM)),
        input_output_aliases={i: i + 2 for i in range(m)},
        compiler_params=pltpu.CompilerParams(has_side_effects=EFFECT),
    )(*[pltpu.with_memory_space_constraint(a, pltpu.HBM) for a in arrays], after)
    return res[0], res[1], list(res[2:2 + m]), res[2 + m]


def _to_chips_wait(name, send_sem, recv_sem, arrays, n, src_fn, land_fn, after):
    m = len(arrays)
    after = list(after) if isinstance(after, (list, tuple)) else [after]

    def body(*refs):
        send, recv = refs[m], refs[m + 1]
        outs = refs[m + 2 + len(after):]
        x, y, c, chips = _place()
        for k in range(n):
            for j, (px, py) in enumerate(chips):
                cp = pltpu.make_async_remote_copy(
                    src_ref=src_fn(outs, k, px, py, x, y, c), dst_ref=land_fn(outs, k, px, py, x, y, c),
                    send_sem=send.at[3 * k + j], recv_sem=recv.at[3 * k + j],
                    device_id=(px, py, c), device_id_type=MESH)
                cp.wait_send()
                cp.wait_recv()

    res = pl.pallas_call(
        body, name=name,
        out_shape=[pltpu.HBM(a.shape, a.dtype) for a in arrays],
        in_specs=[HBM] * m + [SEM, SEM] + [ANY] * len(after), out_specs=[HBM] * m,
        input_output_aliases={i: i for i in range(m)},
        compiler_params=pltpu.CompilerParams(has_side_effects=EFFECT),
    )(*arrays, send_sem, recv_sem, *after)
    return list(res)


def _slot_half(refs, k, chip, c):
    rh = refs[k].shape[1] // 2
    return refs[k].at[chip, pl.ds(c * rh, rh)]


def _ag_mine(refs, k, px, py, x, y, c):
    return _slot_half(refs, k, 2 * x + y, c)


def _ag_theirs(refs, k, px, py, x, y, c):
    return _slot_half(refs, k, 2 * px + py, c)


def _rs_ends(n):
    def src(refs, k, px, py, x, y, c):
        return refs[k].at[2 * px + py]

    def dst(refs, k, px, py, x, y, c):
        return refs[n + k].at[2 * x + y]

    def land(refs, k, px, py, x, y, c):
        return refs[n + k].at[2 * px + py]

    return src, dst, land


def _half(ref, kind, p, c, rh):
    if kind == "cols":
        return ref.at[p, pl.ds(c * rh, rh)]
    return ref.at[pl.ds((2 * p + c) * rh, rh)]


def _swap_halves(name, grads, kinds, rhs):
    n = len(grads)

    def body(*refs):
        ins, outs = refs[:n], refs[n:2 * n]
        send_sem, recv_sem = refs[2 * n:]
        x, y, c, _ = _place()
        started = []
        for k in range(n):
            for p in range(N_CHIPS):
                cp = pltpu.make_async_remote_copy(
                    src_ref=_half(ins[k], kinds[k], p, 1 - c, rhs[k]), dst_ref=outs[k].at[p],
                    send_sem=send_sem.at[N_CHIPS * k + p], recv_sem=recv_sem.at[N_CHIPS * k + p],
                    device_id=(x, y, 1 - c), device_id_type=MESH)
                cp.start()
                started.append(cp)
        for cp in started:
            cp.wait()

    out_shape = []
    for g, kind, rh in zip(grads, kinds, rhs):
        out_shape.append(jax.ShapeDtypeStruct((N_CHIPS, rh, g.shape[-1]), g.dtype))
    return pl.pallas_call(
        body, name=name,
        in_specs=[ANY] * n, out_specs=[ANY] * n, out_shape=out_shape,
        scratch_shapes=[pltpu.SemaphoreType.DMA((N_CHIPS * n,))] * 2,
        compiler_params=pltpu.CompilerParams(has_side_effects=True),
    )(*grads)


def _join_halves(name, fulls):
    n = len(fulls)

    def body(*refs):
        outs = refs[n:2 * n]
        send_sem, recv_sem = refs[2 * n:]
        x, y, c, _ = _place()
        started = []
        for k in range(n):
            rh = outs[k].shape[0] // 2
            mine = outs[k].at[pl.ds(c * rh, rh)]
            cp = pltpu.make_async_remote_copy(
                src_ref=mine, dst_ref=mine, send_sem=send_sem.at[k], recv_sem=recv_sem.at[k],
                device_id=(x, y, 1 - c), device_id_type=MESH)
            cp.start()
            started.append(cp)
        for k in range(n):
            rh = outs[k].shape[0] // 2
            theirs = outs[k].at[pl.ds((1 - c) * rh, rh)]
            pltpu.make_async_remote_copy(
                src_ref=theirs, dst_ref=theirs, send_sem=send_sem.at[k], recv_sem=recv_sem.at[k],
                device_id=(x, y, 1 - c), device_id_type=MESH).wait_recv()
        for cp in started:
            cp.wait_send()

    return pl.pallas_call(
        body, name=name,
        in_specs=[ANY] * n, out_specs=[ANY] * n,
        out_shape=[jax.ShapeDtypeStruct(f.shape, f.dtype) for f in fulls],
        input_output_aliases={k: k for k in range(n)},
        scratch_shapes=[pltpu.SemaphoreType.DMA((n,))] * 2,
        compiler_params=pltpu.CompilerParams(has_side_effects=True),
    )(*fulls)


def _gather_all(block):
    def body(in_ref, out_ref, send_sem, recv_sem, local_sem):
        x, y, c, _ = _place()

        def slot(px, py, pc):
            return out_ref.at[4 * px + 2 * py + pc]

        loc = pltpu.make_async_copy(in_ref, slot(x, y, c), local_sem)
        loc.start()
        started = []
        for d in range(1, N_DEV):
            fx, fy, fc = d >> 2, (d >> 1) & 1, d & 1
            cp = pltpu.make_async_remote_copy(
                src_ref=in_ref, dst_ref=slot(x, y, c), send_sem=send_sem.at[d - 1], recv_sem=recv_sem.at[d - 1],
                device_id=(x ^ fx, y ^ fy, c ^ fc), device_id_type=MESH)
            cp.start()
            started.append(cp)
        for d in range(1, N_DEV):
            fx, fy, fc = d >> 2, (d >> 1) & 1, d & 1
            landed = slot(x ^ fx, y ^ fy, c ^ fc)
            pltpu.make_async_remote_copy(
                src_ref=in_ref, dst_ref=landed, send_sem=send_sem.at[d - 1], recv_sem=recv_sem.at[d - 1],
                device_id=(x ^ fx, y ^ fy, c ^ fc), device_id_type=MESH).wait_recv()
        for cp in started:
            cp.wait_send()
        loc.wait()

    return pl.pallas_call(
        body, name="gather_small_grads",
        in_specs=[ANY], out_specs=ANY,
        out_shape=jax.ShapeDtypeStruct((N_DEV,) + block.shape, block.dtype),
        scratch_shapes=[pltpu.SemaphoreType.DMA((N_DEV - 1,))] * 2 + [pltpu.SemaphoreType.DMA(())],
        compiler_params=pltpu.CompilerParams(has_side_effects=True),
    )(block)


def _pack(pieces):
    flat = jnp.concatenate([p.reshape(-1) for p in pieces])
    n = flat.shape[0]
    padded = -(-n // (8 * LANES)) * (8 * LANES)
    return jnp.pad(flat, (0, padded - n)).reshape(-1, LANES)


def _unpack(packed, shapes):
    flat = packed.reshape(-1)
    out, at = [], 0
    for s in shapes:
        n = math.prod(s)
        out.append(flat[at:at + n].reshape(s))
        at += n
    return out


def kernel(x, meta_tokens, pre_mix_g, w_in, gate_b, dw_w, dw_b, conv_ln_g, conv_ln_b, w_conv_out, w_attn_out, w_o, post_mix_g, pre_ffn_g, w_ffn_in, w_ffn_out, post_ffn_g, loss_target, m_meta_tokens, m_pre_mix_g, m_w_in, m_gate_b, m_dw_w, m_dw_b, m_conv_ln_g, m_conv_ln_b, m_w_conv_out, m_w_attn_out, m_w_o, m_post_mix_g, m_pre_ffn_g, m_w_ffn_in, m_w_ffn_out, m_post_ffn_g, v_meta_tokens, v_pre_mix_g, v_w_in, v_gate_b, v_dw_w, v_dw_b, v_conv_ln_g, v_conv_ln_b, v_w_conv_out, v_w_attn_out, v_w_o, v_post_mix_g, v_pre_ffn_g, v_w_ffn_in, v_w_ffn_out, v_post_ffn_g):
    S, D = x.shape[1], x.shape[2]
    L = S + N_META
    T = -(-L // ROW_BLOCK) * ROW_BLOCK
    Ta = -(-L // ATT_BLOCK) * ATT_BLOCK
    tm = _tile(T, MM_ROWS)
    tc = _tile(T, CONTRACT_ROWS)
    ts = _tile(T, STAGE_ROWS)
    tw = _tile(T, WIDE_STAGE_ROWS)
    H = D // HEAD_DIM
    F = w_ffn_out.shape[1] * N_CHIPS
    Dc = D // N_CHIPS
    P = N_CHIPS
    me = 2 * lax.axis_index("x") + lax.axis_index("y")
    c_arr = lax.axis_index("c").astype(jnp.int32).reshape(1)

    dw_w_pad = jnp.pad(dw_w[0], ((0, CONV_PAD - CONV_WIDTH), (0, 0)))
    me_arr = me.astype(jnp.int32).reshape(1)
    to_gather = [("w_in", w_in[0], BF16), ("w_conv_out", w_conv_out[0], BF16), ("w_attn_out", w_attn_out[0], BF16),
                 ("w_o", w_o[0], BF16), ("w_ffn_in", w_ffn_in[0], BF16), ("w_ffn_out", w_ffn_out[0], BF16),
                 ("meta", meta_tokens, F32), ("taps", dw_w_pad, F32)]
    slot = {n: _into_slot("slot_" + n, w, dt, me_arr, P) for n, w, dt in to_gather}
    win3, meta4, taps4 = _gather_chips([slot["w_in"], slot["meta"], slot["taps"]])
    meta_full = meta4.transpose(1, 0, 2).reshape(N_META, D)
    taps = taps4.transpose(1, 0, 2).reshape(CONV_PAD, D)[:CONV_WIDTH]
    later = ["w_conv_out", "w_attn_out", "w_o", "w_ffn_in", "w_ffn_out"]
    ag_send, ag_recv, in_flight, ag_token = _to_chips_start(
        "gather_rest_start", [slot[n] for n in later], len(later), _ag_mine, _ag_mine, meta4)

    h0 = jnp.concatenate([meta_full, x[0], jnp.zeros((T - L, D), F32)], axis=0)
    (u1,) = _rowwise_fwd("rms_pre_mix", f_rms, [(h0, D, 0)], [(pre_mix_g + ag_token[0:1, 0:1], D, 0)], [(D, BF16)], T, ts)
    p = _mm_nn_cols("mm_in", u1, win3, tm)
    q, k, v = _qkv_split(p, D, T, Ta)
    o2, rtot = _attn_fwd(q, k, v)
    landed = _to_chips_wait("gather_rest_wait", ag_send, ag_recv, in_flight, len(later), _ag_mine, _ag_theirs, o2)
    wco4, wao4, wo4, wfi3, wfo4 = _gather_forward(landed)
    wco, wao, wo = (w.reshape(D, D) for w in (wco4, wao4, wo4))
    wfo = wfo4.reshape(F, D)
    (uglu,) = _rowwise_fwd("glu", f_glu, [(p, D, 0), (p, D, 1)], [], [(D, F32)], T, ts)
    yc = _shift_conv("dwconv", uglu, taps, CONV_PAD, CONV_PAD - (CONV_WIDTH - 1), T)
    conv_pars = [(dw_b, D, 0), (conv_ln_g, D, 0), (conv_ln_b, D, 0)]
    (ys,) = _rowwise_fwd("conv_post", f_convpost, [(yc, D, 0)], conv_pars, [(D, BF16)], T, ts)
    y_conv = _mm_nn("mm_conv_out", ys, wco, tm)
    y_attn = _mm_nn("mm_attn_out", o2, wao, tm, rows=T)
    mix_rows = [(p, D, 5), (p, D, 6), (y_conv, D, 0), (y_attn, D, 0)]
    mix_pars = [(gate_b, D, 0), (gate_b, D, 1)]
    (mixin,) = _rowwise_fwd("gate_mix", f_mix, mix_rows, mix_pars, [(D, BF16)], T, ts)
    mix = _mm_nn("mm_o", mixin, wo, tm)
    (h1,) = _rowwise_fwd("res_post_mix", f_res_rms, [(h0, D, 0), (mix, D, 0)], [(post_mix_g, D, 0)], [(D, F32)], T, ts)
    (u2,) = _rowwise_fwd("rms_pre_ffn", f_rms, [(h1, D, 0)], [(pre_ffn_g, D, 0)], [(D, BF16)], T, ts)
    ab = _mm_nn_cols("mm_ffn_in", u2, wfi3, tm)
    (fin,) = _rowwise_fwd("swiglu", f_swiglu, [(ab, F, 0), (ab, F, 1)], [], [(F, BF16)], T, tw)
    f = _mm_nn("mm_ffn_out", fin, wfo, tm)
    (h2,) = _rowwise_fwd("res_post_ffn", f_res_rms, [(h1, D, 0), (f, D, 0)], [(post_ffn_g, D, 0)], [(D, F32)], T, ts)

    dy, part = _loss_head(h2[N_META:L], loss_target[0], _row_tile(S))
    loss = lax.psum(0.5 * jnp.sum(part) / D, ("x", "y", "c"))
    dh2 = jnp.pad(dy, ((N_META, T - L), (0, 0)))

    (df,), (g_post_ffn,) = _rowwise_bwd("res_post_ffn_bwd", f_res_rms, [(h1, D, 0), (f, D, 0)], [(post_ffn_g, D, 0)],
                                        [(dh2, D, 0)], [None, BF16], T, ts)
    dfin = _mm_nt("mm_ffn_out_dx", df, wfo, tm)
    g_wfo = _mm_tn("mm_ffn_out_dw", fin, df, tc, F // MXU_WIDTH)
    (da, db), _ = _rowwise_bwd("swiglu_bwd", f_swiglu, [(ab, F, 0), (ab, F, 1)], [], [(dfin, F, 0)], [BF16, BF16], T, tw)
    dab = jnp.concatenate([da, db], axis=1)
    du2 = _mm_nt_cols("mm_ffn_in_dx", dab, wfi3, tm)
    g_wfi = _mm_tn_cols("mm_ffn_in_dw", u2, dab, tc, P)
    (dh1,), (g_pre_ffn,) = _rowwise_bwd("rms_pre_ffn_bwd", f_rms_id, [(h1, D, 0)], [(pre_ffn_g, D, 0)],
                                        [(du2, D, 0), (dh2, D, 0)], [F32], T, ts)
    (dmix,), (g_post_mix,) = _rowwise_bwd("res_post_mix_bwd", f_res_rms, [(h0, D, 0), (mix, D, 0)], [(post_mix_g, D, 0)],
                                          [(dh1, D, 0)], [None, BF16], T, ts)
    dmixin = _mm_nt("mm_o_dx", dmix, wo, tm)
    g_wo = _mm_tn("mm_o_dw", mixin, dmix, tc, D // MXU_WIDTH)
    (dpc, dpa, dyconv, dyattn), (g_gate_c, g_gate_a) = _rowwise_bwd(
        "gate_mix_bwd", f_mix, mix_rows, mix_pars, [(dmixin, D, 0)], [BF16, BF16, BF16, BF16], T, ts)
    g_wco = _mm_tn("mm_conv_out_dw", ys, dyconv, tc, D // MXU_WIDTH)
    dys = _mm_nt("mm_conv_out_dx", dyconv, wco, tm)
    g_wao = _mm_tn("mm_attn_out_dw", o2, dyattn, tc, D // MXU_WIDTH)
    do2 = _mm_nt("mm_attn_out_dx", dyattn, wao, tm, BF16, out_rows=Ta)

    early = [g_wco, g_wao, g_wo, g_wfi, g_wfo]
    early_kinds = ["rows", "rows", "rows", "cols", "rows"]
    early_rhs = [(g.shape[1] if kind == "cols" else g.shape[0] // P) // 2 for g, kind in zip(early, early_kinds)]
    early_sib = _swap_halves("grad_swap_halves_early", early, early_kinds, early_rhs)
    early_pairs = [_pair_add_bf16("grad_pair_add_%d" % (n + 1), g, b1, kind, c_arr, me_arr)
                   for n, (g, b1, kind) in enumerate(zip(early, early_sib, early_kinds))]
    rs_src, rs_dst, rs_land = _rs_ends(len(early))
    rs_send, rs_recv, rs_flight, rs_token = _to_chips_start(
        "grad_scatter_start", [pr[0] for pr in early_pairs] + [pr[1] for pr in early_pairs], len(early),
        rs_src, rs_dst, early_pairs[-1][1])
    rtot_bwd = rtot + rs_token[0, 0]
    (dyc,), (g_dw_b, g_ln_g, g_ln_b) = _rowwise_bwd("conv_post_bwd", f_convpost, [(yc, D, 0)], conv_pars,
                                                    [(dys, D, 0)], [F32], T, ts)
    duglu = _shift_conv("dwconv_dx", dyc, taps[::-1], 0, 0, T)
    g_taps = _conv_dw("dwconv_dw", uglu, dyc, T)
    (dp0, dp1), _ = _rowwise_bwd("glu_bwd", f_glu, [(p, D, 0), (p, D, 1)], [], [(duglu, D, 0)], [BF16, BF16], T, ts)
    dq, dk, dv = _attn_bwd(q, k, v, do2, rtot_bwd)
    rs_done = _to_chips_wait("grad_scatter_wait", rs_send, rs_recv, rs_flight, len(early), rs_src, rs_land, dq)
    early_slots = rs_done[len(early):]
    dp = jnp.concatenate([dp0, dp1, dq[:T], dk[:T], dv[:T], dpc, dpa], axis=1)
    du1 = _mm_nt_cols("mm_in_dx", dp, win3, tm)
    g_win = _mm_tn_cols("mm_in_dw", u1, dp, tc, P)
    (win_sib,) = _swap_halves("grad_swap_halves_in", [g_win], ["cols"], [g_win.shape[1] // 2])
    win_pair = _pair_add_bf16("grad_pair_add_0", g_win, win_sib, "cols", c_arr, me_arr)
    in_src, in_dst, in_land = _rs_ends(1)
    in_send, in_recv, in_flight, in_token = _to_chips_start(
        "grad_scatter_in_start", list(win_pair), 1, in_src, in_dst, win_pair[1])
    (dh0,), (g_pre_mix,) = _rowwise_bwd("rms_pre_mix_bwd", f_rms_id, [(h0, D, 0)],
                                        [(pre_mix_g + in_token[0:1, 0:1], D, 0)],
                                        [(du1, D, 0), (dh1, D, 0)], [F32], T, ts)
    grad_x = dh0[N_META:L][None]
    g_early = _join_halves("grad_join_halves_early", [_sum_slots("grad_chip_sum_%d" % (n + 1), s, c_arr)
                                                      for n, s in enumerate(early_slots)])

    small_shapes = [(1, D), (1, D), (1, D), (CONV_WIDTH, D), (1, D), (1, D), (1, D), (1, D), (1, D), (1, D), (N_META, D)]
    small = _pack([g_pre_mix, g_gate_c, g_gate_a, g_taps, g_dw_b, g_ln_g, g_ln_b, g_post_mix, g_pre_ffn, g_post_ffn,
                   dh0[:N_META]])
    summed = _sum_slots("small_grad_sum", _gather_all(small))
    (s_pre_mix, s_gate_c, s_gate_a, s_taps, s_dw_b, s_ln_g, s_ln_b, s_post_mix, s_pre_ffn, s_post_ffn,
     s_meta) = _unpack(summed, small_shapes)
    s_gate_b = jnp.concatenate([s_gate_c, s_gate_a], axis=1)
    s_taps = lax.dynamic_slice_in_dim(s_taps, me * Dc, Dc, axis=1)[None]
    s_meta = lax.dynamic_slice_in_dim(s_meta, me * Dc, Dc, axis=1)

    grads = {
        "meta_tokens": s_meta, "pre_mix_g": s_pre_mix, "gate_b": s_gate_b, "dw_w": s_taps,
        "dw_b": s_dw_b, "conv_ln_g": s_ln_g, "conv_ln_b": s_ln_b, "w_conv_out": g_early[0][None],
        "w_attn_out": g_early[1][None], "w_o": g_early[2][None], "post_mix_g": s_post_mix, "pre_ffn_g": s_pre_ffn,
        "w_ffn_in": g_early[3][None], "w_ffn_out": g_early[4][None], "post_ffn_g": s_post_ffn,
    }
    weights = {
        "meta_tokens": (meta_tokens, m_meta_tokens, v_meta_tokens), "pre_mix_g": (pre_mix_g, m_pre_mix_g, v_pre_mix_g),
        "w_in": (w_in, m_w_in, v_w_in), "gate_b": (gate_b, m_gate_b, v_gate_b), "dw_w": (dw_w, m_dw_w, v_dw_w),
        "dw_b": (dw_b, m_dw_b, v_dw_b), "conv_ln_g": (conv_ln_g, m_conv_ln_g, v_conv_ln_g),
        "conv_ln_b": (conv_ln_b, m_conv_ln_b, v_conv_ln_b), "w_conv_out": (w_conv_out, m_w_conv_out, v_w_conv_out),
        "w_attn_out": (w_attn_out, m_w_attn_out, v_w_attn_out), "w_o": (w_o, m_w_o, v_w_o),
        "post_mix_g": (post_mix_g, m_post_mix_g, v_post_mix_g), "pre_ffn_g": (pre_ffn_g, m_pre_ffn_g, v_pre_ffn_g),
        "w_ffn_in": (w_ffn_in, m_w_ffn_in, v_w_ffn_in), "w_ffn_out": (w_ffn_out, m_w_ffn_out, v_w_ffn_out),
        "post_ffn_g": (post_ffn_g, m_post_ffn_g, v_post_ffn_g),
    }
    names = list(weights)
    big_names = ["w_in", "w_conv_out", "w_attn_out", "w_o", "w_ffn_in", "w_ffn_out"]
    small_names = [n for n in names if n not in big_names]

    delta, new_m, new_v = {}, {}, {}

    def big_update(n):
        w, m, v2 = weights[n]
        d, nm, nv = _adamw("adamw_" + n, w[0], grads[n][0], m[0], v2[0])
        delta[n], new_m[n], new_v[n] = d[None], nm[None], nv[None]

    for n in big_names[1:]:
        big_update(n)
    shapes = [weights[n][0].shape for n in small_names]
    packed = [_pack([weights[n][k] for n in small_names]) for k in range(3)]
    d, nm, nv = _adamw("adamw_small", packed[0], _pack([grads[n] for n in small_names]), packed[1], packed[2])
    for n, dd, mm, vv in zip(small_names, _unpack(d, shapes), _unpack(nm, shapes), _unpack(nv, shapes)):
        delta[n], new_m[n], new_v[n] = dd, mm, vv

    in_done = _to_chips_wait("grad_scatter_in_wait", in_send, in_recv, in_flight, 1, in_src, in_land,
                             [d] + [delta[n] for n in big_names[1:]])
    (g_in,) = _join_halves("grad_join_halves_in", [_sum_slots("grad_chip_sum_0", in_done[1], c_arr)])
    grads["w_in"] = g_in[None]
    big_update("w_in")

    return (loss, grad_x, *[grads[n].reshape(weights[n][0].shape) for n in names], *[delta[n] for n in names],
            *[new_m[n] for n in names], *[new_v[n] for n in names])
```

```python
import math

import jax
import jax.numpy as jnp
from jax import lax
from jax.experimental import pallas as pl
from jax.experimental.pallas import tpu as pltpu

F32 = jnp.float32
BF16 = jnp.bfloat16

N_META = 16
CONV_WIDTH = 31
CONV_PAD = 32
HEAD_DIM = 64
RMS_EPS = 1e-6
LN_EPS = 1e-5
ROW_BLOCK = 128
MXU_WIDTH = 256
ATT_BLOCK = MXU_WIDTH
ATT_HEADS = 4
LANES = 128
N_CHIPS = 4
N_DEV = 8
MM_ROWS = 544
CONTRACT_ROWS = 2176
STAGE_ROWS = 272
WIDE_STAGE_ROWS = 128
VMEM_LIMIT = 56 * 1024 * 1024

ADAM_LR = 0.001
ADAM_B1 = 0.9
ADAM_B2 = 0.999
ADAM_EPS = 1e-08
ADAM_WD = 0.01
ADAM_STEP = 10

MESH = pl.DeviceIdType.MESH
ANY = pl.BlockSpec(memory_space=pl.ANY)
HBM = pl.BlockSpec(memory_space=pltpu.HBM)
SEM = pl.BlockSpec(memory_space=pltpu.SEMAPHORE)
EFFECT = pltpu.SideEffectType.DATAFLOW_SIDE_EFFECTING


def _params(*sem):
    return pltpu.CompilerParams(dimension_semantics=sem if sem else None, vmem_limit_bytes=VMEM_LIMIT)


def _rms(x, g):
    return x * lax.rsqrt(jnp.mean(x * x, axis=-1, keepdims=True) + RMS_EPS) * g


def f_rms(h, g):
    return (_rms(h, g),)


def f_rms_id(h, g):
    return (_rms(h, g), h)


def f_res_rms(h, m, g):
    return (h + _rms(m, g),)


def f_glu(a, gate):
    return (a * lax.logistic(gate),)


def f_convpost(yc, b, ln_g, ln_b):
    y = yc + b
    mu = jnp.mean(y, axis=-1, keepdims=True)
    xc = y - mu
    var = jnp.mean(xc * xc, axis=-1, keepdims=True)
    yl = xc * lax.rsqrt(var + LN_EPS) * ln_g + ln_b
    return (yl * lax.logistic(yl),)


def f_mix(pc, pa, yc, ya, bc, ba):
    return (lax.logistic(pc + bc) * yc + lax.logistic(pa + ba) * ya,)


def f_swiglu(a, b):
    return (a * lax.logistic(a) * b,)


def _tile(T, target):
    return max(t for t in range(16, target + 1, 16) if T % t == 0)


def _row_map(j):
    return lambda i: (i, j)


def _par_map(j):
    return lambda i: (0, j)


def _rowwise_fwd(name, f, rows, pars, outs, T, tm):
    n_in = len(rows) + len(pars)

    def body(*refs):
        vals = [r[...].astype(F32) for r in refs[:n_in]]
        res = f(*vals)
        for o_ref, o in zip(refs[n_in:], res):
            o_ref[...] = o.astype(o_ref.dtype)

    in_specs = [pl.BlockSpec((tm, w), _row_map(j)) for _, w, j in rows]
    in_specs += [pl.BlockSpec((1, w), _par_map(j)) for _, w, j in pars]
    return pl.pallas_call(
        body, name=name, grid=(T // tm,),
        in_specs=in_specs,
        out_specs=[pl.BlockSpec((tm, w), _row_map(0)) for w, _ in outs],
        out_shape=[jax.ShapeDtypeStruct((T, w), dt) for w, dt in outs],
        compiler_params=_params("parallel"),
    )(*[a for a, _, _ in rows], *[a for a, _, _ in pars])


def _rowwise_bwd(name, f, rows, pars, cots, drow_dtypes, T, tm):
    n_r, n_p, n_c = len(rows), len(pars), len(cots)
    n_in = n_r + n_p + n_c
    keep = [k for k, dt in enumerate(drow_dtypes) if dt is not None]

    def body(*refs):
        rv = [r[...].astype(F32) for r in refs[:n_r]]
        pv = [r[...].astype(F32) for r in refs[n_r:n_r + n_p]]
        cv = [r[...].astype(F32) for r in refs[n_r + n_p:n_in]]
        _, vjp = jax.vjp(f, *rv, *pv)
        g = vjp(tuple(cv))
        drow_refs = refs[n_in:n_in + len(keep)]
        dpar_refs = refs[n_in + len(keep):]
        for r, k in zip(drow_refs, keep):
            r[...] = g[k].astype(r.dtype)

        @pl.when(pl.program_id(0) == 0)
        def _():
            for r in dpar_refs:
                r[...] = jnp.zeros_like(r)

        for r, gp in zip(dpar_refs, g[n_r:]):
            r[...] += gp

    in_specs = [pl.BlockSpec((tm, w), _row_map(j)) for _, w, j in rows]
    in_specs += [pl.BlockSpec((1, w), _par_map(j)) for _, w, j in pars]
    in_specs += [pl.BlockSpec((tm, w), _row_map(j)) for _, w, j in cots]
    out_specs = [pl.BlockSpec((tm, rows[k][1]), _row_map(0)) for k in keep]
    out_specs += [pl.BlockSpec((1, w), _par_map(0)) for _, w, _ in pars]
    out_shape = [jax.ShapeDtypeStruct((T, rows[k][1]), drow_dtypes[k]) for k in keep]
    out_shape += [jax.ShapeDtypeStruct((1, w), F32) for _, w, _ in pars]
    res = pl.pallas_call(
        body, name=name, grid=(T // tm,),
        in_specs=in_specs, out_specs=out_specs, out_shape=out_shape,
        compiler_params=_params("arbitrary"),
    )(*[a for a, _, _ in rows], *[a for a, _, _ in pars], *[a for a, _, _ in cots])
    return res[:len(keep)], res[len(keep):]


NN = (((1,), (0,)), ((), ()))
NT = (((1,), (1,)), ((), ()))
TN = (((0,), (0,)), ((), ()))


def _mm(name, a, b, dims, out_shape, grid, a_spec, b_spec, o_spec, red_axis=None, init=None):
    n_red = None if red_axis is None else grid[red_axis]

    def body(a_ref, b_ref, *rest):
        o_ref = rest[-1]
        prod = lax.dot_general(a_ref[...], b_ref[...], dims, preferred_element_type=F32)
        if n_red is None:
            o_ref[...] = prod.astype(o_ref.dtype)
        else:
            @pl.when(pl.program_id(red_axis) == 0)
            def _():
                o_ref[...] = prod

            @pl.when(pl.program_id(red_axis) > 0)
            def _():
                o_ref[...] += prod

    sem = ["parallel"] * len(grid)
    if red_axis is not None:
        sem[red_axis] = "arbitrary"
    if init is None:
        return pl.pallas_call(
            body, name=name, grid=grid, in_specs=[a_spec, b_spec], out_specs=o_spec, out_shape=out_shape,
            compiler_params=_params(*sem),
        )(a, b)
    return pl.pallas_call(
        body, name=name, grid=grid, in_specs=[a_spec, b_spec, ANY], out_specs=o_spec, out_shape=out_shape,
        input_output_aliases={2: 0}, compiler_params=_params(*sem),
    )(a, b, init)


def _mm_nn(name, a, w, tm, out_dtype=F32, rows=None):
    T, K = a.shape
    T = rows or T
    N = w.shape[1]
    return _mm(name, a, w, NN, jax.ShapeDtypeStruct((T, N), out_dtype), (T // tm,),
               pl.BlockSpec((tm, K), lambda i: (i, 0)), pl.BlockSpec((K, N), lambda i: (0, 0)),
               pl.BlockSpec((tm, N), lambda i: (i, 0)))


def _mm_nt(name, a, w, tm, out_dtype=F32, out_rows=None):
    T, N = a.shape
    K = w.shape[0]
    init = None if out_rows is None else jnp.zeros((out_rows, K), out_dtype)
    return _mm(name, a, w, NT, jax.ShapeDtypeStruct((out_rows or T, K), out_dtype), (T // tm,),
               pl.BlockSpec((tm, N), lambda i: (i, 0)), pl.BlockSpec((K, N), lambda i: (0, 0)),
               pl.BlockSpec((tm, K), lambda i: (i, 0)), init=init)


def _mm_tn(name, a, b, tm, n_row_blocks):
    K = a.shape[1]
    T, N = b.shape
    kb = K // n_row_blocks
    return _mm(name, a, b, TN, jax.ShapeDtypeStruct((K, N), F32), (n_row_blocks, T // tm),
               pl.BlockSpec((tm, kb), lambda r, t: (t, r)), pl.BlockSpec((tm, N), lambda r, t: (t, 0)),
               pl.BlockSpec((kb, N), lambda r, t: (r, 0)), red_axis=1)


def _mm_nn_cols(name, a, w3, tm):
    T, K = a.shape
    P, _, Ns = w3.shape
    return _mm(name, a, w3, NN, jax.ShapeDtypeStruct((T, P * Ns), F32), (P, T // tm),
               pl.BlockSpec((tm, K), lambda p, i: (i, 0)), pl.BlockSpec((None, K, Ns), lambda p, i: (p, 0, 0)),
               pl.BlockSpec((tm, Ns), lambda p, i: (i, p)))


def _mm_nt_cols(name, a, w3, tm):
    T = a.shape[0]
    P, K, Ns = w3.shape
    return _mm(name, a, w3, NT, jax.ShapeDtypeStruct((T, K), F32), (T // tm, P),
               pl.BlockSpec((tm, Ns), lambda i, p: (i, p)), pl.BlockSpec((None, K, Ns), lambda i, p: (p, 0, 0)),
               pl.BlockSpec((tm, K), lambda i, p: (i, 0)), red_axis=1)


def _mm_tn_cols(name, a, b, tm, P):
    T, K = a.shape
    Ns = b.shape[1] // P
    return _mm(name, a, b, TN, jax.ShapeDtypeStruct((P, K, Ns), F32), (P, T // tm),
               pl.BlockSpec((tm, K), lambda p, t: (t, 0)), pl.BlockSpec((tm, Ns), lambda p, t: (t, p)),
               pl.BlockSpec((None, K, Ns), lambda p, t: (p, 0, 0)), red_axis=1)


def _tap_windows(win, off, tb):
    out = []
    for b in range(8):
        taps = [j for j in range(CONV_WIDTH) if (off + j) % 8 == b]
        if taps:
            shifted = win[b:b + tb + CONV_PAD, :]
            out += [(shifted, off + j - b, j) for j in taps]
    return out


def _shift_conv(name, x, w, place, off, T):
    C = x.shape[1]
    tb = ROW_BLOCK
    zero_at = 0 if place else T

    def body(x_ref, w_ref, o_ref, xp_ref):
        xp_ref[pl.ds(zero_at, CONV_PAD), :] = jnp.zeros((CONV_PAD, LANES), F32)
        xp_ref[pl.ds(T + CONV_PAD, 8), :] = jnp.zeros((8, LANES), F32)
        xp_ref[pl.ds(place, T), :] = x_ref[...]

        def step(t, carry):
            base = pl.multiple_of(t * tb, tb)
            win = xp_ref[pl.ds(base, tb + CONV_PAD + 8), :]
            acc = jnp.zeros((tb, LANES), F32)
            for shifted, at, j in _tap_windows(win, off, tb):
                acc = acc + shifted[at:at + tb, :] * w_ref[pl.ds(j, 1), :]
            o_ref[pl.ds(base, tb), :] = acc
            return carry

        lax.fori_loop(0, T // tb, step, 0)

    return pl.pallas_call(
        body, name=name, grid=(C // LANES,),
        in_specs=[pl.BlockSpec((T, LANES), lambda c: (0, c)), pl.BlockSpec((CONV_WIDTH, LANES), lambda c: (0, c))],
        out_specs=pl.BlockSpec((T, LANES), lambda c: (0, c)),
        out_shape=jax.ShapeDtypeStruct((T, C), F32),
        scratch_shapes=[pltpu.VMEM((T + CONV_PAD + 8, LANES), F32)],
        compiler_params=_params("parallel"),
    )(x, w)


def _conv_dw(name, x, dy, T):
    C = x.shape[1]
    tb = ROW_BLOCK
    off = CONV_PAD - (CONV_WIDTH - 1)

    def body(x_ref, dy_ref, o_ref, xp_ref, acc_ref):
        xp_ref[pl.ds(0, CONV_PAD), :] = jnp.zeros((CONV_PAD, LANES), F32)
        xp_ref[pl.ds(T + CONV_PAD, 8), :] = jnp.zeros((8, LANES), F32)
        xp_ref[pl.ds(CONV_PAD, T), :] = x_ref[...]
        acc_ref[...] = jnp.zeros_like(acc_ref)

        def step(t, carry):
            base = pl.multiple_of(t * tb, tb)
            win = xp_ref[pl.ds(base, tb + CONV_PAD + 8), :]
            d = dy_ref[pl.ds(base, tb), :]
            for shifted, at, j in _tap_windows(win, off, tb):
                prod = shifted[at:at + tb, :] * d
                acc_ref[j] += jnp.sum(prod.reshape(tb // 8, 8, LANES), axis=0)
            return carry

        lax.fori_loop(0, T // tb, step, 0)
        for j in range(CONV_WIDTH):
            o_ref[pl.ds(j, 1), :] = jnp.sum(acc_ref[j], axis=0, keepdims=True)

    return pl.pallas_call(
        body, name=name, grid=(C // LANES,),
        in_specs=[pl.BlockSpec((T, LANES), lambda c: (0, c)), pl.BlockSpec((T, LANES), lambda c: (0, c))],
        out_specs=pl.BlockSpec((CONV_WIDTH, LANES), lambda c: (0, c)),
        out_shape=jax.ShapeDtypeStruct((CONV_WIDTH, C), F32),
        scratch_shapes=[pltpu.VMEM((T + CONV_PAD + 8, LANES), F32), pltpu.VMEM((CONV_WIDTH, 8, LANES), F32)],
        compiler_params=_params("parallel"),
    )(x, dy)


def _dot(a, b, dims=NN):
    return lax.dot_general(a, b, dims, preferred_element_type=F32)


def _tri_cumsum(x, tri):
    return _dot(x.astype(BF16), tri)


def _qkv_split(p, D, T, Ta):
    tb = ROW_BLOCK
    nt = T // tb
    scale = 1.0 / math.sqrt(HEAD_DIM)

    def body(q_ref, k_ref, v_ref, qo_ref, ko_ref, vo_ref):
        live = pl.program_id(0) < nt
        qo_ref[...] = jnp.where(live, q_ref[...] * scale, 0.0).astype(BF16)
        ko_ref[...] = jnp.where(live, k_ref[...], 0.0).astype(BF16)
        vo_ref[...] = jnp.where(live, v_ref[...], 0.0).astype(BF16)

    def col(n):
        return lambda i: (jnp.minimum(i, nt - 1), n)

    return pl.pallas_call(
        body, name="qkv_split", grid=(Ta // tb,),
        in_specs=[pl.BlockSpec((tb, D), col(2 + n)) for n in range(3)],
        out_specs=[pl.BlockSpec((tb, D), lambda i: (i, 0))] * 3,
        out_shape=[jax.ShapeDtypeStruct((Ta, D), BF16)] * 3,
        compiler_params=_params("parallel"),
    )(p, p, p)


def _pair_lanes(g):
    return slice((g // 2) * LANES, (g // 2 + 1) * LANES)


def _own_lanes(x_ref, B, G):
    first = lax.broadcasted_iota(jnp.int32, (B, LANES), 1) < HEAD_DIM
    out = []
    for g in range(G):
        x2 = x_ref[:, _pair_lanes(g)]
        out.append(jnp.where(first if g % 2 == 0 else jnp.logical_not(first), x2, jnp.zeros_like(x2)))
    return first, out


def _attn_fwd(q, k, v):
    T, D = q.shape
    H = D // HEAD_DIM
    B = ATT_BLOCK
    G = ATT_HEADS
    W = G // 2 * LANES

    def body(q_ref, k_ref, v_ref, o_ref, rt_ref):
        i = pl.program_id(1)
        row = lax.broadcasted_iota(jnp.int32, (B, B), 0)
        col = lax.broadcasted_iota(jnp.int32, (B, B), 1)
        below = col < row
        tri = (row >= col).astype(BF16)
        first, qs = _own_lanes(q_ref, B, G)

        def tile(j, carry, diagonal):
            sl = pl.ds(pl.multiple_of(j * B, B), B)
            zs, sps = [], []
            for g in range(G):
                z = _dot(qs[g], k_ref[sl, _pair_lanes(g)], NT)
                sp = jnp.maximum(z, 0.0) + jnp.log(1.0 + jnp.exp(-jnp.abs(z)))
                if diagonal:
                    sp = jnp.where(below, sp, 0.0)
                zs.append(z)
                sps.append(sp)
            rws = _tri_cumsum(jnp.concatenate(sps, axis=0), tri)
            out = []
            for g in range(G):
                c, acc = carry[g]
                rw = rws[g * B:(g + 1) * B]
                a = jnp.exp(zs[g] - (rw + c))
                if diagonal:
                    a = jnp.where(below, a, 0.0)
                acc = acc + _dot(a.astype(BF16), v_ref[sl, _pair_lanes(g)])
                out.append((c + rw[:, 0:1], acc))
            return tuple(out)

        carry = tile(i, tuple((jnp.zeros((B, 1), F32), jnp.zeros((B, LANES), F32)) for _ in range(G)), True)
        carry = lax.fori_loop(0, i, lambda jj, cr: tile(i - 1 - jj, cr, False), carry)
        for g in range(0, G, 2):
            o_ref[:, _pair_lanes(g)] = jnp.where(first, carry[g][1], carry[g + 1][1]).astype(o_ref.dtype)
        for g in range(G):
            rt_ref[g] = carry[g][0]

    return pl.pallas_call(
        body, name="attn_fwd", grid=(H // G, T // B),
        in_specs=[pl.BlockSpec((B, W), lambda h, i: (i, h)),
                  pl.BlockSpec((T, W), lambda h, i: (0, h)),
                  pl.BlockSpec((T, W), lambda h, i: (0, h))],
        out_specs=[pl.BlockSpec((B, W), lambda h, i: (i, h)),
                   pl.BlockSpec((G, B, 1), lambda h, i: (h, i, 0))],
        out_shape=[jax.ShapeDtypeStruct((T, D), BF16), jax.ShapeDtypeStruct((H, T, 1), F32)],
        compiler_params=_params("parallel", "arbitrary"),
    )(q, k, v)


def _attn_bwd(q, k, v, do, rt, after):
    T, D = q.shape
    H = D // HEAD_DIM
    B = ATT_BLOCK
    nq = T // B
    scale = 1.0 / math.sqrt(HEAD_DIM)
    G = ATT_HEADS
    W = G // 2 * LANES

    def body(q_ref, k_ref, v_ref, do_ref, rt_ref, after_ref, dq_ref, dk_ref, dv_ref, dk_acc, dv_acc):
        i = pl.program_id(1)

        @pl.when(i == 0)
        def _():
            dk_acc[...] = jnp.zeros_like(dk_acc)
            dv_acc[...] = jnp.zeros_like(dv_acc)

        row = lax.broadcasted_iota(jnp.int32, (B, B), 0)
        col = lax.broadcasted_iota(jnp.int32, (B, B), 1)
        below = col < row
        tri = (row <= col).astype(BF16)
        first, qs = _own_lanes(q_ref, B, G)
        _, dos = _own_lanes(do_ref, B, G)

        q_pairs = [jnp.concatenate([qs[h], qs[h + 1]], axis=0) for h in range(0, G, 2)]
        do_pairs = [jnp.concatenate([dos[h], dos[h + 1]], axis=0) for h in range(0, G, 2)]

        def tile(j, carry, diagonal):
            sl = pl.ds(pl.multiple_of(j * B, B), B)
            zs, sps, sgs = [], [], []
            for h in range(G):
                z = _dot(qs[h], k_ref[sl, _pair_lanes(h)], NT)
                e = jnp.exp(-jnp.abs(z))
                inv = 1.0 / (1.0 + e)
                sp = jnp.maximum(z, 0.0) - jnp.log(inv)
                if diagonal:
                    sp = jnp.where(below, sp, 0.0)
                zs.append(z)
                sps.append(sp)
                sgs.append(jnp.where(z >= 0.0, inv, e * inv))
            pws = _tri_cumsum(jnp.concatenate(sps, axis=0), tri)
            aas, gs = [], []
            for h in range(G):
                pw = pws[h * B:(h + 1) * B]
                a = jnp.exp(zs[h] - (rt_ref[h] - carry[h][0] - pw + sps[h]))
                if diagonal:
                    a = jnp.where(below, a, 0.0)
                aas.append(a.astype(BF16))
                gs.append(a * _dot(dos[h], v_ref[sl, _pair_lanes(h)], NT))
            gws = _tri_cumsum(jnp.concatenate(gs, axis=0), tri)
            out, dzs = [], []
            for h in range(G):
                pc, gc, dq = carry[h]
                pw, gw = pws[h * B:(h + 1) * B], gws[h * B:(h + 1) * B]
                dz = gs[h] - sgs[h] * (gc + gw)
                if diagonal:
                    dz = jnp.where(below, dz, 0.0)
                dzs.append(dz.astype(BF16))
                dq = dq + _dot(dzs[h], k_ref[sl, _pair_lanes(h)])
                out.append((pc + pw[:, B - 1:B], gc + gw[:, B - 1:B], dq))
            for h in range(0, G, 2):
                dk_acc[sl, _pair_lanes(h)] += _dot(jnp.concatenate(dzs[h:h + 2], axis=0), q_pairs[h // 2], TN)
                dv_acc[sl, _pair_lanes(h)] += _dot(jnp.concatenate(aas[h:h + 2], axis=0), do_pairs[h // 2], TN)
            return tuple(out)

        zero = jnp.zeros((B, 1), F32)
        carry = lax.fori_loop(0, i, lambda j, cr: tile(j, cr, False),
                              tuple((zero, zero, jnp.zeros((B, LANES), F32)) for _ in range(G)))
        carry = tile(i, carry, True)
        for h in range(0, G, 2):
            dq_ref[:, _pair_lanes(h)] = (jnp.where(first, carry[h][2], carry[h + 1][2]) * scale).astype(dq_ref.dtype)

        @pl.when(i == nq - 1)
        def _():
            dk_ref[...] = dk_acc[...].astype(dk_ref.dtype)
            dv_ref[...] = dv_acc[...].astype(dv_ref.dtype)

    blk = pl.BlockSpec((B, W), lambda h, i: (i, h))
    full = pl.BlockSpec((T, W), lambda h, i: (0, h))
    return pl.pallas_call(
        body, name="attn_bwd", grid=(H // G, nq),
        in_specs=[blk, full, full, blk, pl.BlockSpec((G, B, 1), lambda h, i: (h, i, 0)), ANY],
        out_specs=[blk, full, full],
        out_shape=[jax.ShapeDtypeStruct((T, D), BF16)] * 3,
        scratch_shapes=[pltpu.VMEM((T, W), F32)] * 2,
        compiler_params=_params("parallel", "arbitrary"),
    )(q, k, v, do, rt, after)


def _loss_head(y, target, tm):
    S, D = y.shape

    def body(y_ref, t_ref, dy_ref, part_ref):
        err = y_ref[...] - t_ref[...]
        dy_ref[...] = err * (1.0 / D)

        @pl.when(pl.program_id(0) == 0)
        def _():
            part_ref[...] = jnp.zeros_like(part_ref)

        part_ref[...] += jnp.sum(err * err, axis=0, keepdims=True)

    spec = pl.BlockSpec((tm, D), lambda i: (i, 0))
    return pl.pallas_call(
        body, name="loss_head", grid=(S // tm,), in_specs=[spec, spec],
        out_specs=[spec, pl.BlockSpec((1, D), lambda i: (0, 0))],
        out_shape=[jax.ShapeDtypeStruct((S, D), F32), jax.ShapeDtypeStruct((1, D), F32)],
        compiler_params=_params("arbitrary"),
    )(y, target)


def _row_tile(R):
    for t in (256, 128, 64, 32, 16, 8):
        if R % t == 0:
            return t
    return R


def _pair_add_bf16(name, g, b1, kind, c_arr, me_arr=None):
    P, Rh, C = b1.shape
    tr = _row_tile(Rh)
    nb = Rh // tr
    own = me_arr is not None

    def body(*refs):
        g_ref, b_ref, o_ref = refs[1 + own:4 + own]
        val = (g_ref[...] + b_ref[...]).astype(o_ref.dtype)
        o_ref[...] = val
        if own:
            @pl.when(pl.program_id(1) == refs[1][0])
            def _():
                refs[-1][...] = val

    if kind == "cols":
        g_spec = pl.BlockSpec((None, tr, C), lambda i, p, c_ref, *_: (p, c_ref[0] * nb + i, 0))
    else:
        g_spec = pl.BlockSpec((tr, C), lambda i, p, c_ref, *_: ((2 * p + c_ref[0]) * nb + i, 0))
    blk = pl.BlockSpec((None, tr, C), lambda i, p, *_: (p, i, 0))
    shape = jax.ShapeDtypeStruct((P, Rh, C), BF16)
    if not own:
        return pl.pallas_call(
            body, name=name,
            grid_spec=pltpu.PrefetchScalarGridSpec(num_scalar_prefetch=1, grid=(nb, P), in_specs=[g_spec, blk], out_specs=blk),
            out_shape=shape, compiler_params=_params("parallel", "parallel"),
        )(c_arr, g, b1)
    mine = pl.BlockSpec((None, tr, C), lambda i, p, c_ref, me_ref: (me_ref[0], i, 0))
    return pl.pallas_call(
        body, name=name,
        grid_spec=pltpu.PrefetchScalarGridSpec(num_scalar_prefetch=2, grid=(nb, P), in_specs=[g_spec, blk], out_specs=[blk, mine]),
        out_shape=[shape, shape], compiler_params=_params("parallel", "arbitrary"),
    )(c_arr, me_arr, g, b1)


def _sum_slots(name, b, half_arr=None):
    P, R, C = b.shape
    tr = _row_tile(R)
    nb = R // tr

    def body(*refs):
        b_ref, o_ref = refs[-2:]
        acc = b_ref[0].astype(F32)
        for s in range(1, P):
            acc = acc + b_ref[s].astype(F32)
        o_ref[...] = acc

    if half_arr is None:
        return pl.pallas_call(
            body, name=name, grid=(nb,),
            in_specs=[pl.BlockSpec((P, tr, C), lambda i: (0, i, 0))],
            out_specs=pl.BlockSpec((tr, C), lambda i: (i, 0)),
            out_shape=jax.ShapeDtypeStruct((R, C), F32),
            compiler_params=_params("parallel"),
        )(b)
    return pl.pallas_call(
        body, name=name,
        grid_spec=pltpu.PrefetchScalarGridSpec(
            num_scalar_prefetch=1, grid=(nb,),
            in_specs=[pl.BlockSpec((P, tr, C), lambda i, half: (0, i, 0))],
            out_specs=pl.BlockSpec((tr, C), lambda i, half: (half[0] * nb + i, 0))),
        out_shape=jax.ShapeDtypeStruct((2 * R, C), F32),
        compiler_params=_params("parallel"),
    )(half_arr, b)


def _adamw(name, w, g, m, v):
    R, C = w.shape
    tr = _row_tile(R)
    c1 = 1.0 - ADAM_B1 ** ADAM_STEP
    c2 = 1.0 - ADAM_B2 ** ADAM_STEP

    def body(w_ref, g_ref, m_ref, v_ref, d_ref, nm_ref, nv_ref):
        gg = g_ref[...]
        nm = ADAM_B1 * m_ref[...] + (1.0 - ADAM_B1) * gg
        nv = ADAM_B2 * v_ref[...] + (1.0 - ADAM_B2) * (gg * gg)
        m_hat = nm / c1
        v_hat = nv / c2
        d_ref[...] = -ADAM_LR * (m_hat / (jnp.sqrt(v_hat) + ADAM_EPS) + ADAM_WD * w_ref[...])
        nm_ref[...] = nm
        nv_ref[...] = nv

    spec = pl.BlockSpec((tr, C), lambda i: (i, 0))
    return pl.pallas_call(
        body, name=name, grid=(R // tr,), in_specs=[spec] * 4, out_specs=[spec] * 3,
        out_shape=[jax.ShapeDtypeStruct((R, C), F32)] * 3,
        compiler_params=_params("parallel"),
    )(w, g, m, v)


def _place():
    x, y, c = lax.axis_index("x"), lax.axis_index("y"), lax.axis_index("c")
    other_chips = [(1 - x, y), (x, 1 - y), (1 - x, 1 - y)]
    return x, y, c, other_chips


def _into_slot(name, w, dtype, slot_arr, n_slots):
    R, C = w.shape
    tr = _row_tile(R)

    def body(slot_ref, w_ref, o_ref):
        o_ref[...] = w_ref[...].astype(o_ref.dtype)

    return pl.pallas_call(
        body, name=name,
        grid_spec=pltpu.PrefetchScalarGridSpec(
            num_scalar_prefetch=1, grid=(R // tr,),
            in_specs=[pl.BlockSpec((tr, C), lambda i, slot: (i, 0))],
            out_specs=pl.BlockSpec((None, tr, C), lambda i, slot: (slot[0], i, 0))),
        out_shape=jax.ShapeDtypeStruct((n_slots, R, C), dtype),
        compiler_params=_params("parallel"),
    )(slot_arr, w)


def _gather_chips(bufs):
    n = len(bufs)

    def body(*refs):
        outs = refs[n:2 * n]
        ici_send, ici_recv, d2d_send, d2d_recv = refs[2 * n:]
        x, y, c, chips = _place()
        me = 2 * x + y
        started = []
        for k in range(n):
            rh = outs[k].shape[1] // 2
            mine = outs[k].at[me, pl.ds(c * rh, rh)]
            for j, (px, py) in enumerate(chips):
                cp = pltpu.make_async_remote_copy(
                    src_ref=mine, dst_ref=mine,
                    send_sem=ici_send.at[3 * k + j], recv_sem=ici_recv.at[3 * k + j],
                    device_id=(px, py, c), device_id_type=MESH)
                cp.start()
                started.append(cp)
        for k in range(n):
            rh = outs[k].shape[1] // 2
            for j, (px, py) in enumerate(chips):
                landed = outs[k].at[2 * px + py, pl.ds(c * rh, rh)]
                pltpu.make_async_remote_copy(
                    src_ref=landed, dst_ref=landed,
                    send_sem=ici_send.at[3 * k + j], recv_sem=ici_recv.at[3 * k + j],
                    device_id=(px, py, c), device_id_type=MESH).wait_recv()
                cp = pltpu.make_async_remote_copy(
                    src_ref=landed, dst_ref=landed,
                    send_sem=d2d_send.at[3 * k + j], recv_sem=d2d_recv.at[3 * k + j],
                    device_id=(x, y, 1 - c), device_id_type=MESH)
                cp.start()
                started.append(cp)
        for k in range(n):
            rh = outs[k].shape[1] // 2
            for j, (px, py) in enumerate(chips):
                landed = outs[k].at[2 * px + py, pl.ds((1 - c) * rh, rh)]
                pltpu.make_async_remote_copy(
                    src_ref=landed, dst_ref=landed,
                    send_sem=d2d_send.at[3 * k + j], recv_sem=d2d_recv.at[3 * k + j],
                    device_id=(x, y, 1 - c), device_id_type=MESH).wait_recv()
        for cp in started:
            cp.wait_send()

    return pl.pallas_call(
        body, name="gather_weights",
        in_specs=[ANY] * n, out_specs=[ANY] * n,
        out_shape=[jax.ShapeDtypeStruct(b.shape, b.dtype) for b in bufs],
        input_output_aliases={k: k for k in range(n)},
        scratch_shapes=[pltpu.SemaphoreType.DMA((3 * n,))] * 4,
        compiler_params=pltpu.CompilerParams(has_side_effects=True),
    )(*bufs)


def _gather_forward(bufs):
    n = len(bufs)

    def body(*refs):
        outs = refs[n:2 * n]
        d2d_send, d2d_recv = refs[2 * n:]
        x, y, c, chips = _place()
        started = []
        for k in range(n):
            rh = outs[k].shape[1] // 2
            for j, (px, py) in enumerate(chips):
                landed = outs[k].at[2 * px + py, pl.ds(c * rh, rh)]
                cp = pltpu.make_async_remote_copy(
                    src_ref=landed, dst_ref=landed,
                    send_sem=d2d_send.at[3 * k + j], recv_sem=d2d_recv.at[3 * k + j],
                    device_id=(x, y, 1 - c), device_id_type=MESH)
                cp.start()
                started.append(cp)
        for k in range(n):
            rh = outs[k].shape[1] // 2
            for j, (px, py) in enumerate(chips):
                landed = outs[k].at[2 * px + py, pl.ds((1 - c) * rh, rh)]
                pltpu.make_async_remote_copy(
                    src_ref=landed, dst_ref=landed,
                    send_sem=d2d_send.at[3 * k + j], recv_sem=d2d_recv.at[3 * k + j],
                    device_id=(x, y, 1 - c), device_id_type=MESH).wait_recv()
        for cp in started:
            cp.wait_send()

    return pl.pallas_call(
        body, name="gather_rest_forward",
        in_specs=[ANY] * n, out_specs=[ANY] * n,
        out_shape=[jax.ShapeDtypeStruct(b.shape, b.dtype) for b in bufs],
        input_output_aliases={k: k for k in range(n)},
        scratch_shapes=[pltpu.SemaphoreType.DMA((3 * n,))] * 2,
        compiler_params=pltpu.CompilerParams(has_side_effects=True),
    )(*bufs)


def _to_chips_start(name, arrays, n, src_fn, dst_fn, after):
    m = len(arrays)

    def body(*refs):
        send_sem, recv_sem = refs[m + 1], refs[m + 2]
        thru = refs[m + 3:2 * m + 3]
        token = refs[2 * m + 3]
        x, y, c, chips = _place()
        for k in range(n):
            for j, (px, py) in enumerate(chips):
                pltpu.make_async_remote_copy(
                    src_ref=src_fn(thru, k, px, py, x, y, c), dst_ref=dst_fn(thru, k, px, py, x, y, c),
                    send_sem=send_sem.at[3 * k + j], recv_sem=recv_sem.at[3 * k + j],
                    device_id=(px, py, c), device_id_type=MESH).start()
        token[...] = jnp.zeros_like(token)

    res = pl.pallas_call(
        body, name=name,
        out_shape=(pltpu.SemaphoreType.DMA((3 * n,)), pltpu.SemaphoreType.DMA((3 * n,)),
                   *[pltpu.HBM(a.shape, a.dtype) for a in arrays], jax.ShapeDtypeStruct((8, LANES), F32)),
        in_specs=[HBM] * m + [ANY],
        out_specs=(SEM, SEM, *[HBM] * m, pl.BlockSpec(memory_space=pltpu.VMEM)),
        input_output_aliases={i: i + 2 for i in range(m)},
        compiler_params=pltpu.CompilerParams(has_side_effects=EFFECT),
    )(*[pltpu.with_memory_space_constraint(a, pltpu.HBM) for a in arrays], after)
    return res[0], res[1], list(res[2:2 + m]), res[2 + m]


def _to_chips_wait(name, send_sem, recv_sem, arrays, n, src_fn, land_fn, after):
    m = len(arrays)
    after = list(after) if isinstance(after, (list, tuple)) else [after]

    def body(*refs):
        send, recv = refs[m], refs[m + 1]
        outs = refs[m + 2 + len(after):]
        x, y, c, chips = _place()
        for k in range(n):
            for j, (px, py) in enumerate(chips):
                cp = pltpu.make_async_remote_copy(
                    src_ref=src_fn(outs, k, px, py, x, y, c), dst_ref=land_fn(outs, k, px, py, x, y, c),
                    send_sem=send.at[3 * k + j], recv_sem=recv.at[3 * k + j],
                    device_id=(px, py, c), device_id_type=MESH)
                cp.wait_send()
                cp.wait_recv()

    res = pl.pallas_call(
        body, name=name,
        out_shape=[pltpu.HBM(a.shape, a.dtype) for a in arrays],
        in_specs=[HBM] * m + [SEM, SEM] + [ANY] * len(after), out_specs=[HBM] * m,
        input_output_aliases={i: i for i in range(m)},
        compiler_params=pltpu.CompilerParams(has_side_effects=EFFECT),
    )(*arrays, send_sem, recv_sem, *after)
    return list(res)


def _slot_half(refs, k, chip, c):
    rh = refs[k].shape[1] // 2
    return refs[k].at[chip, pl.ds(c * rh, rh)]


def _ag_mine(refs, k, px, py, x, y, c):
    return _slot_half(refs, k, 2 * x + y, c)


def _ag_theirs(refs, k, px, py, x, y, c):
    return _slot_half(refs, k, 2 * px + py, c)


def _rs_ends(n):
    def src(refs, k, px, py, x, y, c):
        return refs[k].at[2 * px + py]

    def dst(refs, k, px, py, x, y, c):
        return refs[n + k].at[2 * x + y]

    def land(refs, k, px, py, x, y, c):
        return refs[n + k].at[2 * px + py]

    return src, dst, land


def _half(ref, kind, p, c, rh):
    if kind == "cols":
        return ref.at[p, pl.ds(c * rh, rh)]
    return ref.at[pl.ds((2 * p + c) * rh, rh)]


def _swap_halves(name, grads, kinds, rhs):
    n = len(grads)

    def body(*refs):
        ins, outs = refs[:n], refs[n:2 * n]
        send_sem, recv_sem = refs[2 * n:]
        x, y, c, _ = _place()
        started = []
        for k in range(n):
            for p in range(N_CHIPS):
                cp = pltpu.make_async_remote_copy(
                    src_ref=_half(ins[k], kinds[k], p, 1 - c, rhs[k]), dst_ref=outs[k].at[p],
                    send_sem=send_sem.at[N_CHIPS * k + p], recv_sem=recv_sem.at[N_CHIPS * k + p],
                    device_id=(x, y, 1 - c), device_id_type=MESH)
                cp.start()
                started.append(cp)
        for cp in started:
            cp.wait()

    out_shape = []
    for g, kind, rh in zip(grads, kinds, rhs):
        out_shape.append(jax.ShapeDtypeStruct((N_CHIPS, rh, g.shape[-1]), g.dtype))
    return pl.pallas_call(
        body, name=name,
        in_specs=[ANY] * n, out_specs=[ANY] * n, out_shape=out_shape,
        scratch_shapes=[pltpu.SemaphoreType.DMA((N_CHIPS * n,))] * 2,
        compiler_params=pltpu.CompilerParams(has_side_effects=True),
    )(*grads)


def _join_halves(name, fulls):
    n = len(fulls)

    def body(*refs):
        outs = refs[n:2 * n]
        send_sem, recv_sem = refs[2 * n:]
        x, y, c, _ = _place()
        started = []
        for k in range(n):
            rh = outs[k].shape[0] // 2
            mine = outs[k].at[pl.ds(c * rh, rh)]
            cp = pltpu.make_async_remote_copy(
                src_ref=mine, dst_ref=mine, send_sem=send_sem.at[k], recv_sem=recv_sem.at[k],
                device_id=(x, y, 1 - c), device_id_type=MESH)
            cp.start()
            started.append(cp)
        for k in range(n):
            rh = outs[k].shape[0] // 2
            theirs = outs[k].at[pl.ds((1 - c) * rh, rh)]
            pltpu.make_async_remote_copy(
                src_ref=theirs, dst_ref=theirs, send_sem=send_sem.at[k], recv_sem=recv_sem.at[k],
                device_id=(x, y, 1 - c), device_id_type=MESH).wait_recv()
        for cp in started:
            cp.wait_send()

    return pl.pallas_call(
        body, name=name,
        in_specs=[ANY] * n, out_specs=[ANY] * n,
        out_shape=[jax.ShapeDtypeStruct(f.shape, f.dtype) for f in fulls],
        input_output_aliases={k: k for k in range(n)},
        scratch_shapes=[pltpu.SemaphoreType.DMA((n,))] * 2,
        compiler_params=pltpu.CompilerParams(has_side_effects=True),
    )(*fulls)


def _gather_all(block):
    def body(in_ref, out_ref, send_sem, recv_sem, local_sem):
        x, y, c, _ = _place()

        def slot(px, py, pc):
            return out_ref.at[4 * px + 2 * py + pc]

        loc = pltpu.make_async_copy(in_ref, slot(x, y, c), local_sem)
        loc.start()
        started = []
        for d in range(1, N_DEV):
            fx, fy, fc = d >> 2, (d >> 1) & 1, d & 1
            cp = pltpu.make_async_remote_copy(
                src_ref=in_ref, dst_ref=slot(x, y, c), send_sem=send_sem.at[d - 1], recv_sem=recv_sem.at[d - 1],
                device_id=(x ^ fx, y ^ fy, c ^ fc), device_id_type=MESH)
            cp.start()
            started.append(cp)
        for d in range(1, N_DEV):
            fx, fy, fc = d >> 2, (d >> 1) & 1, d & 1
            landed = slot(x ^ fx, y ^ fy, c ^ fc)
            pltpu.make_async_remote_copy(
                src_ref=in_ref, dst_ref=landed, send_sem=send_sem.at[d - 1], recv_sem=recv_sem.at[d - 1],
                device_id=(x ^ fx, y ^ fy, c ^ fc), device_id_type=MESH).wait_recv()
        for cp in started:
            cp.wait_send()
        loc.wait()

    return pl.pallas_call(
        body, name="gather_small_grads",
        in_specs=[ANY], out_specs=ANY,
        out_shape=jax.ShapeDtypeStruct((N_DEV,) + block.shape, block.dtype),
        scratch_shapes=[pltpu.SemaphoreType.DMA((N_DEV - 1,))] * 2 + [pltpu.SemaphoreType.DMA(())],
        compiler_params=pltpu.CompilerParams(has_side_effects=True),
    )(block)


def _pack(pieces):
    flat = jnp.concatenate([p.reshape(-1) for p in pieces])
    n = flat.shape[0]
    padded = -(-n // (8 * LANES)) * (8 * LANES)
    return jnp.pad(flat, (0, padded - n)).reshape(-1, LANES)


def _unpack(packed, shapes):
    flat = packed.reshape(-1)
    out, at = [], 0
    for s in shapes:
        n = math.prod(s)
        out.append(flat[at:at + n].reshape(s))
        at += n
    return out


def kernel(x, meta_tokens, pre_mix_g, w_in, gate_b, dw_w, dw_b, conv_ln_g, conv_ln_b, w_conv_out, w_attn_out, w_o, post_mix_g, pre_ffn_g, w_ffn_in, w_ffn_out, post_ffn_g, loss_target, m_meta_tokens, m_pre_mix_g, m_w_in, m_gate_b, m_dw_w, m_dw_b, m_conv_ln_g, m_conv_ln_b, m_w_conv_out, m_w_attn_out, m_w_o, m_post_mix_g, m_pre_ffn_g, m_w_ffn_in, m_w_ffn_out, m_post_ffn_g, v_meta_tokens, v_pre_mix_g, v_w_in, v_gate_b, v_dw_w, v_dw_b, v_conv_ln_g, v_conv_ln_b, v_w_conv_out, v_w_attn_out, v_w_o, v_post_mix_g, v_pre_ffn_g, v_w_ffn_in, v_w_ffn_out, v_post_ffn_g):
    S, D = x.shape[1], x.shape[2]
    L = S + N_META
    T = -(-L // ROW_BLOCK) * ROW_BLOCK
    Ta = -(-L // ATT_BLOCK) * ATT_BLOCK
    tm = _tile(T, MM_ROWS)
    tc = _tile(T, CONTRACT_ROWS)
    ts = _tile(T, STAGE_ROWS)
    tw = _tile(T, WIDE_STAGE_ROWS)
    H = D // HEAD_DIM
    F = w_ffn_out.shape[1] * N_CHIPS
    Dc = D // N_CHIPS
    P = N_CHIPS
    me = 2 * lax.axis_index("x") + lax.axis_index("y")
    c_arr = lax.axis_index("c").astype(jnp.int32).reshape(1)

    dw_w_pad = jnp.pad(dw_w[0], ((0, CONV_PAD - CONV_WIDTH), (0, 0)))
    me_arr = me.astype(jnp.int32).reshape(1)
    to_gather = [("w_in", w_in[0], BF16), ("w_conv_out", w_conv_out[0], BF16), ("w_attn_out", w_attn_out[0], BF16),
                 ("w_o", w_o[0], BF16), ("w_ffn_in", w_ffn_in[0], BF16), ("w_ffn_out", w_ffn_out[0], BF16),
                 ("meta", meta_tokens, F32), ("taps", dw_w_pad, F32)]
    slot = {n: _into_slot("slot_" + n, w, dt, me_arr, P) for n, w, dt in to_gather}
    win3, meta4, taps4 = _gather_chips([slot["w_in"], slot["meta"], slot["taps"]])
    meta_full = meta4.transpose(1, 0, 2).reshape(N_META, D)
    taps = taps4.transpose(1, 0, 2).reshape(CONV_PAD, D)[:CONV_WIDTH]
    later = ["w_conv_out", "w_attn_out", "w_o", "w_ffn_in", "w_ffn_out"]
    ag_send, ag_recv, in_flight, ag_token = _to_chips_start(
        "gather_rest_start", [slot[n] for n in later], len(later), _ag_mine, _ag_mine, meta4)

    h0 = jnp.concatenate([meta_full, x[0], jnp.zeros((T - L, D), F32)], axis=0)
    (u1,) = _rowwise_fwd("rms_pre_mix", f_rms, [(h0, D, 0)], [(pre_mix_g + ag_token[0:1, 0:1], D, 0)], [(D, BF16)], T, ts)
    p = _mm_nn_cols("mm_in", u1, win3, tm)
    q, k, v = _qkv_split(p, D, T, Ta)
    o2, rtot = _attn_fwd(q, k, v)
    landed = _to_chips_wait("gather_rest_wait", ag_send, ag_recv, in_flight, len(later), _ag_mine, _ag_theirs, o2)
    wco4, wao4, wo4, wfi3, wfo4 = _gather_forward(landed)
    wco, wao, wo = (w.reshape(D, D) for w in (wco4, wao4, wo4))
    wfo = wfo4.reshape(F, D)
    (uglu,) = _rowwise_fwd("glu", f_glu, [(p, D, 0), (p, D, 1)], [], [(D, F32)], T, ts)
    yc = _shift_conv("dwconv", uglu, taps, CONV_PAD, CONV_PAD - (CONV_WIDTH - 1), T)
    conv_pars = [(dw_b, D, 0), (conv_ln_g, D, 0), (conv_ln_b, D, 0)]
    (ys,) = _rowwise_fwd("conv_post", f_convpost, [(yc, D, 0)], conv_pars, [(D, BF16)], T, ts)
    y_conv = _mm_nn("mm_conv_out", ys, wco, tm)
    y_attn = _mm_nn("mm_attn_out", o2, wao, tm, rows=T)
    mix_rows = [(p, D, 5), (p, D, 6), (y_conv, D, 0), (y_attn, D, 0)]
    mix_pars = [(gate_b, D, 0), (gate_b, D, 1)]
    (mixin,) = _rowwise_fwd("gate_mix", f_mix, mix_rows, mix_pars, [(D, BF16)], T, ts)
    mix = _mm_nn("mm_o", mixin, wo, tm)
    (h1,) = _rowwise_fwd("res_post_mix", f_res_rms, [(h0, D, 0), (mix, D, 0)], [(post_mix_g, D, 0)], [(D, F32)], T, ts)
    (u2,) = _rowwise_fwd("rms_pre_ffn", f_rms, [(h1, D, 0)], [(pre_ffn_g, D, 0)], [(D, BF16)], T, ts)
    ab = _mm_nn_cols("mm_ffn_in", u2, wfi3, tm)
    (fin,) = _rowwise_fwd("swiglu", f_swiglu, [(ab, F, 0), (ab, F, 1)], [], [(F, BF16)], T, tw)
    f = _mm_nn("mm_ffn_out", fin, wfo, tm)
    (h2,) = _rowwise_fwd("res_post_ffn", f_res_rms, [(h1, D, 0), (f, D, 0)], [(post_ffn_g, D, 0)], [(D, F32)], T, ts)

    dy, part = _loss_head(h2[N_META:L], loss_target[0], _row_tile(S))
    loss = lax.psum(0.5 * jnp.sum(part) / D, ("x", "y", "c"))
    dh2 = jnp.pad(dy, ((N_META, T - L), (0, 0)))

    (df,), (g_post_ffn,) = _rowwise_bwd("res_post_ffn_bwd", f_res_rms, [(h1, D, 0), (f, D, 0)], [(post_ffn_g, D, 0)],
                                        [(dh2, D, 0)], [None, BF16], T, ts)
    dfin = _mm_nt("mm_ffn_out_dx", df, wfo, tm)
    g_wfo = _mm_tn("mm_ffn_out_dw", fin, df, tc, F // MXU_WIDTH)
    (da, db), _ = _rowwise_bwd("swiglu_bwd", f_swiglu, [(ab, F, 0), (ab, F, 1)], [], [(dfin, F, 0)], [BF16, BF16], T, tw)
    dab = jnp.concatenate([da, db], axis=1)
    du2 = _mm_nt_cols("mm_ffn_in_dx", dab, wfi3, tm)
    g_wfi = _mm_tn_cols("mm_ffn_in_dw", u2, dab, tc, P)
    (dh1,), (g_pre_ffn,) = _rowwise_bwd("rms_pre_ffn_bwd", f_rms_id, [(h1, D, 0)], [(pre_ffn_g, D, 0)],
                                        [(du2, D, 0), (dh2, D, 0)], [F32], T, ts)
    (dmix,), (g_post_mix,) = _rowwise_bwd("res_post_mix_bwd", f_res_rms, [(h0, D, 0), (mix, D, 0)], [(post_mix_g, D, 0)],
                                          [(dh1, D, 0)], [None, BF16], T, ts)
    dmixin = _mm_nt("mm_o_dx", dmix, wo, tm)
    g_wo = _mm_tn("mm_o_dw", mixin, dmix, tc, D // MXU_WIDTH)
    (dpc, dpa, dyconv, dyattn), (g_gate_c, g_gate_a) = _rowwise_bwd(
        "gate_mix_bwd", f_mix, mix_rows, mix_pars, [(dmixin, D, 0)], [BF16, BF16, BF16, BF16], T, ts)
    g_wco = _mm_tn("mm_conv_out_dw", ys, dyconv, tc, D // MXU_WIDTH)
    dys = _mm_nt("mm_conv_out_dx", dyconv, wco, tm)
    g_wao = _mm_tn("mm_attn_out_dw", o2, dyattn, tc, D // MXU_WIDTH)
    do2 = _mm_nt("mm_attn_out_dx", dyattn, wao, tm, BF16, out_rows=Ta)

    early = [g_wco, g_wao, g_wo, g_wfi, g_wfo]
    early_kinds = ["rows", "rows", "rows", "cols", "rows"]
    early_rhs = [(g.shape[1] if kind == "cols" else g.shape[0] // P) // 2 for g, kind in zip(early, early_kinds)]
    early_sib = _swap_halves("grad_swap_halves_early", early, early_kinds, early_rhs)
    early_pairs = [_pair_add_bf16("grad_pair_add_%d" % (n + 1), g, b1, kind, c_arr, me_arr)
                   for n, (g, b1, kind) in enumerate(zip(early, early_sib, early_kinds))]
    rs_src, rs_dst, rs_land = _rs_ends(len(early))
    rs_send, rs_recv, rs_flight, rs_token = _to_chips_start(
        "grad_scatter_start", [pr[0] for pr in early_pairs] + [pr[1] for pr in early_pairs], len(early),
        rs_src, rs_dst, early_pairs[-1][1])
    (dyc,), (g_dw_b, g_ln_g, g_ln_b) = _rowwise_bwd("conv_post_bwd", f_convpost, [(yc, D, 0)], conv_pars,
                                                    [(dys, D, 0)], [F32], T, ts)
    duglu = _shift_conv("dwconv_dx", dyc, taps[::-1], 0, 0, T)
    g_taps = _conv_dw("dwconv_dw", uglu, dyc, T)
    (dp0, dp1), _ = _rowwise_bwd("glu_bwd", f_glu, [(p, D, 0), (p, D, 1)], [], [(duglu, D, 0)], [BF16, BF16], T, ts)
    dq, dk, dv = _attn_bwd(q, k, v, do2, rtot, rs_token)
    rs_done = _to_chips_wait("grad_scatter_wait", rs_send, rs_recv, rs_flight, len(early), rs_src, rs_land, dq)
    early_slots = rs_done[len(early):]
    dp = jnp.concatenate([dp0, dp1, dq[:T], dk[:T], dv[:T], dpc, dpa], axis=1)
    du1 = _mm_nt_cols("mm_in_dx", dp, win3, tm)
    g_win = _mm_tn_cols("mm_in_dw", u1, dp, tc, P)
    (win_sib,) = _swap_halves("grad_swap_halves_in", [g_win], ["cols"], [g_win.shape[1] // 2])
    win_pair = _pair_add_bf16("grad_pair_add_0", g_win, win_sib, "cols", c_arr, me_arr)
    in_src, in_dst, in_land = _rs_ends(1)
    in_send, in_recv, in_flight, in_token = _to_chips_start(
        "grad_scatter_in_start", list(win_pair), 1, in_src, in_dst, win_pair[1])
    (dh0,), (g_pre_mix,) = _rowwise_bwd("rms_pre_mix_bwd", f_rms_id, [(h0, D, 0)],
                                        [(pre_mix_g + in_token[0:1, 0:1], D, 0)],
                                        [(du1, D, 0), (dh1, D, 0)], [F32], T, ts)
    grad_x = dh0[N_META:L][None]
    g_early = _join_halves("grad_join_halves_early", [_sum_slots("grad_chip_sum_%d" % (n + 1), s, c_arr)
                                                      for n, s in enumerate(early_slots)])

    small_shapes = [(1, D), (1, D), (1, D), (CONV_WIDTH, D), (1, D), (1, D), (1, D), (1, D), (1, D), (1, D), (N_META, D)]
    small = _pack([g_pre_mix, g_gate_c, g_gate_a, g_taps, g_dw_b, g_ln_g, g_ln_b, g_post_mix, g_pre_ffn, g_post_ffn,
                   dh0[:N_META]])
    summed = _sum_slots("small_grad_sum", _gather_all(small))
    (s_pre_mix, s_gate_c, s_gate_a, s_taps, s_dw_b, s_ln_g, s_ln_b, s_post_mix, s_pre_ffn, s_post_ffn,
     s_meta) = _unpack(summed, small_shapes)
    s_gate_b = jnp.concatenate([s_gate_c, s_gate_a], axis=1)
    s_taps = lax.dynamic_slice_in_dim(s_taps, me * Dc, Dc, axis=1)[None]
    s_meta = lax.dynamic_slice_in_dim(s_meta, me * Dc, Dc, axis=1)

    grads = {
        "meta_tokens": s_meta, "pre_mix_g": s_pre_mix, "gate_b": s_gate_b, "dw_w": s_taps,
        "dw_b": s_dw_b, "conv_ln_g": s_ln_g, "conv_ln_b": s_ln_b, "w_conv_out": g_early[0][None],
        "w_attn_out": g_early[1][None], "w_o": g_early[2][None], "post_mix_g": s_post_mix, "pre_ffn_g": s_pre_ffn,
        "w_ffn_in": g_early[3][None], "w_ffn_out": g_early[4][None], "post_ffn_g": s_post_ffn,
    }
    weights = {
        "meta_tokens": (meta_tokens, m_meta_tokens, v_meta_tokens), "pre_mix_g": (pre_mix_g, m_pre_mix_g, v_pre_mix_g),
        "w_in": (w_in, m_w_in, v_w_in), "gate_b": (gate_b, m_gate_b, v_gate_b), "dw_w": (dw_w, m_dw_w, v_dw_w),
        "dw_b": (dw_b, m_dw_b, v_dw_b), "conv_ln_g": (conv_ln_g, m_conv_ln_g, v_conv_ln_g),
        "conv_ln_b": (conv_ln_b, m_conv_ln_b, v_conv_ln_b), "w_conv_out": (w_conv_out, m_w_conv_out, v_w_conv_out),
        "w_attn_out": (w_attn_out, m_w_attn_out, v_w_attn_out), "w_o": (w_o, m_w_o, v_w_o),
        "post_mix_g": (post_mix_g, m_post_mix_g, v_post_mix_g), "pre_ffn_g": (pre_ffn_g, m_pre_ffn_g, v_pre_ffn_g),
        "w_ffn_in": (w_ffn_in, m_w_ffn_in, v_w_ffn_in), "w_ffn_out": (w_ffn_out, m_w_ffn_out, v_w_ffn_out),
        "post_ffn_g": (post_ffn_g, m_post_ffn_g, v_post_ffn_g),
    }
    names = list(weights)
    big_names = ["w_in", "w_conv_out", "w_attn_out", "w_o", "w_ffn_in", "w_ffn_out"]
    small_names = [n for n in names if n not in big_names]

    delta, new_m, new_v = {}, {}, {}

    def big_update(n):
        w, m, v2 = weights[n]
        d, nm, nv = _adamw("adamw_" + n, w[0], grads[n][0], m[0], v2[0])
        delta[n], new_m[n], new_v[n] = d[None], nm[None], nv[None]

    for n in big_names[1:]:
        big_update(n)
    shapes = [weights[n][0].shape for n in small_names]
    packed = [_pack([weights[n][k] for n in small_names]) for k in range(3)]
    d, nm, nv = _adamw("adamw_small", packed[0], _pack([grads[n] for n in small_names]), packed[1], packed[2])
    for n, dd, mm, vv in zip(small_names, _unpack(d, shapes), _unpack(nm, shapes), _unpack(nv, shapes)):
        delta[n], new_m[n], new_v[n] = dd, mm, vv

    in_done = _to_chips_wait("grad_scatter_in_wait", in_send, in_recv, in_flight, 1, in_src, in_land,
                             [d] + [delta[n] for n in big_names[1:]])
    (g_in,) = _join_halves("grad_join_halves_in", [_sum_slots("grad_chip_sum_0", in_done[1], c_arr)])
    grads["w_in"] = g_in[None]
    big_update("w_in")

    return (loss, grad_x, *[grads[n].reshape(weights[n][0].shape) for n in names], *[delta[n] for n in names],
            *[new_m[n] for n in names], *[new_v[n] for n in names])
```

```python
import math

import jax
import jax.numpy as jnp
from jax import lax
from jax.experimental import pallas as pl
from jax.experimental.pallas import tpu as pltpu

F32 = jnp.float32
BF16 = jnp.bfloat16

N_META = 16
CONV_WIDTH = 31
CONV_PAD = 32
HEAD_DIM = 64
RMS_EPS = 1e-6
LN_EPS = 1e-5
ROW_BLOCK = 128
MXU_WIDTH = 256
ATT_BLOCK = MXU_WIDTH
ATT_HEADS = 8
ATT_HEADS_BWD = 4
LANES = 128
N_CHIPS = 4
N_DEV = 8
MM_ROWS = 544
CONTRACT_ROWS = 2176
STAGE_ROWS = 272
WIDE_STAGE_ROWS = 128
VMEM_LIMIT = 56 * 1024 * 1024

ADAM_LR = 0.001
ADAM_B1 = 0.9
ADAM_B2 = 0.999
ADAM_EPS = 1e-08
ADAM_WD = 0.01
ADAM_STEP = 10

MESH = pl.DeviceIdType.MESH
ANY = pl.BlockSpec(memory_space=pl.ANY)
HBM = pl.BlockSpec(memory_space=pltpu.HBM)
SEM = pl.BlockSpec(memory_space=pltpu.SEMAPHORE)
EFFECT = pltpu.SideEffectType.DATAFLOW_SIDE_EFFECTING


def _params(*sem):
    return pltpu.CompilerParams(dimension_semantics=sem if sem else None, vmem_limit_bytes=VMEM_LIMIT)


def _rms(x, g):
    return x * lax.rsqrt(jnp.mean(x * x, axis=-1, keepdims=True) + RMS_EPS) * g


def f_rms(h, g):
    return (_rms(h, g),)


def f_rms_id(h, g):
    return (_rms(h, g), h)


def f_res_rms(h, m, g):
    return (h + _rms(m, g),)


def f_glu(a, gate):
    return (a * lax.logistic(gate),)


def f_convpost(yc, b, ln_g, ln_b):
    y = yc + b
    mu = jnp.mean(y, axis=-1, keepdims=True)
    xc = y - mu
    var = jnp.mean(xc * xc, axis=-1, keepdims=True)
    yl = xc * lax.rsqrt(var + LN_EPS) * ln_g + ln_b
    return (yl * lax.logistic(yl),)


def f_mix(pc, pa, yc, ya, bc, ba):
    return (lax.logistic(pc + bc) * yc + lax.logistic(pa + ba) * ya,)


def f_swiglu(a, b):
    return (a * lax.logistic(a) * b,)


def _tile(T, target):
    return max(t for t in range(16, target + 1, 16) if T % t == 0)


def _row_map(j):
    return lambda i: (i, j)


def _par_map(j):
    return lambda i: (0, j)


def _rowwise_fwd(name, f, rows, pars, outs, T, tm):
    n_in = len(rows) + len(pars)

    def body(*refs):
        vals = [r[...].astype(F32) for r in refs[:n_in]]
        res = f(*vals)
        for o_ref, o in zip(refs[n_in:], res):
            o_ref[...] = o.astype(o_ref.dtype)

    in_specs = [pl.BlockSpec((tm, w), _row_map(j)) for _, w, j in rows]
    in_specs += [pl.BlockSpec((1, w), _par_map(j)) for _, w, j in pars]
    return pl.pallas_call(
        body, name=name, grid=(T // tm,),
        in_specs=in_specs,
        out_specs=[pl.BlockSpec((tm, w), _row_map(0)) for w, _ in outs],
        out_shape=[jax.ShapeDtypeStruct((T, w), dt) for w, dt in outs],
        compiler_params=_params("parallel"),
    )(*[a for a, _, _ in rows], *[a for a, _, _ in pars])


def _rowwise_bwd(name, f, rows, pars, cots, drow_dtypes, T, tm, joined=False):
    n_r, n_p, n_c = len(rows), len(pars), len(cots)
    n_in = n_r + n_p + n_c
    keep = [k for k, dt in enumerate(drow_dtypes) if dt is not None]
    n_out = 1 if joined else len(keep)

    def body(*refs):
        rv = [r[...].astype(F32) for r in refs[:n_r]]
        pv = [r[...].astype(F32) for r in refs[n_r:n_r + n_p]]
        cv = [r[...].astype(F32) for r in refs[n_r + n_p:n_in]]
        _, vjp = jax.vjp(f, *rv, *pv)
        g = vjp(tuple(cv))
        drow_refs = refs[n_in:n_in + n_out]
        dpar_refs = refs[n_in + n_out:]
        if joined:
            at = 0
            for k in keep:
                drow_refs[0][:, at:at + rows[k][1]] = g[k].astype(drow_refs[0].dtype)
                at += rows[k][1]
        else:
            for r, k in zip(drow_refs, keep):
                r[...] = g[k].astype(r.dtype)

        @pl.when(pl.program_id(0) == 0)
        def _():
            for r in dpar_refs:
                r[...] = jnp.zeros_like(r)

        for r, gp in zip(dpar_refs, g[n_r:]):
            r[...] += gp

    in_specs = [pl.BlockSpec((tm, w), _row_map(j)) for _, w, j in rows]
    in_specs += [pl.BlockSpec((1, w), _par_map(j)) for _, w, j in pars]
    in_specs += [pl.BlockSpec((tm, w), _row_map(j)) for _, w, j in cots]
    widths = [sum(rows[k][1] for k in keep)] if joined else [rows[k][1] for k in keep]
    out_specs = [pl.BlockSpec((tm, w), _row_map(0)) for w in widths]
    out_specs += [pl.BlockSpec((1, w), _par_map(0)) for _, w, _ in pars]
    out_shape = [jax.ShapeDtypeStruct((T, w), drow_dtypes[k]) for w, k in zip(widths, keep)]
    out_shape += [jax.ShapeDtypeStruct((1, w), F32) for _, w, _ in pars]
    res = pl.pallas_call(
        body, name=name, grid=(T // tm,),
        in_specs=in_specs, out_specs=out_specs, out_shape=out_shape,
        compiler_params=_params("arbitrary"),
    )(*[a for a, _, _ in rows], *[a for a, _, _ in pars], *[a for a, _, _ in cots])
    return res[:n_out], res[n_out:]


NN = (((1,), (0,)), ((), ()))
NT = (((1,), (1,)), ((), ()))
TN = (((0,), (0,)), ((), ()))


def _mm(name, a, b, dims, out_shape, grid, a_spec, b_spec, o_spec, red_axis=None, init=None):
    n_red = None if red_axis is None else grid[red_axis]

    def body(a_ref, b_ref, *rest):
        o_ref = rest[-1]
        prod = lax.dot_general(a_ref[...], b_ref[...], dims, preferred_element_type=F32)
        if n_red is None:
            o_ref[...] = prod.astype(o_ref.dtype)
        else:
            @pl.when(pl.program_id(red_axis) == 0)
            def _():
                o_ref[...] = prod

            @pl.when(pl.program_id(red_axis) > 0)
            def _():
                o_ref[...] += prod

    sem = ["parallel"] * len(grid)
    if red_axis is not None:
        sem[red_axis] = "arbitrary"
    if init is None:
        return pl.pallas_call(
            body, name=name, grid=grid, in_specs=[a_spec, b_spec], out_specs=o_spec, out_shape=out_shape,
            compiler_params=_params(*sem),
        )(a, b)
    return pl.pallas_call(
        body, name=name, grid=grid, in_specs=[a_spec, b_spec, ANY], out_specs=o_spec, out_shape=out_shape,
        input_output_aliases={2: 0}, compiler_params=_params(*sem),
    )(a, b, init)


def _mm_nn(name, a, w, tm, out_dtype=F32, rows=None):
    T, K = a.shape
    T = rows or T
    N = w.shape[1]
    return _mm(name, a, w, NN, jax.ShapeDtypeStruct((T, N), out_dtype), (T // tm,),
               pl.BlockSpec((tm, K), lambda i: (i, 0)), pl.BlockSpec((K, N), lambda i: (0, 0)),
               pl.BlockSpec((tm, N), lambda i: (i, 0)))


def _mm_nt(name, a, w, tm, out_dtype=F32, out_rows=None):
    T, N = a.shape
    K = w.shape[0]
    init = None if out_rows is None else jnp.zeros((out_rows, K), out_dtype)
    return _mm(name, a, w, NT, jax.ShapeDtypeStruct((out_rows or T, K), out_dtype), (T // tm,),
               pl.BlockSpec((tm, N), lambda i: (i, 0)), pl.BlockSpec((K, N), lambda i: (0, 0)),
               pl.BlockSpec((tm, K), lambda i: (i, 0)), init=init)


def _mm_tn(name, a, b, tm, n_row_blocks):
    K = a.shape[1]
    T, N = b.shape
    kb = K // n_row_blocks
    return _mm(name, a, b, TN, jax.ShapeDtypeStruct((K, N), F32), (n_row_blocks, T // tm),
               pl.BlockSpec((tm, kb), lambda r, t: (t, r)), pl.BlockSpec((tm, N), lambda r, t: (t, 0)),
               pl.BlockSpec((kb, N), lambda r, t: (r, 0)), red_axis=1)


def _mm_nn_cols(name, a, w3, tm):
    T, K = a.shape
    P, _, Ns = w3.shape
    return _mm(name, a, w3, NN, jax.ShapeDtypeStruct((T, P * Ns), F32), (P, T // tm),
               pl.BlockSpec((tm, K), lambda p, i: (i, 0)), pl.BlockSpec((None, K, Ns), lambda p, i: (p, 0, 0)),
               pl.BlockSpec((tm, Ns), lambda p, i: (i, p)))


def _mm_nt_cols(name, a, w3, tm):
    T = a.shape[0]
    P, K, Ns = w3.shape
    return _mm(name, a, w3, NT, jax.ShapeDtypeStruct((T, K), F32), (T // tm, P),
               pl.BlockSpec((tm, Ns), lambda i, p: (i, p)), pl.BlockSpec((None, K, Ns), lambda i, p: (p, 0, 0)),
               pl.BlockSpec((tm, K), lambda i, p: (i, 0)), red_axis=1)


def _mm_tn_cols(name, a, b, tm, P):
    T, K = a.shape
    Ns = b.shape[1] // P
    return _mm(name, a, b, TN, jax.ShapeDtypeStruct((P, K, Ns), F32), (P, T // tm),
               pl.BlockSpec((tm, K), lambda p, t: (t, 0)), pl.BlockSpec((tm, Ns), lambda p, t: (t, p)),
               pl.BlockSpec((None, K, Ns), lambda p, t: (p, 0, 0)), red_axis=1)


def _tap_windows(win, off, tb):
    out = []
    for b in range(8):
        taps = [j for j in range(CONV_WIDTH) if (off + j) % 8 == b]
        if taps:
            shifted = win[b:b + tb + CONV_PAD, :]
            out += [(shifted, off + j - b, j) for j in taps]
    return out


def _shift_conv(name, x, w, place, off, T):
    C = x.shape[1]
    tb = ROW_BLOCK
    zero_at = 0 if place else T

    def body(x_ref, w_ref, o_ref, xp_ref):
        xp_ref[pl.ds(zero_at, CONV_PAD), :] = jnp.zeros((CONV_PAD, LANES), F32)
        xp_ref[pl.ds(T + CONV_PAD, 8), :] = jnp.zeros((8, LANES), F32)
        xp_ref[pl.ds(place, T), :] = x_ref[...]

        def step(t, carry):
            base = pl.multiple_of(t * tb, tb)
            win = xp_ref[pl.ds(base, tb + CONV_PAD + 8), :]
            acc = jnp.zeros((tb, LANES), F32)
            for shifted, at, j in _tap_windows(win, off, tb):
                acc = acc + shifted[at:at + tb, :] * w_ref[pl.ds(j, 1), :]
            o_ref[pl.ds(base, tb), :] = acc
            return carry

        lax.fori_loop(0, T // tb, step, 0)

    return pl.pallas_call(
        body, name=name, grid=(C // LANES,),
        in_specs=[pl.BlockSpec((T, LANES), lambda c: (0, c)), pl.BlockSpec((CONV_WIDTH, LANES), lambda c: (0, c))],
        out_specs=pl.BlockSpec((T, LANES), lambda c: (0, c)),
        out_shape=jax.ShapeDtypeStruct((T, C), F32),
        scratch_shapes=[pltpu.VMEM((T + CONV_PAD + 8, LANES), F32)],
        compiler_params=_params("parallel"),
    )(x, w)


def _conv_dw(name, x, dy, T):
    C = x.shape[1]
    tb = ROW_BLOCK
    off = CONV_PAD - (CONV_WIDTH - 1)

    def body(x_ref, dy_ref, o_ref, xp_ref, acc_ref):
        xp_ref[pl.ds(0, CONV_PAD), :] = jnp.zeros((CONV_PAD, LANES), F32)
        xp_ref[pl.ds(T + CONV_PAD, 8), :] = jnp.zeros((8, LANES), F32)
        xp_ref[pl.ds(CONV_PAD, T), :] = x_ref[...]
        acc_ref[...] = jnp.zeros_like(acc_ref)

        def step(t, carry):
            base = pl.multiple_of(t * tb, tb)
            win = xp_ref[pl.ds(base, tb + CONV_PAD + 8), :]
            d = dy_ref[pl.ds(base, tb), :]
            for shifted, at, j in _tap_windows(win, off, tb):
                prod = shifted[at:at + tb, :] * d
                acc_ref[j] += jnp.sum(prod.reshape(tb // 8, 8, LANES), axis=0)
            return carry

        lax.fori_loop(0, T // tb, step, 0)
        for j in range(CONV_WIDTH):
            o_ref[pl.ds(j, 1), :] = jnp.sum(acc_ref[j], axis=0, keepdims=True)

    return pl.pallas_call(
        body, name=name, grid=(C // LANES,),
        in_specs=[pl.BlockSpec((T, LANES), lambda c: (0, c)), pl.BlockSpec((T, LANES), lambda c: (0, c))],
        out_specs=pl.BlockSpec((CONV_WIDTH, LANES), lambda c: (0, c)),
        out_shape=jax.ShapeDtypeStruct((CONV_WIDTH, C), F32),
        scratch_shapes=[pltpu.VMEM((T + CONV_PAD + 8, LANES), F32), pltpu.VMEM((CONV_WIDTH, 8, LANES), F32)],
        compiler_params=_params("parallel"),
    )(x, dy)


def _dot(a, b, dims=NN):
    return lax.dot_general(a, b, dims, preferred_element_type=F32)


def _tri_cumsum(x, tri):
    return _dot(x.astype(BF16), tri)


def _qkv_split(p, D, T, Ta):
    tb = ROW_BLOCK
    nt = T // tb
    scale = 1.0 / math.sqrt(HEAD_DIM)

    def body(q_ref, k_ref, v_ref, qo_ref, ko_ref, vo_ref):
        live = pl.program_id(0) < nt
        qo_ref[...] = jnp.where(live, q_ref[...] * scale, 0.0).astype(BF16)
        ko_ref[...] = jnp.where(live, k_ref[...], 0.0).astype(BF16)
        vo_ref[...] = jnp.where(live, v_ref[...], 0.0).astype(BF16)

    def col(n):
        return lambda i: (jnp.minimum(i, nt - 1), n)

    return pl.pallas_call(
        body, name="qkv_split", grid=(Ta // tb,),
        in_specs=[pl.BlockSpec((tb, D), col(2 + n)) for n in range(3)],
        out_specs=[pl.BlockSpec((tb, D), lambda i: (i, 0))] * 3,
        out_shape=[jax.ShapeDtypeStruct((Ta, D), BF16)] * 3,
        compiler_params=_params("parallel"),
    )(p, p, p)


def _pair_lanes(g):
    return slice((g // 2) * LANES, (g // 2 + 1) * LANES)


def _stacked_pairs(x_ref, B, G):
    first = lax.broadcasted_iota(jnp.int32, (B, LANES), 1) < HEAD_DIM
    out = []
    for g in range(0, G, 2):
        x2 = x_ref[:, _pair_lanes(g)]
        zero = jnp.zeros_like(x2)
        out.append(jnp.concatenate([jnp.where(first, x2, zero), jnp.where(first, zero, x2)], axis=0))
    return first, out


def _attn_fwd(q, k, v):
    T, D = q.shape
    H = D // HEAD_DIM
    B = ATT_BLOCK
    G = ATT_HEADS
    NP = G // 2
    W = NP * LANES

    def body(q_ref, k_ref, v_ref, o_ref, rt_ref):
        i = pl.program_id(1)
        row = lax.broadcasted_iota(jnp.int32, (B, B), 0)
        col = lax.broadcasted_iota(jnp.int32, (B, B), 1)
        tri = (row >= col).astype(BF16)
        below = jnp.concatenate([col < row] * 2, axis=0)
        first, qp = _stacked_pairs(q_ref, B, G)

        def tile(j, carry, diagonal):
            sl = pl.ds(pl.multiple_of(j * B, B), B)
            zs, sps = [], []
            for p in range(NP):
                z = _dot(qp[p], k_ref[sl, _pair_lanes(2 * p)], NT)
                sp = jnp.maximum(z, 0.0) + jnp.log(1.0 + jnp.exp(-jnp.abs(z)))
                if diagonal:
                    sp = jnp.where(below, sp, 0.0)
                zs.append(z)
                sps.append(sp)
            rws = _tri_cumsum(jnp.concatenate(sps, axis=0), tri)
            out = []
            for p in range(NP):
                c, acc = carry[p]
                rw = rws[2 * B * p:2 * B * (p + 1)]
                a = jnp.exp(zs[p] - (rw + c))
                if diagonal:
                    a = jnp.where(below, a, 0.0)
                acc = acc + _dot(a.astype(BF16), v_ref[sl, _pair_lanes(2 * p)])
                out.append((c + rw[:, 0:1], acc))
            return tuple(out)

        carry = tile(i, tuple((jnp.zeros((2 * B, 1), F32), jnp.zeros((2 * B, LANES), F32)) for _ in range(NP)), True)
        carry = lax.fori_loop(0, i, lambda jj, cr: tile(i - 1 - jj, cr, False), carry)
        for p in range(NP):
            c, acc = carry[p]
            o_ref[:, _pair_lanes(2 * p)] = jnp.where(first, acc[:B], acc[B:]).astype(o_ref.dtype)
            rt_ref[2 * p] = c[:B]
            rt_ref[2 * p + 1] = c[B:]

    return pl.pallas_call(
        body, name="attn_fwd", grid=(H // G, T // B),
        in_specs=[pl.BlockSpec((B, W), lambda h, i: (i, h)),
                  pl.BlockSpec((T, W), lambda h, i: (0, h)),
                  pl.BlockSpec((T, W), lambda h, i: (0, h))],
        out_specs=[pl.BlockSpec((B, W), lambda h, i: (i, h)),
                   pl.BlockSpec((G, B, 1), lambda h, i: (h, i, 0))],
        out_shape=[jax.ShapeDtypeStruct((T, D), BF16), jax.ShapeDtypeStruct((H, T, 1), F32)],
        compiler_params=_params("parallel", "arbitrary"),
    )(q, k, v)


def _attn_bwd(q, k, v, do, rt, after):
    T, D = q.shape
    H = D // HEAD_DIM
    B = ATT_BLOCK
    nq = T // B
    scale = 1.0 / math.sqrt(HEAD_DIM)
    G = ATT_HEADS_BWD
    NP = G // 2
    W = NP * LANES

    def body(q_ref, k_ref, v_ref, do_ref, rt_ref, after_ref, dq_ref, dk_ref, dv_ref, dk_acc, dv_acc):
        i = pl.program_id(1)

        @pl.when(i == 0)
        def _():
            dk_acc[...] = jnp.zeros_like(dk_acc)
            dv_acc[...] = jnp.zeros_like(dv_acc)

        row = lax.broadcasted_iota(jnp.int32, (B, B), 0)
        col = lax.broadcasted_iota(jnp.int32, (B, B), 1)
        tri = (row <= col).astype(BF16)
        below = jnp.concatenate([col < row] * 2, axis=0)
        first, qp = _stacked_pairs(q_ref, B, G)
        _, dop = _stacked_pairs(do_ref, B, G)
        rtp = [jnp.concatenate([rt_ref[2 * p], rt_ref[2 * p + 1]], axis=0) for p in range(NP)]

        def tile(j, carry, diagonal):
            sl = pl.ds(pl.multiple_of(j * B, B), B)
            zs, sps, sgs = [], [], []
            for p in range(NP):
                z = _dot(qp[p], k_ref[sl, _pair_lanes(2 * p)], NT)
                e = jnp.exp(-jnp.abs(z))
                inv = 1.0 / (1.0 + e)
                sp = jnp.maximum(z, 0.0) - jnp.log(inv)
                if diagonal:
                    sp = jnp.where(below, sp, 0.0)
                zs.append(z)
                sps.append(sp)
                sgs.append(jnp.where(z >= 0.0, inv, e * inv))
            pws = _tri_cumsum(jnp.concatenate(sps, axis=0), tri)
            aas, gs = [], []
            for p in range(NP):
                pw = pws[2 * B * p:2 * B * (p + 1)]
                a = jnp.exp(zs[p] - (rtp[p] - carry[p][0] - pw + sps[p]))
                if diagonal:
                    a = jnp.where(below, a, 0.0)
                aas.append(a.astype(BF16))
                gs.append(a * _dot(dop[p], v_ref[sl, _pair_lanes(2 * p)], NT))
            gws = _tri_cumsum(jnp.concatenate(gs, axis=0), tri)
            out = []
            for p in range(NP):
                pc, gc, dq = carry[p]
                pw, gw = pws[2 * B * p:2 * B * (p + 1)], gws[2 * B * p:2 * B * (p + 1)]
                dz = gs[p] - sgs[p] * (gc + gw)
                if diagonal:
                    dz = jnp.where(below, dz, 0.0)
                dzb = dz.astype(BF16)
                dq = dq + _dot(dzb, k_ref[sl, _pair_lanes(2 * p)])
                dk_acc[sl, _pair_lanes(2 * p)] += _dot(dzb, qp[p], TN)
                dv_acc[sl, _pair_lanes(2 * p)] += _dot(aas[p], dop[p], TN)
                out.append((pc + pw[:, B - 1:B], gc + gw[:, B - 1:B], dq))
            return tuple(out)

        zero = jnp.zeros((2 * B, 1), F32)
        carry = lax.fori_loop(0, i, lambda j, cr: tile(j, cr, False),
                              tuple((zero, zero, jnp.zeros((2 * B, LANES), F32)) for _ in range(NP)))
        carry = tile(i, carry, True)
        for p in range(NP):
            dq = carry[p][2]
            dq_ref[:, _pair_lanes(2 * p)] = (jnp.where(first, dq[:B], dq[B:]) * scale).astype(dq_ref.dtype)

        @pl.when(i == nq - 1)
        def _():
            dk_ref[...] = dk_acc[...].astype(dk_ref.dtype)
            dv_ref[...] = dv_acc[...].astype(dv_ref.dtype)

    blk = pl.BlockSpec((B, W), lambda h, i: (i, h))
    full = pl.BlockSpec((T, W), lambda h, i: (0, h))
    return pl.pallas_call(
        body, name="attn_bwd", grid=(H // G, nq),
        in_specs=[blk, full, full, blk, pl.BlockSpec((G, B, 1), lambda h, i: (h, i, 0)), ANY],
        out_specs=[blk, full, full],
        out_shape=[jax.ShapeDtypeStruct((T, D), BF16)] * 3,
        scratch_shapes=[pltpu.VMEM((T, W), F32)] * 2,
        compiler_params=_params("parallel", "arbitrary"),
    )(q, k, v, do, rt, after)


def _loss_head(y, target, tm):
    S, D = y.shape

    def body(y_ref, t_ref, dy_ref, part_ref):
        err = y_ref[...] - t_ref[...]
        dy_ref[...] = err * (1.0 / D)

        @pl.when(pl.program_id(0) == 0)
        def _():
            part_ref[...] = jnp.zeros_like(part_ref)

        part_ref[...] += jnp.sum(err * err, axis=0, keepdims=True)

    spec = pl.BlockSpec((tm, D), lambda i: (i, 0))
    return pl.pallas_call(
        body, name="loss_head", grid=(S // tm,), in_specs=[spec, spec],
        out_specs=[spec, pl.BlockSpec((1, D), lambda i: (0, 0))],
        out_shape=[jax.ShapeDtypeStruct((S, D), F32), jax.ShapeDtypeStruct((1, D), F32)],
        compiler_params=_params("arbitrary"),
    )(y, target)


def _row_tile(R):
    for t in (256, 128, 64, 32, 16, 8):
        if R % t == 0:
            return t
    return R


def _pair_add_bf16(name, g, b1, kind, c_arr, me_arr=None):
    P, Rh, C = b1.shape
    tr = _row_tile(Rh)
    nb = Rh // tr
    own = me_arr is not None

    def body(*refs):
        g_ref, b_ref, o_ref = refs[1 + own:4 + own]
        val = (g_ref[...] + b_ref[...]).astype(o_ref.dtype)
        o_ref[...] = val
        if own:
            @pl.when(pl.program_id(1) == refs[1][0])
            def _():
                refs[-1][...] = val

    if kind == "cols":
        g_spec = pl.BlockSpec((None, tr, C), lambda i, p, c_ref, *_: (p, c_ref[0] * nb + i, 0))
    else:
        g_spec = pl.BlockSpec((tr, C), lambda i, p, c_ref, *_: ((2 * p + c_ref[0]) * nb + i, 0))
    blk = pl.BlockSpec((None, tr, C), lambda i, p, *_: (p, i, 0))
    shape = jax.ShapeDtypeStruct((P, Rh, C), BF16)
    if not own:
        return pl.pallas_call(
            body, name=name,
            grid_spec=pltpu.PrefetchScalarGridSpec(num_scalar_prefetch=1, grid=(nb, P), in_specs=[g_spec, blk], out_specs=blk),
            out_shape=shape, compiler_params=_params("parallel", "parallel"),
        )(c_arr, g, b1)
    mine = pl.BlockSpec((None, tr, C), lambda i, p, c_ref, me_ref: (me_ref[0], i, 0))
    return pl.pallas_call(
        body, name=name,
        grid_spec=pltpu.PrefetchScalarGridSpec(num_scalar_prefetch=2, grid=(nb, P), in_specs=[g_spec, blk], out_specs=[blk, mine]),
        out_shape=[shape, shape], compiler_params=_params("parallel", "arbitrary"),
    )(c_arr, me_arr, g, b1)


def _sum_slots(name, b, half_arr=None):
    P, R, C = b.shape
    tr = _row_tile(R)
    nb = R // tr

    def body(*refs):
        b_ref, o_ref = refs[-2:]
        acc = b_ref[0].astype(F32)
        for s in range(1, P):
            acc = acc + b_ref[s].astype(F32)
        o_ref[...] = acc

    if half_arr is None:
        return pl.pallas_call(
            body, name=name, grid=(nb,),
            in_specs=[pl.BlockSpec((P, tr, C), lambda i: (0, i, 0))],
            out_specs=pl.BlockSpec((tr, C), lambda i: (i, 0)),
            out_shape=jax.ShapeDtypeStruct((R, C), F32),
            compiler_params=_params("parallel"),
        )(b)
    return pl.pallas_call(
        body, name=name,
        grid_spec=pltpu.PrefetchScalarGridSpec(
            num_scalar_prefetch=1, grid=(nb,),
            in_specs=[pl.BlockSpec((P, tr, C), lambda i, half: (0, i, 0))],
            out_specs=pl.BlockSpec((tr, C), lambda i, half: (half[0] * nb + i, 0))),
        out_shape=jax.ShapeDtypeStruct((2 * R, C), F32),
        compiler_params=_params("parallel"),
    )(half_arr, b)


def _adamw(name, w, g, m, v):
    R, C = w.shape
    tr = _row_tile(R)
    c1 = 1.0 - ADAM_B1 ** ADAM_STEP
    c2 = 1.0 - ADAM_B2 ** ADAM_STEP

    def body(w_ref, g_ref, m_ref, v_ref, d_ref, nm_ref, nv_ref):
        gg = g_ref[...]
        nm = ADAM_B1 * m_ref[...] + (1.0 - ADAM_B1) * gg
        nv = ADAM_B2 * v_ref[...] + (1.0 - ADAM_B2) * (gg * gg)
        m_hat = nm / c1
        v_hat = nv / c2
        d_ref[...] = -ADAM_LR * (m_hat / (jnp.sqrt(v_hat) + ADAM_EPS) + ADAM_WD * w_ref[...])
        nm_ref[...] = nm
        nv_ref[...] = nv

    spec = pl.BlockSpec((tr, C), lambda i: (i, 0))
    return pl.pallas_call(
        body, name=name, grid=(R // tr,), in_specs=[spec] * 4, out_specs=[spec] * 3,
        out_shape=[jax.ShapeDtypeStruct((R, C), F32)] * 3,
        compiler_params=_params("parallel"),
    )(w, g, m, v)


def _place():
    x, y, c = lax.axis_index("x"), lax.axis_index("y"), lax.axis_index("c")
    other_chips = [(1 - x, y), (x, 1 - y), (1 - x, 1 - y)]
    return x, y, c, other_chips


def _into_slot(name, w, dtype, slot_arr, n_slots):
    R, C = w.shape
    tr = _row_tile(R)

    def body(slot_ref, w_ref, o_ref):
        o_ref[...] = w_ref[...].astype(o_ref.dtype)

    return pl.pallas_call(
        body, name=name,
        grid_spec=pltpu.PrefetchScalarGridSpec(
            num_scalar_prefetch=1, grid=(R // tr,),
            in_specs=[pl.BlockSpec((tr, C), lambda i, slot: (i, 0))],
            out_specs=pl.BlockSpec((None, tr, C), lambda i, slot: (slot[0], i, 0))),
        out_shape=jax.ShapeDtypeStruct((n_slots, R, C), dtype),
        compiler_params=_params("parallel"),
    )(slot_arr, w)


def _gather_chips(bufs):
    n = len(bufs)

    def body(*refs):
        outs = refs[n:2 * n]
        ici_send, ici_recv, d2d_send, d2d_recv = refs[2 * n:]
        x, y, c, chips = _place()
        me = 2 * x + y
        started = []
        for k in range(n):
            rh = outs[k].shape[1] // 2
            mine = outs[k].at[me, pl.ds(c * rh, rh)]
            for j, (px, py) in enumerate(chips):
                cp = pltpu.make_async_remote_copy(
                    src_ref=mine, dst_ref=mine,
                    send_sem=ici_send.at[3 * k + j], recv_sem=ici_recv.at[3 * k + j],
                    device_id=(px, py, c), device_id_type=MESH)
                cp.start()
                started.append(cp)
        for k in range(n):
            rh = outs[k].shape[1] // 2
            for j, (px, py) in enumerate(chips):
                landed = outs[k].at[2 * px + py, pl.ds(c * rh, rh)]
                pltpu.make_async_remote_copy(
                    src_ref=landed, dst_ref=landed,
                    send_sem=ici_send.at[3 * k + j], recv_sem=ici_recv.at[3 * k + j],
                    device_id=(px, py, c), device_id_type=MESH).wait_recv()
                cp = pltpu.make_async_remote_copy(
                    src_ref=landed, dst_ref=landed,
                    send_sem=d2d_send.at[3 * k + j], recv_sem=d2d_recv.at[3 * k + j],
                    device_id=(x, y, 1 - c), device_id_type=MESH)
                cp.start()
                started.append(cp)
        for k in range(n):
            rh = outs[k].shape[1] // 2
            for j, (px, py) in enumerate(chips):
                landed = outs[k].at[2 * px + py, pl.ds((1 - c) * rh, rh)]
                pltpu.make_async_remote_copy(
                    src_ref=landed, dst_ref=landed,
                    send_sem=d2d_send.at[3 * k + j], recv_sem=d2d_recv.at[3 * k + j],
                    device_id=(x, y, 1 - c), device_id_type=MESH).wait_recv()
        for cp in started:
            cp.wait_send()

    return pl.pallas_call(
        body, name="gather_weights",
        in_specs=[ANY] * n, out_specs=[ANY] * n,
        out_shape=[jax.ShapeDtypeStruct(b.shape, b.dtype) for b in bufs],
        input_output_aliases={k: k for k in range(n)},
        scratch_shapes=[pltpu.SemaphoreType.DMA((3 * n,))] * 4,
        compiler_params=pltpu.CompilerParams(has_side_effects=True),
    )(*bufs)


def _gather_forward(bufs):
    n = len(bufs)

    def body(*refs):
        outs = refs[n:2 * n]
        d2d_send, d2d_recv = refs[2 * n:]
        x, y, c, chips = _place()
        started = []
        for k in range(n):
            rh = outs[k].shape[1] // 2
            for j, (px, py) in enumerate(chips):
                landed = outs[k].at[2 * px + py, pl.ds(c * rh, rh)]
                cp = pltpu.make_async_remote_copy(
                    src_ref=landed, dst_ref=landed,
                    send_sem=d2d_send.at[3 * k + j], recv_sem=d2d_recv.at[3 * k + j],
                    device_id=(x, y, 1 - c), device_id_type=MESH)
                cp.start()
                started.append(cp)
        for k in range(n):
            rh = outs[k].shape[1] // 2
            for j, (px, py) in enumerate(chips):
                landed = outs[k].at[2 * px + py, pl.ds((1 - c) * rh, rh)]
                pltpu.make_async_remote_copy(
                    src_ref=landed, dst_ref=landed,
                    send_sem=d2d_send.at[3 * k + j], recv_sem=d2d_recv.at[3 * k + j],
                    device_id=(x, y, 1 - c), device_id_type=MESH).wait_recv()
        for cp in started:
            cp.wait_send()

    return pl.pallas_call(
        body, name="gather_rest_forward",
        in_specs=[ANY] * n, out_specs=[ANY] * n,
        out_shape=[jax.ShapeDtypeStruct(b.shape, b.dtype) for b in bufs],
        input_output_aliases={k: k for k in range(n)},
        scratch_shapes=[pltpu.SemaphoreType.DMA((3 * n,))] * 2,
        compiler_params=pltpu.CompilerParams(has_side_effects=True),
    )(*bufs)


def _to_chips_start(name, arrays, n, src_fn, dst_fn, after):
    m = len(arrays)

    def body(*refs):
        send_sem, recv_sem = refs[m + 1], refs[m + 2]
        thru = refs[m + 3:2 * m + 3]
        token = refs[2 * m + 3]
        x, y, c, chips = _place()
        for k in range(n):
            for j, (px, py) in enumerate(chips):
                pltpu.make_async_remote_copy(
                    src_ref=src_fn(thru, k, px, py, x, y, c), dst_ref=dst_fn(thru, k, px, py, x, y, c),
                    send_sem=send_sem.at[3 * k + j], recv_sem=recv_sem.at[3 * k + j],
                    device_id=(px, py, c), device_id_type=MESH).start()
        token[...] = jnp.zeros_like(token)

    res = pl.pallas_call(
        body, name=name,
        out_shape=(pltpu.SemaphoreType.DMA((3 * n,)), pltpu.SemaphoreType.DMA((3 * n,)),
                   *[pltpu.HBM(a.shape, a.dtype) for a in arrays], jax.ShapeDtypeStruct((8, LANES), F32)),
        in_specs=[HBM] * m + [ANY],
        out_specs=(SEM, SEM, *[HBM] * m, pl.BlockSpec(memory_space=pltpu.VMEM)),
        input_output_aliases={i: i + 2 for i in range(m)},
        compiler_params=pltpu.CompilerParams(has_side_effects=EFFECT),
    )(*[pltpu.with_memory_space_constraint(a, pltpu.HBM) for a in arrays], after)
    return res[0], res[1], list(res[2:2 + m]), res[2 + m]


def _to_chips_wait(name, send_sem, recv_sem, arrays, n, src_fn, land_fn, after):
    m = len(arrays)
    after = list(after) if isinstance(after, (list, tuple)) else [after]

    def body(*refs):
        send, recv = refs[m], refs[m + 1]
        outs = refs[m + 2 + len(after):]
        x, y, c, chips = _place()
        for k in range(n):
            for j, (px, py) in enumerate(chips):
                cp = pltpu.make_async_remote_copy(
                    src_ref=src_fn(outs, k, px, py, x, y, c), dst_ref=land_fn(outs, k, px, py, x, y, c),
                    send_sem=send.at[3 * k + j], recv_sem=recv.at[3 * k + j],
                    device_id=(px, py, c), device_id_type=MESH)
                cp.wait_send()
                cp.wait_recv()

    res = pl.pallas_call(
        body, name=name,
        out_shape=[pltpu.HBM(a.shape, a.dtype) for a in arrays],
        in_specs=[HBM] * m + [SEM, SEM] + [ANY] * len(after), out_specs=[HBM] * m,
        input_output_aliases={i: i for i in range(m)},
        compiler_params=pltpu.CompilerParams(has_side_effects=EFFECT),
    )(*arrays, send_sem, recv_sem, *after)
    return list(res)


def _slot_half(refs, k, chip, c):
    rh = refs[k].shape[1] // 2
    return refs[k].at[chip, pl.ds(c * rh, rh)]


def _ag_mine(refs, k, px, py, x, y, c):
    return _slot_half(refs, k, 2 * x + y, c)


def _ag_theirs(refs, k, px, py, x, y, c):
    return _slot_half(refs, k, 2 * px + py, c)


def _rs_ends(n):
    def src(refs, k, px, py, x, y, c):
        return refs[k].at[2 * px + py]

    def dst(refs, k, px, py, x, y, c):
        return refs[n + k].at[2 * x + y]

    def land(refs, k, px, py, x, y, c):
        return refs[n + k].at[2 * px + py]

    return src, dst, land


def _half(ref, kind, p, c, rh):
    if kind == "cols":
        return ref.at[p, pl.ds(c * rh, rh)]
    return ref.at[pl.ds((2 * p + c) * rh, rh)]


def _swap_halves(name, grads, kinds, rhs):
    n = len(grads)

    def body(*refs):
        ins, outs = refs[:n], refs[n:2 * n]
        send_sem, recv_sem = refs[2 * n:]
        x, y, c, _ = _place()
        started = []
        for k in range(n):
            for p in range(N_CHIPS):
                cp = pltpu.make_async_remote_copy(
                    src_ref=_half(ins[k], kinds[k], p, 1 - c, rhs[k]), dst_ref=outs[k].at[p],
                    send_sem=send_sem.at[N_CHIPS * k + p], recv_sem=recv_sem.at[N_CHIPS * k + p],
                    device_id=(x, y, 1 - c), device_id_type=MESH)
                cp.start()
                started.append(cp)
        for cp in started:
            cp.wait()

    out_shape = []
    for g, kind, rh in zip(grads, kinds, rhs):
        out_shape.append(jax.ShapeDtypeStruct((N_CHIPS, rh, g.shape[-1]), g.dtype))
    return pl.pallas_call(
        body, name=name,
        in_specs=[ANY] * n, out_specs=[ANY] * n, out_shape=out_shape,
        scratch_shapes=[pltpu.SemaphoreType.DMA((N_CHIPS * n,))] * 2,
        compiler_params=pltpu.CompilerParams(has_side_effects=True),
    )(*grads)


def _join_halves(name, fulls):
    n = len(fulls)

    def body(*refs):
        outs = refs[n:2 * n]
        send_sem, recv_sem = refs[2 * n:]
        x, y, c, _ = _place()
        started = []
        for k in range(n):
            rh = outs[k].shape[0] // 2
            mine = outs[k].at[pl.ds(c * rh, rh)]
            cp = pltpu.make_async_remote_copy(
                src_ref=mine, dst_ref=mine, send_sem=send_sem.at[k], recv_sem=recv_sem.at[k],
                device_id=(x, y, 1 - c), device_id_type=MESH)
            cp.start()
            started.append(cp)
        for k in range(n):
            rh = outs[k].shape[0] // 2
            theirs = outs[k].at[pl.ds((1 - c) * rh, rh)]
            pltpu.make_async_remote_copy(
                src_ref=theirs, dst_ref=theirs, send_sem=send_sem.at[k], recv_sem=recv_sem.at[k],
                device_id=(x, y, 1 - c), device_id_type=MESH).wait_recv()
        for cp in started:
            cp.wait_send()

    return pl.pallas_call(
        body, name=name,
        in_specs=[ANY] * n, out_specs=[ANY] * n,
        out_shape=[jax.ShapeDtypeStruct(f.shape, f.dtype) for f in fulls],
        input_output_aliases={k: k for k in range(n)},
        scratch_shapes=[pltpu.SemaphoreType.DMA((n,))] * 2,
        compiler_params=pltpu.CompilerParams(has_side_effects=True),
    )(*fulls)


def _gather_all(block):
    def body(in_ref, out_ref, send_sem, recv_sem, local_sem):
        x, y, c, _ = _place()

        def slot(px, py, pc):
            return out_ref.at[4 * px + 2 * py + pc]

        loc = pltpu.make_async_copy(in_ref, slot(x, y, c), local_sem)
        loc.start()
        started = []
        for d in range(1, N_DEV):
            fx, fy, fc = d >> 2, (d >> 1) & 1, d & 1
            cp = pltpu.make_async_remote_copy(
                src_ref=in_ref, dst_ref=slot(x, y, c), send_sem=send_sem.at[d - 1], recv_sem=recv_sem.at[d - 1],
                device_id=(x ^ fx, y ^ fy, c ^ fc), device_id_type=MESH)
            cp.start()
            started.append(cp)
        for d in range(1, N_DEV):
            fx, fy, fc = d >> 2, (d >> 1) & 1, d & 1
            landed = slot(x ^ fx, y ^ fy, c ^ fc)
            pltpu.make_async_remote_copy(
                src_ref=in_ref, dst_ref=landed, send_sem=send_sem.at[d - 1], recv_sem=recv_sem.at[d - 1],
                device_id=(x ^ fx, y ^ fy, c ^ fc), device_id_type=MESH).wait_recv()
        for cp in started:
            cp.wait_send()
        loc.wait()

    return pl.pallas_call(
        body, name="gather_small_grads",
        in_specs=[ANY], out_specs=ANY,
        out_shape=jax.ShapeDtypeStruct((N_DEV,) + block.shape, block.dtype),
        scratch_shapes=[pltpu.SemaphoreType.DMA((N_DEV - 1,))] * 2 + [pltpu.SemaphoreType.DMA(())],
        compiler_params=pltpu.CompilerParams(has_side_effects=True),
    )(block)


def _pack(pieces):
    flat = jnp.concatenate([p.reshape(-1) for p in pieces])
    n = flat.shape[0]
    padded = -(-n // (8 * LANES)) * (8 * LANES)
    return jnp.pad(flat, (0, padded - n)).reshape(-1, LANES)


def _unpack(packed, shapes):
    flat = packed.reshape(-1)
    out, at = [], 0
    for s in shapes:
        n = math.prod(s)
        out.append(flat[at:at + n].reshape(s))
        at += n
    return out


def kernel(x, meta_tokens, pre_mix_g, w_in, gate_b, dw_w, dw_b, conv_ln_g, conv_ln_b, w_conv_out, w_attn_out, w_o, post_mix_g, pre_ffn_g, w_ffn_in, w_ffn_out, post_ffn_g, loss_target, m_meta_tokens, m_pre_mix_g, m_w_in, m_gate_b, m_dw_w, m_dw_b, m_conv_ln_g, m_conv_ln_b, m_w_conv_out, m_w_attn_out, m_w_o, m_post_mix_g, m_pre_ffn_g, m_w_ffn_in, m_w_ffn_out, m_post_ffn_g, v_meta_tokens, v_pre_mix_g, v_w_in, v_gate_b, v_dw_w, v_dw_b, v_conv_ln_g, v_conv_ln_b, v_w_conv_out, v_w_attn_out, v_w_o, v_post_mix_g, v_pre_ffn_g, v_w_ffn_in, v_w_ffn_out, v_post_ffn_g):
    S, D = x.shape[1], x.shape[2]
    L = S + N_META
    T = -(-L // ROW_BLOCK) * ROW_BLOCK
    Ta = -(-L // ATT_BLOCK) * ATT_BLOCK
    tm = _tile(T, MM_ROWS)
    tc = _tile(T, CONTRACT_ROWS)
    ts = _tile(T, STAGE_ROWS)
    tw = _tile(T, WIDE_STAGE_ROWS)
    H = D // HEAD_DIM
    F = w_ffn_out.shape[1] * N_CHIPS
    Dc = D // N_CHIPS
    P = N_CHIPS
    me = 2 * lax.axis_index("x") + lax.axis_index("y")
    c_arr = lax.axis_index("c").astype(jnp.int32).reshape(1)

    dw_w_pad = jnp.pad(dw_w[0], ((0, CONV_PAD - CONV_WIDTH), (0, 0)))
    me_arr = me.astype(jnp.int32).reshape(1)
    to_gather = [("w_in", w_in[0], BF16), ("w_conv_out", w_conv_out[0], BF16), ("w_attn_out", w_attn_out[0], BF16),
                 ("w_o", w_o[0], BF16), ("w_ffn_in", w_ffn_in[0], BF16), ("w_ffn_out", w_ffn_out[0], BF16),
                 ("meta", meta_tokens, F32), ("taps", dw_w_pad, F32)]
    slot = {n: _into_slot("slot_" + n, w, dt, me_arr, P) for n, w, dt in to_gather}
    win3, meta4, taps4 = _gather_chips([slot["w_in"], slot["meta"], slot["taps"]])
    meta_full = meta4.transpose(1, 0, 2).reshape(N_META, D)
    taps = taps4.transpose(1, 0, 2).reshape(CONV_PAD, D)[:CONV_WIDTH]
    later = ["w_conv_out", "w_attn_out", "w_o", "w_ffn_in", "w_ffn_out"]
    ag_send, ag_recv, in_flight, ag_token = _to_chips_start(
        "gather_rest_start", [slot[n] for n in later], len(later), _ag_mine, _ag_mine, meta4)

    h0 = jnp.concatenate([meta_full, x[0], jnp.zeros((T - L, D), F32)], axis=0)
    (u1,) = _rowwise_fwd("rms_pre_mix", f_rms, [(h0, D, 0)], [(pre_mix_g + ag_token[0:1, 0:1], D, 0)], [(D, BF16)], T, ts)
    p = _mm_nn_cols("mm_in", u1, win3, tm)
    q, k, v = _qkv_split(p, D, T, Ta)
    o2, rtot = _attn_fwd(q, k, v)
    landed = _to_chips_wait("gather_rest_wait", ag_send, ag_recv, in_flight, len(later), _ag_mine, _ag_theirs, o2)
    wco4, wao4, wo4, wfi3, wfo4 = _gather_forward(landed)
    wco, wao, wo = (w.reshape(D, D) for w in (wco4, wao4, wo4))
    wfo = wfo4.reshape(F, D)
    (uglu,) = _rowwise_fwd("glu", f_glu, [(p, D, 0), (p, D, 1)], [], [(D, F32)], T, ts)
    yc = _shift_conv("dwconv", uglu, taps, CONV_PAD, CONV_PAD - (CONV_WIDTH - 1), T)
    conv_pars = [(dw_b, D, 0), (conv_ln_g, D, 0), (conv_ln_b, D, 0)]
    (ys,) = _rowwise_fwd("conv_post", f_convpost, [(yc, D, 0)], conv_pars, [(D, BF16)], T, ts)
    y_conv = _mm_nn("mm_conv_out", ys, wco, tm)
    y_attn = _mm_nn("mm_attn_out", o2, wao, tm, rows=T)
    mix_rows = [(p, D, 5), (p, D, 6), (y_conv, D, 0), (y_attn, D, 0)]
    mix_pars = [(gate_b, D, 0), (gate_b, D, 1)]
    (mixin,) = _rowwise_fwd("gate_mix", f_mix, mix_rows, mix_pars, [(D, BF16)], T, ts)
    mix = _mm_nn("mm_o", mixin, wo, tm)
    (h1,) = _rowwise_fwd("res_post_mix", f_res_rms, [(h0, D, 0), (mix, D, 0)], [(post_mix_g, D, 0)], [(D, F32)], T, ts)
    (u2,) = _rowwise_fwd("rms_pre_ffn", f_rms, [(h1, D, 0)], [(pre_ffn_g, D, 0)], [(D, BF16)], T, ts)
    ab = _mm_nn_cols("mm_ffn_in", u2, wfi3, tm)
    (fin,) = _rowwise_fwd("swiglu", f_swiglu, [(ab, F, 0), (ab, F, 1)], [], [(F, BF16)], T, tw)
    f = _mm_nn("mm_ffn_out", fin, wfo, tm)
    (h2,) = _rowwise_fwd("res_post_ffn", f_res_rms, [(h1, D, 0), (f, D, 0)], [(post_ffn_g, D, 0)], [(D, F32)], T, ts)

    dy, part = _loss_head(h2[N_META:L], loss_target[0], _row_tile(S))
    loss = lax.psum(0.5 * jnp.sum(part) / D, ("x", "y", "c"))
    dh2 = jnp.pad(dy, ((N_META, T - L), (0, 0)))

    (df,), (g_post_ffn,) = _rowwise_bwd("res_post_ffn_bwd", f_res_rms, [(h1, D, 0), (f, D, 0)], [(post_ffn_g, D, 0)],
                                        [(dh2, D, 0)], [None, BF16], T, ts)
    dfin = _mm_nt("mm_ffn_out_dx", df, wfo, tm)
    g_wfo = _mm_tn("mm_ffn_out_dw", fin, df, tc, F // MXU_WIDTH)
    (dab,), _ = _rowwise_bwd("swiglu_bwd", f_swiglu, [(ab, F, 0), (ab, F, 1)], [], [(dfin, F, 0)], [BF16, BF16], T, tw,
                             joined=True)
    du2 = _mm_nt_cols("mm_ffn_in_dx", dab, wfi3, tm)
    g_wfi = _mm_tn_cols("mm_ffn_in_dw", u2, dab, tc, P)
    (dh1,), (g_pre_ffn,) = _rowwise_bwd("rms_pre_ffn_bwd", f_rms_id, [(h1, D, 0)], [(pre_ffn_g, D, 0)],
                                        [(du2, D, 0), (dh2, D, 0)], [F32], T, ts)
    (dmix,), (g_post_mix,) = _rowwise_bwd("res_post_mix_bwd", f_res_rms, [(h0, D, 0), (mix, D, 0)], [(post_mix_g, D, 0)],
                                          [(dh1, D, 0)], [None, BF16], T, ts)
    dmixin = _mm_nt("mm_o_dx", dmix, wo, tm)
    g_wo = _mm_tn("mm_o_dw", mixin, dmix, tc, D // MXU_WIDTH)
    (dpc, dpa, dyconv, dyattn), (g_gate_c, g_gate_a) = _rowwise_bwd(
        "gate_mix_bwd", f_mix, mix_rows, mix_pars, [(dmixin, D, 0)], [BF16, BF16, BF16, BF16], T, ts)
    g_wco = _mm_tn("mm_conv_out_dw", ys, dyconv, tc, D // MXU_WIDTH)
    dys = _mm_nt("mm_conv_out_dx", dyconv, wco, tm)
    g_wao = _mm_tn("mm_attn_out_dw", o2, dyattn, tc, D // MXU_WIDTH)
    do2 = _mm_nt("mm_attn_out_dx", dyattn, wao, tm, BF16, out_rows=Ta)

    early = [g_wco, g_wao, g_wo, g_wfi, g_wfo]
    early_kinds = ["rows", "rows", "rows", "cols", "rows"]
    early_rhs = [(g.shape[1] if kind == "cols" else g.shape[0] // P) // 2 for g, kind in zip(early, early_kinds)]
    early_sib = _swap_halves("grad_swap_halves_early", early, early_kinds, early_rhs)
    early_pairs = [_pair_add_bf16("grad_pair_add_%d" % (n + 1), g, b1, kind, c_arr, me_arr)
                   for n, (g, b1, kind) in enumerate(zip(early, early_sib, early_kinds))]
    rs_src, rs_dst, rs_land = _rs_ends(len(early))
    rs_send, rs_recv, rs_flight, rs_token = _to_chips_start(
        "grad_scatter_start", [pr[0] for pr in early_pairs] + [pr[1] for pr in early_pairs], len(early),
        rs_src, rs_dst, early_pairs[-1][1])
    (dyc,), (g_dw_b, g_ln_g, g_ln_b) = _rowwise_bwd("conv_post_bwd", f_convpost, [(yc, D, 0)], conv_pars,
                                                    [(dys, D, 0)], [F32], T, ts)
    duglu = _shift_conv("dwconv_dx", dyc, taps[::-1], 0, 0, T)
    g_taps = _conv_dw("dwconv_dw", uglu, dyc, T)
    (dp01,), _ = _rowwise_bwd("glu_bwd", f_glu, [(p, D, 0), (p, D, 1)], [], [(duglu, D, 0)], [BF16, BF16], T, ts,
                              joined=True)
    dq, dk, dv = _attn_bwd(q, k, v, do2, rtot, rs_token)
    rs_done = _to_chips_wait("grad_scatter_wait", rs_send, rs_recv, rs_flight, len(early), rs_src, rs_land, dq)
    early_slots = rs_done[len(early):]
    dp = jnp.concatenate([dp01, dq[:T], dk[:T], dv[:T], dpc, dpa], axis=1)
    du1 = _mm_nt_cols("mm_in_dx", dp, win3, tm)
    g_win = _mm_tn_cols("mm_in_dw", u1, dp, tc, P)
    (win_sib,) = _swap_halves("grad_swap_halves_in", [g_win], ["cols"], [g_win.shape[1] // 2])
    win_pair = _pair_add_bf16("grad_pair_add_0", g_win, win_sib, "cols", c_arr, me_arr)
    in_src, in_dst, in_land = _rs_ends(1)
    in_send, in_recv, in_flight, in_token = _to_chips_start(
        "grad_scatter_in_start", list(win_pair), 1, in_src, in_dst, win_pair[1])
    (dh0,), (g_pre_mix,) = _rowwise_bwd("rms_pre_mix_bwd", f_rms_id, [(h0, D, 0)],
                                        [(pre_mix_g + in_token[0:1, 0:1], D, 0)],
                                        [(du1, D, 0), (dh1, D, 0)], [F32], T, ts)
    grad_x = dh0[N_META:L][None]
    g_early = _join_halves("grad_join_halves_early", [_sum_slots("grad_chip_sum_%d" % (n + 1), s, c_arr)
                                                      for n, s in enumerate(early_slots)])

    small_shapes = [(1, D), (1, D), (1, D), (CONV_WIDTH, D), (1, D), (1, D), (1, D), (1, D), (1, D), (1, D), (N_META, D)]
    small = _pack([g_pre_mix, g_gate_c, g_gate_a, g_taps, g_dw_b, g_ln_g, g_ln_b, g_post_mix, g_pre_ffn, g_post_ffn,
                   dh0[:N_META]])
    summed = _sum_slots("small_grad_sum", _gather_all(small))
    (s_pre_mix, s_gate_c, s_gate_a, s_taps, s_dw_b, s_ln_g, s_ln_b, s_post_mix, s_pre_ffn, s_post_ffn,
     s_meta) = _unpack(summed, small_shapes)
    s_gate_b = jnp.concatenate([s_gate_c, s_gate_a], axis=1)
    s_taps = lax.dynamic_slice_in_dim(s_taps, me * Dc, Dc, axis=1)[None]
    s_meta = lax.dynamic_slice_in_dim(s_meta, me * Dc, Dc, axis=1)

    grads = {
        "meta_tokens": s_meta, "pre_mix_g": s_pre_mix, "gate_b": s_gate_b, "dw_w": s_taps,
        "dw_b": s_dw_b, "conv_ln_g": s_ln_g, "conv_ln_b": s_ln_b, "w_conv_out": g_early[0][None],
        "w_attn_out": g_early[1][None], "w_o": g_early[2][None], "post_mix_g": s_post_mix, "pre_ffn_g": s_pre_ffn,
        "w_ffn_in": g_early[3][None], "w_ffn_out": g_early[4][None], "post_ffn_g": s_post_ffn,
    }
    weights = {
        "meta_tokens": (meta_tokens, m_meta_tokens, v_meta_tokens), "pre_mix_g": (pre_mix_g, m_pre_mix_g, v_pre_mix_g),
        "w_in": (w_in, m_w_in, v_w_in), "gate_b": (gate_b, m_gate_b, v_gate_b), "dw_w": (dw_w, m_dw_w, v_dw_w),
        "dw_b": (dw_b, m_dw_b, v_dw_b), "conv_ln_g": (conv_ln_g, m_conv_ln_g, v_conv_ln_g),
        "conv_ln_b": (conv_ln_b, m_conv_ln_b, v_conv_ln_b), "w_conv_out": (w_conv_out, m_w_conv_out, v_w_conv_out),
        "w_attn_out": (w_attn_out, m_w_attn_out, v_w_attn_out), "w_o": (w_o, m_w_o, v_w_o),
        "post_mix_g": (post_mix_g, m_post_mix_g, v_post_mix_g), "pre_ffn_g": (pre_ffn_g, m_pre_ffn_g, v_pre_ffn_g),
        "w_ffn_in": (w_ffn_in, m_w_ffn_in, v_w_ffn_in), "w_ffn_out": (w_ffn_out, m_w_ffn_out, v_w_ffn_out),
        "post_ffn_g": (post_ffn_g, m_post_ffn_g, v_post_ffn_g),
    }
    names = list(weights)
    big_names = ["w_in", "w_conv_out", "w_attn_out", "w_o", "w_ffn_in", "w_ffn_out"]
    small_names = [n for n in names if n not in big_names]

    delta, new_m, new_v = {}, {}, {}

    def big_update(n):
        w, m, v2 = weights[n]
        d, nm, nv = _adamw("adamw_" + n, w[0], grads[n][0], m[0], v2[0])
        delta[n], new_m[n], new_v[n] = d[None], nm[None], nv[None]

    for n in big_names[1:]:
        big_update(n)
    shapes = [weights[n][0].shape for n in small_names]
    packed = [_pack([weights[n][k] for n in small_names]) for k in range(3)]
    d, nm, nv = _adamw("adamw_small", packed[0], _pack([grads[n] for n in small_names]), packed[1], packed[2])
    for n, dd, mm, vv in zip(small_names, _unpack(d, shapes), _unpack(nm, shapes), _unpack(nv, shapes)):
        delta[n], new_m[n], new_v[n] = dd, mm, vv

    in_done = _to_chips_wait("grad_scatter_in_wait", in_send, in_recv, in_flight, 1, in_src, in_land,
                             [d] + [delta[n] for n in big_names[1:]])
    (g_in,) = _join_halves("grad_join_halves_in", [_sum_slots("grad_chip_sum_0", in_done[1], c_arr)])
    grads["w_in"] = g_in[None]
    big_update("w_in")

    return (loss, grad_x, *[grads[n].reshape(weights[n][0].shape) for n in names], *[delta[n] for n in names],
            *[new_m[n] for n in names], *[new_v[n] for n in names])
```

```python
import math

import jax
import jax.numpy as jnp
from jax import lax
from jax.experimental import pallas as pl
from jax.experimental.pallas import tpu as pltpu

F32 = jnp.float32
BF16 = jnp.bfloat16

N_META = 16
CONV_WIDTH = 31
CONV_PAD = 32
HEAD_DIM = 64
RMS_EPS = 1e-6
LN_EPS = 1e-5
ROW_BLOCK = 128
MXU_WIDTH = 256
ATT_BLOCK = MXU_WIDTH
ATT_HEADS = 8
ATT_HEADS_BWD = 4
LANES = 128
N_CHIPS = 4
N_DEV = 8
MM_ROWS = 544
CONTRACT_ROWS = 2176
STAGE_ROWS = 272
WIDE_STAGE_ROWS = 128
VMEM_LIMIT = 56 * 1024 * 1024

ADAM_LR = 0.001
ADAM_B1 = 0.9
ADAM_B2 = 0.999
ADAM_EPS = 1e-08
ADAM_WD = 0.01
ADAM_STEP = 10

MESH = pl.DeviceIdType.MESH
ANY = pl.BlockSpec(memory_space=pl.ANY)
HBM = pl.BlockSpec(memory_space=pltpu.HBM)
SEM = pl.BlockSpec(memory_space=pltpu.SEMAPHORE)
EFFECT = pltpu.SideEffectType.DATAFLOW_SIDE_EFFECTING


def _params(*sem):
    return pltpu.CompilerParams(dimension_semantics=sem if sem else None, vmem_limit_bytes=VMEM_LIMIT)


def _rms(x, g):
    return x * lax.rsqrt(jnp.mean(x * x, axis=-1, keepdims=True) + RMS_EPS) * g


def f_rms(h, g):
    return (_rms(h, g),)


def f_rms_id(h, g):
    return (_rms(h, g), h)


def f_res_rms(h, m, g):
    return (h + _rms(m, g),)


def f_glu(a, gate):
    return (a * lax.logistic(gate),)


def f_convpost(yc, b, ln_g, ln_b):
    y = yc + b
    mu = jnp.mean(y, axis=-1, keepdims=True)
    xc = y - mu
    var = jnp.mean(xc * xc, axis=-1, keepdims=True)
    yl = xc * lax.rsqrt(var + LN_EPS) * ln_g + ln_b
    return (yl * lax.logistic(yl),)


def f_mix(pc, pa, yc, ya, bc, ba):
    return (lax.logistic(pc + bc) * yc + lax.logistic(pa + ba) * ya,)


def f_swiglu(a, b):
    return (a * lax.logistic(a) * b,)


def _tile(T, target):
    return max(t for t in range(16, target + 1, 16) if T % t == 0)


def _row_map(j):
    return lambda i: (i, j)


def _par_map(j):
    return lambda i: (0, j)


def _rowwise_fwd(name, f, rows, pars, outs, T, tm):
    n_in = len(rows) + len(pars)

    def body(*refs):
        vals = [r[...].astype(F32) for r in refs[:n_in]]
        res = f(*vals)
        for o_ref, o in zip(refs[n_in:], res):
            o_ref[...] = o.astype(o_ref.dtype)

    in_specs = [pl.BlockSpec((tm, w), _row_map(j)) for _, w, j in rows]
    in_specs += [pl.BlockSpec((1, w), _par_map(j)) for _, w, j in pars]
    return pl.pallas_call(
        body, name=name, grid=(T // tm,),
        in_specs=in_specs,
        out_specs=[pl.BlockSpec((tm, w), _row_map(0)) for w, _ in outs],
        out_shape=[jax.ShapeDtypeStruct((T, w), dt) for w, dt in outs],
        compiler_params=_params("parallel"),
    )(*[a for a, _, _ in rows], *[a for a, _, _ in pars])


def _rowwise_bwd(name, f, rows, pars, cots, drow_dtypes, T, tm, joined=False):
    n_r, n_p, n_c = len(rows), len(pars), len(cots)
    n_in = n_r + n_p + n_c
    keep = [k for k, dt in enumerate(drow_dtypes) if dt is not None]
    n_out = 1 if joined else len(keep)

    def body(*refs):
        rv = [r[...].astype(F32) for r in refs[:n_r]]
        pv = [r[...].astype(F32) for r in refs[n_r:n_r + n_p]]
        cv = [r[...].astype(F32) for r in refs[n_r + n_p:n_in]]
        _, vjp = jax.vjp(f, *rv, *pv)
        g = vjp(tuple(cv))
        drow_refs = refs[n_in:n_in + n_out]
        dpar_refs = refs[n_in + n_out:]
        if joined:
            at = 0
            for k in keep:
                drow_refs[0][:, at:at + rows[k][1]] = g[k].astype(drow_refs[0].dtype)
                at += rows[k][1]
        else:
            for r, k in zip(drow_refs, keep):
                r[...] = g[k].astype(r.dtype)

        @pl.when(pl.program_id(0) == 0)
        def _():
            for r in dpar_refs:
                r[...] = jnp.zeros_like(r)

        for r, gp in zip(dpar_refs, g[n_r:]):
            r[...] += gp

    in_specs = [pl.BlockSpec((tm, w), _row_map(j)) for _, w, j in rows]
    in_specs += [pl.BlockSpec((1, w), _par_map(j)) for _, w, j in pars]
    in_specs += [pl.BlockSpec((tm, w), _row_map(j)) for _, w, j in cots]
    widths = [sum(rows[k][1] for k in keep)] if joined else [rows[k][1] for k in keep]
    out_specs = [pl.BlockSpec((tm, w), _row_map(0)) for w in widths]
    out_specs += [pl.BlockSpec((1, w), _par_map(0)) for _, w, _ in pars]
    out_shape = [jax.ShapeDtypeStruct((T, w), drow_dtypes[k]) for w, k in zip(widths, keep)]
    out_shape += [jax.ShapeDtypeStruct((1, w), F32) for _, w, _ in pars]
    res = pl.pallas_call(
        body, name=name, grid=(T // tm,),
        in_specs=in_specs, out_specs=out_specs, out_shape=out_shape,
        compiler_params=_params("arbitrary"),
    )(*[a for a, _, _ in rows], *[a for a, _, _ in pars], *[a for a, _, _ in cots])
    return res[:n_out], res[n_out:]


NN = (((1,), (0,)), ((), ()))
NT = (((1,), (1,)), ((), ()))
TN = (((0,), (0,)), ((), ()))


def _mm(name, a, b, dims, out_shape, grid, a_spec, b_spec, o_spec, red_axis=None, init=None):
    n_red = None if red_axis is None else grid[red_axis]

    def body(a_ref, b_ref, *rest):
        o_ref = rest[-1]
        prod = lax.dot_general(a_ref[...], b_ref[...], dims, preferred_element_type=F32)
        if n_red is None:
            o_ref[...] = prod.astype(o_ref.dtype)
        else:
            @pl.when(pl.program_id(red_axis) == 0)
            def _():
                o_ref[...] = prod

            @pl.when(pl.program_id(red_axis) > 0)
            def _():
                o_ref[...] += prod

    sem = ["parallel"] * len(grid)
    if red_axis is not None:
        sem[red_axis] = "arbitrary"
    if init is None:
        return pl.pallas_call(
            body, name=name, grid=grid, in_specs=[a_spec, b_spec], out_specs=o_spec, out_shape=out_shape,
            compiler_params=_params(*sem),
        )(a, b)
    return pl.pallas_call(
        body, name=name, grid=grid, in_specs=[a_spec, b_spec, ANY], out_specs=o_spec, out_shape=out_shape,
        input_output_aliases={2: 0}, compiler_params=_params(*sem),
    )(a, b, init)


def _mm_nn(name, a, w, tm, out_dtype=F32, rows=None):
    T, K = a.shape
    T = rows or T
    N = w.shape[1]
    return _mm(name, a, w, NN, jax.ShapeDtypeStruct((T, N), out_dtype), (T // tm,),
               pl.BlockSpec((tm, K), lambda i: (i, 0)), pl.BlockSpec((K, N), lambda i: (0, 0)),
               pl.BlockSpec((tm, N), lambda i: (i, 0)))


def _mm_nt(name, a, w, tm, out_dtype=F32, out_rows=None):
    T, N = a.shape
    K = w.shape[0]
    init = None if out_rows is None else jnp.zeros((out_rows, K), out_dtype)
    return _mm(name, a, w, NT, jax.ShapeDtypeStruct((out_rows or T, K), out_dtype), (T // tm,),
               pl.BlockSpec((tm, N), lambda i: (i, 0)), pl.BlockSpec((K, N), lambda i: (0, 0)),
               pl.BlockSpec((tm, K), lambda i: (i, 0)), init=init)


def _mm_tn(name, a, b, tm, n_row_blocks):
    K = a.shape[1]
    T, N = b.shape
    kb = K // n_row_blocks
    return _mm(name, a, b, TN, jax.ShapeDtypeStruct((K, N), F32), (n_row_blocks, T // tm),
               pl.BlockSpec((tm, kb), lambda r, t: (t, r)), pl.BlockSpec((tm, N), lambda r, t: (t, 0)),
               pl.BlockSpec((kb, N), lambda r, t: (r, 0)), red_axis=1)


def _mm_nn_cols(name, a, w3, tm):
    T, K = a.shape
    P, _, Ns = w3.shape
    return _mm(name, a, w3, NN, jax.ShapeDtypeStruct((T, P * Ns), F32), (P, T // tm),
               pl.BlockSpec((tm, K), lambda p, i: (i, 0)), pl.BlockSpec((None, K, Ns), lambda p, i: (p, 0, 0)),
               pl.BlockSpec((tm, Ns), lambda p, i: (i, p)))


def _mm_nt_cols(name, a, w3, tm):
    T = a.shape[0]
    P, K, Ns = w3.shape
    return _mm(name, a, w3, NT, jax.ShapeDtypeStruct((T, K), F32), (T // tm, P),
               pl.BlockSpec((tm, Ns), lambda i, p: (i, p)), pl.BlockSpec((None, K, Ns), lambda i, p: (p, 0, 0)),
               pl.BlockSpec((tm, K), lambda i, p: (i, 0)), red_axis=1)


def _mm_tn_cols(name, a, b, tm, P):
    T, K = a.shape
    Ns = b.shape[1] // P
    return _mm(name, a, b, TN, jax.ShapeDtypeStruct((P, K, Ns), F32), (P, T // tm),
               pl.BlockSpec((tm, K), lambda p, t: (t, 0)), pl.BlockSpec((tm, Ns), lambda p, t: (t, p)),
               pl.BlockSpec((None, K, Ns), lambda p, t: (p, 0, 0)), red_axis=1)


def _tap_windows(win, off, tb):
    out = []
    for b in range(8):
        taps = [j for j in range(CONV_WIDTH) if (off + j) % 8 == b]
        if taps:
            shifted = win[b:b + tb + CONV_PAD, :]
            out += [(shifted, off + j - b, j) for j in taps]
    return out


def _shift_conv(name, x, w, place, off, T):
    C = x.shape[1]
    tb = ROW_BLOCK
    zero_at = 0 if place else T

    def body(x_ref, w_ref, o_ref, xp_ref):
        xp_ref[pl.ds(zero_at, CONV_PAD), :] = jnp.zeros((CONV_PAD, LANES), F32)
        xp_ref[pl.ds(T + CONV_PAD, 8), :] = jnp.zeros((8, LANES), F32)
        xp_ref[pl.ds(place, T), :] = x_ref[...]

        def step(t, carry):
            base = pl.multiple_of(t * tb, tb)
            win = xp_ref[pl.ds(base, tb + CONV_PAD + 8), :]
            acc = jnp.zeros((tb, LANES), F32)
            for shifted, at, j in _tap_windows(win, off, tb):
                acc = acc + shifted[at:at + tb, :] * w_ref[pl.ds(j, 1), :]
            o_ref[pl.ds(base, tb), :] = acc
            return carry

        lax.fori_loop(0, T // tb, step, 0)

    return pl.pallas_call(
        body, name=name, grid=(C // LANES,),
        in_specs=[pl.BlockSpec((T, LANES), lambda c: (0, c)), pl.BlockSpec((CONV_WIDTH, LANES), lambda c: (0, c))],
        out_specs=pl.BlockSpec((T, LANES), lambda c: (0, c)),
        out_shape=jax.ShapeDtypeStruct((T, C), F32),
        scratch_shapes=[pltpu.VMEM((T + CONV_PAD + 8, LANES), F32)],
        compiler_params=_params("parallel"),
    )(x, w)


def _conv_dw(name, x, dy, T):
    C = x.shape[1]
    tb = ROW_BLOCK
    off = CONV_PAD - (CONV_WIDTH - 1)

    def body(x_ref, dy_ref, o_ref, xp_ref, acc_ref):
        xp_ref[pl.ds(0, CONV_PAD), :] = jnp.zeros((CONV_PAD, LANES), F32)
        xp_ref[pl.ds(T + CONV_PAD, 8), :] = jnp.zeros((8, LANES), F32)
        xp_ref[pl.ds(CONV_PAD, T), :] = x_ref[...]
        acc_ref[...] = jnp.zeros_like(acc_ref)

        def step(t, carry):
            base = pl.multiple_of(t * tb, tb)
            win = xp_ref[pl.ds(base, tb + CONV_PAD + 8), :]
            d = dy_ref[pl.ds(base, tb), :]
            for shifted, at, j in _tap_windows(win, off, tb):
                prod = shifted[at:at + tb, :] * d
                acc_ref[j] += jnp.sum(prod.reshape(tb // 8, 8, LANES), axis=0)
            return carry

        lax.fori_loop(0, T // tb, step, 0)
        for j in range(CONV_WIDTH):
            o_ref[pl.ds(j, 1), :] = jnp.sum(acc_ref[j], axis=0, keepdims=True)

    return pl.pallas_call(
        body, name=name, grid=(C // LANES,),
        in_specs=[pl.BlockSpec((T, LANES), lambda c: (0, c)), pl.BlockSpec((T, LANES), lambda c: (0, c))],
        out_specs=pl.BlockSpec((CONV_WIDTH, LANES), lambda c: (0, c)),
        out_shape=jax.ShapeDtypeStruct((CONV_WIDTH, C), F32),
        scratch_shapes=[pltpu.VMEM((T + CONV_PAD + 8, LANES), F32), pltpu.VMEM((CONV_WIDTH, 8, LANES), F32)],
        compiler_params=_params("parallel"),
    )(x, dy)


def _dot(a, b, dims=NN):
    return lax.dot_general(a, b, dims, preferred_element_type=F32)


def _tri_cumsum(x, tri):
    return _dot(x.astype(BF16), tri)


def _qkv_split(p, D, T, Ta):
    tb = ROW_BLOCK
    nt = T // tb
    scale = 1.0 / math.sqrt(HEAD_DIM)

    def body(q_ref, k_ref, v_ref, qo_ref, ko_ref, vo_ref):
        live = pl.program_id(0) < nt
        qo_ref[...] = jnp.where(live, q_ref[...] * scale, 0.0).astype(BF16)
        ko_ref[...] = jnp.where(live, k_ref[...], 0.0).astype(BF16)
        vo_ref[...] = jnp.where(live, v_ref[...], 0.0).astype(BF16)

    def col(n):
        return lambda i: (jnp.minimum(i, nt - 1), n)

    return pl.pallas_call(
        body, name="qkv_split", grid=(Ta // tb,),
        in_specs=[pl.BlockSpec((tb, D), col(2 + n)) for n in range(3)],
        out_specs=[pl.BlockSpec((tb, D), lambda i: (i, 0))] * 3,
        out_shape=[jax.ShapeDtypeStruct((Ta, D), BF16)] * 3,
        compiler_params=_params("parallel"),
    )(p, p, p)


def _pair_lanes(g):
    return slice((g // 2) * LANES, (g // 2 + 1) * LANES)


def _stacked_pairs(x_ref, B, G):
    first = lax.broadcasted_iota(jnp.int32, (B, LANES), 1) < HEAD_DIM
    out = []
    for g in range(0, G, 2):
        x2 = x_ref[:, _pair_lanes(g)]
        zero = jnp.zeros_like(x2)
        out.append(jnp.concatenate([jnp.where(first, x2, zero), jnp.where(first, zero, x2)], axis=0))
    return first, out


def _attn_fwd(q, k, v):
    T, D = q.shape
    H = D // HEAD_DIM
    B = ATT_BLOCK
    G = ATT_HEADS
    NP = G // 2
    W = NP * LANES

    def body(q_ref, k_ref, v_ref, o_ref, rt_ref):
        i = pl.program_id(1)
        row = lax.broadcasted_iota(jnp.int32, (B, B), 0)
        col = lax.broadcasted_iota(jnp.int32, (B, B), 1)
        tri = (row >= col).astype(BF16)
        below = jnp.concatenate([col < row] * 2, axis=0)
        first, qp = _stacked_pairs(q_ref, B, G)

        def tile(j, carry, diagonal):
            sl = pl.ds(pl.multiple_of(j * B, B), B)
            zs, sps = [], []
            for p in range(NP):
                z = _dot(qp[p], k_ref[sl, _pair_lanes(2 * p)], NT)
                sp = jnp.maximum(z, 0.0) + jnp.log(1.0 + jnp.exp(-jnp.abs(z)))
                if diagonal:
                    sp = jnp.where(below, sp, 0.0)
                zs.append(z)
                sps.append(sp)
            rws = _tri_cumsum(jnp.concatenate(sps, axis=0), tri)
            out = []
            for p in range(NP):
                c, acc = carry[p]
                rw = rws[2 * B * p:2 * B * (p + 1)]
                a = jnp.exp(zs[p] - (rw + c))
                if diagonal:
                    a = jnp.where(below, a, 0.0)
                acc = acc + _dot(a.astype(BF16), v_ref[sl, _pair_lanes(2 * p)])
                out.append((c + rw[:, 0:1], acc))
            return tuple(out)

        carry = tile(i, tuple((jnp.zeros((2 * B, 1), F32), jnp.zeros((2 * B, LANES), F32)) for _ in range(NP)), True)
        carry = lax.fori_loop(0, i, lambda jj, cr: tile(i - 1 - jj, cr, False), carry)
        for p in range(NP):
            c, acc = carry[p]
            o_ref[:, _pair_lanes(2 * p)] = jnp.where(first, acc[:B], acc[B:]).astype(o_ref.dtype)
            rt_ref[2 * p] = c[:B]
            rt_ref[2 * p + 1] = c[B:]

    return pl.pallas_call(
        body, name="attn_fwd", grid=(H // G, T // B),
        in_specs=[pl.BlockSpec((B, W), lambda h, i: (i, h)),
                  pl.BlockSpec((T, W), lambda h, i: (0, h)),
                  pl.BlockSpec((T, W), lambda h, i: (0, h))],
        out_specs=[pl.BlockSpec((B, W), lambda h, i: (i, h)),
                   pl.BlockSpec((G, B, 1), lambda h, i: (h, i, 0))],
        out_shape=[jax.ShapeDtypeStruct((T, D), BF16), jax.ShapeDtypeStruct((H, T, 1), F32)],
        compiler_params=_params("parallel", "arbitrary"),
    )(q, k, v)


def _attn_bwd(q, k, v, do, rt, after):
    T, D = q.shape
    H = D // HEAD_DIM
    B = ATT_BLOCK
    nq = T // B
    scale = 1.0 / math.sqrt(HEAD_DIM)
    G = ATT_HEADS_BWD
    NP = G // 2
    W = NP * LANES

    def body(q_ref, k_ref, v_ref, do_ref, rt_ref, after_ref, dq_ref, dk_ref, dv_ref, dk_acc, dv_acc):
        i = pl.program_id(1)

        @pl.when(i == 0)
        def _():
            dk_acc[...] = jnp.zeros_like(dk_acc)
            dv_acc[...] = jnp.zeros_like(dv_acc)

        row = lax.broadcasted_iota(jnp.int32, (B, B), 0)
        col = lax.broadcasted_iota(jnp.int32, (B, B), 1)
        tri = (row <= col).astype(BF16)
        below = jnp.concatenate([col < row] * 2, axis=0)
        first, qp = _stacked_pairs(q_ref, B, G)
        _, dop = _stacked_pairs(do_ref, B, G)
        rtp = [jnp.concatenate([rt_ref[2 * p], rt_ref[2 * p + 1]], axis=0) for p in range(NP)]

        def tile(j, carry, diagonal):
            sl = pl.ds(pl.multiple_of(j * B, B), B)
            zs, sps, sgs = [], [], []
            for p in range(NP):
                z = _dot(qp[p], k_ref[sl, _pair_lanes(2 * p)], NT)
                e = jnp.exp(-jnp.abs(z))
                inv = 1.0 / (1.0 + e)
                sp = jnp.maximum(z, 0.0) - jnp.log(inv)
                if diagonal:
                    sp = jnp.where(below, sp, 0.0)
                zs.append(z)
                sps.append(sp)
                sgs.append(jnp.where(z >= 0.0, inv, e * inv))
            pws = _tri_cumsum(jnp.concatenate(sps, axis=0), tri)
            aas, gs = [], []
            for p in range(NP):
                pw = pws[2 * B * p:2 * B * (p + 1)]
                a = jnp.exp(zs[p] - (rtp[p] - carry[p][0] - pw + sps[p]))
                if diagonal:
                    a = jnp.where(below, a, 0.0)
                aas.append(a.astype(BF16))
                gs.append(a * _dot(dop[p], v_ref[sl, _pair_lanes(2 * p)], NT))
            gws = _tri_cumsum(jnp.concatenate(gs, axis=0), tri)
            out = []
            for p in range(NP):
                pc, gc, dq = carry[p]
                pw, gw = pws[2 * B * p:2 * B * (p + 1)], gws[2 * B * p:2 * B * (p + 1)]
                dz = gs[p] - sgs[p] * (gc + gw)
                if diagonal:
                    dz = jnp.where(below, dz, 0.0)
                dzb = dz.astype(BF16)
                dq = dq + _dot(dzb, k_ref[sl, _pair_lanes(2 * p)])
                dk_acc[sl, _pair_lanes(2 * p)] += _dot(dzb, qp[p], TN)
                dv_acc[sl, _pair_lanes(2 * p)] += _dot(aas[p], dop[p], TN)
                out.append((pc + pw[:, B - 1:B], gc + gw[:, B - 1:B], dq))
            return tuple(out)

        zero = jnp.zeros((2 * B, 1), F32)
        carry = lax.fori_loop(0, i, lambda j, cr: tile(j, cr, False),
                              tuple((zero, zero, jnp.zeros((2 * B, LANES), F32)) for _ in range(NP)))
        carry = tile(i, carry, True)
        for p in range(NP):
            dq = carry[p][2]
            dq_ref[:, _pair_lanes(2 * p)] = (jnp.where(first, dq[:B], dq[B:]) * scale).astype(dq_ref.dtype)

        @pl.when(i == nq - 1)
        def _():
            dk_ref[...] = dk_acc[...].astype(dk_ref.dtype)
            dv_ref[...] = dv_acc[...].astype(dv_ref.dtype)

    blk = pl.BlockSpec((B, W), lambda h, i: (i, h))
    full = pl.BlockSpec((T, W), lambda h, i: (0, h))
    return pl.pallas_call(
        body, name="attn_bwd", grid=(H // G, nq),
        in_specs=[blk, full, full, blk, pl.BlockSpec((G, B, 1), lambda h, i: (h, i, 0)), ANY],
        out_specs=[blk, full, full],
        out_shape=[jax.ShapeDtypeStruct((T, D), BF16)] * 3,
        scratch_shapes=[pltpu.VMEM((T, W), F32)] * 2,
        compiler_params=_params("parallel", "arbitrary"),
    )(q, k, v, do, rt, after)


def _loss_head(y, target, tm):
    S, D = y.shape

    def body(y_ref, t_ref, dy_ref, part_ref):
        err = y_ref[...] - t_ref[...]
        dy_ref[...] = err * (1.0 / D)

        @pl.when(pl.program_id(0) == 0)
        def _():
            part_ref[...] = jnp.zeros_like(part_ref)

        part_ref[...] += jnp.sum(err * err, axis=0, keepdims=True)

    spec = pl.BlockSpec((tm, D), lambda i: (i, 0))
    return pl.pallas_call(
        body, name="loss_head", grid=(S // tm,), in_specs=[spec, spec],
        out_specs=[spec, pl.BlockSpec((1, D), lambda i: (0, 0))],
        out_shape=[jax.ShapeDtypeStruct((S, D), F32), jax.ShapeDtypeStruct((1, D), F32)],
        compiler_params=_params("arbitrary"),
    )(y, target)


def _row_tile(R):
    for t in (256, 128, 64, 32, 16, 8):
        if R % t == 0:
            return t
    return R


def _pair_add_bf16(name, g, b1, kind, c_arr, me_arr=None):
    P, Rh, C = b1.shape
    tr = _row_tile(Rh)
    nb = Rh // tr
    own = me_arr is not None

    def body(*refs):
        g_ref, b_ref, o_ref = refs[1 + own:4 + own]
        val = (g_ref[...] + b_ref[...]).astype(o_ref.dtype)
        o_ref[...] = val
        if own:
            @pl.when(pl.program_id(1) == refs[1][0])
            def _():
                refs[-1][...] = val

    if kind == "cols":
        g_spec = pl.BlockSpec((None, tr, C), lambda i, p, c_ref, *_: (p, c_ref[0] * nb + i, 0))
    else:
        g_spec = pl.BlockSpec((tr, C), lambda i, p, c_ref, *_: ((2 * p + c_ref[0]) * nb + i, 0))
    blk = pl.BlockSpec((None, tr, C), lambda i, p, *_: (p, i, 0))
    shape = jax.ShapeDtypeStruct((P, Rh, C), BF16)
    if not own:
        return pl.pallas_call(
            body, name=name,
            grid_spec=pltpu.PrefetchScalarGridSpec(num_scalar_prefetch=1, grid=(nb, P), in_specs=[g_spec, blk], out_specs=blk),
            out_shape=shape, compiler_params=_params("parallel", "parallel"),
        )(c_arr, g, b1)
    mine = pl.BlockSpec((None, tr, C), lambda i, p, c_ref, me_ref: (me_ref[0], i, 0))
    return pl.pallas_call(
        body, name=name,
        grid_spec=pltpu.PrefetchScalarGridSpec(num_scalar_prefetch=2, grid=(nb, P), in_specs=[g_spec, blk], out_specs=[blk, mine]),
        out_shape=[shape, shape], compiler_params=_params("parallel", "arbitrary"),
    )(c_arr, me_arr, g, b1)


def _sum_slots(name, b, half_arr=None):
    P, R, C = b.shape
    tr = _row_tile(R)
    nb = R // tr

    def body(*refs):
        b_ref, o_ref = refs[-2:]
        acc = b_ref[0].astype(F32)
        for s in range(1, P):
            acc = acc + b_ref[s].astype(F32)
        o_ref[...] = acc

    if half_arr is None:
        return pl.pallas_call(
            body, name=name, grid=(nb,),
            in_specs=[pl.BlockSpec((P, tr, C), lambda i: (0, i, 0))],
            out_specs=pl.BlockSpec((tr, C), lambda i: (i, 0)),
            out_shape=jax.ShapeDtypeStruct((R, C), F32),
            compiler_params=_params("parallel"),
        )(b)
    return pl.pallas_call(
        body, name=name,
        grid_spec=pltpu.PrefetchScalarGridSpec(
            num_scalar_prefetch=1, grid=(nb,),
            in_specs=[pl.BlockSpec((P, tr, C), lambda i, half: (0, i, 0))],
            out_specs=pl.BlockSpec((tr, C), lambda i, half: (half[0] * nb + i, 0))),
        out_shape=jax.ShapeDtypeStruct((2 * R, C), F32),
        compiler_params=_params("parallel"),
    )(half_arr, b)


def _adamw(name, w, g, m, v):
    R, C = w.shape
    tr = _row_tile(R)
    c1 = 1.0 - ADAM_B1 ** ADAM_STEP
    c2 = 1.0 - ADAM_B2 ** ADAM_STEP

    def body(w_ref, g_ref, m_ref, v_ref, d_ref, nm_ref, nv_ref):
        gg = g_ref[...]
        nm = ADAM_B1 * m_ref[...] + (1.0 - ADAM_B1) * gg
        nv = ADAM_B2 * v_ref[...] + (1.0 - ADAM_B2) * (gg * gg)
        m_hat = nm / c1
        v_hat = nv / c2
        d_ref[...] = -ADAM_LR * (m_hat / (jnp.sqrt(v_hat) + ADAM_EPS) + ADAM_WD * w_ref[...])
        nm_ref[...] = nm
        nv_ref[...] = nv

    spec = pl.BlockSpec((tr, C), lambda i: (i, 0))
    return pl.pallas_call(
        body, name=name, grid=(R // tr,), in_specs=[spec] * 4, out_specs=[spec] * 3,
        out_shape=[jax.ShapeDtypeStruct((R, C), F32)] * 3,
        compiler_params=_params("parallel"),
    )(w, g, m, v)


def _place():
    x, y, c = lax.axis_index("x"), lax.axis_index("y"), lax.axis_index("c")
    other_chips = [(1 - x, y), (x, 1 - y), (1 - x, 1 - y)]
    return x, y, c, other_chips


def _into_slot(name, w, dtype, slot_arr, n_slots):
    R, C = w.shape
    tr = _row_tile(R)

    def body(slot_ref, w_ref, o_ref):
        o_ref[...] = w_ref[...].astype(o_ref.dtype)

    return pl.pallas_call(
        body, name=name,
        grid_spec=pltpu.PrefetchScalarGridSpec(
            num_scalar_prefetch=1, grid=(R // tr,),
            in_specs=[pl.BlockSpec((tr, C), lambda i, slot: (i, 0))],
            out_specs=pl.BlockSpec((None, tr, C), lambda i, slot: (slot[0], i, 0))),
        out_shape=jax.ShapeDtypeStruct((n_slots, R, C), dtype),
        compiler_params=_params("parallel"),
    )(slot_arr, w)


def _gather_chips(bufs):
    n = len(bufs)

    def body(*refs):
        outs = refs[n:2 * n]
        ici_send, ici_recv, d2d_send, d2d_recv = refs[2 * n:]
        x, y, c, chips = _place()
        me = 2 * x + y
        started = []
        for k in range(n):
            rh = outs[k].shape[1] // 2
            mine = outs[k].at[me, pl.ds(c * rh, rh)]
            for j, (px, py) in enumerate(chips):
                cp = pltpu.make_async_remote_copy(
                    src_ref=mine, dst_ref=mine,
                    send_sem=ici_send.at[3 * k + j], recv_sem=ici_recv.at[3 * k + j],
                    device_id=(px, py, c), device_id_type=MESH)
                cp.start()
                started.append(cp)
        for k in range(n):
            rh = outs[k].shape[1] // 2
            for j, (px, py) in enumerate(chips):
                landed = outs[k].at[2 * px + py, pl.ds(c * rh, rh)]
                pltpu.make_async_remote_copy(
                    src_ref=landed, dst_ref=landed,
                    send_sem=ici_send.at[3 * k + j], recv_sem=ici_recv.at[3 * k + j],
                    device_id=(px, py, c), device_id_type=MESH).wait_recv()
                cp = pltpu.make_async_remote_copy(
                    src_ref=landed, dst_ref=landed,
                    send_sem=d2d_send.at[3 * k + j], recv_sem=d2d_recv.at[3 * k + j],
                    device_id=(x, y, 1 - c), device_id_type=MESH)
                cp.start()
                started.append(cp)
        for k in range(n):
            rh = outs[k].shape[1] // 2
            for j, (px, py) in enumerate(chips):
                landed = outs[k].at[2 * px + py, pl.ds((1 - c) * rh, rh)]
                pltpu.make_async_remote_copy(
                    src_ref=landed, dst_ref=landed,
                    send_sem=d2d_send.at[3 * k + j], recv_sem=d2d_recv.at[3 * k + j],
                    device_id=(x, y, 1 - c), device_id_type=MESH).wait_recv()
        for cp in started:
            cp.wait_send()

    return pl.pallas_call(
        body, name="gather_weights",
        in_specs=[ANY] * n, out_specs=[ANY] * n,
        out_shape=[jax.ShapeDtypeStruct(b.shape, b.dtype) for b in bufs],
        input_output_aliases={k: k for k in range(n)},
        scratch_shapes=[pltpu.SemaphoreType.DMA((3 * n,))] * 4,
        compiler_params=pltpu.CompilerParams(has_side_effects=True),
    )(*bufs)


def _gather_forward(bufs):
    n = len(bufs)

    def body(*refs):
        outs = refs[n:2 * n]
        d2d_send, d2d_recv = refs[2 * n:]
        x, y, c, chips = _place()
        started = []
        for k in range(n):
            rh = outs[k].shape[1] // 2
            for j, (px, py) in enumerate(chips):
                landed = outs[k].at[2 * px + py, pl.ds(c * rh, rh)]
                cp = pltpu.make_async_remote_copy(
                    src_ref=landed, dst_ref=landed,
                    send_sem=d2d_send.at[3 * k + j], recv_sem=d2d_recv.at[3 * k + j],
                    device_id=(x, y, 1 - c), device_id_type=MESH)
                cp.start()
                started.append(cp)
        for k in range(n):
            rh = outs[k].shape[1] // 2
            for j, (px, py) in enumerate(chips):
                landed = outs[k].at[2 * px + py, pl.ds((1 - c) * rh, rh)]
                pltpu.make_async_remote_copy(
                    src_ref=landed, dst_ref=landed,
                    send_sem=d2d_send.at[3 * k + j], recv_sem=d2d_recv.at[3 * k + j],
                    device_id=(x, y, 1 - c), device_id_type=MESH).wait_recv()
        for cp in started:
            cp.wait_send()

    return pl.pallas_call(
        body, name="gather_rest_forward",
        in_specs=[ANY] * n, out_specs=[ANY] * n,
        out_shape=[jax.ShapeDtypeStruct(b.shape, b.dtype) for b in bufs],
        input_output_aliases={k: k for k in range(n)},
        scratch_shapes=[pltpu.SemaphoreType.DMA((3 * n,))] * 2,
        compiler_params=pltpu.CompilerParams(has_side_effects=True),
    )(*bufs)


def _to_chips_start(name, arrays, n, src_fn, dst_fn, after, to_sibling=False):
    m = len(arrays)
    n_peers = 1 if to_sibling else N_CHIPS - 1

    def body(*refs):
        send_sem, recv_sem = refs[m + 1], refs[m + 2]
        thru = refs[m + 3:2 * m + 3]
        token = refs[2 * m + 3]
        x, y, c, chips = _place()
        peers = [(x, y, 1 - c)] if to_sibling else [(px, py, c) for px, py in chips]
        for k in range(n):
            for j, (px, py, pc) in enumerate(peers):
                pltpu.make_async_remote_copy(
                    src_ref=src_fn(thru, k, px, py, x, y, c), dst_ref=dst_fn(thru, k, px, py, x, y, c),
                    send_sem=send_sem.at[n_peers * k + j], recv_sem=recv_sem.at[n_peers * k + j],
                    device_id=(px, py, pc), device_id_type=MESH).start()
        token[...] = jnp.zeros_like(token)

    res = pl.pallas_call(
        body, name=name,
        out_shape=(pltpu.SemaphoreType.DMA((n_peers * n,)), pltpu.SemaphoreType.DMA((n_peers * n,)),
                   *[pltpu.HBM(a.shape, a.dtype) for a in arrays], jax.ShapeDtypeStruct((8, LANES), F32)),
        in_specs=[HBM] * m + [ANY],
        out_specs=(SEM, SEM, *[HBM] * m, pl.BlockSpec(memory_space=pltpu.VMEM)),
        input_output_aliases={i: i + 2 for i in range(m)},
        compiler_params=pltpu.CompilerParams(has_side_effects=EFFECT),
    )(*[pltpu.with_memory_space_constraint(a, pltpu.HBM) for a in arrays], after)
    return res[0], res[1], list(res[2:2 + m]), res[2 + m]


def _to_chips_wait(name, send_sem, recv_sem, arrays, n, src_fn, land_fn, after, to_sibling=False):
    m = len(arrays)
    after = list(after) if isinstance(after, (list, tuple)) else [after]
    n_peers = 1 if to_sibling else N_CHIPS - 1

    def body(*refs):
        send, recv = refs[m], refs[m + 1]
        outs = refs[m + 2 + len(after):]
        x, y, c, chips = _place()
        peers = [(x, y, 1 - c)] if to_sibling else [(px, py, c) for px, py in chips]
        for k in range(n):
            for j, (px, py, pc) in enumerate(peers):
                cp = pltpu.make_async_remote_copy(
                    src_ref=src_fn(outs, k, px, py, x, y, c), dst_ref=land_fn(outs, k, px, py, x, y, c),
                    send_sem=send.at[n_peers * k + j], recv_sem=recv.at[n_peers * k + j],
                    device_id=(px, py, pc), device_id_type=MESH)
                cp.wait_send()
                cp.wait_recv()

    res = pl.pallas_call(
        body, name=name,
        out_shape=[pltpu.HBM(a.shape, a.dtype) for a in arrays],
        in_specs=[HBM] * m + [SEM, SEM] + [ANY] * len(after), out_specs=[HBM] * m,
        input_output_aliases={i: i for i in range(m)},
        compiler_params=pltpu.CompilerParams(has_side_effects=EFFECT),
    )(*arrays, send_sem, recv_sem, *after)
    return list(res)


def _slot_half(refs, k, chip, c):
    rh = refs[k].shape[1] // 2
    return refs[k].at[chip, pl.ds(c * rh, rh)]


def _ag_mine(refs, k, px, py, x, y, c):
    return _slot_half(refs, k, 2 * x + y, c)


def _ag_theirs(refs, k, px, py, x, y, c):
    return _slot_half(refs, k, 2 * px + py, c)


def _rs_ends(n):
    def src(refs, k, px, py, x, y, c):
        return refs[k].at[2 * px + py]

    def dst(refs, k, px, py, x, y, c):
        return refs[n + k].at[2 * x + y]

    def land(refs, k, px, py, x, y, c):
        return refs[n + k].at[2 * px + py]

    return src, dst, land


def _half(ref, kind, p, c, rh):
    if kind == "cols":
        return ref.at[p, pl.ds(c * rh, rh)]
    return ref.at[pl.ds((2 * p + c) * rh, rh)]


def _swap_ends(kinds, rhs):
    n = len(kinds)

    def src(refs, k, px, py, x, y, c):
        return _half(refs[k // N_CHIPS], kinds[k // N_CHIPS], k % N_CHIPS, 1 - c, rhs[k // N_CHIPS])

    def dst(refs, k, px, py, x, y, c):
        return refs[n + k // N_CHIPS].at[k % N_CHIPS]

    return src, dst


def _join_halves(name, fulls, after=None):
    n = len(fulls)
    extra = [] if after is None else [after]

    def body(*refs):
        outs = refs[n + len(extra):2 * n + len(extra)]
        send_sem, recv_sem = refs[2 * n + len(extra):]
        x, y, c, _ = _place()
        started = []
        for k in range(n):
            rh = outs[k].shape[0] // 2
            mine = outs[k].at[pl.ds(c * rh, rh)]
            cp = pltpu.make_async_remote_copy(
                src_ref=mine, dst_ref=mine, send_sem=send_sem.at[k], recv_sem=recv_sem.at[k],
                device_id=(x, y, 1 - c), device_id_type=MESH)
            cp.start()
            started.append(cp)
        for k in range(n):
            rh = outs[k].shape[0] // 2
            theirs = outs[k].at[pl.ds((1 - c) * rh, rh)]
            pltpu.make_async_remote_copy(
                src_ref=theirs, dst_ref=theirs, send_sem=send_sem.at[k], recv_sem=recv_sem.at[k],
                device_id=(x, y, 1 - c), device_id_type=MESH).wait_recv()
        for cp in started:
            cp.wait_send()

    return pl.pallas_call(
        body, name=name,
        in_specs=[ANY] * (n + len(extra)), out_specs=[ANY] * n,
        out_shape=[jax.ShapeDtypeStruct(f.shape, f.dtype) for f in fulls],
        input_output_aliases={k: k for k in range(n)},
        scratch_shapes=[pltpu.SemaphoreType.DMA((n,))] * 2,
        compiler_params=pltpu.CompilerParams(has_side_effects=True),
    )(*fulls, *extra)


def _gather_all(block):
    def body(in_ref, out_ref, send_sem, recv_sem, local_sem):
        x, y, c, _ = _place()

        def slot(px, py, pc):
            return out_ref.at[4 * px + 2 * py + pc]

        loc = pltpu.make_async_copy(in_ref, slot(x, y, c), local_sem)
        loc.start()
        started = []
        for d in range(1, N_DEV):
            fx, fy, fc = d >> 2, (d >> 1) & 1, d & 1
            cp = pltpu.make_async_remote_copy(
                src_ref=in_ref, dst_ref=slot(x, y, c), send_sem=send_sem.at[d - 1], recv_sem=recv_sem.at[d - 1],
                device_id=(x ^ fx, y ^ fy, c ^ fc), device_id_type=MESH)
            cp.start()
            started.append(cp)
        for d in range(1, N_DEV):
            fx, fy, fc = d >> 2, (d >> 1) & 1, d & 1
            landed = slot(x ^ fx, y ^ fy, c ^ fc)
            pltpu.make_async_remote_copy(
                src_ref=in_ref, dst_ref=landed, send_sem=send_sem.at[d - 1], recv_sem=recv_sem.at[d - 1],
                device_id=(x ^ fx, y ^ fy, c ^ fc), device_id_type=MESH).wait_recv()
        for cp in started:
            cp.wait_send()
        loc.wait()

    return pl.pallas_call(
        body, name="gather_small_grads",
        in_specs=[ANY], out_specs=ANY,
        out_shape=jax.ShapeDtypeStruct((N_DEV,) + block.shape, block.dtype),
        scratch_shapes=[pltpu.SemaphoreType.DMA((N_DEV - 1,))] * 2 + [pltpu.SemaphoreType.DMA(())],
        compiler_params=pltpu.CompilerParams(has_side_effects=True),
    )(block)


def _pack(pieces):
    flat = jnp.concatenate([p.reshape(-1) for p in pieces])
    n = flat.shape[0]
    padded = -(-n // (8 * LANES)) * (8 * LANES)
    return jnp.pad(flat, (0, padded - n)).reshape(-1, LANES)


def _unpack(packed, shapes):
    flat = packed.reshape(-1)
    out, at = [], 0
    for s in shapes:
        n = math.prod(s)
        out.append(flat[at:at + n].reshape(s))
        at += n
    return out


def kernel(x, meta_tokens, pre_mix_g, w_in, gate_b, dw_w, dw_b, conv_ln_g, conv_ln_b, w_conv_out, w_attn_out, w_o, post_mix_g, pre_ffn_g, w_ffn_in, w_ffn_out, post_ffn_g, loss_target, m_meta_tokens, m_pre_mix_g, m_w_in, m_gate_b, m_dw_w, m_dw_b, m_conv_ln_g, m_conv_ln_b, m_w_conv_out, m_w_attn_out, m_w_o, m_post_mix_g, m_pre_ffn_g, m_w_ffn_in, m_w_ffn_out, m_post_ffn_g, v_meta_tokens, v_pre_mix_g, v_w_in, v_gate_b, v_dw_w, v_dw_b, v_conv_ln_g, v_conv_ln_b, v_w_conv_out, v_w_attn_out, v_w_o, v_post_mix_g, v_pre_ffn_g, v_w_ffn_in, v_w_ffn_out, v_post_ffn_g):
    S, D = x.shape[1], x.shape[2]
    L = S + N_META
    T = -(-L // ROW_BLOCK) * ROW_BLOCK
    Ta = -(-L // ATT_BLOCK) * ATT_BLOCK
    tm = _tile(T, MM_ROWS)
    tc = _tile(T, CONTRACT_ROWS)
    ts = _tile(T, STAGE_ROWS)
    tw = _tile(T, WIDE_STAGE_ROWS)
    H = D // HEAD_DIM
    F = w_ffn_out.shape[1] * N_CHIPS
    Dc = D // N_CHIPS
    P = N_CHIPS
    me = 2 * lax.axis_index("x") + lax.axis_index("y")
    c_arr = lax.axis_index("c").astype(jnp.int32).reshape(1)

    dw_w_pad = jnp.pad(dw_w[0], ((0, CONV_PAD - CONV_WIDTH), (0, 0)))
    me_arr = me.astype(jnp.int32).reshape(1)
    to_gather = [("w_in", w_in[0], BF16), ("w_conv_out", w_conv_out[0], BF16), ("w_attn_out", w_attn_out[0], BF16),
                 ("w_o", w_o[0], BF16), ("w_ffn_in", w_ffn_in[0], BF16), ("w_ffn_out", w_ffn_out[0], BF16),
                 ("meta", meta_tokens, F32), ("taps", dw_w_pad, F32)]
    slot = {n: _into_slot("slot_" + n, w, dt, me_arr, P) for n, w, dt in to_gather}
    win3, meta4, taps4 = _gather_chips([slot["w_in"], slot["meta"], slot["taps"]])
    meta_full = meta4.transpose(1, 0, 2).reshape(N_META, D)
    taps = taps4.transpose(1, 0, 2).reshape(CONV_PAD, D)[:CONV_WIDTH]
    later = ["w_conv_out", "w_attn_out", "w_o", "w_ffn_in", "w_ffn_out"]
    ag_send, ag_recv, in_flight, ag_token = _to_chips_start(
        "gather_rest_start", [slot[n] for n in later], len(later), _ag_mine, _ag_mine, meta4)

    h0 = jnp.concatenate([meta_full, x[0], jnp.zeros((T - L, D), F32)], axis=0)
    (u1,) = _rowwise_fwd("rms_pre_mix", f_rms, [(h0, D, 0)], [(pre_mix_g + ag_token[0:1, 0:1], D, 0)], [(D, BF16)], T, ts)
    p = _mm_nn_cols("mm_in", u1, win3, tm)
    q, k, v = _qkv_split(p, D, T, Ta)
    o2, rtot = _attn_fwd(q, k, v)
    landed = _to_chips_wait("gather_rest_wait", ag_send, ag_recv, in_flight, len(later), _ag_mine, _ag_theirs, o2)
    wco4, wao4, wo4, wfi3, wfo4 = _gather_forward(landed)
    wco, wao, wo = (w.reshape(D, D) for w in (wco4, wao4, wo4))
    wfo = wfo4.reshape(F, D)
    (uglu,) = _rowwise_fwd("glu", f_glu, [(p, D, 0), (p, D, 1)], [], [(D, F32)], T, ts)
    yc = _shift_conv("dwconv", uglu, taps, CONV_PAD, CONV_PAD - (CONV_WIDTH - 1), T)
    conv_pars = [(dw_b, D, 0), (conv_ln_g, D, 0), (conv_ln_b, D, 0)]
    (ys,) = _rowwise_fwd("conv_post", f_convpost, [(yc, D, 0)], conv_pars, [(D, BF16)], T, ts)
    y_conv = _mm_nn("mm_conv_out", ys, wco, tm)
    y_attn = _mm_nn("mm_attn_out", o2, wao, tm, rows=T)
    mix_rows = [(p, D, 5), (p, D, 6), (y_conv, D, 0), (y_attn, D, 0)]
    mix_pars = [(gate_b, D, 0), (gate_b, D, 1)]
    (mixin,) = _rowwise_fwd("gate_mix", f_mix, mix_rows, mix_pars, [(D, BF16)], T, ts)
    mix = _mm_nn("mm_o", mixin, wo, tm)
    (h1,) = _rowwise_fwd("res_post_mix", f_res_rms, [(h0, D, 0), (mix, D, 0)], [(post_mix_g, D, 0)], [(D, F32)], T, ts)
    (u2,) = _rowwise_fwd("rms_pre_ffn", f_rms, [(h1, D, 0)], [(pre_ffn_g, D, 0)], [(D, BF16)], T, ts)
    ab = _mm_nn_cols("mm_ffn_in", u2, wfi3, tm)
    (fin,) = _rowwise_fwd("swiglu", f_swiglu, [(ab, F, 0), (ab, F, 1)], [], [(F, BF16)], T, tw)
    f = _mm_nn("mm_ffn_out", fin, wfo, tm)
    (h2,) = _rowwise_fwd("res_post_ffn", f_res_rms, [(h1, D, 0), (f, D, 0)], [(post_ffn_g, D, 0)], [(D, F32)], T, ts)

    dy, part = _loss_head(h2[N_META:L], loss_target[0], _row_tile(S))
    loss = lax.psum(0.5 * jnp.sum(part) / D, ("x", "y", "c"))
    dh2 = jnp.pad(dy, ((N_META, T - L), (0, 0)))

    (df,), (g_post_ffn,) = _rowwise_bwd("res_post_ffn_bwd", f_res_rms, [(h1, D, 0), (f, D, 0)], [(post_ffn_g, D, 0)],
                                        [(dh2, D, 0)], [None, BF16], T, ts)
    dfin = _mm_nt("mm_ffn_out_dx", df, wfo, tm)
    g_wfo = _mm_tn("mm_ffn_out_dw", fin, df, tc, F // MXU_WIDTH)
    (dab,), _ = _rowwise_bwd("swiglu_bwd", f_swiglu, [(ab, F, 0), (ab, F, 1)], [], [(dfin, F, 0)], [BF16, BF16], T, tw,
                             joined=True)
    du2 = _mm_nt_cols("mm_ffn_in_dx", dab, wfi3, tm)
    g_wfi = _mm_tn_cols("mm_ffn_in_dw", u2, dab, tc, P)
    (dh1,), (g_pre_ffn,) = _rowwise_bwd("rms_pre_ffn_bwd", f_rms_id, [(h1, D, 0)], [(pre_ffn_g, D, 0)],
                                        [(du2, D, 0), (dh2, D, 0)], [F32], T, ts)
    (dmix,), (g_post_mix,) = _rowwise_bwd("res_post_mix_bwd", f_res_rms, [(h0, D, 0), (mix, D, 0)], [(post_mix_g, D, 0)],
                                          [(dh1, D, 0)], [None, BF16], T, ts)
    dmixin = _mm_nt("mm_o_dx", dmix, wo, tm)
    g_wo = _mm_tn("mm_o_dw", mixin, dmix, tc, D // MXU_WIDTH)
    (dpc, dpa, dyconv, dyattn), (g_gate_c, g_gate_a) = _rowwise_bwd(
        "gate_mix_bwd", f_mix, mix_rows, mix_pars, [(dmixin, D, 0)], [BF16, BF16, BF16, BF16], T, ts)
    g_wco = _mm_tn("mm_conv_out_dw", ys, dyconv, tc, D // MXU_WIDTH)
    dys = _mm_nt("mm_conv_out_dx", dyconv, wco, tm)
    g_wao = _mm_tn("mm_attn_out_dw", o2, dyattn, tc, D // MXU_WIDTH)
    do2 = _mm_nt("mm_attn_out_dx", dyattn, wao, tm, BF16, out_rows=Ta)

    early = [g_wco, g_wao, g_wo, g_wfi, g_wfo]
    n_early = len(early)
    early_kinds = ["rows", "rows", "rows", "cols", "rows"]
    early_rhs = [(g.shape[1] if kind == "cols" else g.shape[0] // P) // 2 for g, kind in zip(early, early_kinds)]
    sw_src, sw_dst = _swap_ends(early_kinds, early_rhs)
    sw_send, sw_recv, sw_flight, sw_token = _to_chips_start(
        "grad_swap_early_start", early + [lax.empty((P, rh, g.shape[-1]), F32) for g, rh in zip(early, early_rhs)],
        P * n_early, sw_src, sw_dst, do2, to_sibling=True)
    (dyc,), (g_dw_b, g_ln_g, g_ln_b) = _rowwise_bwd(
        "conv_post_bwd", f_convpost, [(yc, D, 0)], [(dw_b + sw_token[0:1, 0:1], D, 0)] + conv_pars[1:],
        [(dys, D, 0)], [F32], T, ts)
    duglu = _shift_conv("dwconv_dx", dyc, taps[::-1], 0, 0, T)
    g_taps = _conv_dw("dwconv_dw", uglu, dyc, T)
    (dp01,), _ = _rowwise_bwd("glu_bwd", f_glu, [(p, D, 0), (p, D, 1)], [], [(duglu, D, 0)], [BF16, BF16], T, ts,
                              joined=True)
    sw_done = _to_chips_wait("grad_swap_early_wait", sw_send, sw_recv, sw_flight, P * n_early, sw_src, sw_dst,
                             [dp01, g_taps], to_sibling=True)
    early_pairs = [_pair_add_bf16("grad_pair_add_%d" % (n + 1), g, b1, kind, c_arr, me_arr)
                   for n, (g, b1, kind) in enumerate(zip(sw_done[:n_early], sw_done[n_early:], early_kinds))]
    rs_src, rs_dst, rs_land = _rs_ends(n_early)
    rs_send, rs_recv, rs_flight, rs_token = _to_chips_start(
        "grad_scatter_start", [pr[0] for pr in early_pairs] + [pr[1] for pr in early_pairs], n_early,
        rs_src, rs_dst, early_pairs[-1][1])
    dq, dk, dv = _attn_bwd(q, k, v, do2, rtot, rs_token)
    rs_done = _to_chips_wait("grad_scatter_wait", rs_send, rs_recv, rs_flight, n_early, rs_src, rs_land, dq)
    early_slots = rs_done[n_early:]
    dp = jnp.concatenate([dp01, dq[:T], dk[:T], dv[:T], dpc, dpa], axis=1)
    du1 = _mm_nt_cols("mm_in_dx", dp, win3, tm)
    g_win = _mm_tn_cols("mm_in_dw", u1, dp, tc, P)

    in_rh = g_win.shape[1] // 2
    s2_src, s2_dst = _swap_ends(["cols"], [in_rh])
    s2_send, s2_recv, s2_flight, s2_token = _to_chips_start(
        "grad_swap_in_start", [g_win, lax.empty((P, in_rh, g_win.shape[-1]), F32)], P, s2_src, s2_dst, du1,
        to_sibling=True)
    (dh0,), (g_pre_mix,) = _rowwise_bwd("rms_pre_mix_bwd", f_rms_id, [(h0, D, 0)],
                                        [(pre_mix_g + s2_token[0:1, 0:1], D, 0)],
                                        [(du1, D, 0), (dh1, D, 0)], [F32], T, ts)
    grad_x = dh0[N_META:L][None]

    small_shapes = [(1, D), (1, D), (1, D), (CONV_WIDTH, D), (1, D), (1, D), (1, D), (1, D), (1, D), (1, D), (N_META, D)]
    small = _pack([g_pre_mix, g_gate_c, g_gate_a, g_taps, g_dw_b, g_ln_g, g_ln_b, g_post_mix, g_pre_ffn, g_post_ffn,
                   dh0[:N_META]])
    summed = _sum_slots("small_grad_sum", _gather_all(small))
    (s_pre_mix, s_gate_c, s_gate_a, s_taps, s_dw_b, s_ln_g, s_ln_b, s_post_mix, s_pre_ffn, s_post_ffn,
     s_meta) = _unpack(summed, small_shapes)
    s_gate_b = jnp.concatenate([s_gate_c, s_gate_a], axis=1)
    s_taps = lax.dynamic_slice_in_dim(s_taps, me * Dc, Dc, axis=1)[None]
    s_meta = lax.dynamic_slice_in_dim(s_meta, me * Dc, Dc, axis=1)

    s2_done = _to_chips_wait("grad_swap_in_wait", s2_send, s2_recv, s2_flight, P, s2_src, s2_dst, summed, to_sibling=True)
    win_pair = _pair_add_bf16("grad_pair_add_0", s2_done[0], s2_done[1], "cols", c_arr, me_arr)
    in_src, in_dst, in_land = _rs_ends(1)
    in_send, in_recv, in_flight, in_token = _to_chips_start(
        "grad_scatter_in_start", list(win_pair), 1, in_src, in_dst, win_pair[1])
    g_early = _join_halves("grad_join_halves_early", [_sum_slots("grad_chip_sum_%d" % (n + 1), s, c_arr)
                                                      for n, s in enumerate(early_slots)], in_token)

    grads = {
        "meta_tokens": s_meta, "pre_mix_g": s_pre_mix, "gate_b": s_gate_b, "dw_w": s_taps,
        "dw_b": s_dw_b, "conv_ln_g": s_ln_g, "conv_ln_b": s_ln_b, "w_conv_out": g_early[0][None],
        "w_attn_out": g_early[1][None], "w_o": g_early[2][None], "post_mix_g": s_post_mix, "pre_ffn_g": s_pre_ffn,
        "w_ffn_in": g_early[3][None], "w_ffn_out": g_early[4][None], "post_ffn_g": s_post_ffn,
    }
    weights = {
        "meta_tokens": (meta_tokens, m_meta_tokens, v_meta_tokens), "pre_mix_g": (pre_mix_g, m_pre_mix_g, v_pre_mix_g),
        "w_in": (w_in, m_w_in, v_w_in), "gate_b": (gate_b, m_gate_b, v_gate_b), "dw_w": (dw_w, m_dw_w, v_dw_w),
        "dw_b": (dw_b, m_dw_b, v_dw_b), "conv_ln_g": (conv_ln_g, m_conv_ln_g, v_conv_ln_g),
        "conv_ln_b": (conv_ln_b, m_conv_ln_b, v_conv_ln_b), "w_conv_out": (w_conv_out, m_w_conv_out, v_w_conv_out),
        "w_attn_out": (w_attn_out, m_w_attn_out, v_w_attn_out), "w_o": (w_o, m_w_o, v_w_o),
        "post_mix_g": (post_mix_g, m_post_mix_g, v_post_mix_g), "pre_ffn_g": (pre_ffn_g, m_pre_ffn_g, v_pre_ffn_g),
        "w_ffn_in": (w_ffn_in, m_w_ffn_in, v_w_ffn_in), "w_ffn_out": (w_ffn_out, m_w_ffn_out, v_w_ffn_out),
        "post_ffn_g": (post_ffn_g, m_post_ffn_g, v_post_ffn_g),
    }
    names = list(weights)
    big_names = ["w_in", "w_conv_out", "w_attn_out", "w_o", "w_ffn_in", "w_ffn_out"]
    small_names = [n for n in names if n not in big_names]

    delta, new_m, new_v = {}, {}, {}

    def big_update(n):
        w, m, v2 = weights[n]
        d, nm, nv = _adamw("adamw_" + n, w[0], grads[n][0], m[0], v2[0])
        delta[n], new_m[n], new_v[n] = d[None], nm[None], nv[None]

    for n in big_names[1:]:
        big_update(n)
    shapes = [weights[n][0].shape for n in small_names]
    packed = [_pack([weights[n][k] for n in small_names]) for k in range(3)]
    d, nm, nv = _adamw("adamw_small", packed[0], _pack([grads[n] for n in small_names]), packed[1], packed[2])
    for n, dd, mm, vv in zip(small_names, _unpack(d, shapes), _unpack(nm, shapes), _unpack(nv, shapes)):
        delta[n], new_m[n], new_v[n] = dd, mm, vv

    in_done = _to_chips_wait("grad_scatter_in_wait", in_send, in_recv, in_flight, 1, in_src, in_land,
                             [d] + [delta[n] for n in big_names[1:]])
    (g_in,) = _join_halves("grad_join_halves_in", [_sum_slots("grad_chip_sum_0", in_done[1], c_arr)])
    grads["w_in"] = g_in[None]
    big_update("w_in")

    return (loss, grad_x, *[grads[n].reshape(weights[n][0].shape) for n in names], *[delta[n] for n in names],
            *[new_m[n] for n in names], *[new_v[n] for n in names])
```

```python
import math

import jax
import jax.numpy as jnp
from jax import lax
from jax.experimental import pallas as pl
from jax.experimental.pallas import tpu as pltpu

F32 = jnp.float32
BF16 = jnp.bfloat16

N_META = 16
CONV_WIDTH = 31
CONV_PAD = 32
HEAD_DIM = 64
RMS_EPS = 1e-6
LN_EPS = 1e-5
ROW_BLOCK = 128
MXU_WIDTH = 256
ATT_BLOCK = MXU_WIDTH
ATT_HEADS = 8
ATT_HEADS_BWD = 4
LANES = 128
N_CHIPS = 4
N_DEV = 8
MM_ROWS = 544
CONTRACT_ROWS = 2176
STAGE_ROWS = 272
WIDE_STAGE_ROWS = 128
VMEM_LIMIT = 56 * 1024 * 1024

ADAM_LR = 0.001
ADAM_B1 = 0.9
ADAM_B2 = 0.999
ADAM_EPS = 1e-08
ADAM_WD = 0.01
ADAM_STEP = 10

MESH = pl.DeviceIdType.MESH
ANY = pl.BlockSpec(memory_space=pl.ANY)
HBM = pl.BlockSpec(memory_space=pltpu.HBM)
SEM = pl.BlockSpec(memory_space=pltpu.SEMAPHORE)
EFFECT = pltpu.SideEffectType.DATAFLOW_SIDE_EFFECTING


def _params(*sem):
    return pltpu.CompilerParams(dimension_semantics=sem if sem else None, vmem_limit_bytes=VMEM_LIMIT)


def _rms(x, g):
    return x * lax.rsqrt(jnp.mean(x * x, axis=-1, keepdims=True) + RMS_EPS) * g


def f_rms(h, g):
    return (_rms(h, g),)


def f_rms_id(h, g):
    return (_rms(h, g), h)


def f_res_rms(h, m, g):
    return (h + _rms(m, g),)


def f_glu(a, gate):
    return (a * lax.logistic(gate),)


def f_convpost(yc, b, ln_g, ln_b):
    y = yc + b
    mu = jnp.mean(y, axis=-1, keepdims=True)
    xc = y - mu
    var = jnp.mean(xc * xc, axis=-1, keepdims=True)
    yl = xc * lax.rsqrt(var + LN_EPS) * ln_g + ln_b
    return (yl * lax.logistic(yl),)


def f_mix(pc, pa, yc, ya, bc, ba):
    return (lax.logistic(pc + bc) * yc + lax.logistic(pa + ba) * ya,)


def f_swiglu(a, b):
    return (a * lax.logistic(a) * b,)


def _tile(T, target):
    return max(t for t in range(16, target + 1, 16) if T % t == 0)


def _row_map(j):
    return lambda i: (i, j)


def _par_map(j):
    return lambda i: (0, j)


def _rowwise_fwd(name, f, rows, pars, outs, T, tm):
    n_in = len(rows) + len(pars)

    def body(*refs):
        vals = [r[...].astype(F32) for r in refs[:n_in]]
        res = f(*vals)
        for o_ref, o in zip(refs[n_in:], res):
            o_ref[...] = o.astype(o_ref.dtype)

    in_specs = [pl.BlockSpec((tm, w), _row_map(j)) for _, w, j in rows]
    in_specs += [pl.BlockSpec((1, w), _par_map(j)) for _, w, j in pars]
    return pl.pallas_call(
        body, name=name, grid=(T // tm,),
        in_specs=in_specs,
        out_specs=[pl.BlockSpec((tm, w), _row_map(0)) for w, _ in outs],
        out_shape=[jax.ShapeDtypeStruct((T, w), dt) for w, dt in outs],
        compiler_params=_params("parallel"),
    )(*[a for a, _, _ in rows], *[a for a, _, _ in pars])


def _rowwise_bwd(name, f, rows, pars, cots, drow_dtypes, T, tm, joined=False):
    n_r, n_p, n_c = len(rows), len(pars), len(cots)
    n_in = n_r + n_p + n_c
    keep = [k for k, dt in enumerate(drow_dtypes) if dt is not None]
    n_out = 1 if joined else len(keep)

    def body(*refs):
        rv = [r[...].astype(F32) for r in refs[:n_r]]
        pv = [r[...].astype(F32) for r in refs[n_r:n_r + n_p]]
        cv = [r[...].astype(F32) for r in refs[n_r + n_p:n_in]]
        _, vjp = jax.vjp(f, *rv, *pv)
        g = vjp(tuple(cv))
        drow_refs = refs[n_in:n_in + n_out]
        dpar_refs = refs[n_in + n_out:]
        if joined:
            at = 0
            for k in keep:
                drow_refs[0][:, at:at + rows[k][1]] = g[k].astype(drow_refs[0].dtype)
                at += rows[k][1]
        else:
            for r, k in zip(drow_refs, keep):
                r[...] = g[k].astype(r.dtype)

        @pl.when(pl.program_id(0) == 0)
        def _():
            for r in dpar_refs:
                r[...] = jnp.zeros_like(r)

        for r, gp in zip(dpar_refs, g[n_r:]):
            r[...] += gp

    in_specs = [pl.BlockSpec((tm, w), _row_map(j)) for _, w, j in rows]
    in_specs += [pl.BlockSpec((1, w), _par_map(j)) for _, w, j in pars]
    in_specs += [pl.BlockSpec((tm, w), _row_map(j)) for _, w, j in cots]
    widths = [sum(rows[k][1] for k in keep)] if joined else [rows[k][1] for k in keep]
    out_specs = [pl.BlockSpec((tm, w), _row_map(0)) for w in widths]
    out_specs += [pl.BlockSpec((1, w), _par_map(0)) for _, w, _ in pars]
    out_shape = [jax.ShapeDtypeStruct((T, w), drow_dtypes[k]) for w, k in zip(widths, keep)]
    out_shape += [jax.ShapeDtypeStruct((1, w), F32) for _, w, _ in pars]
    res = pl.pallas_call(
        body, name=name, grid=(T // tm,),
        in_specs=in_specs, out_specs=out_specs, out_shape=out_shape,
        compiler_params=_params("arbitrary"),
    )(*[a for a, _, _ in rows], *[a for a, _, _ in pars], *[a for a, _, _ in cots])
    return res[:n_out], res[n_out:]


NN = (((1,), (0,)), ((), ()))
NT = (((1,), (1,)), ((), ()))
TN = (((0,), (0,)), ((), ()))


def _mm(name, a, b, dims, out_shape, grid, a_spec, b_spec, o_spec, red_axis=None, init=None):
    n_red = None if red_axis is None else grid[red_axis]

    def body(a_ref, b_ref, *rest):
        o_ref = rest[-1]
        prod = lax.dot_general(a_ref[...], b_ref[...], dims, preferred_element_type=F32)
        if n_red is None:
            o_ref[...] = prod.astype(o_ref.dtype)
        else:
            @pl.when(pl.program_id(red_axis) == 0)
            def _():
                o_ref[...] = prod

            @pl.when(pl.program_id(red_axis) > 0)
            def _():
                o_ref[...] += prod

    sem = ["parallel"] * len(grid)
    if red_axis is not None:
        sem[red_axis] = "arbitrary"
    if init is None:
        return pl.pallas_call(
            body, name=name, grid=grid, in_specs=[a_spec, b_spec], out_specs=o_spec, out_shape=out_shape,
            compiler_params=_params(*sem),
        )(a, b)
    return pl.pallas_call(
        body, name=name, grid=grid, in_specs=[a_spec, b_spec, ANY], out_specs=o_spec, out_shape=out_shape,
        input_output_aliases={2: 0}, compiler_params=_params(*sem),
    )(a, b, init)


def _mm_nn(name, a, w, tm, out_dtype=F32, rows=None):
    T, K = a.shape
    T = rows or T
    N = w.shape[1]
    return _mm(name, a, w, NN, jax.ShapeDtypeStruct((T, N), out_dtype), (T // tm,),
               pl.BlockSpec((tm, K), lambda i: (i, 0)), pl.BlockSpec((K, N), lambda i: (0, 0)),
               pl.BlockSpec((tm, N), lambda i: (i, 0)))


def _mm_nt(name, a, w, tm, out_dtype=F32, out_rows=None):
    T, N = a.shape
    K = w.shape[0]
    init = None if out_rows is None else jnp.zeros((out_rows, K), out_dtype)
    return _mm(name, a, w, NT, jax.ShapeDtypeStruct((out_rows or T, K), out_dtype), (T // tm,),
               pl.BlockSpec((tm, N), lambda i: (i, 0)), pl.BlockSpec((K, N), lambda i: (0, 0)),
               pl.BlockSpec((tm, K), lambda i: (i, 0)), init=init)


def _mm_tn(name, a, b, tm, n_row_blocks):
    K = a.shape[1]
    T, N = b.shape
    kb = K // n_row_blocks
    return _mm(name, a, b, TN, jax.ShapeDtypeStruct((K, N), F32), (n_row_blocks, T // tm),
               pl.BlockSpec((tm, kb), lambda r, t: (t, r)), pl.BlockSpec((tm, N), lambda r, t: (t, 0)),
               pl.BlockSpec((kb, N), lambda r, t: (r, 0)), red_axis=1)


def _mm_nn_cols(name, a, w3, tm):
    T, K = a.shape
    P, _, Ns = w3.shape
    return _mm(name, a, w3, NN, jax.ShapeDtypeStruct((T, P * Ns), F32), (P, T // tm),
               pl.BlockSpec((tm, K), lambda p, i: (i, 0)), pl.BlockSpec((None, K, Ns), lambda p, i: (p, 0, 0)),
               pl.BlockSpec((tm, Ns), lambda p, i: (i, p)))


def _mm_nt_cols(name, a, w3, tm):
    T = a.shape[0]
    P, K, Ns = w3.shape
    return _mm(name, a, w3, NT, jax.ShapeDtypeStruct((T, K), F32), (T // tm, P),
               pl.BlockSpec((tm, Ns), lambda i, p: (i, p)), pl.BlockSpec((None, K, Ns), lambda i, p: (p, 0, 0)),
               pl.BlockSpec((tm, K), lambda i, p: (i, 0)), red_axis=1)


def _mm_tn_cols(name, a, b, tm, P):
    T, K = a.shape
    Ns = b.shape[1] // P
    return _mm(name, a, b, TN, jax.ShapeDtypeStruct((P, K, Ns), F32), (P, T // tm),
               pl.BlockSpec((tm, K), lambda p, t: (t, 0)), pl.BlockSpec((tm, Ns), lambda p, t: (t, p)),
               pl.BlockSpec((None, K, Ns), lambda p, t: (p, 0, 0)), red_axis=1)


def _tap_windows(win, off, tb):
    out = []
    for b in range(8):
        taps = [j for j in range(CONV_WIDTH) if (off + j) % 8 == b]
        if taps:
            shifted = win if b == 0 else pltpu.roll(win, win.shape[0] - b, axis=0)
            out += [(shifted, off + j - b, j) for j in taps]
    return out


def _shift_conv(name, x, w, place, off, T):
    C = x.shape[1]
    tb = ROW_BLOCK
    zero_at = 0 if place else T

    def body(x_ref, w_ref, o_ref, xp_ref):
        xp_ref[pl.ds(zero_at, CONV_PAD), :] = jnp.zeros((CONV_PAD, LANES), F32)
        xp_ref[pl.ds(T + CONV_PAD, 8), :] = jnp.zeros((8, LANES), F32)
        xp_ref[pl.ds(place, T), :] = x_ref[...]

        def step(t, carry):
            base = pl.multiple_of(t * tb, tb)
            win = xp_ref[pl.ds(base, tb + CONV_PAD + 8), :]
            acc = jnp.zeros((tb, LANES), F32)
            for shifted, at, j in _tap_windows(win, off, tb):
                acc = acc + shifted[at:at + tb, :] * w_ref[pl.ds(j, 1), :]
            o_ref[pl.ds(base, tb), :] = acc
            return carry

        lax.fori_loop(0, T // tb, step, 0)

    return pl.pallas_call(
        body, name=name, grid=(C // LANES,),
        in_specs=[pl.BlockSpec((T, LANES), lambda c: (0, c)), pl.BlockSpec((CONV_WIDTH, LANES), lambda c: (0, c))],
        out_specs=pl.BlockSpec((T, LANES), lambda c: (0, c)),
        out_shape=jax.ShapeDtypeStruct((T, C), F32),
        scratch_shapes=[pltpu.VMEM((T + CONV_PAD + 8, LANES), F32)],
        compiler_params=_params("parallel"),
    )(x, w)


def _conv_dw(name, x, dy, T):
    C = x.shape[1]
    tb = ROW_BLOCK
    off = CONV_PAD - (CONV_WIDTH - 1)

    def body(x_ref, dy_ref, o_ref, xp_ref, acc_ref):
        xp_ref[pl.ds(0, CONV_PAD), :] = jnp.zeros((CONV_PAD, LANES), F32)
        xp_ref[pl.ds(T + CONV_PAD, 8), :] = jnp.zeros((8, LANES), F32)
        xp_ref[pl.ds(CONV_PAD, T), :] = x_ref[...]
        acc_ref[...] = jnp.zeros_like(acc_ref)

        def step(t, carry):
            base = pl.multiple_of(t * tb, tb)
            win = xp_ref[pl.ds(base, tb + CONV_PAD + 8), :]
            d = dy_ref[pl.ds(base, tb), :]
            for shifted, at, j in _tap_windows(win, off, tb):
                prod = shifted[at:at + tb, :] * d
                acc_ref[j] += jnp.sum(prod.reshape(tb // 8, 8, LANES), axis=0)
            return carry

        lax.fori_loop(0, T // tb, step, 0)
        for j in range(CONV_WIDTH):
            o_ref[pl.ds(j, 1), :] = jnp.sum(acc_ref[j], axis=0, keepdims=True)

    return pl.pallas_call(
        body, name=name, grid=(C // LANES,),
        in_specs=[pl.BlockSpec((T, LANES), lambda c: (0, c)), pl.BlockSpec((T, LANES), lambda c: (0, c))],
        out_specs=pl.BlockSpec((CONV_WIDTH, LANES), lambda c: (0, c)),
        out_shape=jax.ShapeDtypeStruct((CONV_WIDTH, C), F32),
        scratch_shapes=[pltpu.VMEM((T + CONV_PAD + 8, LANES), F32), pltpu.VMEM((CONV_WIDTH, 8, LANES), F32)],
        compiler_params=_params("parallel"),
    )(x, dy)


def _dot(a, b, dims=NN):
    return lax.dot_general(a, b, dims, preferred_element_type=F32)


def _tri_cumsum(x, tri):
    return _dot(x.astype(BF16), tri)


def _qkv_split(p, D, T, Ta):
    tb = ROW_BLOCK
    nt = T // tb
    scale = 1.0 / math.sqrt(HEAD_DIM)

    def body(q_ref, k_ref, v_ref, qo_ref, ko_ref, vo_ref):
        live = pl.program_id(0) < nt
        qo_ref[...] = jnp.where(live, q_ref[...] * scale, 0.0).astype(BF16)
        ko_ref[...] = jnp.where(live, k_ref[...], 0.0).astype(BF16)
        vo_ref[...] = jnp.where(live, v_ref[...], 0.0).astype(BF16)

    def col(n):
        return lambda i: (jnp.minimum(i, nt - 1), n)

    return pl.pallas_call(
        body, name="qkv_split", grid=(Ta // tb,),
        in_specs=[pl.BlockSpec((tb, D), col(2 + n)) for n in range(3)],
        out_specs=[pl.BlockSpec((tb, D), lambda i: (i, 0))] * 3,
        out_shape=[jax.ShapeDtypeStruct((Ta, D), BF16)] * 3,
        compiler_params=_params("parallel"),
    )(p, p, p)


def _pair_lanes(g):
    return slice((g // 2) * LANES, (g // 2 + 1) * LANES)


def _stacked_pairs(x_ref, B, G):
    first = lax.broadcasted_iota(jnp.int32, (B, LANES), 1) < HEAD_DIM
    out = []
    for g in range(0, G, 2):
        x2 = x_ref[:, _pair_lanes(g)]
        zero = jnp.zeros_like(x2)
        out.append(jnp.concatenate([jnp.where(first, x2, zero), jnp.where(first, zero, x2)], axis=0))
    return first, out


def _attn_fwd(q, k, v):
    T, D = q.shape
    H = D // HEAD_DIM
    B = ATT_BLOCK
    G = ATT_HEADS
    NP = G // 2
    W = NP * LANES

    def body(q_ref, k_ref, v_ref, o_ref, rt_ref):
        i = pl.program_id(1)
        row = lax.broadcasted_iota(jnp.int32, (B, B), 0)
        col = lax.broadcasted_iota(jnp.int32, (B, B), 1)
        tri = (row >= col).astype(BF16)
        below = jnp.concatenate([col < row] * 2, axis=0)
        first, qp = _stacked_pairs(q_ref, B, G)

        def tile(j, carry, diagonal):
            sl = pl.ds(pl.multiple_of(j * B, B), B)
            zs, sps = [], []
            for p in range(NP):
                z = _dot(qp[p], k_ref[sl, _pair_lanes(2 * p)], NT)
                sp = jnp.maximum(z, 0.0) + jnp.log(1.0 + jnp.exp(-jnp.abs(z)))
                if diagonal:
                    sp = jnp.where(below, sp, 0.0)
                zs.append(z)
                sps.append(sp)
            rws = _tri_cumsum(jnp.concatenate(sps, axis=0), tri)
            out = []
            for p in range(NP):
                c, acc = carry[p]
                rw = rws[2 * B * p:2 * B * (p + 1)]
                a = jnp.exp(zs[p] - (rw + c))
                if diagonal:
                    a = jnp.where(below, a, 0.0)
                acc = acc + _dot(a.astype(BF16), v_ref[sl, _pair_lanes(2 * p)])
                out.append((c + rw[:, 0:1], acc))
            return tuple(out)

        carry = tile(i, tuple((jnp.zeros((2 * B, 1), F32), jnp.zeros((2 * B, LANES), F32)) for _ in range(NP)), True)
        carry = lax.fori_loop(0, i, lambda jj, cr: tile(i - 1 - jj, cr, False), carry)
        for p in range(NP):
            c, acc = carry[p]
            o_ref[:, _pair_lanes(2 * p)] = jnp.where(first, acc[:B], acc[B:]).astype(o_ref.dtype)
            rt_ref[2 * p] = c[:B]
            rt_ref[2 * p + 1] = c[B:]

    return pl.pallas_call(
        body, name="attn_fwd", grid=(H // G, T // B),
        in_specs=[pl.BlockSpec((B, W), lambda h, i: (i, h)),
                  pl.BlockSpec((T, W), lambda h, i: (0, h)),
                  pl.BlockSpec((T, W), lambda h, i: (0, h))],
        out_specs=[pl.BlockSpec((B, W), lambda h, i: (i, h)),
                   pl.BlockSpec((G, B, 1), lambda h, i: (h, i, 0))],
        out_shape=[jax.ShapeDtypeStruct((T, D), BF16), jax.ShapeDtypeStruct((H, T, 1), F32)],
        compiler_params=_params("parallel", "arbitrary"),
    )(q, k, v)


def _attn_bwd(q, k, v, do, rt, after):
    T, D = q.shape
    H = D // HEAD_DIM
    B = ATT_BLOCK
    nq = T // B
    scale = 1.0 / math.sqrt(HEAD_DIM)
    G = ATT_HEADS_BWD
    NP = G // 2
    W = NP * LANES

    def body(q_ref, k_ref, v_ref, do_ref, rt_ref, after_ref, dq_ref, dk_ref, dv_ref, dk_acc, dv_acc):
        i = pl.program_id(1)

        @pl.when(i == 0)
        def _():
            dk_acc[...] = jnp.zeros_like(dk_acc)
            dv_acc[...] = jnp.zeros_like(dv_acc)

        row = lax.broadcasted_iota(jnp.int32, (B, B), 0)
        col = lax.broadcasted_iota(jnp.int32, (B, B), 1)
        tri = (row <= col).astype(BF16)
        below = jnp.concatenate([col < row] * 2, axis=0)
        first, qp = _stacked_pairs(q_ref, B, G)
        _, dop = _stacked_pairs(do_ref, B, G)
        rtp = [jnp.concatenate([rt_ref[2 * p], rt_ref[2 * p + 1]], axis=0) for p in range(NP)]

        def tile(j, carry, diagonal):
            sl = pl.ds(pl.multiple_of(j * B, B), B)
            zs, sps, sgs = [], [], []
            for p in range(NP):
                z = _dot(qp[p], k_ref[sl, _pair_lanes(2 * p)], NT)
                e = jnp.exp(-jnp.abs(z))
                inv = 1.0 / (1.0 + e)
                sp = jnp.maximum(z, 0.0) - jnp.log(inv)
                if diagonal:
                    sp = jnp.where(below, sp, 0.0)
                zs.append(z)
                sps.append(sp)
                sgs.append(jnp.where(z >= 0.0, inv, e * inv))
            pws = _tri_cumsum(jnp.concatenate(sps, axis=0), tri)
            aas, gs = [], []
            for p in range(NP):
                pw = pws[2 * B * p:2 * B * (p + 1)]
                a = jnp.exp(zs[p] - (rtp[p] - carry[p][0] - pw + sps[p]))
                if diagonal:
                    a = jnp.where(below, a, 0.0)
                aas.append(a.astype(BF16))
                gs.append(a * _dot(dop[p], v_ref[sl, _pair_lanes(2 * p)], NT))
            gws = _tri_cumsum(jnp.concatenate(gs, axis=0), tri)
            out = []
            for p in range(NP):
                pc, gc, dq = carry[p]
                pw, gw = pws[2 * B * p:2 * B * (p + 1)], gws[2 * B * p:2 * B * (p + 1)]
                dz = gs[p] - sgs[p] * (gc + gw)
                if diagonal:
                    dz = jnp.where(below, dz, 0.0)
                dzb = dz.astype(BF16)
                dq = dq + _dot(dzb, k_ref[sl, _pair_lanes(2 * p)])
                dk_acc[sl, _pair_lanes(2 * p)] += _dot(dzb, qp[p], TN)
                dv_acc[sl, _pair_lanes(2 * p)] += _dot(aas[p], dop[p], TN)
                out.append((pc + pw[:, B - 1:B], gc + gw[:, B - 1:B], dq))
            return tuple(out)

        zero = jnp.zeros((2 * B, 1), F32)
        carry = lax.fori_loop(0, i, lambda j, cr: tile(j, cr, False),
                              tuple((zero, zero, jnp.zeros((2 * B, LANES), F32)) for _ in range(NP)))
        carry = tile(i, carry, True)
        for p in range(NP):
            dq = carry[p][2]
            dq_ref[:, _pair_lanes(2 * p)] = (jnp.where(first, dq[:B], dq[B:]) * scale).astype(dq_ref.dtype)

        @pl.when(i == nq - 1)
        def _():
            dk_ref[...] = dk_acc[...].astype(dk_ref.dtype)
            dv_ref[...] = dv_acc[...].astype(dv_ref.dtype)

    blk = pl.BlockSpec((B, W), lambda h, i: (i, h))
    full = pl.BlockSpec((T, W), lambda h, i: (0, h))
    return pl.pallas_call(
        body, name="attn_bwd", grid=(H // G, nq),
        in_specs=[blk, full, full, blk, pl.BlockSpec((G, B, 1), lambda h, i: (h, i, 0)), ANY],
        out_specs=[blk, full, full],
        out_shape=[jax.ShapeDtypeStruct((T, D), BF16)] * 3,
        scratch_shapes=[pltpu.VMEM((T, W), F32)] * 2,
        compiler_params=_params("parallel", "arbitrary"),
    )(q, k, v, do, rt, after)


def _loss_head(y, target, tm):
    S, D = y.shape

    def body(y_ref, t_ref, dy_ref, part_ref):
        err = y_ref[...] - t_ref[...]
        dy_ref[...] = err * (1.0 / D)

        @pl.when(pl.program_id(0) == 0)
        def _():
            part_ref[...] = jnp.zeros_like(part_ref)

        part_ref[...] += jnp.sum(err * err, axis=0, keepdims=True)

    spec = pl.BlockSpec((tm, D), lambda i: (i, 0))
    return pl.pallas_call(
        body, name="loss_head", grid=(S // tm,), in_specs=[spec, spec],
        out_specs=[spec, pl.BlockSpec((1, D), lambda i: (0, 0))],
        out_shape=[jax.ShapeDtypeStruct((S, D), F32), jax.ShapeDtypeStruct((1, D), F32)],
        compiler_params=_params("arbitrary"),
    )(y, target)


def _row_tile(R):
    for t in (256, 128, 64, 32, 16, 8):
        if R % t == 0:
            return t
    return R


def _pair_add_bf16(name, g, b1, kind, c_arr, me_arr=None):
    P, Rh, C = b1.shape
    tr = _row_tile(Rh)
    nb = Rh // tr
    own = me_arr is not None

    def body(*refs):
        g_ref, b_ref, o_ref = refs[1 + own:4 + own]
        val = (g_ref[...] + b_ref[...]).astype(o_ref.dtype)
        o_ref[...] = val
        if own:
            @pl.when(pl.program_id(1) == refs[1][0])
            def _():
                refs[-1][...] = val

    if kind == "cols":
        g_spec = pl.BlockSpec((None, tr, C), lambda i, p, c_ref, *_: (p, c_ref[0] * nb + i, 0))
    else:
        g_spec = pl.BlockSpec((tr, C), lambda i, p, c_ref, *_: ((2 * p + c_ref[0]) * nb + i, 0))
    blk = pl.BlockSpec((None, tr, C), lambda i, p, *_: (p, i, 0))
    shape = jax.ShapeDtypeStruct((P, Rh, C), BF16)
    if not own:
        return pl.pallas_call(
            body, name=name,
            grid_spec=pltpu.PrefetchScalarGridSpec(num_scalar_prefetch=1, grid=(nb, P), in_specs=[g_spec, blk], out_specs=blk),
            out_shape=shape, compiler_params=_params("parallel", "parallel"),
        )(c_arr, g, b1)
    mine = pl.BlockSpec((None, tr, C), lambda i, p, c_ref, me_ref: (me_ref[0], i, 0))
    return pl.pallas_call(
        body, name=name,
        grid_spec=pltpu.PrefetchScalarGridSpec(num_scalar_prefetch=2, grid=(nb, P), in_specs=[g_spec, blk], out_specs=[blk, mine]),
        out_shape=[shape, shape], compiler_params=_params("parallel", "arbitrary"),
    )(c_arr, me_arr, g, b1)


def _sum_slots(name, b, half_arr=None):
    P, R, C = b.shape
    tr = _row_tile(R)
    nb = R // tr

    def body(*refs):
        b_ref, o_ref = refs[-2:]
        acc = b_ref[0].astype(F32)
        for s in range(1, P):
            acc = acc + b_ref[s].astype(F32)
        o_ref[...] = acc

    if half_arr is None:
        return pl.pallas_call(
            body, name=name, grid=(nb,),
            in_specs=[pl.BlockSpec((P, tr, C), lambda i: (0, i, 0))],
            out_specs=pl.BlockSpec((tr, C), lambda i: (i, 0)),
            out_shape=jax.ShapeDtypeStruct((R, C), F32),
            compiler_params=_params("parallel"),
        )(b)
    return pl.pallas_call(
        body, name=name,
        grid_spec=pltpu.PrefetchScalarGridSpec(
            num_scalar_prefetch=1, grid=(nb,),
            in_specs=[pl.BlockSpec((P, tr, C), lambda i, half: (0, i, 0))],
            out_specs=pl.BlockSpec((tr, C), lambda i, half: (half[0] * nb + i, 0))),
        out_shape=jax.ShapeDtypeStruct((2 * R, C), F32),
        compiler_params=_params("parallel"),
    )(half_arr, b)


def _adamw(name, w, g, m, v):
    R, C = w.shape
    tr = _row_tile(R)
    c1 = 1.0 - ADAM_B1 ** ADAM_STEP
    c2 = 1.0 - ADAM_B2 ** ADAM_STEP

    def body(w_ref, g_ref, m_ref, v_ref, d_ref, nm_ref, nv_ref):
        gg = g_ref[...]
        nm = ADAM_B1 * m_ref[...] + (1.0 - ADAM_B1) * gg
        nv = ADAM_B2 * v_ref[...] + (1.0 - ADAM_B2) * (gg * gg)
        m_hat = nm / c1
        v_hat = nv / c2
        d_ref[...] = -ADAM_LR * (m_hat / (jnp.sqrt(v_hat) + ADAM_EPS) + ADAM_WD * w_ref[...])
        nm_ref[...] = nm
        nv_ref[...] = nv

    spec = pl.BlockSpec((tr, C), lambda i: (i, 0))
    return pl.pallas_call(
        body, name=name, grid=(R // tr,), in_specs=[spec] * 4, out_specs=[spec] * 3,
        out_shape=[jax.ShapeDtypeStruct((R, C), F32)] * 3,
        compiler_params=_params("parallel"),
    )(w, g, m, v)


def _place():
    x, y, c = lax.axis_index("x"), lax.axis_index("y"), lax.axis_index("c")
    other_chips = [(1 - x, y), (x, 1 - y), (1 - x, 1 - y)]
    return x, y, c, other_chips


def _into_slot(name, w, dtype, slot_arr, n_slots):
    R, C = w.shape
    tr = _row_tile(R)

    def body(slot_ref, w_ref, o_ref):
        o_ref[...] = w_ref[...].astype(o_ref.dtype)

    return pl.pallas_call(
        body, name=name,
        grid_spec=pltpu.PrefetchScalarGridSpec(
            num_scalar_prefetch=1, grid=(R // tr,),
            in_specs=[pl.BlockSpec((tr, C), lambda i, slot: (i, 0))],
            out_specs=pl.BlockSpec((None, tr, C), lambda i, slot: (slot[0], i, 0))),
        out_shape=jax.ShapeDtypeStruct((n_slots, R, C), dtype),
        compiler_params=_params("parallel"),
    )(slot_arr, w)


def _gather_chips(bufs):
    n = len(bufs)

    def body(*refs):
        outs = refs[n:2 * n]
        ici_send, ici_recv, d2d_send, d2d_recv = refs[2 * n:]
        x, y, c, chips = _place()
        me = 2 * x + y
        started = []
        for k in range(n):
            rh = outs[k].shape[1] // 2
            mine = outs[k].at[me, pl.ds(c * rh, rh)]
            for j, (px, py) in enumerate(chips):
                cp = pltpu.make_async_remote_copy(
                    src_ref=mine, dst_ref=mine,
                    send_sem=ici_send.at[3 * k + j], recv_sem=ici_recv.at[3 * k + j],
                    device_id=(px, py, c), device_id_type=MESH)
                cp.start()
                started.append(cp)
        for k in range(n):
            rh = outs[k].shape[1] // 2
            for j, (px, py) in enumerate(chips):
                landed = outs[k].at[2 * px + py, pl.ds(c * rh, rh)]
                pltpu.make_async_remote_copy(
                    src_ref=landed, dst_ref=landed,
                    send_sem=ici_send.at[3 * k + j], recv_sem=ici_recv.at[3 * k + j],
                    device_id=(px, py, c), device_id_type=MESH).wait_recv()
                cp = pltpu.make_async_remote_copy(
                    src_ref=landed, dst_ref=landed,
                    send_sem=d2d_send.at[3 * k + j], recv_sem=d2d_recv.at[3 * k + j],
                    device_id=(x, y, 1 - c), device_id_type=MESH)
                cp.start()
                started.append(cp)
        for k in range(n):
            rh = outs[k].shape[1] // 2
            for j, (px, py) in enumerate(chips):
                landed = outs[k].at[2 * px + py, pl.ds((1 - c) * rh, rh)]
                pltpu.make_async_remote_copy(
                    src_ref=landed, dst_ref=landed,
                    send_sem=d2d_send.at[3 * k + j], recv_sem=d2d_recv.at[3 * k + j],
                    device_id=(x, y, 1 - c), device_id_type=MESH).wait_recv()
        for cp in started:
            cp.wait_send()

    return pl.pallas_call(
        body, name="gather_weights",
        in_specs=[ANY] * n, out_specs=[ANY] * n,
        out_shape=[jax.ShapeDtypeStruct(b.shape, b.dtype) for b in bufs],
        input_output_aliases={k: k for k in range(n)},
        scratch_shapes=[pltpu.SemaphoreType.DMA((3 * n,))] * 4,
        compiler_params=pltpu.CompilerParams(has_side_effects=True),
    )(*bufs)


def _gather_forward(bufs):
    n = len(bufs)

    def body(*refs):
        outs = refs[n:2 * n]
        d2d_send, d2d_recv = refs[2 * n:]
        x, y, c, chips = _place()
        started = []
        for k in range(n):
            rh = outs[k].shape[1] // 2
            for j, (px, py) in enumerate(chips):
                landed = outs[k].at[2 * px + py, pl.ds(c * rh, rh)]
                cp = pltpu.make_async_remote_copy(
                    src_ref=landed, dst_ref=landed,
                    send_sem=d2d_send.at[3 * k + j], recv_sem=d2d_recv.at[3 * k + j],
                    device_id=(x, y, 1 - c), device_id_type=MESH)
                cp.start()
                started.append(cp)
        for k in range(n):
            rh = outs[k].shape[1] // 2
            for j, (px, py) in enumerate(chips):
                landed = outs[k].at[2 * px + py, pl.ds((1 - c) * rh, rh)]
                pltpu.make_async_remote_copy(
                    src_ref=landed, dst_ref=landed,
                    send_sem=d2d_send.at[3 * k + j], recv_sem=d2d_recv.at[3 * k + j],
                    device_id=(x, y, 1 - c), device_id_type=MESH).wait_recv()
        for cp in started:
            cp.wait_send()

    return pl.pallas_call(
        body, name="gather_rest_forward",
        in_specs=[ANY] * n, out_specs=[ANY] * n,
        out_shape=[jax.ShapeDtypeStruct(b.shape, b.dtype) for b in bufs],
        input_output_aliases={k: k for k in range(n)},
        scratch_shapes=[pltpu.SemaphoreType.DMA((3 * n,))] * 2,
        compiler_params=pltpu.CompilerParams(has_side_effects=True),
    )(*bufs)


def _to_chips_start(name, arrays, n, src_fn, dst_fn, after, to_sibling=False):
    m = len(arrays)
    n_peers = 1 if to_sibling else N_CHIPS - 1

    def body(*refs):
        send_sem, recv_sem = refs[m + 1], refs[m + 2]
        thru = refs[m + 3:2 * m + 3]
        token = refs[2 * m + 3]
        x, y, c, chips = _place()
        peers = [(x, y, 1 - c)] if to_sibling else [(px, py, c) for px, py in chips]
        for k in range(n):
            for j, (px, py, pc) in enumerate(peers):
                pltpu.make_async_remote_copy(
                    src_ref=src_fn(thru, k, px, py, x, y, c), dst_ref=dst_fn(thru, k, px, py, x, y, c),
                    send_sem=send_sem.at[n_peers * k + j], recv_sem=recv_sem.at[n_peers * k + j],
                    device_id=(px, py, pc), device_id_type=MESH).start()
        token[...] = jnp.zeros_like(token)

    res = pl.pallas_call(
        body, name=name,
        out_shape=(pltpu.SemaphoreType.DMA((n_peers * n,)), pltpu.SemaphoreType.DMA((n_peers * n,)),
                   *[pltpu.HBM(a.shape, a.dtype) for a in arrays], jax.ShapeDtypeStruct((8, LANES), F32)),
        in_specs=[HBM] * m + [ANY],
        out_specs=(SEM, SEM, *[HBM] * m, pl.BlockSpec(memory_space=pltpu.VMEM)),
        input_output_aliases={i: i + 2 for i in range(m)},
        compiler_params=pltpu.CompilerParams(has_side_effects=EFFECT),
    )(*[pltpu.with_memory_space_constraint(a, pltpu.HBM) for a in arrays], after)
    return res[0], res[1], list(res[2:2 + m]), res[2 + m]


def _to_chips_wait(name, send_sem, recv_sem, arrays, n, src_fn, land_fn, after, to_sibling=False):
    m = len(arrays)
    after = list(after) if isinstance(after, (list, tuple)) else [after]
    n_peers = 1 if to_sibling else N_CHIPS - 1

    def body(*refs):
        send, recv = refs[m], refs[m + 1]
        outs = refs[m + 2 + len(after):]
        x, y, c, chips = _place()
        peers = [(x, y, 1 - c)] if to_sibling else [(px, py, c) for px, py in chips]
        for k in range(n):
            for j, (px, py, pc) in enumerate(peers):
                cp = pltpu.make_async_remote_copy(
                    src_ref=src_fn(outs, k, px, py, x, y, c), dst_ref=land_fn(outs, k, px, py, x, y, c),
                    send_sem=send.at[n_peers * k + j], recv_sem=recv.at[n_peers * k + j],
                    device_id=(px, py, pc), device_id_type=MESH)
                cp.wait_send()
                cp.wait_recv()

    res = pl.pallas_call(
        body, name=name,
        out_shape=[pltpu.HBM(a.shape, a.dtype) for a in arrays],
        in_specs=[HBM] * m + [SEM, SEM] + [ANY] * len(after), out_specs=[HBM] * m,
        input_output_aliases={i: i for i in range(m)},
        compiler_params=pltpu.CompilerParams(has_side_effects=EFFECT),
    )(*arrays, send_sem, recv_sem, *after)
    return list(res)


def _slot_half(refs, k, chip, c):
    rh = refs[k].shape[1] // 2
    return refs[k].at[chip, pl.ds(c * rh, rh)]


def _ag_mine(refs, k, px, py, x, y, c):
    return _slot_half(refs, k, 2 * x + y, c)


def _ag_theirs(refs, k, px, py, x, y, c):
    return _slot_half(refs, k, 2 * px + py, c)


def _rs_ends(n):
    def src(refs, k, px, py, x, y, c):
        return refs[k].at[2 * px + py]

    def dst(refs, k, px, py, x, y, c):
        return refs[n + k].at[2 * x + y]

    def land(refs, k, px, py, x, y, c):
        return refs[n + k].at[2 * px + py]

    return src, dst, land


def _half(ref, kind, p, c, rh):
    if kind == "cols":
        return ref.at[p, pl.ds(c * rh, rh)]
    return ref.at[pl.ds((2 * p + c) * rh, rh)]


def _swap_ends(kinds, rhs):
    n = len(kinds)

    def src(refs, k, px, py, x, y, c):
        return _half(refs[k // N_CHIPS], kinds[k // N_CHIPS], k % N_CHIPS, 1 - c, rhs[k // N_CHIPS])

    def dst(refs, k, px, py, x, y, c):
        return refs[n + k // N_CHIPS].at[k % N_CHIPS]

    return src, dst


def _join_halves(name, fulls, after=None):
    n = len(fulls)
    extra = [] if after is None else [after]

    def body(*refs):
        outs = refs[n + len(extra):2 * n + len(extra)]
        send_sem, recv_sem = refs[2 * n + len(extra):]
        x, y, c, _ = _place()
        started = []
        for k in range(n):
            rh = outs[k].shape[0] // 2
            mine = outs[k].at[pl.ds(c * rh, rh)]
            cp = pltpu.make_async_remote_copy(
                src_ref=mine, dst_ref=mine, send_sem=send_sem.at[k], recv_sem=recv_sem.at[k],
                device_id=(x, y, 1 - c), device_id_type=MESH)
            cp.start()
            started.append(cp)
        for k in range(n):
            rh = outs[k].shape[0] // 2
            theirs = outs[k].at[pl.ds((1 - c) * rh, rh)]
            pltpu.make_async_remote_copy(
                src_ref=theirs, dst_ref=theirs, send_sem=send_sem.at[k], recv_sem=recv_sem.at[k],
                device_id=(x, y, 1 - c), device_id_type=MESH).wait_recv()
        for cp in started:
            cp.wait_send()

    return pl.pallas_call(
        body, name=name,
        in_specs=[ANY] * (n + len(extra)), out_specs=[ANY] * n,
        out_shape=[jax.ShapeDtypeStruct(f.shape, f.dtype) for f in fulls],
        input_output_aliases={k: k for k in range(n)},
        scratch_shapes=[pltpu.SemaphoreType.DMA((n,))] * 2,
        compiler_params=pltpu.CompilerParams(has_side_effects=True),
    )(*fulls, *extra)


def _gather_all(block):
    def body(in_ref, out_ref, send_sem, recv_sem, local_sem):
        x, y, c, _ = _place()

        def slot(px, py, pc):
            return out_ref.at[4 * px + 2 * py + pc]

        loc = pltpu.make_async_copy(in_ref, slot(x, y, c), local_sem)
        loc.start()
        started = []
        for d in range(1, N_DEV):
            fx, fy, fc = d >> 2, (d >> 1) & 1, d & 1
            cp = pltpu.make_async_remote_copy(
                src_ref=in_ref, dst_ref=slot(x, y, c), send_sem=send_sem.at[d - 1], recv_sem=recv_sem.at[d - 1],
                device_id=(x ^ fx, y ^ fy, c ^ fc), device_id_type=MESH)
            cp.start()
            started.append(cp)
        for d in range(1, N_DEV):
            fx, fy, fc = d >> 2, (d >> 1) & 1, d & 1
            landed = slot(x ^ fx, y ^ fy, c ^ fc)
            pltpu.make_async_remote_copy(
                src_ref=in_ref, dst_ref=landed, send_sem=send_sem.at[d - 1], recv_sem=recv_sem.at[d - 1],
                device_id=(x ^ fx, y ^ fy, c ^ fc), device_id_type=MESH).wait_recv()
        for cp in started:
            cp.wait_send()
        loc.wait()

    return pl.pallas_call(
        body, name="gather_small_grads",
        in_specs=[ANY], out_specs=ANY,
        out_shape=jax.ShapeDtypeStruct((N_DEV,) + block.shape, block.dtype),
        scratch_shapes=[pltpu.SemaphoreType.DMA((N_DEV - 1,))] * 2 + [pltpu.SemaphoreType.DMA(())],
        compiler_params=pltpu.CompilerParams(has_side_effects=True),
    )(block)


def _pack(pieces):
    flat = jnp.concatenate([p.reshape(-1) for p in pieces])
    n = flat.shape[0]
    padded = -(-n // (8 * LANES)) * (8 * LANES)
    return jnp.pad(flat, (0, padded - n)).reshape(-1, LANES)


def _unpack(packed, shapes):
    flat = packed.reshape(-1)
    out, at = [], 0
    for s in shapes:
        n = math.prod(s)
        out.append(flat[at:at + n].reshape(s))
        at += n
    return out


def kernel(x, meta_tokens, pre_mix_g, w_in, gate_b, dw_w, dw_b, conv_ln_g, conv_ln_b, w_conv_out, w_attn_out, w_o, post_mix_g, pre_ffn_g, w_ffn_in, w_ffn_out, post_ffn_g, loss_target, m_meta_tokens, m_pre_mix_g, m_w_in, m_gate_b, m_dw_w, m_dw_b, m_conv_ln_g, m_conv_ln_b, m_w_conv_out, m_w_attn_out, m_w_o, m_post_mix_g, m_pre_ffn_g, m_w_ffn_in, m_w_ffn_out, m_post_ffn_g, v_meta_tokens, v_pre_mix_g, v_w_in, v_gate_b, v_dw_w, v_dw_b, v_conv_ln_g, v_conv_ln_b, v_w_conv_out, v_w_attn_out, v_w_o, v_post_mix_g, v_pre_ffn_g, v_w_ffn_in, v_w_ffn_out, v_post_ffn_g):
    S, D = x.shape[1], x.shape[2]
    L = S + N_META
    T = -(-L // ROW_BLOCK) * ROW_BLOCK
    Ta = -(-L // ATT_BLOCK) * ATT_BLOCK
    tm = _tile(T, MM_ROWS)
    tc = _tile(T, CONTRACT_ROWS)
    ts = _tile(T, STAGE_ROWS)
    tw = _tile(T, WIDE_STAGE_ROWS)
    H = D // HEAD_DIM
    F = w_ffn_out.shape[1] * N_CHIPS
    Dc = D // N_CHIPS
    P = N_CHIPS
    me = 2 * lax.axis_index("x") + lax.axis_index("y")
    c_arr = lax.axis_index("c").astype(jnp.int32).reshape(1)

    dw_w_pad = jnp.pad(dw_w[0], ((0, CONV_PAD - CONV_WIDTH), (0, 0)))
    me_arr = me.astype(jnp.int32).reshape(1)
    to_gather = [("w_in", w_in[0], BF16), ("w_conv_out", w_conv_out[0], BF16), ("w_attn_out", w_attn_out[0], BF16),
                 ("w_o", w_o[0], BF16), ("w_ffn_in", w_ffn_in[0], BF16), ("w_ffn_out", w_ffn_out[0], BF16),
                 ("meta", meta_tokens, F32), ("taps", dw_w_pad, F32)]
    slot = {n: _into_slot("slot_" + n, w, dt, me_arr, P) for n, w, dt in to_gather}
    win3, meta4, taps4 = _gather_chips([slot["w_in"], slot["meta"], slot["taps"]])
    meta_full = meta4.transpose(1, 0, 2).reshape(N_META, D)
    taps = taps4.transpose(1, 0, 2).reshape(CONV_PAD, D)[:CONV_WIDTH]
    later = ["w_conv_out", "w_attn_out", "w_o", "w_ffn_in", "w_ffn_out"]
    ag_send, ag_recv, in_flight, ag_token = _to_chips_start(
        "gather_rest_start", [slot[n] for n in later], len(later), _ag_mine, _ag_mine, meta4)

    h0 = jnp.concatenate([meta_full, x[0], jnp.zeros((T - L, D), F32)], axis=0)
    (u1,) = _rowwise_fwd("rms_pre_mix", f_rms, [(h0, D, 0)], [(pre_mix_g + ag_token[0:1, 0:1], D, 0)], [(D, BF16)], T, ts)
    p = _mm_nn_cols("mm_in", u1, win3, tm)
    q, k, v = _qkv_split(p, D, T, Ta)
    o2, rtot = _attn_fwd(q, k, v)
    landed = _to_chips_wait("gather_rest_wait", ag_send, ag_recv, in_flight, len(later), _ag_mine, _ag_theirs, o2)
    wco4, wao4, wo4, wfi3, wfo4 = _gather_forward(landed)
    wco, wao, wo = (w.reshape(D, D) for w in (wco4, wao4, wo4))
    wfo = wfo4.reshape(F, D)
    (uglu,) = _rowwise_fwd("glu", f_glu, [(p, D, 0), (p, D, 1)], [], [(D, F32)], T, ts)
    yc = _shift_conv("dwconv", uglu, taps, CONV_PAD, CONV_PAD - (CONV_WIDTH - 1), T)
    conv_pars = [(dw_b, D, 0), (conv_ln_g, D, 0), (conv_ln_b, D, 0)]
    (ys,) = _rowwise_fwd("conv_post", f_convpost, [(yc, D, 0)], conv_pars, [(D, BF16)], T, ts)
    y_conv = _mm_nn("mm_conv_out", ys, wco, tm)
    y_attn = _mm_nn("mm_attn_out", o2, wao, tm, rows=T)
    mix_rows = [(p, D, 5), (p, D, 6), (y_conv, D, 0), (y_attn, D, 0)]
    mix_pars = [(gate_b, D, 0), (gate_b, D, 1)]
    (mixin,) = _rowwise_fwd("gate_mix", f_mix, mix_rows, mix_pars, [(D, BF16)], T, ts)
    mix = _mm_nn("mm_o", mixin, wo, tm)
    (h1,) = _rowwise_fwd("res_post_mix", f_res_rms, [(h0, D, 0), (mix, D, 0)], [(post_mix_g, D, 0)], [(D, F32)], T, ts)
    (u2,) = _rowwise_fwd("rms_pre_ffn", f_rms, [(h1, D, 0)], [(pre_ffn_g, D, 0)], [(D, BF16)], T, ts)
    ab = _mm_nn_cols("mm_ffn_in", u2, wfi3, tm)
    (fin,) = _rowwise_fwd("swiglu", f_swiglu, [(ab, F, 0), (ab, F, 1)], [], [(F, BF16)], T, tw)
    f = _mm_nn("mm_ffn_out", fin, wfo, tm)
    (h2,) = _rowwise_fwd("res_post_ffn", f_res_rms, [(h1, D, 0), (f, D, 0)], [(post_ffn_g, D, 0)], [(D, F32)], T, ts)

    dy, part = _loss_head(h2[N_META:L], loss_target[0], _row_tile(S))
    loss = lax.psum(0.5 * jnp.sum(part) / D, ("x", "y", "c"))
    dh2 = jnp.pad(dy, ((N_META, T - L), (0, 0)))

    (df,), (g_post_ffn,) = _rowwise_bwd("res_post_ffn_bwd", f_res_rms, [(h1, D, 0), (f, D, 0)], [(post_ffn_g, D, 0)],
                                        [(dh2, D, 0)], [None, BF16], T, ts)
    dfin = _mm_nt("mm_ffn_out_dx", df, wfo, tm)
    g_wfo = _mm_tn("mm_ffn_out_dw", fin, df, tc, F // MXU_WIDTH)
    (dab,), _ = _rowwise_bwd("swiglu_bwd", f_swiglu, [(ab, F, 0), (ab, F, 1)], [], [(dfin, F, 0)], [BF16, BF16], T, tw,
                             joined=True)
    du2 = _mm_nt_cols("mm_ffn_in_dx", dab, wfi3, _tile(T, 2 * MM_ROWS))
    g_wfi = _mm_tn_cols("mm_ffn_in_dw", u2, dab, tc, P)
    (dh1,), (g_pre_ffn,) = _rowwise_bwd("rms_pre_ffn_bwd", f_rms_id, [(h1, D, 0)], [(pre_ffn_g, D, 0)],
                                        [(du2, D, 0), (dh2, D, 0)], [F32], T, ts)
    (dmix,), (g_post_mix,) = _rowwise_bwd("res_post_mix_bwd", f_res_rms, [(h0, D, 0), (mix, D, 0)], [(post_mix_g, D, 0)],
                                          [(dh1, D, 0)], [None, BF16], T, ts)
    dmixin = _mm_nt("mm_o_dx", dmix, wo, tm)
    g_wo = _mm_tn("mm_o_dw", mixin, dmix, tc, D // MXU_WIDTH)
    (dpc, dpa, dyconv, dyattn), (g_gate_c, g_gate_a) = _rowwise_bwd(
        "gate_mix_bwd", f_mix, mix_rows, mix_pars, [(dmixin, D, 0)], [BF16, BF16, BF16, BF16], T, ts)
    g_wco = _mm_tn("mm_conv_out_dw", ys, dyconv, tc, D // MXU_WIDTH)
    dys = _mm_nt("mm_conv_out_dx", dyconv, wco, tm)
    g_wao = _mm_tn("mm_attn_out_dw", o2, dyattn, tc, D // MXU_WIDTH)
    do2 = _mm_nt("mm_attn_out_dx", dyattn, wao, tm, BF16, out_rows=Ta)

    early = [g_wco, g_wao, g_wo, g_wfi, g_wfo]
    n_early = len(early)
    early_kinds = ["rows", "rows", "rows", "cols", "rows"]
    early_rhs = [(g.shape[1] if kind == "cols" else g.shape[0] // P) // 2 for g, kind in zip(early, early_kinds)]
    sw_src, sw_dst = _swap_ends(early_kinds, early_rhs)
    sw_send, sw_recv, sw_flight, sw_token = _to_chips_start(
        "grad_swap_early_start", early + [lax.empty((P, rh, g.shape[-1]), F32) for g, rh in zip(early, early_rhs)],
        P * n_early, sw_src, sw_dst, do2, to_sibling=True)
    (dyc,), (g_dw_b, g_ln_g, g_ln_b) = _rowwise_bwd(
        "conv_post_bwd", f_convpost, [(yc, D, 0)], [(dw_b + sw_token[0:1, 0:1], D, 0)] + conv_pars[1:],
        [(dys, D, 0)], [F32], T, ts)
    duglu = _shift_conv("dwconv_dx", dyc, taps[::-1], 0, 0, T)
    g_taps = _conv_dw("dwconv_dw", uglu, dyc, T)
    (dp01,), _ = _rowwise_bwd("glu_bwd", f_glu, [(p, D, 0), (p, D, 1)], [], [(duglu, D, 0)], [BF16, BF16], T, ts,
                              joined=True)
    sw_done = _to_chips_wait("grad_swap_early_wait", sw_send, sw_recv, sw_flight, P * n_early, sw_src, sw_dst,
                             [dp01, g_taps], to_sibling=True)
    early_pairs = [_pair_add_bf16("grad_pair_add_%d" % (n + 1), g, b1, kind, c_arr, me_arr)
                   for n, (g, b1, kind) in enumerate(zip(sw_done[:n_early], sw_done[n_early:], early_kinds))]
    rs_src, rs_dst, rs_land = _rs_ends(n_early)
    rs_send, rs_recv, rs_flight, rs_token = _to_chips_start(
        "grad_scatter_start", [pr[0] for pr in early_pairs] + [pr[1] for pr in early_pairs], n_early,
        rs_src, rs_dst, early_pairs[-1][1])
    dq, dk, dv = _attn_bwd(q, k, v, do2, rtot, rs_token)
    rs_done = _to_chips_wait("grad_scatter_wait", rs_send, rs_recv, rs_flight, n_early, rs_src, rs_land, dq)
    early_slots = rs_done[n_early:]
    dp = jnp.concatenate([dp01, dq[:T], dk[:T], dv[:T], dpc, dpa], axis=1)
    du1 = _mm_nt_cols("mm_in_dx", dp, win3, _tile(T, 2 * MM_ROWS))
    g_win = _mm_tn_cols("mm_in_dw", u1, dp, tc, P)

    in_rh = g_win.shape[1] // 2
    s2_src, s2_dst = _swap_ends(["cols"], [in_rh])
    s2_send, s2_recv, s2_flight, s2_token = _to_chips_start(
        "grad_swap_in_start", [g_win, lax.empty((P, in_rh, g_win.shape[-1]), F32)], P, s2_src, s2_dst, du1,
        to_sibling=True)
    (dh0,), (g_pre_mix,) = _rowwise_bwd("rms_pre_mix_bwd", f_rms_id, [(h0, D, 0)],
                                        [(pre_mix_g + s2_token[0:1, 0:1], D, 0)],
                                        [(du1, D, 0), (dh1, D, 0)], [F32], T, ts)
    grad_x = dh0[N_META:L][None]

    small_shapes = [(1, D), (1, D), (1, D), (CONV_WIDTH, D), (1, D), (1, D), (1, D), (1, D), (1, D), (1, D), (N_META, D)]
    small = _pack([g_pre_mix, g_gate_c, g_gate_a, g_taps, g_dw_b, g_ln_g, g_ln_b, g_post_mix, g_pre_ffn, g_post_ffn,
                   dh0[:N_META]])
    summed = _sum_slots("small_grad_sum", _gather_all(small))
    (s_pre_mix, s_gate_c, s_gate_a, s_taps, s_dw_b, s_ln_g, s_ln_b, s_post_mix, s_pre_ffn, s_post_ffn,
     s_meta) = _unpack(summed, small_shapes)
    s_gate_b = jnp.concatenate([s_gate_c, s_gate_a], axis=1)
    s_taps = lax.dynamic_slice_in_dim(s_taps, me * Dc, Dc, axis=1)[None]
    s_meta = lax.dynamic_slice_in_dim(s_meta, me * Dc, Dc, axis=1)

    s2_done = _to_chips_wait("grad_swap_in_wait", s2_send, s2_recv, s2_flight, P, s2_src, s2_dst, summed, to_sibling=True)
    win_pair = _pair_add_bf16("grad_pair_add_0", s2_done[0], s2_done[1], "cols", c_arr, me_arr)
    in_src, in_dst, in_land = _rs_ends(1)
    in_send, in_recv, in_flight, in_token = _to_chips_start(
        "grad_scatter_in_start", list(win_pair), 1, in_src, in_dst, win_pair[1])
    g_early = _join_halves("grad_join_halves_early", [_sum_slots("grad_chip_sum_%d" % (n + 1), s, c_arr)
                                                      for n, s in enumerate(early_slots)], in_token)

    grads = {
        "meta_tokens": s_meta, "pre_mix_g": s_pre_mix, "gate_b": s_gate_b, "dw_w": s_taps,
        "dw_b": s_dw_b, "conv_ln_g": s_ln_g, "conv_ln_b": s_ln_b, "w_conv_out": g_early[0][None],
        "w_attn_out": g_early[1][None], "w_o": g_early[2][None], "post_mix_g": s_post_mix, "pre_ffn_g": s_pre_ffn,
        "w_ffn_in": g_early[3][None], "w_ffn_out": g_early[4][None], "post_ffn_g": s_post_ffn,
    }
    weights = {
        "meta_tokens": (meta_tokens, m_meta_tokens, v_meta_tokens), "pre_mix_g": (pre_mix_g, m_pre_mix_g, v_pre_mix_g),
        "w_in": (w_in, m_w_in, v_w_in), "gate_b": (gate_b, m_gate_b, v_gate_b), "dw_w": (dw_w, m_dw_w, v_dw_w),
        "dw_b": (dw_b, m_dw_b, v_dw_b), "conv_ln_g": (conv_ln_g, m_conv_ln_g, v_conv_ln_g),
        "conv_ln_b": (conv_ln_b, m_conv_ln_b, v_conv_ln_b), "w_conv_out": (w_conv_out, m_w_conv_out, v_w_conv_out),
        "w_attn_out": (w_attn_out, m_w_attn_out, v_w_attn_out), "w_o": (w_o, m_w_o, v_w_o),
        "post_mix_g": (post_mix_g, m_post_mix_g, v_post_mix_g), "pre_ffn_g": (pre_ffn_g, m_pre_ffn_g, v_pre_ffn_g),
        "w_ffn_in": (w_ffn_in, m_w_ffn_in, v_w_ffn_in), "w_ffn_out": (w_ffn_out, m_w_ffn_out, v_w_ffn_out),
        "post_ffn_g": (post_ffn_g, m_post_ffn_g, v_post_ffn_g),
    }
    names = list(weights)
    big_names = ["w_in", "w_conv_out", "w_attn_out", "w_o", "w_ffn_in", "w_ffn_out"]
    small_names = [n for n in names if n not in big_names]

    delta, new_m, new_v = {}, {}, {}

    def big_update(n):
        w, m, v2 = weights[n]
        d, nm, nv = _adamw("adamw_" + n, w[0], grads[n][0], m[0], v2[0])
        delta[n], new_m[n], new_v[n] = d[None], nm[None], nv[None]

    for n in big_names[1:]:
        big_update(n)
    shapes = [weights[n][0].shape for n in small_names]
    packed = [_pack([weights[n][k] for n in small_names]) for k in range(3)]
    d, nm, nv = _adamw("adamw_small", packed[0], _pack([grads[n] for n in small_names]), packed[1], packed[2])
    for n, dd, mm, vv in zip(small_names, _unpack(d, shapes), _unpack(nm, shapes), _unpack(nv, shapes)):
        delta[n], new_m[n], new_v[n] = dd, mm, vv

    in_done = _to_chips_wait("grad_scatter_in_wait", in_send, in_recv, in_flight, 1, in_src, in_land,
                             [d] + [delta[n] for n in big_names[1:]])
    (g_in,) = _join_halves("grad_join_halves_in", [_sum_slots("grad_chip_sum_0", in_done[1], c_arr)])
    grads["w_in"] = g_in[None]
    big_update("w_in")

    return (loss, grad_x, *[grads[n].reshape(weights[n][0].shape) for n in names], *[delta[n] for n in names],
            *[new_m[n] for n in names], *[new_v[n] for n in names])
```

```python
import math

import jax
import jax.numpy as jnp
from jax import lax
from jax.experimental import pallas as pl
from jax.experimental.pallas import tpu as pltpu

F32 = jnp.float32
BF16 = jnp.bfloat16

N_META = 16
CONV_WIDTH = 31
CONV_PAD = 32
HEAD_DIM = 64
RMS_EPS = 1e-6
LN_EPS = 1e-5
ROW_BLOCK = 128
MXU_WIDTH = 256
ATT_BLOCK = MXU_WIDTH
ATT_HEADS = 8
ATT_HEADS_BWD = 4
LANES = 128
N_CHIPS = 4
N_DEV = 8
MM_ROWS = 544
CONTRACT_ROWS = 2176
STAGE_ROWS = 272
WIDE_STAGE_ROWS = 128
VMEM_LIMIT = 56 * 1024 * 1024

ADAM_LR = 0.001
ADAM_B1 = 0.9
ADAM_B2 = 0.999
ADAM_EPS = 1e-08
ADAM_WD = 0.01
ADAM_STEP = 10

MESH = pl.DeviceIdType.MESH
ANY = pl.BlockSpec(memory_space=pl.ANY)
HBM = pl.BlockSpec(memory_space=pltpu.HBM)
SEM = pl.BlockSpec(memory_space=pltpu.SEMAPHORE)
EFFECT = pltpu.SideEffectType.DATAFLOW_SIDE_EFFECTING


def _params(*sem):
    return pltpu.CompilerParams(dimension_semantics=sem if sem else None, vmem_limit_bytes=VMEM_LIMIT)


def _rms(x, g):
    return x * lax.rsqrt(jnp.mean(x * x, axis=-1, keepdims=True) + RMS_EPS) * g


def f_rms(h, g):
    return (_rms(h, g),)


def f_rms_id(h, g):
    return (_rms(h, g), h)


def f_res_rms(h, m, g):
    return (h + _rms(m, g),)


def f_glu(a, gate):
    return (a * lax.logistic(gate),)


def f_convpost(yc, b, ln_g, ln_b):
    y = yc + b
    mu = jnp.mean(y, axis=-1, keepdims=True)
    xc = y - mu
    var = jnp.mean(xc * xc, axis=-1, keepdims=True)
    yl = xc * lax.rsqrt(var + LN_EPS) * ln_g + ln_b
    return (yl * lax.logistic(yl),)


def f_mix(pc, pa, yc, ya, bc, ba):
    return (lax.logistic(pc + bc) * yc + lax.logistic(pa + ba) * ya,)


def f_swiglu(a, b):
    return (a * lax.logistic(a) * b,)


def _tile(T, target):
    return max(t for t in range(16, target + 1, 16) if T % t == 0)


def _row_map(j):
    return lambda i: (i, j)


def _par_map(j):
    return lambda i: (0, j)


def _rowwise_fwd(name, f, rows, pars, outs, T, tm):
    n_in = len(rows) + len(pars)

    def body(*refs):
        vals = [r[...].astype(F32) for r in refs[:n_in]]
        res = f(*vals)
        for o_ref, o in zip(refs[n_in:], res):
            o_ref[...] = o.astype(o_ref.dtype)

    in_specs = [pl.BlockSpec((tm, w), _row_map(j)) for _, w, j in rows]
    in_specs += [pl.BlockSpec((1, w), _par_map(j)) for _, w, j in pars]
    return pl.pallas_call(
        body, name=name, grid=(T // tm,),
        in_specs=in_specs,
        out_specs=[pl.BlockSpec((tm, w), _row_map(0)) for w, _ in outs],
        out_shape=[jax.ShapeDtypeStruct((T, w), dt) for w, dt in outs],
        compiler_params=_params("parallel"),
    )(*[a for a, _, _ in rows], *[a for a, _, _ in pars])


def _rowwise_bwd(name, f, rows, pars, cots, drow_dtypes, T, tm, joined=False):
    n_r, n_p, n_c = len(rows), len(pars), len(cots)
    n_in = n_r + n_p + n_c
    keep = [k for k, dt in enumerate(drow_dtypes) if dt is not None]
    n_out = 1 if joined else len(keep)

    def body(*refs):
        rv = [r[...].astype(F32) for r in refs[:n_r]]
        pv = [r[...].astype(F32) for r in refs[n_r:n_r + n_p]]
        cv = [r[...].astype(F32) for r in refs[n_r + n_p:n_in]]
        _, vjp = jax.vjp(f, *rv, *pv)
        g = vjp(tuple(cv))
        drow_refs = refs[n_in:n_in + n_out]
        dpar_refs = refs[n_in + n_out:]
        if joined:
            at = 0
            for k in keep:
                drow_refs[0][:, at:at + rows[k][1]] = g[k].astype(drow_refs[0].dtype)
                at += rows[k][1]
        else:
            for r, k in zip(drow_refs, keep):
                r[...] = g[k].astype(r.dtype)

        @pl.when(pl.program_id(0) == 0)
        def _():
            for r in dpar_refs:
                r[...] = jnp.zeros_like(r)

        for r, gp in zip(dpar_refs, g[n_r:]):
            r[...] += gp

    in_specs = [pl.BlockSpec((tm, w), _row_map(j)) for _, w, j in rows]
    in_specs += [pl.BlockSpec((1, w), _par_map(j)) for _, w, j in pars]
    in_specs += [pl.BlockSpec((tm, w), _row_map(j)) for _, w, j in cots]
    widths = [sum(rows[k][1] for k in keep)] if joined else [rows[k][1] for k in keep]
    out_specs = [pl.BlockSpec((tm, w), _row_map(0)) for w in widths]
    out_specs += [pl.BlockSpec((1, w), _par_map(0)) for _, w, _ in pars]
    out_shape = [jax.ShapeDtypeStruct((T, w), drow_dtypes[k]) for w, k in zip(widths, keep)]
    out_shape += [jax.ShapeDtypeStruct((1, w), F32) for _, w, _ in pars]
    res = pl.pallas_call(
        body, name=name, grid=(T // tm,),
        in_specs=in_specs, out_specs=out_specs, out_shape=out_shape,
        compiler_params=_params("arbitrary"),
    )(*[a for a, _, _ in rows], *[a for a, _, _ in pars], *[a for a, _, _ in cots])
    return res[:n_out], res[n_out:]


NN = (((1,), (0,)), ((), ()))
NT = (((1,), (1,)), ((), ()))
TN = (((0,), (0,)), ((), ()))


def _mm(name, a, b, dims, out_shape, grid, a_spec, b_spec, o_spec, red_axis=None, init=None):
    n_red = None if red_axis is None else grid[red_axis]

    def body(a_ref, b_ref, *rest):
        o_ref = rest[-1]
        prod = lax.dot_general(a_ref[...], b_ref[...], dims, preferred_element_type=F32)
        if n_red is None:
            o_ref[...] = prod.astype(o_ref.dtype)
        else:
            @pl.when(pl.program_id(red_axis) == 0)
            def _():
                o_ref[...] = prod

            @pl.when(pl.program_id(red_axis) > 0)
            def _():
                o_ref[...] += prod

    sem = ["parallel"] * len(grid)
    if red_axis is not None:
        sem[red_axis] = "arbitrary"
    if init is None:
        return pl.pallas_call(
            body, name=name, grid=grid, in_specs=[a_spec, b_spec], out_specs=o_spec, out_shape=out_shape,
            compiler_params=_params(*sem),
        )(a, b)
    return pl.pallas_call(
        body, name=name, grid=grid, in_specs=[a_spec, b_spec, ANY], out_specs=o_spec, out_shape=out_shape,
        input_output_aliases={2: 0}, compiler_params=_params(*sem),
    )(a, b, init)


def _mm_nn(name, a, w, tm, out_dtype=F32, rows=None):
    T, K = a.shape
    T = rows or T
    N = w.shape[1]
    return _mm(name, a, w, NN, jax.ShapeDtypeStruct((T, N), out_dtype), (T // tm,),
               pl.BlockSpec((tm, K), lambda i: (i, 0)), pl.BlockSpec((K, N), lambda i: (0, 0)),
               pl.BlockSpec((tm, N), lambda i: (i, 0)))


def _mm_nt(name, a, w, tm, out_dtype=F32, out_rows=None):
    T, N = a.shape
    K = w.shape[0]
    init = None if out_rows is None else jnp.zeros((out_rows, K), out_dtype)
    return _mm(name, a, w, NT, jax.ShapeDtypeStruct((out_rows or T, K), out_dtype), (T // tm,),
               pl.BlockSpec((tm, N), lambda i: (i, 0)), pl.BlockSpec((K, N), lambda i: (0, 0)),
               pl.BlockSpec((tm, K), lambda i: (i, 0)), init=init)


def _mm_tn(name, a, b, tm, n_row_blocks):
    K = a.shape[1]
    T, N = b.shape
    kb = K // n_row_blocks
    return _mm(name, a, b, TN, jax.ShapeDtypeStruct((K, N), F32), (n_row_blocks, T // tm),
               pl.BlockSpec((tm, kb), lambda r, t: (t, r)), pl.BlockSpec((tm, N), lambda r, t: (t, 0)),
               pl.BlockSpec((kb, N), lambda r, t: (r, 0)), red_axis=1)


def _mm_nn_cols(name, a, w3, tm):
    T, K = a.shape
    P, _, Ns = w3.shape
    return _mm(name, a, w3, NN, jax.ShapeDtypeStruct((T, P * Ns), F32), (P, T // tm),
               pl.BlockSpec((tm, K), lambda p, i: (i, 0)), pl.BlockSpec((None, K, Ns), lambda p, i: (p, 0, 0)),
               pl.BlockSpec((tm, Ns), lambda p, i: (i, p)))


def _mm_nt_cols(name, a, w3, tm):
    T = a.shape[0]
    P, K, Ns = w3.shape
    return _mm(name, a, w3, NT, jax.ShapeDtypeStruct((T, K), F32), (T // tm, P),
               pl.BlockSpec((tm, Ns), lambda i, p: (i, p)), pl.BlockSpec((None, K, Ns), lambda i, p: (p, 0, 0)),
               pl.BlockSpec((tm, K), lambda i, p: (i, 0)), red_axis=1)


def _mm_tn_cols(name, a, b, tm, P):
    T, K = a.shape
    Ns = b.shape[1] // P
    return _mm(name, a, b, TN, jax.ShapeDtypeStruct((P, K, Ns), F32), (P, T // tm),
               pl.BlockSpec((tm, K), lambda p, t: (t, 0)), pl.BlockSpec((tm, Ns), lambda p, t: (t, p)),
               pl.BlockSpec((None, K, Ns), lambda p, t: (p, 0, 0)), red_axis=1)


def _tap_windows(win, off, tb):
    out = []
    for b in range(8):
        taps = [j for j in range(CONV_WIDTH) if (off + j) % 8 == b]
        if taps:
            shifted = win if b == 0 else pltpu.roll(win, win.shape[0] - b, axis=0)
            out += [(shifted, off + j - b, j) for j in taps]
    return out


def _shift_conv(name, x, w, place, off, T):
    C = x.shape[1]
    tb = ROW_BLOCK
    zero_at = 0 if place else T

    def body(x_ref, w_ref, o_ref, xp_ref):
        xp_ref[pl.ds(zero_at, CONV_PAD), :] = jnp.zeros((CONV_PAD, LANES), F32)
        xp_ref[pl.ds(T + CONV_PAD, 8), :] = jnp.zeros((8, LANES), F32)
        xp_ref[pl.ds(place, T), :] = x_ref[...]

        def step(t, carry):
            base = pl.multiple_of(t * tb, tb)
            win = xp_ref[pl.ds(base, tb + CONV_PAD + 8), :]
            acc = jnp.zeros((tb, LANES), F32)
            for shifted, at, j in _tap_windows(win, off, tb):
                acc = acc + shifted[at:at + tb, :] * w_ref[pl.ds(j, 1), :]
            o_ref[pl.ds(base, tb), :] = acc
            return carry

        lax.fori_loop(0, T // tb, step, 0)

    return pl.pallas_call(
        body, name=name, grid=(C // LANES,),
        in_specs=[pl.BlockSpec((T, LANES), lambda c: (0, c)), pl.BlockSpec((CONV_WIDTH, LANES), lambda c: (0, c))],
        out_specs=pl.BlockSpec((T, LANES), lambda c: (0, c)),
        out_shape=jax.ShapeDtypeStruct((T, C), F32),
        scratch_shapes=[pltpu.VMEM((T + CONV_PAD + 8, LANES), F32)],
        compiler_params=_params("parallel"),
    )(x, w)


def _conv_dw(name, x, dy, T):
    C = x.shape[1]
    tb = ROW_BLOCK
    off = CONV_PAD - (CONV_WIDTH - 1)

    def body(x_ref, dy_ref, o_ref, xp_ref, acc_ref):
        xp_ref[pl.ds(0, CONV_PAD), :] = jnp.zeros((CONV_PAD, LANES), F32)
        xp_ref[pl.ds(T + CONV_PAD, 8), :] = jnp.zeros((8, LANES), F32)
        xp_ref[pl.ds(CONV_PAD, T), :] = x_ref[...]
        acc_ref[...] = jnp.zeros_like(acc_ref)

        def step(t, carry):
            base = pl.multiple_of(t * tb, tb)
            win = xp_ref[pl.ds(base, tb + CONV_PAD + 8), :]
            d = dy_ref[pl.ds(base, tb), :]
            for shifted, at, j in _tap_windows(win, off, tb):
                prod = shifted[at:at + tb, :] * d
                acc_ref[j] += jnp.sum(prod.reshape(tb // 8, 8, LANES), axis=0)
            return carry

        lax.fori_loop(0, T // tb, step, 0)
        for j in range(CONV_WIDTH):
            o_ref[pl.ds(j, 1), :] = jnp.sum(acc_ref[j], axis=0, keepdims=True)

    return pl.pallas_call(
        body, name=name, grid=(C // LANES,),
        in_specs=[pl.BlockSpec((T, LANES), lambda c: (0, c)), pl.BlockSpec((T, LANES), lambda c: (0, c))],
        out_specs=pl.BlockSpec((CONV_WIDTH, LANES), lambda c: (0, c)),
        out_shape=jax.ShapeDtypeStruct((CONV_WIDTH, C), F32),
        scratch_shapes=[pltpu.VMEM((T + CONV_PAD + 8, LANES), F32), pltpu.VMEM((CONV_WIDTH, 8, LANES), F32)],
        compiler_params=_params("parallel"),
    )(x, dy)


def _dot(a, b, dims=NN):
    return lax.dot_general(a, b, dims, preferred_element_type=F32)


def _tri_cumsum(x, tri):
    return _dot(x.astype(BF16), tri)


def _qkv_split(p, D, T, Ta):
    tb = ROW_BLOCK
    nt = T // tb
    scale = 1.0 / math.sqrt(HEAD_DIM)

    def body(q_ref, k_ref, v_ref, qo_ref, ko_ref, vo_ref):
        live = pl.program_id(0) < nt
        qo_ref[...] = jnp.where(live, q_ref[...] * scale, 0.0).astype(BF16)
        ko_ref[...] = jnp.where(live, k_ref[...], 0.0).astype(BF16)
        vo_ref[...] = jnp.where(live, v_ref[...], 0.0).astype(BF16)

    def col(n):
        return lambda i: (jnp.minimum(i, nt - 1), n)

    return pl.pallas_call(
        body, name="qkv_split", grid=(Ta // tb,),
        in_specs=[pl.BlockSpec((tb, D), col(2 + n)) for n in range(3)],
        out_specs=[pl.BlockSpec((tb, D), lambda i: (i, 0))] * 3,
        out_shape=[jax.ShapeDtypeStruct((Ta, D), BF16)] * 3,
        compiler_params=_params("parallel"),
    )(p, p, p)


def _pair_lanes(g):
    return slice((g // 2) * LANES, (g // 2 + 1) * LANES)


def _stacked_pairs(x_ref, B, G):
    first = lax.broadcasted_iota(jnp.int32, (B, LANES), 1) < HEAD_DIM
    out = []
    for g in range(0, G, 2):
        x2 = x_ref[:, _pair_lanes(g)]
        zero = jnp.zeros_like(x2)
        out.append(jnp.concatenate([jnp.where(first, x2, zero), jnp.where(first, zero, x2)], axis=0))
    return first, out


def _tail_rows(T, live):
    return -(-(live - (T // ATT_BLOCK - 1) * ATT_BLOCK) // 16) * 16


def _attn_fwd(q, k, v, live):
    T, D = q.shape
    H = D // HEAD_DIM
    B = ATT_BLOCK
    nq = T // B
    tail = _tail_rows(T, live)
    G = ATT_HEADS
    NP = G // 2
    W = NP * LANES

    def body(q_ref, k_ref, v_ref, o_ref, rt_ref):
        i = pl.program_id(1)
        row = lax.broadcasted_iota(jnp.int32, (B, B), 0)
        col = lax.broadcasted_iota(jnp.int32, (B, B), 1)
        tri = (row >= col).astype(BF16)
        first, q_pairs = _stacked_pairs(q_ref, B, G)

        def sweep(R):
            below = jnp.concatenate([(col < row)[:R]] * 2, axis=0)
            qp = [x if R == B else jnp.concatenate([x[:R], x[B:B + R]], axis=0) for x in q_pairs]

            def tile(j, carry, diagonal):
                sl = pl.ds(pl.multiple_of(j * B, B), B)
                zs, sps = [], []
                for p in range(NP):
                    z = _dot(qp[p], k_ref[sl, _pair_lanes(2 * p)], NT)
                    sp = jnp.maximum(z, 0.0) + jnp.log(1.0 + jnp.exp(-jnp.abs(z)))
                    if diagonal:
                        sp = jnp.where(below, sp, 0.0)
                    zs.append(z)
                    sps.append(sp)
                rws = _tri_cumsum(jnp.concatenate(sps, axis=0), tri)
                out = []
                for p in range(NP):
                    c, acc = carry[p]
                    rw = rws[2 * R * p:2 * R * (p + 1)]
                    a = jnp.exp(zs[p] - (rw + c))
                    if diagonal:
                        a = jnp.where(below, a, 0.0)
                    acc = acc + _dot(a.astype(BF16), v_ref[sl, _pair_lanes(2 * p)])
                    out.append((c + rw[:, 0:1], acc))
                return tuple(out)

            carry = tile(i, tuple((jnp.zeros((2 * R, 1), F32), jnp.zeros((2 * R, LANES), F32)) for _ in range(NP)), True)
            carry = lax.fori_loop(0, i, lambda jj, cr: tile(i - 1 - jj, cr, False), carry)
            for p in range(NP):
                c, acc = carry[p]
                o_ref[pl.ds(0, R), _pair_lanes(2 * p)] = jnp.where(first[:R], acc[:R], acc[R:]).astype(o_ref.dtype)
                rt_ref[2 * p, pl.ds(0, R), :] = c[:R]
                rt_ref[2 * p + 1, pl.ds(0, R), :] = c[R:]
                if R < B:
                    o_ref[pl.ds(R, B - R), _pair_lanes(2 * p)] = jnp.zeros((B - R, LANES), o_ref.dtype)
                    rt_ref[2 * p, pl.ds(R, B - R), :] = jnp.zeros((B - R, 1), F32)
                    rt_ref[2 * p + 1, pl.ds(R, B - R), :] = jnp.zeros((B - R, 1), F32)

        if tail == B:
            sweep(B)
        else:
            pl.when(i < nq - 1)(lambda: sweep(B))
            pl.when(i == nq - 1)(lambda: sweep(tail))

    return pl.pallas_call(
        body, name="attn_fwd", grid=(H // G, T // B),
        in_specs=[pl.BlockSpec((B, W), lambda h, i: (i, h)),
                  pl.BlockSpec((T, W), lambda h, i: (0, h)),
                  pl.BlockSpec((T, W), lambda h, i: (0, h))],
        out_specs=[pl.BlockSpec((B, W), lambda h, i: (i, h)),
                   pl.BlockSpec((G, B, 1), lambda h, i: (h, i, 0))],
        out_shape=[jax.ShapeDtypeStruct((T, D), BF16), jax.ShapeDtypeStruct((H, T, 1), F32)],
        compiler_params=_params("parallel", "arbitrary"),
    )(q, k, v)


def _attn_bwd(q, k, v, do, rt, after, live):
    T, D = q.shape
    H = D // HEAD_DIM
    B = ATT_BLOCK
    nq = T // B
    tail = _tail_rows(T, live)
    scale = 1.0 / math.sqrt(HEAD_DIM)
    G = ATT_HEADS_BWD
    NP = G // 2
    W = NP * LANES

    def body(q_ref, k_ref, v_ref, do_ref, rt_ref, after_ref, dq_ref, dk_ref, dv_ref, dk_acc, dv_acc):
        i = pl.program_id(1)

        @pl.when(i == 0)
        def _():
            dk_acc[...] = jnp.zeros_like(dk_acc)
            dv_acc[...] = jnp.zeros_like(dv_acc)

        row = lax.broadcasted_iota(jnp.int32, (B, B), 0)
        col = lax.broadcasted_iota(jnp.int32, (B, B), 1)
        tri = (row <= col).astype(BF16)
        first, q_pairs = _stacked_pairs(q_ref, B, G)
        _, do_pairs = _stacked_pairs(do_ref, B, G)

        def sweep(R):
            below = jnp.concatenate([(col < row)[:R]] * 2, axis=0)
            qp = [x if R == B else jnp.concatenate([x[:R], x[B:B + R]], axis=0) for x in q_pairs]
            dop = [x if R == B else jnp.concatenate([x[:R], x[B:B + R]], axis=0) for x in do_pairs]
            rtp = [jnp.concatenate([rt_ref[2 * p, pl.ds(0, R), :], rt_ref[2 * p + 1, pl.ds(0, R), :]], axis=0)
                   for p in range(NP)]

            def tile(j, carry, diagonal):
                sl = pl.ds(pl.multiple_of(j * B, B), B)
                zs, sps, sgs = [], [], []
                for p in range(NP):
                    z = _dot(qp[p], k_ref[sl, _pair_lanes(2 * p)], NT)
                    e = jnp.exp(-jnp.abs(z))
                    inv = 1.0 / (1.0 + e)
                    sp = jnp.maximum(z, 0.0) - jnp.log(inv)
                    if diagonal:
                        sp = jnp.where(below, sp, 0.0)
                    zs.append(z)
                    sps.append(sp)
                    sgs.append(jnp.where(z >= 0.0, inv, e * inv))
                pws = _tri_cumsum(jnp.concatenate(sps, axis=0), tri)
                aas, gs = [], []
                for p in range(NP):
                    pw = pws[2 * R * p:2 * R * (p + 1)]
                    a = jnp.exp(zs[p] - (rtp[p] - carry[p][0] - pw + sps[p]))
                    if diagonal:
                        a = jnp.where(below, a, 0.0)
                    aas.append(a.astype(BF16))
                    gs.append(a * _dot(dop[p], v_ref[sl, _pair_lanes(2 * p)], NT))
                gws = _tri_cumsum(jnp.concatenate(gs, axis=0), tri)
                out = []
                for p in range(NP):
                    pc, gc, dq = carry[p]
                    pw, gw = pws[2 * R * p:2 * R * (p + 1)], gws[2 * R * p:2 * R * (p + 1)]
                    dz = gs[p] - sgs[p] * (gc + gw)
                    if diagonal:
                        dz = jnp.where(below, dz, 0.0)
                    dzb = dz.astype(BF16)
                    dq = dq + _dot(dzb, k_ref[sl, _pair_lanes(2 * p)])
                    dk_acc[sl, _pair_lanes(2 * p)] += _dot(dzb, qp[p], TN)
                    dv_acc[sl, _pair_lanes(2 * p)] += _dot(aas[p], dop[p], TN)
                    out.append((pc + pw[:, B - 1:B], gc + gw[:, B - 1:B], dq))
                return tuple(out)

            zero = jnp.zeros((2 * R, 1), F32)
            carry = lax.fori_loop(0, i, lambda j, cr: tile(j, cr, False),
                                  tuple((zero, zero, jnp.zeros((2 * R, LANES), F32)) for _ in range(NP)))
            carry = tile(i, carry, True)
            for p in range(NP):
                dq = carry[p][2]
                dq_ref[pl.ds(0, R), _pair_lanes(2 * p)] = (jnp.where(first[:R], dq[:R], dq[R:]) * scale).astype(dq_ref.dtype)
                if R < B:
                    dq_ref[pl.ds(R, B - R), _pair_lanes(2 * p)] = jnp.zeros((B - R, LANES), dq_ref.dtype)

        if tail == B:
            sweep(B)
        else:
            pl.when(i < nq - 1)(lambda: sweep(B))
            pl.when(i == nq - 1)(lambda: sweep(tail))

        @pl.when(i == nq - 1)
        def _():
            dk_ref[...] = dk_acc[...].astype(dk_ref.dtype)
            dv_ref[...] = dv_acc[...].astype(dv_ref.dtype)

    blk = pl.BlockSpec((B, W), lambda h, i: (i, h))
    full = pl.BlockSpec((T, W), lambda h, i: (0, h))
    return pl.pallas_call(
        body, name="attn_bwd", grid=(H // G, nq),
        in_specs=[blk, full, full, blk, pl.BlockSpec((G, B, 1), lambda h, i: (h, i, 0)), ANY],
        out_specs=[blk, full, full],
        out_shape=[jax.ShapeDtypeStruct((T, D), BF16)] * 3,
        scratch_shapes=[pltpu.VMEM((T, W), F32)] * 2,
        compiler_params=_params("parallel", "arbitrary"),
    )(q, k, v, do, rt, after)


def _loss_head(y, target, tm):
    S, D = y.shape

    def body(y_ref, t_ref, dy_ref, part_ref):
        err = y_ref[...] - t_ref[...]
        dy_ref[...] = err * (1.0 / D)

        @pl.when(pl.program_id(0) == 0)
        def _():
            part_ref[...] = jnp.zeros_like(part_ref)

        part_ref[...] += jnp.sum(err * err, axis=0, keepdims=True)

    spec = pl.BlockSpec((tm, D), lambda i: (i, 0))
    return pl.pallas_call(
        body, name="loss_head", grid=(S // tm,), in_specs=[spec, spec],
        out_specs=[spec, pl.BlockSpec((1, D), lambda i: (0, 0))],
        out_shape=[jax.ShapeDtypeStruct((S, D), F32), jax.ShapeDtypeStruct((1, D), F32)],
        compiler_params=_params("arbitrary"),
    )(y, target)


def _row_tile(R):
    for t in (256, 128, 64, 32, 16, 8):
        if R % t == 0:
            return t
    return R


def _pair_add_bf16(name, g, b1, kind, c_arr, me_arr=None):
    P, Rh, C = b1.shape
    tr = _row_tile(Rh)
    nb = Rh // tr
    own = me_arr is not None

    def body(*refs):
        g_ref, b_ref, o_ref = refs[1 + own:4 + own]
        val = (g_ref[...] + b_ref[...]).astype(o_ref.dtype)
        o_ref[...] = val
        if own:
            @pl.when(pl.program_id(1) == refs[1][0])
            def _():
                refs[-1][...] = val

    if kind == "cols":
        g_spec = pl.BlockSpec((None, tr, C), lambda i, p, c_ref, *_: (p, c_ref[0] * nb + i, 0))
    else:
        g_spec = pl.BlockSpec((tr, C), lambda i, p, c_ref, *_: ((2 * p + c_ref[0]) * nb + i, 0))
    blk = pl.BlockSpec((None, tr, C), lambda i, p, *_: (p, i, 0))
    shape = jax.ShapeDtypeStruct((P, Rh, C), BF16)
    if not own:
        return pl.pallas_call(
            body, name=name,
            grid_spec=pltpu.PrefetchScalarGridSpec(num_scalar_prefetch=1, grid=(nb, P), in_specs=[g_spec, blk], out_specs=blk),
            out_shape=shape, compiler_params=_params("parallel", "parallel"),
        )(c_arr, g, b1)
    mine = pl.BlockSpec((None, tr, C), lambda i, p, c_ref, me_ref: (me_ref[0], i, 0))
    return pl.pallas_call(
        body, name=name,
        grid_spec=pltpu.PrefetchScalarGridSpec(num_scalar_prefetch=2, grid=(nb, P), in_specs=[g_spec, blk], out_specs=[blk, mine]),
        out_shape=[shape, shape], compiler_params=_params("parallel", "arbitrary"),
    )(c_arr, me_arr, g, b1)


def _sum_slots(name, b, half_arr=None):
    P, R, C = b.shape
    tr = _row_tile(R)
    nb = R // tr

    def body(*refs):
        b_ref, o_ref = refs[-2:]
        acc = b_ref[0].astype(F32)
        for s in range(1, P):
            acc = acc + b_ref[s].astype(F32)
        o_ref[...] = acc

    if half_arr is None:
        return pl.pallas_call(
            body, name=name, grid=(nb,),
            in_specs=[pl.BlockSpec((P, tr, C), lambda i: (0, i, 0))],
            out_specs=pl.BlockSpec((tr, C), lambda i: (i, 0)),
            out_shape=jax.ShapeDtypeStruct((R, C), F32),
            compiler_params=_params("parallel"),
        )(b)
    return pl.pallas_call(
        body, name=name,
        grid_spec=pltpu.PrefetchScalarGridSpec(
            num_scalar_prefetch=1, grid=(nb,),
            in_specs=[pl.BlockSpec((P, tr, C), lambda i, half: (0, i, 0))],
            out_specs=pl.BlockSpec((tr, C), lambda i, half: (half[0] * nb + i, 0))),
        out_shape=jax.ShapeDtypeStruct((2 * R, C), F32),
        compiler_params=_params("parallel"),
    )(half_arr, b)


def _adamw(name, w, g, m, v):
    R, C = w.shape
    tr = _row_tile(R)
    c1 = 1.0 - ADAM_B1 ** ADAM_STEP
    c2 = 1.0 - ADAM_B2 ** ADAM_STEP

    def body(w_ref, g_ref, m_ref, v_ref, d_ref, nm_ref, nv_ref):
        gg = g_ref[...]
        nm = ADAM_B1 * m_ref[...] + (1.0 - ADAM_B1) * gg
        nv = ADAM_B2 * v_ref[...] + (1.0 - ADAM_B2) * (gg * gg)
        m_hat = nm / c1
        v_hat = nv / c2
        d_ref[...] = -ADAM_LR * (m_hat / (jnp.sqrt(v_hat) + ADAM_EPS) + ADAM_WD * w_ref[...])
        nm_ref[...] = nm
        nv_ref[...] = nv

    spec = pl.BlockSpec((tr, C), lambda i: (i, 0))
    return pl.pallas_call(
        body, name=name, grid=(R // tr,), in_specs=[spec] * 4, out_specs=[spec] * 3,
        out_shape=[jax.ShapeDtypeStruct((R, C), F32)] * 3,
        compiler_params=_params("parallel"),
    )(w, g, m, v)


def _place():
    x, y, c = lax.axis_index("x"), lax.axis_index("y"), lax.axis_index("c")
    other_chips = [(1 - x, y), (x, 1 - y), (1 - x, 1 - y)]
    return x, y, c, other_chips


def _into_slot(name, w, dtype, slot_arr, n_slots):
    R, C = w.shape
    tr = _row_tile(R)

    def body(slot_ref, w_ref, o_ref):
        o_ref[...] = w_ref[...].astype(o_ref.dtype)

    return pl.pallas_call(
        body, name=name,
        grid_spec=pltpu.PrefetchScalarGridSpec(
            num_scalar_prefetch=1, grid=(R // tr,),
            in_specs=[pl.BlockSpec((tr, C), lambda i, slot: (i, 0))],
            out_specs=pl.BlockSpec((None, tr, C), lambda i, slot: (slot[0], i, 0))),
        out_shape=jax.ShapeDtypeStruct((n_slots, R, C), dtype),
        compiler_params=_params("parallel"),
    )(slot_arr, w)


def _gather_chips(bufs):
    n = len(bufs)

    def body(*refs):
        outs = refs[n:2 * n]
        ici_send, ici_recv, d2d_send, d2d_recv = refs[2 * n:]
        x, y, c, chips = _place()
        me = 2 * x + y
        started = []
        for k in range(n):
            rh = outs[k].shape[1] // 2
            mine = outs[k].at[me, pl.ds(c * rh, rh)]
            for j, (px, py) in enumerate(chips):
                cp = pltpu.make_async_remote_copy(
                    src_ref=mine, dst_ref=mine,
                    send_sem=ici_send.at[3 * k + j], recv_sem=ici_recv.at[3 * k + j],
                    device_id=(px, py, c), device_id_type=MESH)
                cp.start()
                started.append(cp)
        for k in range(n):
            rh = outs[k].shape[1] // 2
            for j, (px, py) in enumerate(chips):
                landed = outs[k].at[2 * px + py, pl.ds(c * rh, rh)]
                pltpu.make_async_remote_copy(
                    src_ref=landed, dst_ref=landed,
                    send_sem=ici_send.at[3 * k + j], recv_sem=ici_recv.at[3 * k + j],
                    device_id=(px, py, c), device_id_type=MESH).wait_recv()
                cp = pltpu.make_async_remote_copy(
                    src_ref=landed, dst_ref=landed,
                    send_sem=d2d_send.at[3 * k + j], recv_sem=d2d_recv.at[3 * k + j],
                    device_id=(x, y, 1 - c), device_id_type=MESH)
                cp.start()
                started.append(cp)
        for k in range(n):
            rh = outs[k].shape[1] // 2
            for j, (px, py) in enumerate(chips):
                landed = outs[k].at[2 * px + py, pl.ds((1 - c) * rh, rh)]
                pltpu.make_async_remote_copy(
                    src_ref=landed, dst_ref=landed,
                    send_sem=d2d_send.at[3 * k + j], recv_sem=d2d_recv.at[3 * k + j],
                    device_id=(x, y, 1 - c), device_id_type=MESH).wait_recv()
        for cp in started:
            cp.wait_send()

    return pl.pallas_call(
        body, name="gather_weights",
        in_specs=[ANY] * n, out_specs=[ANY] * n,
        out_shape=[jax.ShapeDtypeStruct(b.shape, b.dtype) for b in bufs],
        input_output_aliases={k: k for k in range(n)},
        scratch_shapes=[pltpu.SemaphoreType.DMA((3 * n,))] * 4,
        compiler_params=pltpu.CompilerParams(has_side_effects=True),
    )(*bufs)


def _gather_forward(bufs):
    n = len(bufs)

    def body(*refs):
        outs = refs[n:2 * n]
        d2d_send, d2d_recv = refs[2 * n:]
        x, y, c, chips = _place()
        started = []
        for k in range(n):
            rh = outs[k].shape[1] // 2
            for j, (px, py) in enumerate(chips):
                landed = outs[k].at[2 * px + py, pl.ds(c * rh, rh)]
                cp = pltpu.make_async_remote_copy(
                    src_ref=landed, dst_ref=landed,
                    send_sem=d2d_send.at[3 * k + j], recv_sem=d2d_recv.at[3 * k + j],
                    device_id=(x, y, 1 - c), device_id_type=MESH)
                cp.start()
                started.append(cp)
        for k in range(n):
            rh = outs[k].shape[1] // 2
            for j, (px, py) in enumerate(chips):
                landed = outs[k].at[2 * px + py, pl.ds((1 - c) * rh, rh)]
                pltpu.make_async_remote_copy(
                    src_ref=landed, dst_ref=landed,
                    send_sem=d2d_send.at[3 * k + j], recv_sem=d2d_recv.at[3 * k + j],
                    device_id=(x, y, 1 - c), device_id_type=MESH).wait_recv()
        for cp in started:
            cp.wait_send()

    return pl.pallas_call(
        body, name="gather_rest_forward",
        in_specs=[ANY] * n, out_specs=[ANY] * n,
        out_shape=[jax.ShapeDtypeStruct(b.shape, b.dtype) for b in bufs],
        input_output_aliases={k: k for k in range(n)},
        scratch_shapes=[pltpu.SemaphoreType.DMA((3 * n,))] * 2,
        compiler_params=pltpu.CompilerParams(has_side_effects=True),
    )(*bufs)


def _to_chips_start(name, arrays, n, src_fn, dst_fn, after, to_sibling=False):
    m = len(arrays)
    n_peers = 1 if to_sibling else N_CHIPS - 1

    def body(*refs):
        send_sem, recv_sem = refs[m + 1], refs[m + 2]
        thru = refs[m + 3:2 * m + 3]
        token = refs[2 * m + 3]
        x, y, c, chips = _place()
        peers = [(x, y, 1 - c)] if to_sibling else [(px, py, c) for px, py in chips]
        for k in range(n):
            for j, (px, py, pc) in enumerate(peers):
                pltpu.make_async_remote_copy(
                    src_ref=src_fn(thru, k, px, py, x, y, c), dst_ref=dst_fn(thru, k, px, py, x, y, c),
                    send_sem=send_sem.at[n_peers * k + j], recv_sem=recv_sem.at[n_peers * k + j],
                    device_id=(px, py, pc), device_id_type=MESH).start()
        token[...] = jnp.zeros_like(token)

    res = pl.pallas_call(
        body, name=name,
        out_shape=(pltpu.SemaphoreType.DMA((n_peers * n,)), pltpu.SemaphoreType.DMA((n_peers * n,)),
                   *[pltpu.HBM(a.shape, a.dtype) for a in arrays], jax.ShapeDtypeStruct((8, LANES), F32)),
        in_specs=[HBM] * m + [ANY],
        out_specs=(SEM, SEM, *[HBM] * m, pl.BlockSpec(memory_space=pltpu.VMEM)),
        input_output_aliases={i: i + 2 for i in range(m)},
        compiler_params=pltpu.CompilerParams(has_side_effects=EFFECT),
    )(*[pltpu.with_memory_space_constraint(a, pltpu.HBM) for a in arrays], after)
    return res[0], res[1], list(res[2:2 + m]), res[2 + m]


def _to_chips_wait(name, send_sem, recv_sem, arrays, n, src_fn, land_fn, after, to_sibling=False):
    m = len(arrays)
    after = list(after) if isinstance(after, (list, tuple)) else [after]
    n_peers = 1 if to_sibling else N_CHIPS - 1

    def body(*refs):
        send, recv = refs[m], refs[m + 1]
        outs = refs[m + 2 + len(after):]
        x, y, c, chips = _place()
        peers = [(x, y, 1 - c)] if to_sibling else [(px, py, c) for px, py in chips]
        for k in range(n):
            for j, (px, py, pc) in enumerate(peers):
                cp = pltpu.make_async_remote_copy(
                    src_ref=src_fn(outs, k, px, py, x, y, c), dst_ref=land_fn(outs, k, px, py, x, y, c),
                    send_sem=send.at[n_peers * k + j], recv_sem=recv.at[n_peers * k + j],
                    device_id=(px, py, pc), device_id_type=MESH)
                cp.wait_send()
                cp.wait_recv()

    res = pl.pallas_call(
        body, name=name,
        out_shape=[pltpu.HBM(a.shape, a.dtype) for a in arrays],
        in_specs=[HBM] * m + [SEM, SEM] + [ANY] * len(after), out_specs=[HBM] * m,
        input_output_aliases={i: i for i in range(m)},
        compiler_params=pltpu.CompilerParams(has_side_effects=EFFECT),
    )(*arrays, send_sem, recv_sem, *after)
    return list(res)


def _slot_half(refs, k, chip, c):
    rh = refs[k].shape[1] // 2
    return refs[k].at[chip, pl.ds(c * rh, rh)]


def _ag_mine(refs, k, px, py, x, y, c):
    return _slot_half(refs, k, 2 * x + y, c)


def _ag_theirs(refs, k, px, py, x, y, c):
    return _slot_half(refs, k, 2 * px + py, c)


def _rs_ends(n):
    def src(refs, k, px, py, x, y, c):
        return refs[k].at[2 * px + py]

    def dst(refs, k, px, py, x, y, c):
        return refs[n + k].at[2 * x + y]

    def land(refs, k, px, py, x, y, c):
        return refs[n + k].at[2 * px + py]

    return src, dst, land


def _half(ref, kind, p, c, rh):
    if kind == "cols":
        return ref.at[p, pl.ds(c * rh, rh)]
    return ref.at[pl.ds((2 * p + c) * rh, rh)]


def _swap_ends(kinds, rhs):
    n = len(kinds)

    def src(refs, k, px, py, x, y, c):
        return _half(refs[k // N_CHIPS], kinds[k // N_CHIPS], k % N_CHIPS, 1 - c, rhs[k // N_CHIPS])

    def dst(refs, k, px, py, x, y, c):
        return refs[n + k // N_CHIPS].at[k % N_CHIPS]

    return src, dst


def _join_halves(name, fulls, after=None):
    n = len(fulls)
    extra = [] if after is None else [after]

    def body(*refs):
        outs = refs[n + len(extra):2 * n + len(extra)]
        send_sem, recv_sem = refs[2 * n + len(extra):]
        x, y, c, _ = _place()
        started = []
        for k in range(n):
            rh = outs[k].shape[0] // 2
            mine = outs[k].at[pl.ds(c * rh, rh)]
            cp = pltpu.make_async_remote_copy(
                src_ref=mine, dst_ref=mine, send_sem=send_sem.at[k], recv_sem=recv_sem.at[k],
                device_id=(x, y, 1 - c), device_id_type=MESH)
            cp.start()
            started.append(cp)
        for k in range(n):
            rh = outs[k].shape[0] // 2
            theirs = outs[k].at[pl.ds((1 - c) * rh, rh)]
            pltpu.make_async_remote_copy(
                src_ref=theirs, dst_ref=theirs, send_sem=send_sem.at[k], recv_sem=recv_sem.at[k],
                device_id=(x, y, 1 - c), device_id_type=MESH).wait_recv()
        for cp in started:
            cp.wait_send()

    return pl.pallas_call(
        body, name=name,
        in_specs=[ANY] * (n + len(extra)), out_specs=[ANY] * n,
        out_shape=[jax.ShapeDtypeStruct(f.shape, f.dtype) for f in fulls],
        input_output_aliases={k: k for k in range(n)},
        scratch_shapes=[pltpu.SemaphoreType.DMA((n,))] * 2,
        compiler_params=pltpu.CompilerParams(has_side_effects=True),
    )(*fulls, *extra)


def _gather_all(block):
    def body(in_ref, out_ref, send_sem, recv_sem, local_sem):
        x, y, c, _ = _place()

        def slot(px, py, pc):
            return out_ref.at[4 * px + 2 * py + pc]

        loc = pltpu.make_async_copy(in_ref, slot(x, y, c), local_sem)
        loc.start()
        started = []
        for d in range(1, N_DEV):
            fx, fy, fc = d >> 2, (d >> 1) & 1, d & 1
            cp = pltpu.make_async_remote_copy(
                src_ref=in_ref, dst_ref=slot(x, y, c), send_sem=send_sem.at[d - 1], recv_sem=recv_sem.at[d - 1],
                device_id=(x ^ fx, y ^ fy, c ^ fc), device_id_type=MESH)
            cp.start()
            started.append(cp)
        for d in range(1, N_DEV):
            fx, fy, fc = d >> 2, (d >> 1) & 1, d & 1
            landed = slot(x ^ fx, y ^ fy, c ^ fc)
            pltpu.make_async_remote_copy(
                src_ref=in_ref, dst_ref=landed, send_sem=send_sem.at[d - 1], recv_sem=recv_sem.at[d - 1],
                device_id=(x ^ fx, y ^ fy, c ^ fc), device_id_type=MESH).wait_recv()
        for cp in started:
            cp.wait_send()
        loc.wait()

    return pl.pallas_call(
        body, name="gather_small_grads",
        in_specs=[ANY], out_specs=ANY,
        out_shape=jax.ShapeDtypeStruct((N_DEV,) + block.shape, block.dtype),
        scratch_shapes=[pltpu.SemaphoreType.DMA((N_DEV - 1,))] * 2 + [pltpu.SemaphoreType.DMA(())],
        compiler_params=pltpu.CompilerParams(has_side_effects=True),
    )(block)


def _pack(pieces):
    flat = jnp.concatenate([p.reshape(-1) for p in pieces])
    n = flat.shape[0]
    padded = -(-n // (8 * LANES)) * (8 * LANES)
    return jnp.pad(flat, (0, padded - n)).reshape(-1, LANES)


def _unpack(packed, shapes):
    flat = packed.reshape(-1)
    out, at = [], 0
    for s in shapes:
        n = math.prod(s)
        out.append(flat[at:at + n].reshape(s))
        at += n
    return out


def kernel(x, meta_tokens, pre_mix_g, w_in, gate_b, dw_w, dw_b, conv_ln_g, conv_ln_b, w_conv_out, w_attn_out, w_o, post_mix_g, pre_ffn_g, w_ffn_in, w_ffn_out, post_ffn_g, loss_target, m_meta_tokens, m_pre_mix_g, m_w_in, m_gate_b, m_dw_w, m_dw_b, m_conv_ln_g, m_conv_ln_b, m_w_conv_out, m_w_attn_out, m_w_o, m_post_mix_g, m_pre_ffn_g, m_w_ffn_in, m_w_ffn_out, m_post_ffn_g, v_meta_tokens, v_pre_mix_g, v_w_in, v_gate_b, v_dw_w, v_dw_b, v_conv_ln_g, v_conv_ln_b, v_w_conv_out, v_w_attn_out, v_w_o, v_post_mix_g, v_pre_ffn_g, v_w_ffn_in, v_w_ffn_out, v_post_ffn_g):
    S, D = x.shape[1], x.shape[2]
    L = S + N_META
    T = -(-L // ROW_BLOCK) * ROW_BLOCK
    Ta = -(-L // ATT_BLOCK) * ATT_BLOCK
    tm = _tile(T, MM_ROWS)
    tc = _tile(T, CONTRACT_ROWS)
    ts = _tile(T, STAGE_ROWS)
    tw = _tile(T, WIDE_STAGE_ROWS)
    H = D // HEAD_DIM
    F = w_ffn_out.shape[1] * N_CHIPS
    Dc = D // N_CHIPS
    P = N_CHIPS
    me = 2 * lax.axis_index("x") + lax.axis_index("y")
    c_arr = lax.axis_index("c").astype(jnp.int32).reshape(1)

    dw_w_pad = jnp.pad(dw_w[0], ((0, CONV_PAD - CONV_WIDTH), (0, 0)))
    me_arr = me.astype(jnp.int32).reshape(1)
    to_gather = [("w_in", w_in[0], BF16), ("w_conv_out", w_conv_out[0], BF16), ("w_attn_out", w_attn_out[0], BF16),
                 ("w_o", w_o[0], BF16), ("w_ffn_in", w_ffn_in[0], BF16), ("w_ffn_out", w_ffn_out[0], BF16),
                 ("meta", meta_tokens, F32), ("taps", dw_w_pad, F32)]
    slot = {n: _into_slot("slot_" + n, w, dt, me_arr, P) for n, w, dt in to_gather}
    win3, meta4, taps4 = _gather_chips([slot["w_in"], slot["meta"], slot["taps"]])
    meta_full = meta4.transpose(1, 0, 2).reshape(N_META, D)
    taps = taps4.transpose(1, 0, 2).reshape(CONV_PAD, D)[:CONV_WIDTH]
    later = ["w_conv_out", "w_attn_out", "w_o", "w_ffn_in", "w_ffn_out"]
    ag_send, ag_recv, in_flight, ag_token = _to_chips_start(
        "gather_rest_start", [slot[n] for n in later], len(later), _ag_mine, _ag_mine, meta4)

    h0 = jnp.concatenate([meta_full, x[0], jnp.zeros((T - L, D), F32)], axis=0)
    (u1,) = _rowwise_fwd("rms_pre_mix", f_rms, [(h0, D, 0)], [(pre_mix_g + ag_token[0:1, 0:1], D, 0)], [(D, BF16)], T, ts)
    p = _mm_nn_cols("mm_in", u1, win3, tm)
    q, k, v = _qkv_split(p, D, T, Ta)
    o2, rtot = _attn_fwd(q, k, v, L)
    landed = _to_chips_wait("gather_rest_wait", ag_send, ag_recv, in_flight, len(later), _ag_mine, _ag_theirs, o2)
    wco4, wao4, wo4, wfi3, wfo4 = _gather_forward(landed)
    wco, wao, wo = (w.reshape(D, D) for w in (wco4, wao4, wo4))
    wfo = wfo4.reshape(F, D)
    (uglu,) = _rowwise_fwd("glu", f_glu, [(p, D, 0), (p, D, 1)], [], [(D, F32)], T, ts)
    yc = _shift_conv("dwconv", uglu, taps, CONV_PAD, CONV_PAD - (CONV_WIDTH - 1), T)
    conv_pars = [(dw_b, D, 0), (conv_ln_g, D, 0), (conv_ln_b, D, 0)]
    (ys,) = _rowwise_fwd("conv_post", f_convpost, [(yc, D, 0)], conv_pars, [(D, BF16)], T, ts)
    y_conv = _mm_nn("mm_conv_out", ys, wco, tm)
    y_attn = _mm_nn("mm_attn_out", o2, wao, tm, rows=T)
    mix_rows = [(p, D, 5), (p, D, 6), (y_conv, D, 0), (y_attn, D, 0)]
    mix_pars = [(gate_b, D, 0), (gate_b, D, 1)]
    (mixin,) = _rowwise_fwd("gate_mix", f_mix, mix_rows, mix_pars, [(D, BF16)], T, ts)
    mix = _mm_nn("mm_o", mixin, wo, tm)
    (h1,) = _rowwise_fwd("res_post_mix", f_res_rms, [(h0, D, 0), (mix, D, 0)], [(post_mix_g, D, 0)], [(D, F32)], T, ts)
    (u2,) = _rowwise_fwd("rms_pre_ffn", f_rms, [(h1, D, 0)], [(pre_ffn_g, D, 0)], [(D, BF16)], T, ts)
    ab = _mm_nn_cols("mm_ffn_in", u2, wfi3, tm)
    (fin,) = _rowwise_fwd("swiglu", f_swiglu, [(ab, F, 0), (ab, F, 1)], [], [(F, BF16)], T, tw)
    f = _mm_nn("mm_ffn_out", fin, wfo, tm)
    (h2,) = _rowwise_fwd("res_post_ffn", f_res_rms, [(h1, D, 0), (f, D, 0)], [(post_ffn_g, D, 0)], [(D, F32)], T, ts)

    dy, part = _loss_head(h2[N_META:L], loss_target[0], _row_tile(S))
    loss = lax.psum(0.5 * jnp.sum(part) / D, ("x", "y", "c"))
    dh2 = jnp.pad(dy, ((N_META, T - L), (0, 0)))

    (df,), (g_post_ffn,) = _rowwise_bwd("res_post_ffn_bwd", f_res_rms, [(h1, D, 0), (f, D, 0)], [(post_ffn_g, D, 0)],
                                        [(dh2, D, 0)], [None, BF16], T, ts)
    dfin = _mm_nt("mm_ffn_out_dx", df, wfo, tm)
    g_wfo = _mm_tn("mm_ffn_out_dw", fin, df, tc, F // MXU_WIDTH)
    (dab,), _ = _rowwise_bwd("swiglu_bwd", f_swiglu, [(ab, F, 0), (ab, F, 1)], [], [(dfin, F, 0)], [BF16, BF16], T, tw,
                             joined=True)
    du2 = _mm_nt_cols("mm_ffn_in_dx", dab, wfi3, _tile(T, 2 * MM_ROWS))
    g_wfi = _mm_tn_cols("mm_ffn_in_dw", u2, dab, tc, P)
    (dh1,), (g_pre_ffn,) = _rowwise_bwd("rms_pre_ffn_bwd", f_rms_id, [(h1, D, 0)], [(pre_ffn_g, D, 0)],
                                        [(du2, D, 0), (dh2, D, 0)], [F32], T, ts)
    (dmix,), (g_post_mix,) = _rowwise_bwd("res_post_mix_bwd", f_res_rms, [(h0, D, 0), (mix, D, 0)], [(post_mix_g, D, 0)],
                                          [(dh1, D, 0)], [None, BF16], T, ts)
    dmixin = _mm_nt("mm_o_dx", dmix, wo, tm)
    g_wo = _mm_tn("mm_o_dw", mixin, dmix, tc, D // MXU_WIDTH)
    (dpc, dpa, dyconv, dyattn), (g_gate_c, g_gate_a) = _rowwise_bwd(
        "gate_mix_bwd", f_mix, mix_rows, mix_pars, [(dmixin, D, 0)], [BF16, BF16, BF16, BF16], T, ts)
    g_wco = _mm_tn("mm_conv_out_dw", ys, dyconv, tc, D // MXU_WIDTH)
    dys = _mm_nt("mm_conv_out_dx", dyconv, wco, tm)
    g_wao = _mm_tn("mm_attn_out_dw", o2, dyattn, tc, D // MXU_WIDTH)
    do2 = _mm_nt("mm_attn_out_dx", dyattn, wao, tm, BF16, out_rows=Ta)

    early = [g_wco, g_wao, g_wo, g_wfi, g_wfo]
    n_early = len(early)
    early_kinds = ["rows", "rows", "rows", "cols", "rows"]
    early_rhs = [(g.shape[1] if kind == "cols" else g.shape[0] // P) // 2 for g, kind in zip(early, early_kinds)]
    sw_src, sw_dst = _swap_ends(early_kinds, early_rhs)
    sw_send, sw_recv, sw_flight, sw_token = _to_chips_start(
        "grad_swap_early_start", early + [lax.empty((P, rh, g.shape[-1]), F32) for g, rh in zip(early, early_rhs)],
        P * n_early, sw_src, sw_dst, do2, to_sibling=True)
    (dyc,), (g_dw_b, g_ln_g, g_ln_b) = _rowwise_bwd(
        "conv_post_bwd", f_convpost, [(yc, D, 0)], [(dw_b + sw_token[0:1, 0:1], D, 0)] + conv_pars[1:],
        [(dys, D, 0)], [F32], T, ts)
    duglu = _shift_conv("dwconv_dx", dyc, taps[::-1], 0, 0, T)
    g_taps = _conv_dw("dwconv_dw", uglu, dyc, T)
    (dp01,), _ = _rowwise_bwd("glu_bwd", f_glu, [(p, D, 0), (p, D, 1)], [], [(duglu, D, 0)], [BF16, BF16], T, ts,
                              joined=True)
    sw_done = _to_chips_wait("grad_swap_early_wait", sw_send, sw_recv, sw_flight, P * n_early, sw_src, sw_dst,
                             [dp01, g_taps], to_sibling=True)
    early_pairs = [_pair_add_bf16("grad_pair_add_%d" % (n + 1), g, b1, kind, c_arr, me_arr)
                   for n, (g, b1, kind) in enumerate(zip(sw_done[:n_early], sw_done[n_early:], early_kinds))]
    rs_src, rs_dst, rs_land = _rs_ends(n_early)
    rs_send, rs_recv, rs_flight, rs_token = _to_chips_start(
        "grad_scatter_start", [pr[0] for pr in early_pairs] + [pr[1] for pr in early_pairs], n_early,
        rs_src, rs_dst, early_pairs[-1][1])
    dq, dk, dv = _attn_bwd(q, k, v, do2, rtot, rs_token, L)
    rs_done = _to_chips_wait("grad_scatter_wait", rs_send, rs_recv, rs_flight, n_early, rs_src, rs_land, dq)
    early_slots = rs_done[n_early:]
    dp = jnp.concatenate([dp01, dq[:T], dk[:T], dv[:T], dpc, dpa], axis=1)
    du1 = _mm_nt_cols("mm_in_dx", dp, win3, _tile(T, 2 * MM_ROWS))
    g_win = _mm_tn_cols("mm_in_dw", u1, dp, tc, P)

    in_rh = g_win.shape[1] // 2
    s2_src, s2_dst = _swap_ends(["cols"], [in_rh])
    s2_send, s2_recv, s2_flight, s2_token = _to_chips_start(
        "grad_swap_in_start", [g_win, lax.empty((P, in_rh, g_win.shape[-1]), F32)], P, s2_src, s2_dst, du1,
        to_sibling=True)
    (dh0,), (g_pre_mix,) = _rowwise_bwd("rms_pre_mix_bwd", f_rms_id, [(h0, D, 0)],
                                        [(pre_mix_g + s2_token[0:1, 0:1], D, 0)],
                                        [(du1, D, 0), (dh1, D, 0)], [F32], T, ts)
    grad_x = dh0[N_META:L][None]

    small_shapes = [(1, D), (1, D), (1, D), (CONV_WIDTH, D), (1, D), (1, D), (1, D), (1, D), (1, D), (1, D), (N_META, D)]
    small = _pack([g_pre_mix, g_gate_c, g_gate_a, g_taps, g_dw_b, g_ln_g, g_ln_b, g_post_mix, g_pre_ffn, g_post_ffn,
                   dh0[:N_META]])
    summed = _sum_slots("small_grad_sum", _gather_all(small))
    (s_pre_mix, s_gate_c, s_gate_a, s_taps, s_dw_b, s_ln_g, s_ln_b, s_post_mix, s_pre_ffn, s_post_ffn,
     s_meta) = _unpack(summed, small_shapes)
    s_gate_b = jnp.concatenate([s_gate_c, s_gate_a], axis=1)
    s_taps = lax.dynamic_slice_in_dim(s_taps, me * Dc, Dc, axis=1)[None]
    s_meta = lax.dynamic_slice_in_dim(s_meta, me * Dc, Dc, axis=1)

    s2_done = _to_chips_wait("grad_swap_in_wait", s2_send, s2_recv, s2_flight, P, s2_src, s2_dst, summed, to_sibling=True)
    win_pair = _pair_add_bf16("grad_pair_add_0", s2_done[0], s2_done[1], "cols", c_arr, me_arr)
    in_src, in_dst, in_land = _rs_ends(1)
    in_send, in_recv, in_flight, in_token = _to_chips_start(
        "grad_scatter_in_start", list(win_pair), 1, in_src, in_dst, win_pair[1])
    g_early = _join_halves("grad_join_halves_early", [_sum_slots("grad_chip_sum_%d" % (n + 1), s, c_arr)
                                                      for n, s in enumerate(early_slots)], in_token)

    grads = {
        "meta_tokens": s_meta, "pre_mix_g": s_pre_mix, "gate_b": s_gate_b, "dw_w": s_taps,
        "dw_b": s_dw_b, "conv_ln_g": s_ln_g, "conv_ln_b": s_ln_b, "w_conv_out": g_early[0][None],
        "w_attn_out": g_early[1][None], "w_o": g_early[2][None], "post_mix_g": s_post_mix, "pre_ffn_g": s_pre_ffn,
        "w_ffn_in": g_early[3][None], "w_ffn_out": g_early[4][None], "post_ffn_g": s_post_ffn,
    }
    weights = {
        "meta_tokens": (meta_tokens, m_meta_tokens, v_meta_tokens), "pre_mix_g": (pre_mix_g, m_pre_mix_g, v_pre_mix_g),
        "w_in": (w_in, m_w_in, v_w_in), "gate_b": (gate_b, m_gate_b, v_gate_b), "dw_w": (dw_w, m_dw_w, v_dw_w),
        "dw_b": (dw_b, m_dw_b, v_dw_b), "conv_ln_g": (conv_ln_g, m_conv_ln_g, v_conv_ln_g),
        "conv_ln_b": (conv_ln_b, m_conv_ln_b, v_conv_ln_b), "w_conv_out": (w_conv_out, m_w_conv_out, v_w_conv_out),
        "w_attn_out": (w_attn_out, m_w_attn_out, v_w_attn_out), "w_o": (w_o, m_w_o, v_w_o),
        "post_mix_g": (post_mix_g, m_post_mix_g, v_post_mix_g), "pre_ffn_g": (pre_ffn_g, m_pre_ffn_g, v_pre_ffn_g),
        "w_ffn_in": (w_ffn_in, m_w_ffn_in, v_w_ffn_in), "w_ffn_out": (w_ffn_out, m_w_ffn_out, v_w_ffn_out),
        "post_ffn_g": (post_ffn_g, m_post_ffn_g, v_post_ffn_g),
    }
    names = list(weights)
    big_names = ["w_in", "w_conv_out", "w_attn_out", "w_o", "w_ffn_in", "w_ffn_out"]
    small_names = [n for n in names if n not in big_names]

    delta, new_m, new_v = {}, {}, {}

    def big_update(n):
        w, m, v2 = weights[n]
        d, nm, nv = _adamw("adamw_" + n, w[0], grads[n][0], m[0], v2[0])
        delta[n], new_m[n], new_v[n] = d[None], nm[None], nv[None]

    for n in big_names[1:]:
        big_update(n)
    shapes = [weights[n][0].shape for n in small_names]
    packed = [_pack([weights[n][k] for n in small_names]) for k in range(3)]
    d, nm, nv = _adamw("adamw_small", packed[0], _pack([grads[n] for n in small_names]), packed[1], packed[2])
    for n, dd, mm, vv in zip(small_names, _unpack(d, shapes), _unpack(nm, shapes), _unpack(nv, shapes)):
        delta[n], new_m[n], new_v[n] = dd, mm, vv

    in_done = _to_chips_wait("grad_scatter_in_wait", in_send, in_recv, in_flight, 1, in_src, in_land,
                             [d] + [delta[n] for n in big_names[1:]])
    (g_in,) = _join_halves("grad_join_halves_in", [_sum_slots("grad_chip_sum_0", in_done[1], c_arr)])
    grads["w_in"] = g_in[None]
    big_update("w_in")

    return (loss, grad_x, *[grads[n].reshape(weights[n][0].shape) for n in names], *[delta[n] for n in names],
            *[new_m[n] for n in names], *[new_v[n] for n in names])
```

```python
import math

import jax
import jax.numpy as jnp
from jax import lax
from jax.experimental import pallas as pl
from jax.experimental.pallas import tpu as pltpu

F32 = jnp.float32
BF16 = jnp.bfloat16

N_META = 16
CONV_WIDTH = 31
CONV_PAD = 32
HEAD_DIM = 64
RMS_EPS = 1e-6
LN_EPS = 1e-5
ROW_BLOCK = 128
MXU_WIDTH = 256
ATT_BLOCK = MXU_WIDTH
ATT_HEADS = 8
ATT_HEADS_BWD = 4
LANES = 128
N_CHIPS = 4
N_DEV = 8
MM_ROWS = 544
CONTRACT_ROWS = 2176
STAGE_ROWS = 272
WIDE_STAGE_ROWS = 128
VMEM_LIMIT = 56 * 1024 * 1024

ADAM_LR = 0.001
ADAM_B1 = 0.9
ADAM_B2 = 0.999
ADAM_EPS = 1e-08
ADAM_WD = 0.01
ADAM_STEP = 10

MESH = pl.DeviceIdType.MESH
ANY = pl.BlockSpec(memory_space=pl.ANY)
HBM = pl.BlockSpec(memory_space=pltpu.HBM)
SEM = pl.BlockSpec(memory_space=pltpu.SEMAPHORE)
EFFECT = pltpu.SideEffectType.DATAFLOW_SIDE_EFFECTING


def _params(*sem):
    return pltpu.CompilerParams(dimension_semantics=sem if sem else None, vmem_limit_bytes=VMEM_LIMIT)


def _rms(x, g):
    return x * lax.rsqrt(jnp.mean(x * x, axis=-1, keepdims=True) + RMS_EPS) * g


def f_rms(h, g):
    return (_rms(h, g),)


def f_rms_id(h, g):
    return (_rms(h, g), h)


def f_res_rms(h, m, g):
    return (h + _rms(m, g),)


def f_glu(a, gate):
    return (a * lax.logistic(gate),)


def f_convpost(yc, b, ln_g, ln_b):
    y = yc + b
    mu = jnp.mean(y, axis=-1, keepdims=True)
    xc = y - mu
    var = jnp.mean(xc * xc, axis=-1, keepdims=True)
    yl = xc * lax.rsqrt(var + LN_EPS) * ln_g + ln_b
    return (yl * lax.logistic(yl),)


def f_mix(pc, pa, yc, ya, bc, ba):
    return (lax.logistic(pc + bc) * yc + lax.logistic(pa + ba) * ya,)


def f_swiglu(a, b):
    return (a * lax.logistic(a) * b,)


def _tile(T, target):
    return max(t for t in range(16, target + 1, 16) if T % t == 0)


def _row_map(j):
    return lambda i: (i, j)


def _par_map(j):
    return lambda i: (0, j)


def _rowwise_fwd(name, f, rows, pars, outs, T, tm):
    n_in = len(rows) + len(pars)

    def body(*refs):
        vals = [r[...].astype(F32) for r in refs[:n_in]]
        res = f(*vals)
        for o_ref, o in zip(refs[n_in:], res):
            o_ref[...] = o.astype(o_ref.dtype)

    in_specs = [pl.BlockSpec((tm, w), _row_map(j)) for _, w, j in rows]
    in_specs += [pl.BlockSpec((1, w), _par_map(j)) for _, w, j in pars]
    return pl.pallas_call(
        body, name=name, grid=(T // tm,),
        in_specs=in_specs,
        out_specs=[pl.BlockSpec((tm, w), _row_map(0)) for w, _ in outs],
        out_shape=[jax.ShapeDtypeStruct((T, w), dt) for w, dt in outs],
        compiler_params=_params("parallel"),
    )(*[a for a, _, _ in rows], *[a for a, _, _ in pars])


def _rowwise_bwd(name, f, rows, pars, cots, drow_dtypes, T, tm, joined=False):
    n_r, n_p, n_c = len(rows), len(pars), len(cots)
    n_in = n_r + n_p + n_c
    keep = [k for k, dt in enumerate(drow_dtypes) if dt is not None]
    n_out = 1 if joined else len(keep)

    def body(*refs):
        rv = [r[...].astype(F32) for r in refs[:n_r]]
        pv = [r[...].astype(F32) for r in refs[n_r:n_r + n_p]]
        cv = [r[...].astype(F32) for r in refs[n_r + n_p:n_in]]
        _, vjp = jax.vjp(f, *rv, *pv)
        g = vjp(tuple(cv))
        drow_refs = refs[n_in:n_in + n_out]
        dpar_refs = refs[n_in + n_out:]
        if joined:
            at = 0
            for k in keep:
                drow_refs[0][:, at:at + rows[k][1]] = g[k].astype(drow_refs[0].dtype)
                at += rows[k][1]
        else:
            for r, k in zip(drow_refs, keep):
                r[...] = g[k].astype(r.dtype)

        @pl.when(pl.program_id(0) == 0)
        def _():
            for r in dpar_refs:
                r[...] = jnp.zeros_like(r)

        for r, gp in zip(dpar_refs, g[n_r:]):
            r[...] += gp

    in_specs = [pl.BlockSpec((tm, w), _row_map(j)) for _, w, j in rows]
    in_specs += [pl.BlockSpec((1, w), _par_map(j)) for _, w, j in pars]
    in_specs += [pl.BlockSpec((tm, w), _row_map(j)) for _, w, j in cots]
    widths = [sum(rows[k][1] for k in keep)] if joined else [rows[k][1] for k in keep]
    out_specs = [pl.BlockSpec((tm, w), _row_map(0)) for w in widths]
    out_specs += [pl.BlockSpec((1, w), _par_map(0)) for _, w, _ in pars]
    out_shape = [jax.ShapeDtypeStruct((T, w), drow_dtypes[k]) for w, k in zip(widths, keep)]
    out_shape += [jax.ShapeDtypeStruct((1, w), F32) for _, w, _ in pars]
    res = pl.pallas_call(
        body, name=name, grid=(T // tm,),
        in_specs=in_specs, out_specs=out_specs, out_shape=out_shape,
        compiler_params=_params("arbitrary"),
    )(*[a for a, _, _ in rows], *[a for a, _, _ in pars], *[a for a, _, _ in cots])
    return res[:n_out], res[n_out:]


NN = (((1,), (0,)), ((), ()))
NT = (((1,), (1,)), ((), ()))
TN = (((0,), (0,)), ((), ()))


def _mm(name, a, b, dims, out_shape, grid, a_spec, b_spec, o_spec, red_axis=None, init=None):
    n_red = None if red_axis is None else grid[red_axis]

    def body(a_ref, b_ref, *rest):
        o_ref = rest[-1]
        prod = lax.dot_general(a_ref[...], b_ref[...], dims, preferred_element_type=F32)
        if n_red is None:
            o_ref[...] = prod.astype(o_ref.dtype)
        else:
            @pl.when(pl.program_id(red_axis) == 0)
            def _():
                o_ref[...] = prod

            @pl.when(pl.program_id(red_axis) > 0)
            def _():
                o_ref[...] += prod

    sem = ["parallel"] * len(grid)
    if red_axis is not None:
        sem[red_axis] = "arbitrary"
    if init is None:
        return pl.pallas_call(
            body, name=name, grid=grid, in_specs=[a_spec, b_spec], out_specs=o_spec, out_shape=out_shape,
            compiler_params=_params(*sem),
        )(a, b)
    return pl.pallas_call(
        body, name=name, grid=grid, in_specs=[a_spec, b_spec, ANY], out_specs=o_spec, out_shape=out_shape,
        input_output_aliases={2: 0}, compiler_params=_params(*sem),
    )(a, b, init)


def _mm_nn(name, a, w, tm, out_dtype=F32, rows=None):
    T, K = a.shape
    T = rows or T
    N = w.shape[1]
    return _mm(name, a, w, NN, jax.ShapeDtypeStruct((T, N), out_dtype), (T // tm,),
               pl.BlockSpec((tm, K), lambda i: (i, 0)), pl.BlockSpec((K, N), lambda i: (0, 0)),
               pl.BlockSpec((tm, N), lambda i: (i, 0)))


def _mm_nt(name, a, w, tm, out_dtype=F32, out_rows=None):
    T, N = a.shape
    K = w.shape[0]
    init = None if out_rows is None else jnp.zeros((out_rows, K), out_dtype)
    return _mm(name, a, w, NT, jax.ShapeDtypeStruct((out_rows or T, K), out_dtype), (T // tm,),
               pl.BlockSpec((tm, N), lambda i: (i, 0)), pl.BlockSpec((K, N), lambda i: (0, 0)),
               pl.BlockSpec((tm, K), lambda i: (i, 0)), init=init)


def _mm_tn(name, a, b, tm, n_row_blocks):
    K = a.shape[1]
    T, N = b.shape
    kb = K // n_row_blocks
    return _mm(name, a, b, TN, jax.ShapeDtypeStruct((K, N), F32), (n_row_blocks, T // tm),
               pl.BlockSpec((tm, kb), lambda r, t: (t, r)), pl.BlockSpec((tm, N), lambda r, t: (t, 0)),
               pl.BlockSpec((kb, N), lambda r, t: (r, 0)), red_axis=1)


def _mm_nn_cols(name, a, w3, tm):
    T, K = a.shape
    P, _, Ns = w3.shape
    return _mm(name, a, w3, NN, jax.ShapeDtypeStruct((T, P * Ns), F32), (P, T // tm),
               pl.BlockSpec((tm, K), lambda p, i: (i, 0)), pl.BlockSpec((None, K, Ns), lambda p, i: (p, 0, 0)),
               pl.BlockSpec((tm, Ns), lambda p, i: (i, p)))


def _mm_nt_cols(name, a, w3, tm):
    T = a.shape[0]
    P, K, Ns = w3.shape
    return _mm(name, a, w3, NT, jax.ShapeDtypeStruct((T, K), F32), (T // tm, P),
               pl.BlockSpec((tm, Ns), lambda i, p: (i, p)), pl.BlockSpec((None, K, Ns), lambda i, p: (p, 0, 0)),
               pl.BlockSpec((tm, K), lambda i, p: (i, 0)), red_axis=1)


def _mm_tn_cols(name, a, b, tm, P):
    T, K = a.shape
    Ns = b.shape[1] // P
    return _mm(name, a, b, TN, jax.ShapeDtypeStruct((P, K, Ns), F32), (P, T // tm),
               pl.BlockSpec((tm, K), lambda p, t: (t, 0)), pl.BlockSpec((tm, Ns), lambda p, t: (t, p)),
               pl.BlockSpec((None, K, Ns), lambda p, t: (p, 0, 0)), red_axis=1)


def _tap_windows(win, off, tb):
    out = []
    for b in range(8):
        taps = [j for j in range(CONV_WIDTH) if (off + j) % 8 == b]
        if taps:
            shifted = win if b == 0 else pltpu.roll(win, win.shape[0] - b, axis=0)
            out += [(shifted, off + j - b, j) for j in taps]
    return out


def _shift_conv(name, x, w, place, off, T):
    C = x.shape[1]
    tb = ROW_BLOCK
    zero_at = 0 if place else T

    def body(x_ref, w_ref, o_ref, xp_ref):
        xp_ref[pl.ds(zero_at, CONV_PAD), :] = jnp.zeros((CONV_PAD, LANES), F32)
        xp_ref[pl.ds(T + CONV_PAD, 8), :] = jnp.zeros((8, LANES), F32)
        xp_ref[pl.ds(place, T), :] = x_ref[...]

        def step(t, carry):
            base = pl.multiple_of(t * tb, tb)
            win = xp_ref[pl.ds(base, tb + CONV_PAD + 8), :]
            acc = jnp.zeros((tb, LANES), F32)
            for shifted, at, j in _tap_windows(win, off, tb):
                acc = acc + shifted[at:at + tb, :] * w_ref[pl.ds(j, 1), :]
            o_ref[pl.ds(base, tb), :] = acc
            return carry

        lax.fori_loop(0, T // tb, step, 0)

    return pl.pallas_call(
        body, name=name, grid=(C // LANES,),
        in_specs=[pl.BlockSpec((T, LANES), lambda c: (0, c)), pl.BlockSpec((CONV_WIDTH, LANES), lambda c: (0, c))],
        out_specs=pl.BlockSpec((T, LANES), lambda c: (0, c)),
        out_shape=jax.ShapeDtypeStruct((T, C), F32),
        scratch_shapes=[pltpu.VMEM((T + CONV_PAD + 8, LANES), F32)],
        compiler_params=_params("parallel"),
    )(x, w)


def _conv_dw(name, x, dy, T):
    C = x.shape[1]
    tb = ROW_BLOCK
    off = CONV_PAD - (CONV_WIDTH - 1)

    def body(x_ref, dy_ref, o_ref, xp_ref, acc_ref):
        xp_ref[pl.ds(0, CONV_PAD), :] = jnp.zeros((CONV_PAD, LANES), F32)
        xp_ref[pl.ds(T + CONV_PAD, 8), :] = jnp.zeros((8, LANES), F32)
        xp_ref[pl.ds(CONV_PAD, T), :] = x_ref[...]
        acc_ref[...] = jnp.zeros_like(acc_ref)

        def step(t, carry):
            base = pl.multiple_of(t * tb, tb)
            win = xp_ref[pl.ds(base, tb + CONV_PAD + 8), :]
            d = dy_ref[pl.ds(base, tb), :]
            for shifted, at, j in _tap_windows(win, off, tb):
                prod = shifted[at:at + tb, :] * d
                acc_ref[j] += jnp.sum(prod.reshape(tb // 8, 8, LANES), axis=0)
            return carry

        lax.fori_loop(0, T // tb, step, 0)
        for j in range(CONV_WIDTH):
            o_ref[pl.ds(j, 1), :] = jnp.sum(acc_ref[j], axis=0, keepdims=True)

    return pl.pallas_call(
        body, name=name, grid=(C // LANES,),
        in_specs=[pl.BlockSpec((T, LANES), lambda c: (0, c)), pl.BlockSpec((T, LANES), lambda c: (0, c))],
        out_specs=pl.BlockSpec((CONV_WIDTH, LANES), lambda c: (0, c)),
        out_shape=jax.ShapeDtypeStruct((CONV_WIDTH, C), F32),
        scratch_shapes=[pltpu.VMEM((T + CONV_PAD + 8, LANES), F32), pltpu.VMEM((CONV_WIDTH, 8, LANES), F32)],
        compiler_params=_params("parallel"),
    )(x, dy)


def _dot(a, b, dims=NN):
    return lax.dot_general(a, b, dims, preferred_element_type=F32)


def _tri_cumsum(x, tri):
    return _dot(x.astype(BF16), tri)


def _qkv_split(p, D, T, Ta):
    tb = ROW_BLOCK
    nt = T // tb
    scale = 1.0 / math.sqrt(HEAD_DIM)

    def body(q_ref, k_ref, v_ref, qo_ref, ko_ref, vo_ref):
        live = pl.program_id(0) < nt
        qo_ref[...] = jnp.where(live, q_ref[...] * scale, 0.0).astype(BF16)
        ko_ref[...] = jnp.where(live, k_ref[...], 0.0).astype(BF16)
        vo_ref[...] = jnp.where(live, v_ref[...], 0.0).astype(BF16)

    def col(n):
        return lambda i: (jnp.minimum(i, nt - 1), n)

    return pl.pallas_call(
        body, name="qkv_split", grid=(Ta // tb,),
        in_specs=[pl.BlockSpec((tb, D), col(2 + n)) for n in range(3)],
        out_specs=[pl.BlockSpec((tb, D), lambda i: (i, 0))] * 3,
        out_shape=[jax.ShapeDtypeStruct((Ta, D), BF16)] * 3,
        compiler_params=_params("parallel"),
    )(p, p, p)


def _pair_lanes(g):
    return slice((g // 2) * LANES, (g // 2 + 1) * LANES)


def _stacked_pairs(x_ref, B, G):
    first = lax.broadcasted_iota(jnp.int32, (B, LANES), 1) < HEAD_DIM
    out = []
    for g in range(0, G, 2):
        x2 = x_ref[:, _pair_lanes(g)]
        zero = jnp.zeros_like(x2)
        out.append(jnp.concatenate([jnp.where(first, x2, zero), jnp.where(first, zero, x2)], axis=0))
    return first, out


def _tail_rows(T, live):
    return -(-(live - (T // ATT_BLOCK - 1) * ATT_BLOCK) // 16) * 16


def _attn_fwd(q, k, v, live):
    T, D = q.shape
    H = D // HEAD_DIM
    B = ATT_BLOCK
    nq = T // B
    tail = _tail_rows(T, live)
    G = ATT_HEADS
    NP = G // 2
    W = NP * LANES

    def body(q_ref, k_ref, v_ref, o_ref, rt_ref):
        i = pl.program_id(1)
        row = lax.broadcasted_iota(jnp.int32, (B, B), 0)
        col = lax.broadcasted_iota(jnp.int32, (B, B), 1)
        tri = (row >= col).astype(BF16)
        first, q_pairs = _stacked_pairs(q_ref, B, G)

        def sweep(R):
            below = jnp.concatenate([(col < row)[:R]] * 2, axis=0)
            qp = [x if R == B else jnp.concatenate([x[:R], x[B:B + R]], axis=0) for x in q_pairs]

            def tile(j, carry, diagonal):
                sl = pl.ds(pl.multiple_of(j * B, B), B)
                zs, sps = [], []
                for p in range(NP):
                    z = _dot(qp[p], k_ref[sl, _pair_lanes(2 * p)], NT)
                    sp = jnp.maximum(z, 0.0) + jnp.log(1.0 + jnp.exp(-jnp.abs(z)))
                    if diagonal:
                        sp = jnp.where(below, sp, 0.0)
                    zs.append(z)
                    sps.append(sp)
                rws = _tri_cumsum(jnp.concatenate(sps, axis=0), tri)
                out = []
                for p in range(NP):
                    c, acc = carry[p]
                    rw = rws[2 * R * p:2 * R * (p + 1)]
                    a = jnp.exp(zs[p] - (rw + c))
                    if diagonal:
                        a = jnp.where(below, a, 0.0)
                    acc = acc + _dot(a.astype(BF16), v_ref[sl, _pair_lanes(2 * p)])
                    out.append((c + rw[:, 0:1], acc))
                return tuple(out)

            carry = tile(i, tuple((jnp.zeros((2 * R, 1), F32), jnp.zeros((2 * R, LANES), F32)) for _ in range(NP)), True)
            carry = lax.fori_loop(0, i, lambda jj, cr: tile(i - 1 - jj, cr, False), carry)
            for p in range(NP):
                c, acc = carry[p]
                o_ref[pl.ds(0, R), _pair_lanes(2 * p)] = jnp.where(first[:R], acc[:R], acc[R:]).astype(o_ref.dtype)
                rt_ref[2 * p, pl.ds(0, R), :] = c[:R]
                rt_ref[2 * p + 1, pl.ds(0, R), :] = c[R:]
                if R < B:
                    o_ref[pl.ds(R, B - R), _pair_lanes(2 * p)] = jnp.zeros((B - R, LANES), o_ref.dtype)
                    rt_ref[2 * p, pl.ds(R, B - R), :] = jnp.zeros((B - R, 1), F32)
                    rt_ref[2 * p + 1, pl.ds(R, B - R), :] = jnp.zeros((B - R, 1), F32)

        if tail == B:
            sweep(B)
        else:
            pl.when(i < nq - 1)(lambda: sweep(B))
            pl.when(i == nq - 1)(lambda: sweep(tail))

    return pl.pallas_call(
        body, name="attn_fwd", grid=(H // G, T // B),
        in_specs=[pl.BlockSpec((B, W), lambda h, i: (i, h)),
                  pl.BlockSpec((T, W), lambda h, i: (0, h)),
                  pl.BlockSpec((T, W), lambda h, i: (0, h))],
        out_specs=[pl.BlockSpec((B, W), lambda h, i: (i, h)),
                   pl.BlockSpec((G, B, 1), lambda h, i: (h, i, 0))],
        out_shape=[jax.ShapeDtypeStruct((T, D), BF16), jax.ShapeDtypeStruct((H, T, 1), F32)],
        compiler_params=_params("parallel", "arbitrary"),
    )(q, k, v)


def _attn_bwd(q, k, v, do, rt, after, live):
    T, D = q.shape
    H = D // HEAD_DIM
    B = ATT_BLOCK
    nq = T // B
    tail = _tail_rows(T, live)
    scale = 1.0 / math.sqrt(HEAD_DIM)
    G = ATT_HEADS_BWD
    NP = G // 2
    W = NP * LANES

    def body(q_ref, k_ref, v_ref, do_ref, rt_ref, after_ref, dq_ref, dk_ref, dv_ref, dk_acc, dv_acc):
        i = pl.program_id(1)

        @pl.when(i == 0)
        def _():
            dk_acc[...] = jnp.zeros_like(dk_acc)
            dv_acc[...] = jnp.zeros_like(dv_acc)

        row = lax.broadcasted_iota(jnp.int32, (B, B), 0)
        col = lax.broadcasted_iota(jnp.int32, (B, B), 1)
        tri = (row <= col).astype(BF16)
        first, q_pairs = _stacked_pairs(q_ref, B, G)
        _, do_pairs = _stacked_pairs(do_ref, B, G)

        def sweep(R):
            below = jnp.concatenate([(col < row)[:R]] * 2, axis=0)
            qp = [x if R == B else jnp.concatenate([x[:R], x[B:B + R]], axis=0) for x in q_pairs]
            dop = [x if R == B else jnp.concatenate([x[:R], x[B:B + R]], axis=0) for x in do_pairs]
            rtp = [jnp.concatenate([rt_ref[2 * p, pl.ds(0, R), :], rt_ref[2 * p + 1, pl.ds(0, R), :]], axis=0)
                   for p in range(NP)]

            def tile(j, carry, diagonal):
                sl = pl.ds(pl.multiple_of(j * B, B), B)
                zs, sps, sgs = [], [], []
                for p in range(NP):
                    z = _dot(qp[p], k_ref[sl, _pair_lanes(2 * p)], NT)
                    e = jnp.exp(-jnp.abs(z))
                    inv = 1.0 / (1.0 + e)
                    sp = jnp.maximum(z, 0.0) - jnp.log(inv)
                    if diagonal:
                        sp = jnp.where(below, sp, 0.0)
                    zs.append(z)
                    sps.append(sp)
                    sgs.append(jnp.where(z >= 0.0, inv, e * inv))
                pws = _tri_cumsum(jnp.concatenate(sps, axis=0), tri)
                aas, gs = [], []
                for p in range(NP):
                    pw = pws[2 * R * p:2 * R * (p + 1)]
                    a = jnp.exp(zs[p] - (rtp[p] - carry[p][0] - pw + sps[p]))
                    if diagonal:
                        a = jnp.where(below, a, 0.0)
                    aas.append(a.astype(BF16))
                    gs.append(a * _dot(dop[p], v_ref[sl, _pair_lanes(2 * p)], NT))
                gws = _tri_cumsum(jnp.concatenate(gs, axis=0), tri)
                out = []
                for p in range(NP):
                    pc, gc, dq = carry[p]
                    pw, gw = pws[2 * R * p:2 * R * (p + 1)], gws[2 * R * p:2 * R * (p + 1)]
                    dz = gs[p] - sgs[p] * (gc + gw)
                    if diagonal:
                        dz = jnp.where(below, dz, 0.0)
                    dzb = dz.astype(BF16)
                    dq = dq + _dot(dzb, k_ref[sl, _pair_lanes(2 * p)])
                    dk_acc[sl, _pair_lanes(2 * p)] += _dot(dzb, qp[p], TN)
                    dv_acc[sl, _pair_lanes(2 * p)] += _dot(aas[p], dop[p], TN)
                    out.append((pc + pw[:, B - 1:B], gc + gw[:, B - 1:B], dq))
                return tuple(out)

            zero = jnp.zeros((2 * R, 1), F32)
            carry = lax.fori_loop(0, i, lambda j, cr: tile(j, cr, False),
                                  tuple((zero, zero, jnp.zeros((2 * R, LANES), F32)) for _ in range(NP)))
            carry = tile(i, carry, True)
            for p in range(NP):
                dq = carry[p][2]
                dq_ref[pl.ds(0, R), _pair_lanes(2 * p)] = (jnp.where(first[:R], dq[:R], dq[R:]) * scale).astype(dq_ref.dtype)
                if R < B:
                    dq_ref[pl.ds(R, B - R), _pair_lanes(2 * p)] = jnp.zeros((B - R, LANES), dq_ref.dtype)

        if tail == B:
            sweep(B)
        else:
            pl.when(i < nq - 1)(lambda: sweep(B))
            pl.when(i == nq - 1)(lambda: sweep(tail))

        @pl.when(i == nq - 1)
        def _():
            dk_ref[...] = dk_acc[...].astype(dk_ref.dtype)
            dv_ref[...] = dv_acc[...].astype(dv_ref.dtype)

    blk = pl.BlockSpec((B, W), lambda h, i: (i, h))
    full = pl.BlockSpec((T, W), lambda h, i: (0, h))
    return pl.pallas_call(
        body, name="attn_bwd", grid=(H // G, nq),
        in_specs=[blk, full, full, blk, pl.BlockSpec((G, B, 1), lambda h, i: (h, i, 0)), ANY],
        out_specs=[blk, full, full],
        out_shape=[jax.ShapeDtypeStruct((T, D), BF16)] * 3,
        scratch_shapes=[pltpu.VMEM((T, W), F32)] * 2,
        compiler_params=_params("parallel", "arbitrary"),
    )(q, k, v, do, rt, after)


def _loss_head(y, target, tm):
    S, D = y.shape

    def body(y_ref, t_ref, dy_ref, part_ref):
        err = y_ref[...] - t_ref[...]
        dy_ref[...] = err * (1.0 / D)

        @pl.when(pl.program_id(0) == 0)
        def _():
            part_ref[...] = jnp.zeros_like(part_ref)

        part_ref[...] += jnp.sum(err * err, axis=0, keepdims=True)

    spec = pl.BlockSpec((tm, D), lambda i: (i, 0))
    return pl.pallas_call(
        body, name="loss_head", grid=(S // tm,), in_specs=[spec, spec],
        out_specs=[spec, pl.BlockSpec((1, D), lambda i: (0, 0))],
        out_shape=[jax.ShapeDtypeStruct((S, D), F32), jax.ShapeDtypeStruct((1, D), F32)],
        compiler_params=_params("arbitrary"),
    )(y, target)


def _row_tile(R):
    for t in (256, 128, 64, 32, 16, 8):
        if R % t == 0:
            return t
    return R


def _pair_add_bf16(name, g, b1, kind, c_arr, me_arr=None):
    P, Rh, C = b1.shape
    tr = _row_tile(Rh)
    nb = Rh // tr
    own = me_arr is not None

    def body(*refs):
        g_ref, b_ref, o_ref = refs[1 + own:4 + own]
        val = (g_ref[...] + b_ref[...]).astype(o_ref.dtype)
        o_ref[...] = val
        if own:
            @pl.when(pl.program_id(1) == refs[1][0])
            def _():
                refs[-1][...] = val

    if kind == "cols":
        g_spec = pl.BlockSpec((None, tr, C), lambda i, p, c_ref, *_: (p, c_ref[0] * nb + i, 0))
    else:
        g_spec = pl.BlockSpec((tr, C), lambda i, p, c_ref, *_: ((2 * p + c_ref[0]) * nb + i, 0))
    blk = pl.BlockSpec((None, tr, C), lambda i, p, *_: (p, i, 0))
    shape = jax.ShapeDtypeStruct((P, Rh, C), BF16)
    if not own:
        return pl.pallas_call(
            body, name=name,
            grid_spec=pltpu.PrefetchScalarGridSpec(num_scalar_prefetch=1, grid=(nb, P), in_specs=[g_spec, blk], out_specs=blk),
            out_shape=shape, compiler_params=_params("parallel", "parallel"),
        )(c_arr, g, b1)
    mine = pl.BlockSpec((None, tr, C), lambda i, p, c_ref, me_ref: (me_ref[0], i, 0))
    return pl.pallas_call(
        body, name=name,
        grid_spec=pltpu.PrefetchScalarGridSpec(num_scalar_prefetch=2, grid=(nb, P), in_specs=[g_spec, blk], out_specs=[blk, mine]),
        out_shape=[shape, shape], compiler_params=_params("parallel", "arbitrary"),
    )(c_arr, me_arr, g, b1)


def _sum_slots(name, b, half_arr=None):
    P, R, C = b.shape
    tr = _row_tile(R)
    nb = R // tr

    def body(*refs):
        b_ref, o_ref = refs[-2:]
        acc = b_ref[0].astype(F32)
        for s in range(1, P):
            acc = acc + b_ref[s].astype(F32)
        o_ref[...] = acc

    if half_arr is None:
        return pl.pallas_call(
            body, name=name, grid=(nb,),
            in_specs=[pl.BlockSpec((P, tr, C), lambda i: (0, i, 0))],
            out_specs=pl.BlockSpec((tr, C), lambda i: (i, 0)),
            out_shape=jax.ShapeDtypeStruct((R, C), F32),
            compiler_params=_params("parallel"),
        )(b)
    return pl.pallas_call(
        body, name=name,
        grid_spec=pltpu.PrefetchScalarGridSpec(
            num_scalar_prefetch=1, grid=(nb,),
            in_specs=[pl.BlockSpec((P, tr, C), lambda i, half: (0, i, 0))],
            out_specs=pl.BlockSpec((tr, C), lambda i, half: (half[0] * nb + i, 0))),
        out_shape=jax.ShapeDtypeStruct((2 * R, C), F32),
        compiler_params=_params("parallel"),
    )(half_arr, b)


def _adamw(name, w, g, m, v):
    R, C = w.shape
    tr = _row_tile(R)
    c1 = 1.0 - ADAM_B1 ** ADAM_STEP
    c2 = 1.0 - ADAM_B2 ** ADAM_STEP

    def body(w_ref, g_ref, m_ref, v_ref, d_ref, nm_ref, nv_ref):
        gg = g_ref[...]
        nm = ADAM_B1 * m_ref[...] + (1.0 - ADAM_B1) * gg
        nv = ADAM_B2 * v_ref[...] + (1.0 - ADAM_B2) * (gg * gg)
        m_hat = nm / c1
        v_hat = nv / c2
        d_ref[...] = -ADAM_LR * (m_hat / (jnp.sqrt(v_hat) + ADAM_EPS) + ADAM_WD * w_ref[...])
        nm_ref[...] = nm
        nv_ref[...] = nv

    spec = pl.BlockSpec((tr, C), lambda i: (i, 0))
    return pl.pallas_call(
        body, name=name, grid=(R // tr,), in_specs=[spec] * 4, out_specs=[spec] * 3,
        out_shape=[jax.ShapeDtypeStruct((R, C), F32)] * 3,
        compiler_params=_params("parallel"),
    )(w, g, m, v)


def _place():
    x, y, c = lax.axis_index("x"), lax.axis_index("y"), lax.axis_index("c")
    other_chips = [(1 - x, y), (x, 1 - y), (1 - x, 1 - y)]
    return x, y, c, other_chips


def _into_slot(name, w, dtype, slot_arr, n_slots):
    R, C = w.shape
    tr = _row_tile(R)

    def body(slot_ref, w_ref, o_ref):
        o_ref[...] = w_ref[...].astype(o_ref.dtype)

    return pl.pallas_call(
        body, name=name,
        grid_spec=pltpu.PrefetchScalarGridSpec(
            num_scalar_prefetch=1, grid=(R // tr,),
            in_specs=[pl.BlockSpec((tr, C), lambda i, slot: (i, 0))],
            out_specs=pl.BlockSpec((None, tr, C), lambda i, slot: (slot[0], i, 0))),
        out_shape=jax.ShapeDtypeStruct((n_slots, R, C), dtype),
        compiler_params=_params("parallel"),
    )(slot_arr, w)


def _gather_chips(bufs):
    n = len(bufs)

    def body(*refs):
        outs = refs[n:2 * n]
        ici_send, ici_recv, d2d_send, d2d_recv = refs[2 * n:]
        x, y, c, chips = _place()
        me = 2 * x + y
        started = []
        for k in range(n):
            rh = outs[k].shape[1] // 2
            mine = outs[k].at[me, pl.ds(c * rh, rh)]
            for j, (px, py) in enumerate(chips):
                cp = pltpu.make_async_remote_copy(
                    src_ref=mine, dst_ref=mine,
                    send_sem=ici_send.at[3 * k + j], recv_sem=ici_recv.at[3 * k + j],
                    device_id=(px, py, c), device_id_type=MESH)
                cp.start()
                started.append(cp)
        for k in range(n):
            rh = outs[k].shape[1] // 2
            for j, (px, py) in enumerate(chips):
                landed = outs[k].at[2 * px + py, pl.ds(c * rh, rh)]
                pltpu.make_async_remote_copy(
                    src_ref=landed, dst_ref=landed,
                    send_sem=ici_send.at[3 * k + j], recv_sem=ici_recv.at[3 * k + j],
                    device_id=(px, py, c), device_id_type=MESH).wait_recv()
                cp = pltpu.make_async_remote_copy(
                    src_ref=landed, dst_ref=landed,
                    send_sem=d2d_send.at[3 * k + j], recv_sem=d2d_recv.at[3 * k + j],
                    device_id=(x, y, 1 - c), device_id_type=MESH)
                cp.start()
                started.append(cp)
        for k in range(n):
            rh = outs[k].shape[1] // 2
            for j, (px, py) in enumerate(chips):
                landed = outs[k].at[2 * px + py, pl.ds((1 - c) * rh, rh)]
                pltpu.make_async_remote_copy(
                    src_ref=landed, dst_ref=landed,
                    send_sem=d2d_send.at[3 * k + j], recv_sem=d2d_recv.at[3 * k + j],
                    device_id=(x, y, 1 - c), device_id_type=MESH).wait_recv()
        for cp in started:
            cp.wait_send()

    return pl.pallas_call(
        body, name="gather_weights",
        in_specs=[ANY] * n, out_specs=[ANY] * n,
        out_shape=[jax.ShapeDtypeStruct(b.shape, b.dtype) for b in bufs],
        input_output_aliases={k: k for k in range(n)},
        scratch_shapes=[pltpu.SemaphoreType.DMA((3 * n,))] * 4,
        compiler_params=pltpu.CompilerParams(has_side_effects=True),
    )(*bufs)


def _gather_forward(name, bufs):
    n = len(bufs)

    def body(*refs):
        outs = refs[n:2 * n]
        d2d_send, d2d_recv = refs[2 * n:]
        x, y, c, chips = _place()
        started = []
        for k in range(n):
            rh = outs[k].shape[1] // 2
            for j, (px, py) in enumerate(chips):
                landed = outs[k].at[2 * px + py, pl.ds(c * rh, rh)]
                cp = pltpu.make_async_remote_copy(
                    src_ref=landed, dst_ref=landed,
                    send_sem=d2d_send.at[3 * k + j], recv_sem=d2d_recv.at[3 * k + j],
                    device_id=(x, y, 1 - c), device_id_type=MESH)
                cp.start()
                started.append(cp)
        for k in range(n):
            rh = outs[k].shape[1] // 2
            for j, (px, py) in enumerate(chips):
                landed = outs[k].at[2 * px + py, pl.ds((1 - c) * rh, rh)]
                pltpu.make_async_remote_copy(
                    src_ref=landed, dst_ref=landed,
                    send_sem=d2d_send.at[3 * k + j], recv_sem=d2d_recv.at[3 * k + j],
                    device_id=(x, y, 1 - c), device_id_type=MESH).wait_recv()
        for cp in started:
            cp.wait_send()

    return pl.pallas_call(
        body, name=name,
        in_specs=[ANY] * n, out_specs=[ANY] * n,
        out_shape=[jax.ShapeDtypeStruct(b.shape, b.dtype) for b in bufs],
        input_output_aliases={k: k for k in range(n)},
        scratch_shapes=[pltpu.SemaphoreType.DMA((3 * n,))] * 2,
        compiler_params=pltpu.CompilerParams(has_side_effects=True),
    )(*bufs)


def _to_chips_start(name, arrays, n, src_fn, dst_fn, after, to_sibling=False):
    m = len(arrays)
    n_peers = 1 if to_sibling else N_CHIPS - 1

    def body(*refs):
        send_sem, recv_sem = refs[m + 1], refs[m + 2]
        thru = refs[m + 3:2 * m + 3]
        token = refs[2 * m + 3]
        x, y, c, chips = _place()
        peers = [(x, y, 1 - c)] if to_sibling else [(px, py, c) for px, py in chips]
        for k in range(n):
            for j, (px, py, pc) in enumerate(peers):
                pltpu.make_async_remote_copy(
                    src_ref=src_fn(thru, k, px, py, x, y, c), dst_ref=dst_fn(thru, k, px, py, x, y, c),
                    send_sem=send_sem.at[n_peers * k + j], recv_sem=recv_sem.at[n_peers * k + j],
                    device_id=(px, py, pc), device_id_type=MESH).start()
        token[...] = jnp.zeros_like(token)

    res = pl.pallas_call(
        body, name=name,
        out_shape=(pltpu.SemaphoreType.DMA((n_peers * n,)), pltpu.SemaphoreType.DMA((n_peers * n,)),
                   *[pltpu.HBM(a.shape, a.dtype) for a in arrays], jax.ShapeDtypeStruct((8, LANES), F32)),
        in_specs=[HBM] * m + [ANY],
        out_specs=(SEM, SEM, *[HBM] * m, pl.BlockSpec(memory_space=pltpu.VMEM)),
        input_output_aliases={i: i + 2 for i in range(m)},
        compiler_params=pltpu.CompilerParams(has_side_effects=EFFECT),
    )(*[pltpu.with_memory_space_constraint(a, pltpu.HBM) for a in arrays], after)
    return res[0], res[1], list(res[2:2 + m]), res[2 + m]


def _to_chips_wait(name, send_sem, recv_sem, arrays, n, src_fn, land_fn, after, to_sibling=False):
    m = len(arrays)
    after = list(after) if isinstance(after, (list, tuple)) else [after]
    n_peers = 1 if to_sibling else N_CHIPS - 1

    def body(*refs):
        send, recv = refs[m], refs[m + 1]
        outs = refs[m + 2 + len(after):]
        x, y, c, chips = _place()
        peers = [(x, y, 1 - c)] if to_sibling else [(px, py, c) for px, py in chips]
        for k in range(n):
            for j, (px, py, pc) in enumerate(peers):
                cp = pltpu.make_async_remote_copy(
                    src_ref=src_fn(outs, k, px, py, x, y, c), dst_ref=land_fn(outs, k, px, py, x, y, c),
                    send_sem=send.at[n_peers * k + j], recv_sem=recv.at[n_peers * k + j],
                    device_id=(px, py, pc), device_id_type=MESH)
                cp.wait_send()
                cp.wait_recv()

    res = pl.pallas_call(
        body, name=name,
        out_shape=[pltpu.HBM(a.shape, a.dtype) for a in arrays],
        in_specs=[HBM] * m + [SEM, SEM] + [ANY] * len(after), out_specs=[HBM] * m,
        input_output_aliases={i: i for i in range(m)},
        compiler_params=pltpu.CompilerParams(has_side_effects=EFFECT),
    )(*arrays, send_sem, recv_sem, *after)
    return list(res)


def _slot_half(refs, k, chip, c):
    rh = refs[k].shape[1] // 2
    return refs[k].at[chip, pl.ds(c * rh, rh)]


def _ag_mine(refs, k, px, py, x, y, c):
    return _slot_half(refs, k, 2 * x + y, c)


def _ag_theirs(refs, k, px, py, x, y, c):
    return _slot_half(refs, k, 2 * px + py, c)


def _rs_ends(n):
    def src(refs, k, px, py, x, y, c):
        return refs[k].at[2 * px + py]

    def dst(refs, k, px, py, x, y, c):
        return refs[n + k].at[2 * x + y]

    def land(refs, k, px, py, x, y, c):
        return refs[n + k].at[2 * px + py]

    return src, dst, land


def _half(ref, kind, p, c, rh):
    if kind == "cols":
        return ref.at[p, pl.ds(c * rh, rh)]
    return ref.at[pl.ds((2 * p + c) * rh, rh)]


def _swap_ends(kinds, rhs):
    n = len(kinds)

    def src(refs, k, px, py, x, y, c):
        return _half(refs[k // N_CHIPS], kinds[k // N_CHIPS], k % N_CHIPS, 1 - c, rhs[k // N_CHIPS])

    def dst(refs, k, px, py, x, y, c):
        return refs[n + k // N_CHIPS].at[k % N_CHIPS]

    return src, dst


def _join_halves(name, fulls, after=None):
    n = len(fulls)
    extra = [] if after is None else [after]

    def body(*refs):
        outs = refs[n + len(extra):2 * n + len(extra)]
        send_sem, recv_sem = refs[2 * n + len(extra):]
        x, y, c, _ = _place()
        started = []
        for k in range(n):
            rh = outs[k].shape[0] // 2
            mine = outs[k].at[pl.ds(c * rh, rh)]
            cp = pltpu.make_async_remote_copy(
                src_ref=mine, dst_ref=mine, send_sem=send_sem.at[k], recv_sem=recv_sem.at[k],
                device_id=(x, y, 1 - c), device_id_type=MESH)
            cp.start()
            started.append(cp)
        for k in range(n):
            rh = outs[k].shape[0] // 2
            theirs = outs[k].at[pl.ds((1 - c) * rh, rh)]
            pltpu.make_async_remote_copy(
                src_ref=theirs, dst_ref=theirs, send_sem=send_sem.at[k], recv_sem=recv_sem.at[k],
                device_id=(x, y, 1 - c), device_id_type=MESH).wait_recv()
        for cp in started:
            cp.wait_send()

    return pl.pallas_call(
        body, name=name,
        in_specs=[ANY] * (n + len(extra)), out_specs=[ANY] * n,
        out_shape=[jax.ShapeDtypeStruct(f.shape, f.dtype) for f in fulls],
        input_output_aliases={k: k for k in range(n)},
        scratch_shapes=[pltpu.SemaphoreType.DMA((n,))] * 2,
        compiler_params=pltpu.CompilerParams(has_side_effects=True),
    )(*fulls, *extra)


def _gather_all(block):
    def body(in_ref, out_ref, send_sem, recv_sem, local_sem):
        x, y, c, _ = _place()

        def slot(px, py, pc):
            return out_ref.at[4 * px + 2 * py + pc]

        loc = pltpu.make_async_copy(in_ref, slot(x, y, c), local_sem)
        loc.start()
        started = []
        for d in range(1, N_DEV):
            fx, fy, fc = d >> 2, (d >> 1) & 1, d & 1
            cp = pltpu.make_async_remote_copy(
                src_ref=in_ref, dst_ref=slot(x, y, c), send_sem=send_sem.at[d - 1], recv_sem=recv_sem.at[d - 1],
                device_id=(x ^ fx, y ^ fy, c ^ fc), device_id_type=MESH)
            cp.start()
            started.append(cp)
        for d in range(1, N_DEV):
            fx, fy, fc = d >> 2, (d >> 1) & 1, d & 1
            landed = slot(x ^ fx, y ^ fy, c ^ fc)
            pltpu.make_async_remote_copy(
                src_ref=in_ref, dst_ref=landed, send_sem=send_sem.at[d - 1], recv_sem=recv_sem.at[d - 1],
                device_id=(x ^ fx, y ^ fy, c ^ fc), device_id_type=MESH).wait_recv()
        for cp in started:
            cp.wait_send()
        loc.wait()

    return pl.pallas_call(
        body, name="gather_small_grads",
        in_specs=[ANY], out_specs=ANY,
        out_shape=jax.ShapeDtypeStruct((N_DEV,) + block.shape, block.dtype),
        scratch_shapes=[pltpu.SemaphoreType.DMA((N_DEV - 1,))] * 2 + [pltpu.SemaphoreType.DMA(())],
        compiler_params=pltpu.CompilerParams(has_side_effects=True),
    )(block)


def _pack(pieces):
    flat = jnp.concatenate([p.reshape(-1) for p in pieces])
    n = flat.shape[0]
    padded = -(-n // (8 * LANES)) * (8 * LANES)
    return jnp.pad(flat, (0, padded - n)).reshape(-1, LANES)


def _unpack(packed, shapes):
    flat = packed.reshape(-1)
    out, at = [], 0
    for s in shapes:
        n = math.prod(s)
        out.append(flat[at:at + n].reshape(s))
        at += n
    return out


def kernel(x, meta_tokens, pre_mix_g, w_in, gate_b, dw_w, dw_b, conv_ln_g, conv_ln_b, w_conv_out, w_attn_out, w_o, post_mix_g, pre_ffn_g, w_ffn_in, w_ffn_out, post_ffn_g, loss_target, m_meta_tokens, m_pre_mix_g, m_w_in, m_gate_b, m_dw_w, m_dw_b, m_conv_ln_g, m_conv_ln_b, m_w_conv_out, m_w_attn_out, m_w_o, m_post_mix_g, m_pre_ffn_g, m_w_ffn_in, m_w_ffn_out, m_post_ffn_g, v_meta_tokens, v_pre_mix_g, v_w_in, v_gate_b, v_dw_w, v_dw_b, v_conv_ln_g, v_conv_ln_b, v_w_conv_out, v_w_attn_out, v_w_o, v_post_mix_g, v_pre_ffn_g, v_w_ffn_in, v_w_ffn_out, v_post_ffn_g):
    S, D = x.shape[1], x.shape[2]
    L = S + N_META
    T = -(-L // ROW_BLOCK) * ROW_BLOCK
    Ta = -(-L // ATT_BLOCK) * ATT_BLOCK
    tm = _tile(T, MM_ROWS)
    tc = _tile(T, CONTRACT_ROWS)
    ts = _tile(T, STAGE_ROWS)
    tw = _tile(T, WIDE_STAGE_ROWS)
    H = D // HEAD_DIM
    F = w_ffn_out.shape[1] * N_CHIPS
    Dc = D // N_CHIPS
    P = N_CHIPS
    me = 2 * lax.axis_index("x") + lax.axis_index("y")
    c_arr = lax.axis_index("c").astype(jnp.int32).reshape(1)

    dw_w_pad = jnp.pad(dw_w[0], ((0, CONV_PAD - CONV_WIDTH), (0, 0)))
    me_arr = me.astype(jnp.int32).reshape(1)
    to_gather = [("w_in", w_in[0], BF16), ("w_conv_out", w_conv_out[0], BF16), ("w_attn_out", w_attn_out[0], BF16),
                 ("w_o", w_o[0], BF16), ("w_ffn_in", w_ffn_in[0], BF16), ("w_ffn_out", w_ffn_out[0], BF16),
                 ("meta", meta_tokens, F32), ("taps", dw_w_pad, F32)]
    now = ("w_in", "meta", "taps")
    slot = {n: _into_slot("slot_" + n, w, dt, me_arr, P) for n, w, dt in to_gather if n in now}
    g1_send, g1_recv, g1_flight, _ = _to_chips_start(
        "gather_first_start", [slot[n] for n in now], len(now), _ag_mine, _ag_mine, slot["taps"])
    slot.update({n: _into_slot("slot_" + n, w, dt, me_arr, P) for n, w, dt in to_gather if n not in now})
    win3, meta4, taps4 = _gather_forward("gather_first_forward", _to_chips_wait(
        "gather_first_wait", g1_send, g1_recv, g1_flight, len(now), _ag_mine, _ag_theirs,
        [slot[n] for n in slot if n not in now]))
    meta_full = meta4.transpose(1, 0, 2).reshape(N_META, D)
    taps = taps4.transpose(1, 0, 2).reshape(CONV_PAD, D)[:CONV_WIDTH]
    later = ["w_conv_out", "w_attn_out", "w_o", "w_ffn_in", "w_ffn_out"]
    ag_send, ag_recv, in_flight, ag_token = _to_chips_start(
        "gather_rest_start", [slot[n] for n in later], len(later), _ag_mine, _ag_mine, meta4)

    h0 = jnp.concatenate([meta_full, x[0], jnp.zeros((T - L, D), F32)], axis=0)
    (u1,) = _rowwise_fwd("rms_pre_mix", f_rms, [(h0, D, 0)], [(pre_mix_g + ag_token[0:1, 0:1], D, 0)], [(D, BF16)], T, ts)
    p = _mm_nn_cols("mm_in", u1, win3, tm)
    q, k, v = _qkv_split(p, D, T, Ta)
    o2, rtot = _attn_fwd(q, k, v, L)
    landed = _to_chips_wait("gather_rest_wait", ag_send, ag_recv, in_flight, len(later), _ag_mine, _ag_theirs, o2)
    wco4, wao4, wo4, wfi3, wfo4 = _gather_forward("gather_rest_forward", landed)
    wco, wao, wo = (w.reshape(D, D) for w in (wco4, wao4, wo4))
    wfo = wfo4.reshape(F, D)
    (uglu,) = _rowwise_fwd("glu", f_glu, [(p, D, 0), (p, D, 1)], [], [(D, F32)], T, ts)
    yc = _shift_conv("dwconv", uglu, taps, CONV_PAD, CONV_PAD - (CONV_WIDTH - 1), T)
    conv_pars = [(dw_b, D, 0), (conv_ln_g, D, 0), (conv_ln_b, D, 0)]
    (ys,) = _rowwise_fwd("conv_post", f_convpost, [(yc, D, 0)], conv_pars, [(D, BF16)], T, ts)
    y_conv = _mm_nn("mm_conv_out", ys, wco, tm)
    y_attn = _mm_nn("mm_attn_out", o2, wao, tm, rows=T)
    mix_rows = [(p, D, 5), (p, D, 6), (y_conv, D, 0), (y_attn, D, 0)]
    mix_pars = [(gate_b, D, 0), (gate_b, D, 1)]
    (mixin,) = _rowwise_fwd("gate_mix", f_mix, mix_rows, mix_pars, [(D, BF16)], T, ts)
    mix = _mm_nn("mm_o", mixin, wo, tm)
    (h1,) = _rowwise_fwd("res_post_mix", f_res_rms, [(h0, D, 0), (mix, D, 0)], [(post_mix_g, D, 0)], [(D, F32)], T, ts)
    (u2,) = _rowwise_fwd("rms_pre_ffn", f_rms, [(h1, D, 0)], [(pre_ffn_g, D, 0)], [(D, BF16)], T, ts)
    ab = _mm_nn_cols("mm_ffn_in", u2, wfi3, tm)
    (fin,) = _rowwise_fwd("swiglu", f_swiglu, [(ab, F, 0), (ab, F, 1)], [], [(F, BF16)], T, tw)
    f = _mm_nn("mm_ffn_out", fin, wfo, tm)
    (h2,) = _rowwise_fwd("res_post_ffn", f_res_rms, [(h1, D, 0), (f, D, 0)], [(post_ffn_g, D, 0)], [(D, F32)], T, ts)

    dy, part = _loss_head(h2[N_META:L], loss_target[0], _row_tile(S))
    loss = lax.psum(0.5 * jnp.sum(part) / D, ("x", "y", "c"))
    dh2 = jnp.pad(dy, ((N_META, T - L), (0, 0)))

    (df,), (g_post_ffn,) = _rowwise_bwd("res_post_ffn_bwd", f_res_rms, [(h1, D, 0), (f, D, 0)], [(post_ffn_g, D, 0)],
                                        [(dh2, D, 0)], [None, BF16], T, ts)
    dfin = _mm_nt("mm_ffn_out_dx", df, wfo, tm)
    g_wfo = _mm_tn("mm_ffn_out_dw", fin, df, tc, F // MXU_WIDTH)
    (dab,), _ = _rowwise_bwd("swiglu_bwd", f_swiglu, [(ab, F, 0), (ab, F, 1)], [], [(dfin, F, 0)], [BF16, BF16], T, tw,
                             joined=True)
    du2 = _mm_nt_cols("mm_ffn_in_dx", dab, wfi3, _tile(T, 2 * MM_ROWS))
    g_wfi = _mm_tn_cols("mm_ffn_in_dw", u2, dab, tc, P)
    (dh1,), (g_pre_ffn,) = _rowwise_bwd("rms_pre_ffn_bwd", f_rms_id, [(h1, D, 0)], [(pre_ffn_g, D, 0)],
                                        [(du2, D, 0), (dh2, D, 0)], [F32], T, ts)
    (dmix,), (g_post_mix,) = _rowwise_bwd("res_post_mix_bwd", f_res_rms, [(h0, D, 0), (mix, D, 0)], [(post_mix_g, D, 0)],
                                          [(dh1, D, 0)], [None, BF16], T, ts)
    dmixin = _mm_nt("mm_o_dx", dmix, wo, tm)
    g_wo = _mm_tn("mm_o_dw", mixin, dmix, tc, D // MXU_WIDTH)
    (dpc, dpa, dyconv, dyattn), (g_gate_c, g_gate_a) = _rowwise_bwd(
        "gate_mix_bwd", f_mix, mix_rows, mix_pars, [(dmixin, D, 0)], [BF16, BF16, BF16, BF16], T, ts)
    g_wco = _mm_tn("mm_conv_out_dw", ys, dyconv, tc, D // MXU_WIDTH)
    dys = _mm_nt("mm_conv_out_dx", dyconv, wco, tm)
    g_wao = _mm_tn("mm_attn_out_dw", o2, dyattn, tc, D // MXU_WIDTH)
    do2 = _mm_nt("mm_attn_out_dx", dyattn, wao, tm, BF16, out_rows=Ta)

    early = [g_wco, g_wao, g_wo, g_wfi, g_wfo]
    n_early = len(early)
    early_kinds = ["rows", "rows", "rows", "cols", "rows"]
    early_rhs = [(g.shape[1] if kind == "cols" else g.shape[0] // P) // 2 for g, kind in zip(early, early_kinds)]
    sw_src, sw_dst = _swap_ends(early_kinds, early_rhs)
    sw_send, sw_recv, sw_flight, sw_token = _to_chips_start(
        "grad_swap_early_start", early + [lax.empty((P, rh, g.shape[-1]), F32) for g, rh in zip(early, early_rhs)],
        P * n_early, sw_src, sw_dst, do2, to_sibling=True)
    (dyc,), (g_dw_b, g_ln_g, g_ln_b) = _rowwise_bwd(
        "conv_post_bwd", f_convpost, [(yc, D, 0)], [(dw_b + sw_token[0:1, 0:1], D, 0)] + conv_pars[1:],
        [(dys, D, 0)], [F32], T, ts)
    duglu = _shift_conv("dwconv_dx", dyc, taps[::-1], 0, 0, T)
    g_taps = _conv_dw("dwconv_dw", uglu, dyc, T)
    (dp01,), _ = _rowwise_bwd("glu_bwd", f_glu, [(p, D, 0), (p, D, 1)], [], [(duglu, D, 0)], [BF16, BF16], T, ts,
                              joined=True)
    sw_done = _to_chips_wait("grad_swap_early_wait", sw_send, sw_recv, sw_flight, P * n_early, sw_src, sw_dst,
                             [dp01, g_taps], to_sibling=True)
    early_pairs = [_pair_add_bf16("grad_pair_add_%d" % (n + 1), g, b1, kind, c_arr, me_arr)
                   for n, (g, b1, kind) in enumerate(zip(sw_done[:n_early], sw_done[n_early:], early_kinds))]
    rs_src, rs_dst, rs_land = _rs_ends(n_early)
    rs_send, rs_recv, rs_flight, rs_token = _to_chips_start(
        "grad_scatter_start", [pr[0] for pr in early_pairs] + [pr[1] for pr in early_pairs], n_early,
        rs_src, rs_dst, early_pairs[-1][1])
    dq, dk, dv = _attn_bwd(q, k, v, do2, rtot, rs_token, L)
    rs_done = _to_chips_wait("grad_scatter_wait", rs_send, rs_recv, rs_flight, n_early, rs_src, rs_land, dq)
    early_slots = rs_done[n_early:]
    dp = jnp.concatenate([dp01, dq[:T], dk[:T], dv[:T], dpc, dpa], axis=1)
    du1 = _mm_nt_cols("mm_in_dx", dp, win3, _tile(T, 2 * MM_ROWS))
    g_win = _mm_tn_cols("mm_in_dw", u1, dp, tc, P)

    in_rh = g_win.shape[1] // 2
    s2_src, s2_dst = _swap_ends(["cols"], [in_rh])
    s2_send, s2_recv, s2_flight, s2_token = _to_chips_start(
        "grad_swap_in_start", [g_win, lax.empty((P, in_rh, g_win.shape[-1]), F32)], P, s2_src, s2_dst, du1,
        to_sibling=True)
    (dh0,), (g_pre_mix,) = _rowwise_bwd("rms_pre_mix_bwd", f_rms_id, [(h0, D, 0)],
                                        [(pre_mix_g + s2_token[0:1, 0:1], D, 0)],
                                        [(du1, D, 0), (dh1, D, 0)], [F32], T, ts)
    grad_x = dh0[N_META:L][None]

    small_shapes = [(1, D), (1, D), (1, D), (CONV_WIDTH, D), (1, D), (1, D), (1, D), (1, D), (1, D), (1, D), (N_META, D)]
    small = _pack([g_pre_mix, g_gate_c, g_gate_a, g_taps, g_dw_b, g_ln_g, g_ln_b, g_post_mix, g_pre_ffn, g_post_ffn,
                   dh0[:N_META]])
    summed = _sum_slots("small_grad_sum", _gather_all(small))
    (s_pre_mix, s_gate_c, s_gate_a, s_taps, s_dw_b, s_ln_g, s_ln_b, s_post_mix, s_pre_ffn, s_post_ffn,
     s_meta) = _unpack(summed, small_shapes)
    s_gate_b = jnp.concatenate([s_gate_c, s_gate_a], axis=1)
    s_taps = lax.dynamic_slice_in_dim(s_taps, me * Dc, Dc, axis=1)[None]
    s_meta = lax.dynamic_slice_in_dim(s_meta, me * Dc, Dc, axis=1)

    s2_done = _to_chips_wait("grad_swap_in_wait", s2_send, s2_recv, s2_flight, P, s2_src, s2_dst, summed, to_sibling=True)
    win_pair = _pair_add_bf16("grad_pair_add_0", s2_done[0], s2_done[1], "cols", c_arr, me_arr)
    in_src, in_dst, in_land = _rs_ends(1)
    in_send, in_recv, in_flight, in_token = _to_chips_start(
        "grad_scatter_in_start", list(win_pair), 1, in_src, in_dst, win_pair[1])
    g_early = _join_halves("grad_join_halves_early", [_sum_slots("grad_chip_sum_%d" % (n + 1), s, c_arr)
                                                      for n, s in enumerate(early_slots)], in_token)

    grads = {
        "meta_tokens": s_meta, "pre_mix_g": s_pre_mix, "gate_b": s_gate_b, "dw_w": s_taps,
        "dw_b": s_dw_b, "conv_ln_g": s_ln_g, "conv_ln_b": s_ln_b, "w_conv_out": g_early[0][None],
        "w_attn_out": g_early[1][None], "w_o": g_early[2][None], "post_mix_g": s_post_mix, "pre_ffn_g": s_pre_ffn,
        "w_ffn_in": g_early[3][None], "w_ffn_out": g_early[4][None], "post_ffn_g": s_post_ffn,
    }
    weights = {
        "meta_tokens": (meta_tokens, m_meta_tokens, v_meta_tokens), "pre_mix_g": (pre_mix_g, m_pre_mix_g, v_pre_mix_g),
        "w_in": (w_in, m_w_in, v_w_in), "gate_b": (gate_b, m_gate_b, v_gate_b), "dw_w": (dw_w, m_dw_w, v_dw_w),
        "dw_b": (dw_b, m_dw_b, v_dw_b), "conv_ln_g": (conv_ln_g, m_conv_ln_g, v_conv_ln_g),
        "conv_ln_b": (conv_ln_b, m_conv_ln_b, v_conv_ln_b), "w_conv_out": (w_conv_out, m_w_conv_out, v_w_conv_out),
        "w_attn_out": (w_attn_out, m_w_attn_out, v_w_attn_out), "w_o": (w_o, m_w_o, v_w_o),
        "post_mix_g": (post_mix_g, m_post_mix_g, v_post_mix_g), "pre_ffn_g": (pre_ffn_g, m_pre_ffn_g, v_pre_ffn_g),
        "w_ffn_in": (w_ffn_in, m_w_ffn_in, v_w_ffn_in), "w_ffn_out": (w_ffn_out, m_w_ffn_out, v_w_ffn_out),
        "post_ffn_g": (post_ffn_g, m_post_ffn_g, v_post_ffn_g),
    }
    names = list(weights)
    big_names = ["w_in", "w_conv_out", "w_attn_out", "w_o", "w_ffn_in", "w_ffn_out"]
    small_names = [n for n in names if n not in big_names]

    delta, new_m, new_v = {}, {}, {}

    def big_update(n):
        w, m, v2 = weights[n]
        d, nm, nv = _adamw("adamw_" + n, w[0], grads[n][0], m[0], v2[0])
        delta[n], new_m[n], new_v[n] = d[None], nm[None], nv[None]

    for n in big_names[1:]:
        big_update(n)
    shapes = [weights[n][0].shape for n in small_names]
    packed = [_pack([weights[n][k] for n in small_names]) for k in range(3)]
    d, nm, nv = _adamw("adamw_small", packed[0], _pack([grads[n] for n in small_names]), packed[1], packed[2])
    for n, dd, mm, vv in zip(small_names, _unpack(d, shapes), _unpack(nm, shapes), _unpack(nv, shapes)):
        delta[n], new_m[n], new_v[n] = dd, mm, vv

    in_done = _to_chips_wait("grad_scatter_in_wait", in_send, in_recv, in_flight, 1, in_src, in_land,
                             [d] + [delta[n] for n in big_names[1:]])
    (g_in,) = _join_halves("grad_join_halves_in", [_sum_slots("grad_chip_sum_0", in_done[1], c_arr)])
    grads["w_in"] = g_in[None]
    big_update("w_in")

    return (loss, grad_x, *[grads[n].reshape(weights[n][0].shape) for n in names], *[delta[n] for n in names],
            *[new_m[n] for n in names], *[new_v[n] for n in names])
```
